```python
import math
import jax, jax.numpy as jnp
from jax import lax
import numpy as np

D_MODEL = 2048
BATCH = 4
SEQ = 2048
DEPTH = 1
DEC_BATCH = 1
DEC_SEQ = 8192
PAST_LEN = 128

N_META = 16
D_SSM = 1024
SSM_GROUP = 16
N_SSM_GROUPS = D_SSM // SSM_GROUP
SSM_STATE = 64
N_DIR = 2
D_CONV = 1024
CONV_WIDTH = 31
N_EXPERT_GROUPS = 4
EXPERTS_PER_GROUP = 8
N_EXPERTS = N_EXPERT_GROUPS * EXPERTS_PER_GROUP
D_EXPERT = 1024
TOP_K = 2
IN_COLS = D_SSM + 2 * D_CONV + 2 * D_MODEL
EPS = 1e-6
DT_MIN = 1e-3
DT_MAX = 1e-1
LAM_RE_MAX = -1e-4

kernel_name = "hybrid_s5_conformer_hiermoe_encoder"


def _rmsnorm(x, g):
    xf = x.astype(jnp.float32)
    y = xf * lax.rsqrt(jnp.mean(xf * xf, axis=-1, keepdims=True) + EPS)
    return (y * g.astype(jnp.float32)).astype(x.dtype)


def _layernorm(x, g, b):
    xf = x.astype(jnp.float32)
    mu = jnp.mean(xf, axis=-1, keepdims=True)
    var = jnp.mean(jnp.square(xf - mu), axis=-1, keepdims=True)
    y = (xf - mu) * lax.rsqrt(var + EPS)
    return (y * g.astype(jnp.float32) + b.astype(jnp.float32)).astype(x.dtype)


def _complex_affine_combine(c1, c2):
    a1r, a1i, b1r, b1i = c1
    a2r, a2i, b2r, b2i = c2
    ar = a2r * a1r - a2i * a1i
    ai = a2r * a1i + a2i * a1r
    br = a2r * b1r - a2i * b1i + b2r
    bi = a2r * b1i + a2i * b1r + b2i
    return (ar, ai, br, bi)


def _s5_direction(uf, lam_re, lam_im, log_step, b_re, b_im, c_re, c_im):
    lr = jnp.minimum(lam_re.astype(jnp.float32), LAM_RE_MAX)
    li = lam_im.astype(jnp.float32)
    dt = jnp.exp(log_step.astype(jnp.float32))[:, None]
    mag = jnp.exp(lr * dt)
    ar = mag * jnp.cos(li * dt)
    ai = mag * jnp.sin(li * dt)
    den = lr * lr + li * li
    nr = ar - 1.0
    fr = (nr * lr + ai * li) / den
    fi = (ai * lr - nr * li) / den
    bre = b_re.astype(jnp.float32)
    bim = b_im.astype(jnp.float32)
    bbar_r = fr[..., None] * bre - fi[..., None] * bim
    bbar_i = fr[..., None] * bim + fi[..., None] * bre
    bu_r = jnp.einsum('blgh,gph->blgp', uf, bbar_r)
    bu_i = jnp.einsum('blgh,gph->blgp', uf, bbar_i)
    a_r = jnp.broadcast_to(ar, bu_r.shape)
    a_i = jnp.broadcast_to(ai, bu_r.shape)
    _, _, xr, xi = lax.associative_scan(_complex_affine_combine, (a_r, a_i, bu_r, bu_i), axis=1)
    return (jnp.einsum('blgp,ghp->blgh', xr, c_re.astype(jnp.float32))
            - jnp.einsum('blgp,ghp->blgh', xi, c_im.astype(jnp.float32)))


def _s5_branch(u_s, lam_re, lam_im, log_step, b_re, b_im, c_re, c_im, d_skip, w_glu):
    bsz, L, _ = u_s.shape
    uf = u_s.astype(jnp.float32).reshape(bsz, L, N_SSM_GROUPS, SSM_GROUP)
    y = uf * d_skip.astype(jnp.float32).reshape(N_SSM_GROUPS, SSM_GROUP)
    y = y + _s5_direction(uf, lam_re[0], lam_im[0], log_step[0], b_re[0], b_im[0], c_re[0], c_im[0])
    y = y + jnp.flip(_s5_direction(jnp.flip(uf, axis=1), lam_re[1], lam_im[1], log_step[1],
                                   b_re[1], b_im[1], c_re[1], c_im[1]), axis=1)
    z = jax.nn.gelu(y.reshape(bsz, L, D_SSM)).astype(u_s.dtype)
    v, g = jnp.split(z @ w_glu, 2, axis=-1)
    return v * jax.nn.sigmoid(g)


def _conv_branch(c_v, c_g, w_dw, b_dw, ln_g, ln_b, w_pw):
    c = c_v * jax.nn.sigmoid(c_g)
    pad = CONV_WIDTH // 2
    c = lax.conv_general_dilated(c, w_dw[:, None, :].astype(c.dtype), window_strides=(1,),
                                 padding=[(pad, pad)], dimension_numbers=('NWC', 'WIO', 'NWC'),
                                 feature_group_count=D_CONV) + b_dw
    c = _layernorm(c, ln_g, ln_b)
    c = jax.nn.silu(c)
    return c @ w_pw


def _hier_moe(h, rg_w, rg_b, re_w, re_b, w1, w3, w2):
    bsz, L, d = h.shape
    T = bsz * L
    xf = h.reshape(T, d)
    gp = jax.nn.softmax((xf @ rg_w + rg_b).astype(jnp.float32), axis=-1)
    grp = jnp.argmax(gp, axis=-1)
    p_grp = jnp.take_along_axis(gp, grp[:, None], axis=-1)
    el = (xf @ re_w + re_b).astype(jnp.float32).reshape(T, N_EXPERT_GROUPS, EXPERTS_PER_GROUP)
    el = jnp.take_along_axis(el, grp[:, None, None], axis=1)[:, 0]
    ep = jax.nn.softmax(el, axis=-1)
    top_v, top_i = lax.top_k(ep, TOP_K)
    gate = p_grp * top_v / jnp.sum(top_v, axis=-1, keepdims=True)
    eid = (grp[:, None] * EXPERTS_PER_GROUP + top_i).reshape(-1)
    order = jnp.argsort(eid)
    tok = order // TOP_K
    xs = xf[tok]
    sizes = jnp.bincount(eid, length=N_EXPERTS).astype(jnp.int32)
    a = jax.nn.silu(lax.ragged_dot(xs, w1, sizes)) * lax.ragged_dot(xs, w3, sizes)
    o = lax.ragged_dot(a, w2, sizes)
    o = o * gate.reshape(-1)[order][:, None].astype(o.dtype)
    y = jnp.zeros_like(xf).at[tok].add(o)
    return y.reshape(bsz, L, d)


def _encode(x, meta, norm_mix_g, w_in, ssm_lam_re, ssm_lam_im, ssm_log_step, ssm_b_re, ssm_b_im,
            ssm_c_re, ssm_c_im, ssm_d, ssm_w_glu, conv_w_dw, conv_b_dw, conv_ln_g, conv_ln_b,
            conv_w_pw, w_out, norm_ffn_g, router_group_w, router_group_b, router_expert_w,
            router_expert_b, expert_w1, expert_w3, expert_w2, final_g):
    bsz = x.shape[0]
    m = jnp.broadcast_to(meta[None].astype(x.dtype), (bsz, N_META, D_MODEL))
    h = jnp.concatenate([m, x], axis=1)
    splits = [D_SSM, D_SSM + D_CONV, D_SSM + 2 * D_CONV, D_SSM + 2 * D_CONV + D_MODEL]
    for l in range(DEPTH):
        u = _rmsnorm(h, norm_mix_g[l])
        u_s, c_v, c_g, g_a, g_b = jnp.split(u @ w_in[l], splits, axis=-1)
        y_a = _s5_branch(u_s, ssm_lam_re[l], ssm_lam_im[l], ssm_log_step[l], ssm_b_re[l], ssm_b_im[l],
                         ssm_c_re[l], ssm_c_im[l], ssm_d[l], ssm_w_glu[l])
        y_b = _conv_branch(c_v, c_g, conv_w_dw[l], conv_b_dw[l], conv_ln_g[l], conv_ln_b[l], conv_w_pw[l])
        merged = jax.nn.sigmoid(g_a) * y_a + jax.nn.sigmoid(g_b) * y_b
        h = h + (merged @ w_out[l]).astype(h.dtype)
        v = _rmsnorm(h, norm_ffn_g[l])
        h = h + _hier_moe(v, router_group_w[l], router_group_b[l], router_expert_w[l], router_expert_b[l],
                          expert_w1[l], expert_w3[l], expert_w2[l]).astype(h.dtype)
    h = _rmsnorm(h, final_g)
    return h[:, N_META:].astype(x.dtype)


def setup_inputs(seed: int = 0) -> dict:
    key = jax.random.key(seed)
    ks = jax.random.split(key, 32)
    f32 = jnp.float32
    nrm = lambda k, shape, s: jax.random.normal(k, shape, f32) * s
    G, P, H = N_SSM_GROUPS, SSM_STATE, SSM_GROUP
    lam_im_base = math.pi * jnp.arange(P, dtype=f32)
    return {
        "x_prompt": nrm(ks[0], (BATCH, SEQ, D_MODEL), 1.0),
        "x_sample": nrm(ks[1], (DEC_BATCH, DEC_SEQ, D_MODEL), 1.0),
        "meta": nrm(ks[2], (N_META, D_MODEL), 1.0),
        "norm_mix_g": 1.0 + nrm(ks[3], (DEPTH, D_MODEL), 0.02),
        "w_in": nrm(ks[4], (DEPTH, D_MODEL, IN_COLS), D_MODEL ** -0.5),
        "ssm_lam_re": -0.5 + nrm(ks[5], (DEPTH, N_DIR, G, P), 0.01),
        "ssm_lam_im": lam_im_base + nrm(ks[6], (DEPTH, N_DIR, G, P), 0.01),
        "ssm_log_step": jax.random.uniform(ks[7], (DEPTH, N_DIR, G), f32, math.log(DT_MIN), math.log(DT_MAX)),
        "ssm_b_re": nrm(ks[8], (DEPTH, N_DIR, G, P, H), (2 * H) ** -0.5),
        "ssm_b_im": nrm(ks[9], (DEPTH, N_DIR, G, P, H), (2 * H) ** -0.5),
        "ssm_c_re": nrm(ks[10], (DEPTH, N_DIR, G, H, P), P ** -0.5),
        "ssm_c_im": nrm(ks[11], (DEPTH, N_DIR, G, H, P), P ** -0.5),
        "ssm_d": nrm(ks[12], (DEPTH, D_SSM), 1.0),
        "ssm_w_glu": nrm(ks[13], (DEPTH, D_SSM, 2 * D_MODEL), D_SSM ** -0.5),
        "conv_w_dw": nrm(ks[14], (DEPTH, CONV_WIDTH, D_CONV), CONV_WIDTH ** -0.5),
        "conv_b_dw": nrm(ks[15], (DEPTH, D_CONV), 0.01),
        "conv_ln_g": 1.0 + nrm(ks[16], (DEPTH, D_CONV), 0.02),
        "conv_ln_b": nrm(ks[17], (DEPTH, D_CONV), 0.01),
        "conv_w_pw": nrm(ks[18], (DEPTH, D_CONV, D_MODEL), D_CONV ** -0.5),
        "w_out": nrm(ks[19], (DEPTH, D_MODEL, D_MODEL), D_MODEL ** -0.5),
        "norm_ffn_g": 1.0 + nrm(ks[20], (DEPTH, D_MODEL), 0.02),
        "router_group_w": nrm(ks[21], (DEPTH, D_MODEL, N_EXPERT_GROUPS), D_MODEL ** -0.5),
        "router_group_b": nrm(ks[22], (DEPTH, N_EXPERT_GROUPS), 0.01),
        "router_expert_w": nrm(ks[23], (DEPTH, D_MODEL, N_EXPERTS), D_MODEL ** -0.5),
        "router_expert_b": nrm(ks[24], (DEPTH, N_EXPERTS), 0.01),
        "expert_w1": nrm(ks[25], (DEPTH, N_EXPERTS, D_MODEL, D_EXPERT), D_MODEL ** -0.5),
        "expert_w3": nrm(ks[26], (DEPTH, N_EXPERTS, D_MODEL, D_EXPERT), D_MODEL ** -0.5),
        "expert_w2": nrm(ks[27], (DEPTH, N_EXPERTS, D_EXPERT, D_MODEL), D_EXPERT ** -0.5),
        "final_g": 1.0 + nrm(ks[28], (D_MODEL,), 0.02),
    }


def reference(x_prompt, x_sample, meta, norm_mix_g, w_in, ssm_lam_re, ssm_lam_im, ssm_log_step,
              ssm_b_re, ssm_b_im, ssm_c_re, ssm_c_im, ssm_d, ssm_w_glu, conv_w_dw, conv_b_dw,
              conv_ln_g, conv_ln_b, conv_w_pw, w_out, norm_ffn_g, router_group_w, router_group_b,
              router_expert_w, router_expert_b, expert_w1, expert_w3, expert_w2, final_g):
    params = (meta, norm_mix_g, w_in, ssm_lam_re, ssm_lam_im, ssm_log_step, ssm_b_re, ssm_b_im,
              ssm_c_re, ssm_c_im, ssm_d, ssm_w_glu, conv_w_dw, conv_b_dw, conv_ln_g, conv_ln_b,
              conv_w_pw, w_out, norm_ffn_g, router_group_w, router_group_b, router_expert_w,
              router_expert_b, expert_w1, expert_w3, expert_w2, final_g)
    y_prompt = _encode(x_prompt, *params)
    y_sample = _encode(x_sample, *params)
    return (y_prompt, y_sample)
```

```python
import functools

import jax
import jax.numpy as jnp
from jax import lax
from jax.experimental import pallas as pl
from jax.experimental.pallas import tpu as pltpu

F32 = jnp.float32
BF16 = jnp.bfloat16
U32 = jnp.uint32
I32 = jnp.int32

EPS = 1e-6
LAM_RE_MAX = -1e-4
CHUNK = 16
LANES = 128
VMEM_LIMIT = 56 << 20
HIGHEST = lax.Precision.HIGHEST


def _cparams(n_axes):
    return pltpu.CompilerParams(dimension_semantics=("arbitrary",) * n_axes,
                                vmem_limit_bytes=VMEM_LIMIT)


def _resident(shape):
    nd = len(shape)
    return pl.BlockSpec(shape, lambda *_: (0,) * nd, pipeline_mode=pl.Buffered(1))


def _sigmoid(x):
    return 1.0 / (1.0 + jnp.exp(-x))


def _rms(x, g):
    return x * lax.rsqrt(jnp.mean(x * x, axis=-1, keepdims=True) + EPS) * g


def _pack_bf16_pair(a, b):
    def rnd(x):
        u = pltpu.bitcast(x, U32)
        return (u + jnp.uint32(0x7FFF) + ((u >> 16) & jnp.uint32(1))) >> 16
    return (rnd(a) << 16) | rnd(b)


def _unpack_bf16_pair(p):
    a = pltpu.bitcast(p & jnp.uint32(0xFFFF0000), F32)
    b = pltpu.bitcast(p << 16, F32)
    return a, b


def _inproj_body(xp_ref, xs_ref, g_ref, w_ref, us_ref, c_ref, sga_ref, sgb_ref, *,
                 n_p_tiles, d_ssm, d_conv, d_model, col):
    i = pl.program_id(0)
    x = jnp.where(i < n_p_tiles, xp_ref[...], xs_ref[...])
    y = _rms(x, g_ref[...]).astype(BF16)

    def proj(lo, n):
        return jnp.dot(y, w_ref[:, lo:lo + n], preferred_element_type=F32)

    for k in range(d_ssm // col):
        us_ref[:, k * col:(k + 1) * col] = proj(k * col, col).astype(BF16)
    for k in range(d_conv // col):
        cv = proj(d_ssm + k * col, col)
        cg = proj(d_ssm + d_conv + k * col, col)
        c_ref[:, k * col:(k + 1) * col] = (cv * _sigmoid(cg)).astype(BF16)
    base = d_ssm + 2 * d_conv
    for k in range(d_model // col):
        sga_ref[:, k * col:(k + 1) * col] = _sigmoid(proj(base + k * col, col)).astype(BF16)
        sgb_ref[:, k * col:(k + 1) * col] = _sigmoid(
            proj(base + d_model + k * col, col)).astype(BF16)


def _inproj(xp2, xs2, g, w_bf, *, d_ssm, d_conv, tm):
    n_p, d_model = xp2.shape
    n_s = xs2.shape[0]
    n_p_tiles, n_s_tiles = n_p // tm, n_s // tm
    t = n_p + n_s
    col = min(1024, d_ssm, d_conv, d_model)
    body = functools.partial(_inproj_body, n_p_tiles=n_p_tiles, d_ssm=d_ssm, d_conv=d_conv,
                             d_model=d_model, col=col)
    row = lambda i: (i, 0)
    return pl.pallas_call(
        body,
        grid=(n_p_tiles + n_s_tiles,),
        in_specs=[
            pl.BlockSpec((tm, d_model), lambda i: (jnp.minimum(i, n_p_tiles - 1), 0)),
            pl.BlockSpec((tm, d_model), lambda i: (jnp.maximum(i - n_p_tiles, 0), 0)),
            _resident((1, d_model)),
            _resident(w_bf.shape),
        ],
        out_specs=[pl.BlockSpec((tm, d_ssm), row), pl.BlockSpec((tm, d_conv), row),
                   pl.BlockSpec((tm, d_model), row), pl.BlockSpec((tm, d_model), row)],
        out_shape=[jax.ShapeDtypeStruct((t, d_ssm), BF16), jax.ShapeDtypeStruct((t, d_conv), BF16),
                   jax.ShapeDtypeStruct((t, d_model), BF16), jax.ShapeDtypeStruct((t, d_model), BF16)],
        compiler_params=_cparams(1),
        name="inproj",
    )(xp2, xs2, g, w_bf)


def _meta_body(m_ref, g_ref, w_ref, us_ref, c_ref, *, d_ssm, d_conv):
    y = _rms(m_ref[...], g_ref[...]).astype(BF16)
    us_ref[...] = jnp.dot(y, w_ref[:, 0:d_ssm], preferred_element_type=F32).astype(BF16)
    cv = jnp.dot(y, w_ref[:, d_ssm:d_ssm + d_conv], preferred_element_type=F32)
    cg = jnp.dot(y, w_ref[:, d_ssm + d_conv:d_ssm + 2 * d_conv], preferred_element_type=F32)
    c_ref[...] = (cv * _sigmoid(cg)).astype(BF16)


def _meta_inproj(meta, g, w_bf, *, d_ssm, d_conv):
    n_meta, d_model = meta.shape
    ncol = d_ssm + 2 * d_conv
    return pl.pallas_call(
        functools.partial(_meta_body, d_ssm=d_ssm, d_conv=d_conv),
        grid=(1,),
        in_specs=[pl.BlockSpec((n_meta, d_model), lambda i: (0, 0)),
                  pl.BlockSpec((1, d_model), lambda i: (0, 0)),
                  pl.BlockSpec((d_model, ncol), lambda i: (0, 0))],
        out_specs=[pl.BlockSpec((n_meta, d_ssm), lambda i: (0, 0)),
                   pl.BlockSpec((n_meta, d_conv), lambda i: (0, 0))],
        out_shape=[jax.ShapeDtypeStruct((n_meta, d_ssm), BF16),
                   jax.ShapeDtypeStruct((n_meta, d_conv), BF16)],
        compiler_params=_cparams(1),
        name="meta_inproj",
    )(meta, g, w_bf)


def _cmul(ar, ai, br, bi):
    return ar * br - ai * bi, ar * bi + ai * br


def _discretize(lam_re, lam_im, log_step):
    lr = jnp.minimum(lam_re, LAM_RE_MAX)
    dt = jnp.exp(log_step)
    mag = jnp.exp(lr * dt)
    ar = mag * jnp.cos(lam_im * dt)
    ai = mag * jnp.sin(lam_im * dt)
    den = lr * lr + lam_im * lam_im
    nr = ar - 1.0
    fr = (nr * lr + ai * lam_im) / den
    fi = (ai * lr - nr * lam_im) / den
    return ar, ai, fr, fi


def _cpow(ar, ai, k, nbits, shape):
    pr = jnp.ones(shape, F32)
    pi = jnp.zeros(shape, F32)
    br = jnp.broadcast_to(ar, shape)
    bi = jnp.broadcast_to(ai, shape)
    kk = jnp.broadcast_to(k, shape)
    for b in range(nbits):
        sel = ((kk >> b) & 1) == 1
        nr, ni = _cmul(pr, pi, br, bi)
        pr = jnp.where(sel, nr, pr)
        pi = jnp.where(sel, ni, pi)
        br, bi = _cmul(br, bi, br, bi)
    return pr, pi


def _ssm_prep_body(lam_re_c, lam_im_c, ls_c, ct_re, ct_im, bt_re, bt_im,
                   lam_re_r, lam_im_r, ls_r, btr_re, btr_im, dpad, xmeta,
                   w1_ref, cst_ref, aq_ref, init_ref, *, n_state, n_ch):
    q = CHUNK
    width = q * n_ch
    par = pl.program_id(0) % 2
    kblk = lax.broadcasted_iota(I32, (1, width), 1) // n_ch
    strips = []
    for d in range(2):
        ar, ai, fr, fi = _discretize(lam_re_c[d], lam_im_c[d], ls_c[d])
        kexp = kblk if d == 0 else (q - 1) - kblk
        wr, wi = _cpow(ar, ai, kexp, 4, (n_state, width))
        gcr, gci = _cmul(ct_re[d], ct_im[d], wr, wi)
        gfr, gfi = _cmul(gcr, gci, fr, fi)
        strips.append(jnp.dot(bt_re[d], gfr, precision=HIGHEST, preferred_element_type=F32)
                      - jnp.dot(bt_im[d], gfi, precision=HIGHEST, preferred_element_type=F32))
        g1r, g1i = _cmul(gcr, gci, ar, ai)
        rowmask = (lax.broadcasted_iota(I32, (2 * n_state, width), 0) // n_state) == par
        rep_r = jnp.concatenate([g1r, g1r], axis=0)
        rep_i = jnp.concatenate([g1i, g1i], axis=0)
        r0 = 4 * n_state * d
        cst_ref[r0:r0 + 2 * n_state, :] = jnp.where(rowmask, rep_r, 0.0).astype(BF16)
        cst_ref[r0 + 2 * n_state:r0 + 4 * n_state, :] = jnp.where(rowmask, -rep_i, 0.0).astype(BF16)

    zf, zb = strips
    zero = jnp.zeros((n_ch, width), F32)
    z512 = jnp.concatenate([zb, zero], axis=1) + pltpu.roll(
        jnp.concatenate([zf, zero], axis=1), (q - 1) * n_ch, 1)
    row = lax.broadcasted_iota(I32, (n_ch, 2 * width), 0)
    lane = lax.broadcasted_iota(I32, (n_ch, 2 * width), 1)
    z512 = z512 + jnp.where(lane - (q - 1) * n_ch == row, dpad[...], 0.0)
    for s in range(q):
        sh = (q - 1 - s) * n_ch
        blk = z512 if sh == 0 else pltpu.roll(z512, 2 * width - sh, 1)
        w1_ref[n_ch * s:n_ch * (s + 1), 0:width] = blk[:, 0:width].astype(BF16)

    parmask = (lax.broadcasted_iota(I32, (1, 2 * n_state), 1) // n_state) == par
    for d in range(2):
        ar, ai, fr, fi = _discretize(lam_re_r[d], lam_im_r[d], ls_r[d])
        pw = [(jnp.ones_like(ar), jnp.zeros_like(ar))]
        for _ in range(q):
            pw.append(_cmul(pw[-1][0], pw[-1][1], ar, ai))
        col = width + 4 * n_state * d
        for s in range(q):
            e = (q - 1 - s) if d == 0 else s
            cr, ci = _cmul(fr, fi, pw[e][0], pw[e][1])
            br, bi = _cmul(btr_re[d], btr_im[d], cr, ci)
            w1_ref[n_ch * s:n_ch * (s + 1), col:col + 2 * n_state] = (
                jnp.where(parmask, br, 0.0).astype(BF16))
            w1_ref[n_ch * s:n_ch * (s + 1), col + 2 * n_state:col + 4 * n_state] = (
                jnp.where(parmask, bi, 0.0).astype(BF16))
        aq_ref[:, col - width:col - width + 2 * n_state] = pw[q][0]
        aq_ref[:, col - width + 2 * n_state:col - width + 4 * n_state] = pw[q][1]

    init_ref[...] = jnp.dot(xmeta[...].astype(BF16), w1_ref[:, width:width + 4 * n_state],
                            preferred_element_type=F32)


def _ssm_prep(lam_re, lam_im, log_step, b_re, b_im, c_re, c_im, d_skip, us_meta):
    _, n_g, n_state, n_ch = b_re.shape
    q = CHUNK
    width = q * n_ch
    dup = lambda x: jnp.concatenate([x, x], axis=-1)
    lam_re_c, lam_im_c = lam_re[..., None], lam_im[..., None]
    ls_c = log_step[..., None, None]
    ct_re = jnp.tile(jnp.swapaxes(c_re, -1, -2), (1, 1, 1, q))
    ct_im = jnp.tile(jnp.swapaxes(c_im, -1, -2), (1, 1, 1, q))
    bt_re, bt_im = jnp.swapaxes(b_re, -1, -2), jnp.swapaxes(b_im, -1, -2)
    lam_re_r, lam_im_r = dup(lam_re)[:, :, None, :], dup(lam_im)[:, :, None, :]
    ls_r = jnp.broadcast_to(log_step[..., None, None], (2, n_g, 1, 2 * n_state))
    btr_re, btr_im = dup(bt_re), dup(bt_im)
    dpad = jnp.pad(d_skip.reshape(n_g, 1, n_ch), ((0, 0), (0, 0), ((q - 1) * n_ch, width)))
    xm = us_meta.astype(F32).reshape(q, n_g, n_ch).transpose(1, 0, 2).reshape(n_g, 1, width)
    xm = jnp.pad(xm, ((0, 0), (0, 7), (0, 0)))

    def dspec(shape):
        return pl.BlockSpec((2, None) + shape, lambda g: (0, g, 0, 0))

    def gspec(shape):
        return pl.BlockSpec((None,) + shape, lambda g: (g, 0, 0))

    body = functools.partial(_ssm_prep_body, n_state=n_state, n_ch=n_ch)
    return pl.pallas_call(
        body,
        grid=(n_g,),
        in_specs=[dspec((n_state, 1)), dspec((n_state, 1)), dspec((1, 1)),
                  dspec((n_state, width)), dspec((n_state, width)),
                  dspec((n_ch, n_state)), dspec((n_ch, n_state)),
                  dspec((1, 2 * n_state)), dspec((1, 2 * n_state)), dspec((1, 2 * n_state)),
                  dspec((n_ch, 2 * n_state)), dspec((n_ch, 2 * n_state)),
                  gspec((1, 2 * width)), gspec((8, width))],
        out_specs=[gspec((width, width + 8 * n_state)), gspec((8 * n_state, width)),
                   gspec((1, 8 * n_state)), gspec((8, 4 * n_state))],
        out_shape=[jax.ShapeDtypeStruct((n_g, width, width + 8 * n_state), BF16),
                   jax.ShapeDtypeStruct((n_g, 8 * n_state, width), BF16),
                   jax.ShapeDtypeStruct((n_g, 1, 8 * n_state), F32),
                   jax.ShapeDtypeStruct((n_g, 8, 4 * n_state), F32)],
        compiler_params=_cparams(1),
        name="ssm_prep",
    )(lam_re_c, lam_im_c, ls_c, ct_re, ct_im, bt_re, bt_im,
      lam_re_r, lam_im_r, ls_r, btr_re, btr_im, dpad, xm)


def _ssm_mm1_body(x_ref, w1_ref, yi_ref, s_ref, *, gb, width, sw):
    for j in range(0, gb, 2):
        r0 = jnp.dot(x_ref[j], w1_ref[j], preferred_element_type=F32)
        r1 = jnp.dot(x_ref[j + 1], w1_ref[j + 1], preferred_element_type=F32)
        yi_ref[j] = r0[:, 0:width]
        yi_ref[j + 1] = r1[:, 0:width]
        s_ref[:, (j // 2) * sw:(j // 2 + 1) * sw] = r0[:, width:] + r1[:, width:]


def _ssm_mm1(xt, w1, *, gb):
    n_g, nc, width = xt.shape
    sw = w1.shape[-1] - width
    body = functools.partial(_ssm_mm1_body, gb=gb, width=width, sw=sw)
    return pl.pallas_call(
        body,
        grid=(n_g // gb,),
        in_specs=[pl.BlockSpec((gb, nc, width), lambda i: (i, 0, 0)),
                  pl.BlockSpec((gb, width, width + sw), lambda i: (i, 0, 0))],
        out_specs=[pl.BlockSpec((gb, nc, width), lambda i: (i, 0, 0)),
                   pl.BlockSpec((nc, gb // 2 * sw), lambda i: (0, i))],
        out_shape=[jax.ShapeDtypeStruct((n_g, nc, width), F32),
                   jax.ShapeDtypeStruct((nc, n_g // 2 * sw), F32)],
        compiler_params=_cparams(1),
        name="ssm_mm1",
    )(xt, w1)


def _ssm_scan_body(sf_ref, sb_ref, aq_ref, init_ref, xf_ref, xb_ref, stf, stb, *,
                   cb, hw, n_pblk, blk_pseq, blk_sseq):
    j = pl.program_id(1)
    pos = jnp.where(j < n_pblk, lax.rem(j, blk_pseq), lax.rem(jnp.maximum(j - n_pblk, 0), blk_sseq))

    @pl.when(pos == 0)
    def _():
        stf[...] = init_ref[...]
        stb[...] = jnp.zeros_like(stb)

    afr, afi = aq_ref[:, 0:hw], aq_ref[:, hw:2 * hw]
    abr, abi = aq_ref[:, 2 * hw:3 * hw], aq_ref[:, 3 * hw:4 * hw]

    def body(i, carry):
        fr, fi, br, bi = carry
        ib = cb - 1 - i
        xf_ref[i, :, 0:hw] = fr
        xf_ref[i, :, hw:2 * hw] = fi
        xb_ref[ib, :, 0:hw] = br
        xb_ref[ib, :, hw:2 * hw] = bi
        nfr = afr * fr - afi * fi + sf_ref[i, :, 0:hw]
        nfi = afr * fi + afi * fr + sf_ref[i, :, hw:2 * hw]
        nbr = abr * br - abi * bi + sb_ref[ib, :, 0:hw]
        nbi = abr * bi + abi * br + sb_ref[ib, :, hw:2 * hw]
        return nfr, nfi, nbr, nbi

    fr, fi, br, bi = lax.fori_loop(
        0, cb, body, (stf[:, 0:hw], stf[:, hw:2 * hw], stb[:, 0:hw], stb[:, hw:2 * hw]))
    stf[:, 0:hw] = fr
    stf[:, hw:2 * hw] = fi
    stb[:, 0:hw] = br
    stb[:, hw:2 * hw] = bi


def _ssm_scan(s3, aq, init, *, cb, chunks_pseq, chunks_sseq, n_pchunks):
    nc, n_pair, sw = s3.shape
    hw = sw // 4
    pb = min(8, n_pair)
    n_pblk, blk_pseq, blk_sseq = n_pchunks // cb, chunks_pseq // cb, chunks_sseq // cb

    def bwd_block(j):
        in_p = j < n_pblk
        pos = jnp.where(in_p, lax.rem(j, blk_pseq), lax.rem(jnp.maximum(j - n_pblk, 0), blk_sseq))
        ln = jnp.where(in_p, blk_pseq, blk_sseq)
        return j - pos + ln - 1 - pos

    body = functools.partial(_ssm_scan_body, cb=cb, hw=hw, n_pblk=n_pblk, blk_pseq=blk_pseq,
                             blk_sseq=blk_sseq)
    return pl.pallas_call(
        body,
        grid=(n_pair // pb, nc // cb),
        in_specs=[pl.BlockSpec((cb, pb, 2 * hw), lambda p, j: (j, p, 0)),
                  pl.BlockSpec((cb, pb, 2 * hw), lambda p, j: (bwd_block(j), p, 1)),
                  pl.BlockSpec((pb, sw), lambda p, j: (p, 0)),
                  pl.BlockSpec((pb, 2 * hw), lambda p, j: (p, 0))],
        out_specs=[pl.BlockSpec((cb, pb, 2 * hw), lambda p, j: (j, p, 0)),
                   pl.BlockSpec((cb, pb, 2 * hw), lambda p, j: (bwd_block(j), p, 0))],
        out_shape=[jax.ShapeDtypeStruct((nc, n_pair, 2 * hw), F32),
                   jax.ShapeDtypeStruct((nc, n_pair, 2 * hw), F32)],
        scratch_shapes=[pltpu.VMEM((pb, 2 * hw), F32), pltpu.VMEM((pb, 2 * hw), F32)],
        compiler_params=_cparams(2),
        name="ssm_scan",
    )(s3, s3, aq, init)


def _gelu_tanh(x):
    return 0.5 * x * (1.0 + jnp.tanh(0.7978845608028654 * (x + 0.044715 * (x * x * x))))


def _ssm_mm2_body(yi_ref, xf_ref, xb_ref, cst_ref, z_ref, *, gb, hs):
    for j in range(gb):
        cols = slice((j // 2) * hs, (j // 2 + 1) * hs)
        y = (yi_ref[j]
             + jnp.dot(xf_ref[:, cols].astype(BF16), cst_ref[j, 0:hs, :], preferred_element_type=F32)
             + jnp.dot(xb_ref[:, cols].astype(BF16), cst_ref[j, hs:2 * hs, :],
                       preferred_element_type=F32))
        z_ref[j] = _gelu_tanh(y).astype(BF16)


def _ssm_mm2(yi, xf2, xb2, cst, *, gb):
    n_g, nc, width = yi.shape
    hs = cst.shape[1] // 2
    body = functools.partial(_ssm_mm2_body, gb=gb, hs=hs)
    return pl.pallas_call(
        body,
        grid=(n_g // gb,),
        in_specs=[pl.BlockSpec((gb, nc, width), lambda i: (i, 0, 0)),
                  pl.BlockSpec((nc, gb // 2 * hs), lambda i: (0, i)),
                  pl.BlockSpec((nc, gb // 2 * hs), lambda i: (0, i)),
                  pl.BlockSpec((gb, 2 * hs, width), lambda i: (i, 0, 0))],
        out_specs=pl.BlockSpec((gb, nc, width), lambda i: (i, 0, 0)),
        out_shape=jax.ShapeDtypeStruct((n_g, nc, width), BF16),
        compiler_params=_cparams(1),
        name="ssm_mm2",
    )(yi, xf2, xb2, cst)


def _mix_body(xp_ref, xs_ref, z_ref, c_ref, cprev_ref, cnext_ref, cmeta_ref, sga_ref, sgb_ref,
              wglu_ref, wpw_ref, wout_ref, wdw_ref, bdw_ref, lng_ref, lnb_ref, gffn_ref,
              rw_ref, rb_ref,
              h1_ref, v_ref, route_ref, cnt_ref,
              cw_ref, conv_ref, *,
              tm, n_p_tiles, tiles_per_pseq, tiles_per_sseq, d_model, d_conv, conv_w,
              n_grp, n_exp, exp_per_grp, rc):
    i = pl.program_id(0)
    in_prompt = i < n_p_tiles
    x = jnp.where(in_prompt, xp_ref[...], xs_ref[...])
    pos_p = lax.rem(i, tiles_per_pseq)
    pos_s = lax.rem(jnp.maximum(i - n_p_tiles, 0), tiles_per_sseq)
    is_start = jnp.where(in_prompt, pos_p == 0, pos_s == 0)
    is_end = jnp.where(in_prompt, pos_p == tiles_per_pseq - 1, pos_s == tiles_per_sseq - 1)

    halo = CHUNK
    pad = conv_w // 2
    cw_ref[0:halo, :] = jnp.where(is_start, cmeta_ref[...], cprev_ref[...]).astype(F32)
    cw_ref[halo:halo + tm, :] = c_ref[...].astype(F32)
    cw_ref[halo + tm:2 * halo + tm, :] = jnp.where(is_end, 0.0, cnext_ref[...].astype(F32))
    for lc in range(d_conv // LANES):
        ls = slice(lc * LANES, (lc + 1) * LANES)
        for r0 in range(0, tm, rc):
            acc = jnp.zeros((rc, LANES), F32)
            for k in range(conv_w):
                off = halo - pad + k + r0
                acc = acc + wdw_ref[k:k + 1, ls] * cw_ref[off:off + rc, ls]
            conv_ref[r0:r0 + rc, ls] = acc
    cc = conv_ref[...] + bdw_ref[...]
    mu = jnp.mean(cc, axis=-1, keepdims=True)
    var = jnp.mean(jnp.square(cc - mu), axis=-1, keepdims=True)
    cc = (cc - mu) * lax.rsqrt(var + EPS) * lng_ref[...] + lnb_ref[...]
    cc = (cc * _sigmoid(cc)).astype(BF16)
    y_b = jnp.dot(cc, wpw_ref[...], preferred_element_type=F32)

    z = z_ref[...]
    va = jnp.dot(z, wglu_ref[:, 0:d_model], preferred_element_type=F32)
    ga = jnp.dot(z, wglu_ref[:, d_model:2 * d_model], preferred_element_type=F32)
    y_a = va * _sigmoid(ga)
    merged = (sga_ref[...].astype(F32) * y_a + sgb_ref[...].astype(F32) * y_b).astype(BF16)
    h1 = x + jnp.dot(merged, wout_ref[...], preferred_element_type=F32)
    h1_ref[...] = h1
    v = _rms(h1, gffn_ref[...])
    half = d_model // 2
    v_ref[...] = _pack_bf16_pair(v[:, 0:half], v[:, half:d_model])

    logits = jnp.dot(v, rw_ref[...], precision=HIGHEST, preferred_element_type=F32) + rb_ref[...]
    lane = lax.broadcasted_iota(I32, (tm, LANES), 1).astype(F32)
    big = jnp.float32(1e9)
    neg = jnp.float32(-jnp.inf)
    gmask = lane < n_grp
    lg = jnp.where(gmask, logits, neg)
    gmax = jnp.max(lg, axis=-1, keepdims=True)
    grp = jnp.min(jnp.where(lg == gmax, lane, big), axis=-1, keepdims=True)
    p_grp = 1.0 / jnp.sum(jnp.where(gmask, jnp.exp(logits - gmax), 0.0), axis=-1, keepdims=True)
    lo = n_grp + grp * exp_per_grp
    emask = (lane >= lo) & (lane < lo + exp_per_grp)
    le = jnp.where(emask, logits, neg)
    m1 = jnp.max(le, axis=-1, keepdims=True)
    i1 = jnp.min(jnp.where(le == m1, lane, big), axis=-1, keepdims=True)
    le2 = jnp.where(lane == i1, neg, le)
    m2 = jnp.max(le2, axis=-1, keepdims=True)
    i2 = jnp.min(jnp.where(le2 == m2, lane, big), axis=-1, keepdims=True)
    t = jnp.exp(m2 - m1)
    w1 = 1.0 / (1.0 + t)
    e1 = i1 - n_grp
    e2 = i2 - n_grp
    route_ref[...] = jnp.where(lane == 0, e1, jnp.where(lane == 1, e2, jnp.where(
        lane == 2, p_grp * w1, jnp.where(lane == 3, p_grp * (t * w1), 0.0))))

    @pl.when(i == 0)
    def _():
        cnt_ref[...] = jnp.zeros_like(cnt_ref)

    hot = jnp.where((lane == e1) | (lane == e2), 1.0, 0.0)
    cnt_ref[...] += jnp.sum(hot, axis=0, keepdims=True)


def _mix(xp2, xs2, z, c, c_meta, sga, sgb, wglu, wpw, wout, wdw, bdw, lng, lnb, gffn, rw, rb, *,
         tm, p_seq, s_seq, n_grp, n_exp):
    n_p, d_model = xp2.shape
    n_s = xs2.shape[0]
    t = n_p + n_s
    d_ssm, d_conv = z.shape[1], c.shape[1]
    conv_w = wdw.shape[0]
    n_p_tiles, n_s_tiles = n_p // tm, n_s // tm
    hpt = tm // CHUNK
    n_hblk = t // CHUNK
    body = functools.partial(
        _mix_body, tm=tm, n_p_tiles=n_p_tiles, tiles_per_pseq=p_seq // tm,
        tiles_per_sseq=s_seq // tm, d_model=d_model, d_conv=d_conv, conv_w=conv_w,
        n_grp=n_grp, n_exp=n_exp, exp_per_grp=n_exp // n_grp, rc=min(64, tm))
    row = lambda i: (i, 0)
    return pl.pallas_call(
        body,
        grid=(n_p_tiles + n_s_tiles,),
        in_specs=[
            pl.BlockSpec((tm, d_model), lambda i: (jnp.minimum(i, n_p_tiles - 1), 0)),
            pl.BlockSpec((tm, d_model), lambda i: (jnp.maximum(i - n_p_tiles, 0), 0)),
            pl.BlockSpec((tm, d_ssm), row),
            pl.BlockSpec((tm, d_conv), row),
            pl.BlockSpec((CHUNK, d_conv), lambda i: (jnp.maximum(i * hpt - 1, 0), 0)),
            pl.BlockSpec((CHUNK, d_conv), lambda i: (jnp.minimum((i + 1) * hpt, n_hblk - 1), 0)),
            _resident(c_meta.shape),
            pl.BlockSpec((tm, d_model), row),
            pl.BlockSpec((tm, d_model), row),
            _resident(wglu.shape), _resident(wpw.shape), _resident(wout.shape),
            _resident(wdw.shape), _resident(bdw.shape), _resident(lng.shape), _resident(lnb.shape),
            _resident(gffn.shape), _resident(rw.shape), _resident(rb.shape),
        ],
        out_specs=[pl.BlockSpec((tm, d_model), row), pl.BlockSpec((tm, d_model // 2), row),
                   pl.BlockSpec((tm, LANES), row), pl.BlockSpec((1, LANES), lambda i: (0, 0))],
        out_shape=[jax.ShapeDtypeStruct((t, d_model), F32),
                   jax.ShapeDtypeStruct((t, d_model // 2), U32),
                   jax.ShapeDtypeStruct((t, LANES), F32),
                   jax.ShapeDtypeStruct((1, LANES), F32)],
        scratch_shapes=[pltpu.VMEM((tm + 2 * CHUNK, d_conv), F32), pltpu.VMEM((tm, d_conv), F32)],
        compiler_params=_cparams(1),
        name="mix",
    )(xp2, xs2, z, c, c, c, c_meta, sga, sgb, wglu, wpw, wout, wdw, bdw, lng, lnb, gffn, rw, rb)


def _positions_body(route_ref, offs_ref, dest_ref, carry_ref, *, te):
    @pl.when(pl.program_id(0) == 0)
    def _():
        carry_ref[...] = jnp.zeros_like(carry_ref)

    lane = lax.broadcasted_iota(I32, (te, LANES), 1).astype(F32)
    r = route_ref[...]
    oh1 = lane == r[:, 0:1]
    oh2 = lane == r[:, 1:2]
    both = jnp.where(oh1 | oh2, 1.0, 0.0)
    tri = jnp.where(lax.broadcasted_iota(I32, (te, te), 0) > lax.broadcasted_iota(I32, (te, te), 1),
                    1.0, 0.0).astype(BF16)
    before = jnp.dot(tri, both.astype(BF16), preferred_element_type=F32)
    base = before + carry_ref[...] + offs_ref[...]
    d1 = jnp.sum(jnp.where(oh1, base, 0.0), axis=-1, keepdims=True)
    d2 = jnp.sum(jnp.where(oh2, base, 0.0), axis=-1, keepdims=True)
    dest_ref[...] = jnp.where(lane == 0, d1, jnp.where(lane == 1, d2, 0.0)).astype(I32)
    carry_ref[...] += jnp.sum(both, axis=0, keepdims=True)


def _positions(route, offs, *, te):
    t = route.shape[0]
    return pl.pallas_call(
        functools.partial(_positions_body, te=te),
        grid=(t // te,),
        in_specs=[pl.BlockSpec((te, LANES), lambda i: (i, 0)),
                  pl.BlockSpec((1, LANES), lambda i: (0, 0))],
        out_specs=pl.BlockSpec((te, LANES), lambda i: (i, 0)),
        out_shape=jax.ShapeDtypeStruct((t, LANES), I32),
        scratch_shapes=[pltpu.VMEM((1, LANES), F32)],
        compiler_params=_cparams(1),
        name="positions",
    )(route, offs)


def _row_copy(src_ref, src_row, dst_ref, dst_row, sem):
    return pltpu.make_async_copy(src_ref.at[pl.ds(src_row, 1), :],
                                 dst_ref.at[pl.ds(dst_row, 1), :], sem)


def _dispatch_body(dest_hbm, v_ref, xs_in, xs_hbm, dsm, sem, dsem, *, td):
    del xs_in
    i = pl.program_id(0)
    cp = pltpu.make_async_copy(dest_hbm.at[pl.ds(i, 1), :], dsm, dsem)
    cp.start()
    cp.wait()

    def issue(r, _):
        _row_copy(v_ref, r, xs_hbm, dsm[0, 2 * r], sem).start()
        _row_copy(v_ref, r, xs_hbm, dsm[0, 2 * r + 1], sem).start()
        return 0
    lax.fori_loop(0, td, issue, 0)

    def drain(r, _):
        _row_copy(v_ref, 0, xs_hbm, 0, sem).wait()
        _row_copy(v_ref, 0, xs_hbm, 0, sem).wait()
        return 0
    lax.fori_loop(0, td, drain, 0)


def _dispatch(dest2, v, xs0, *, td):
    t, w = v.shape
    return pl.pallas_call(
        functools.partial(_dispatch_body, td=td),
        grid=(t // td,),
        in_specs=[pl.BlockSpec(memory_space=pl.ANY),
                  pl.BlockSpec((td, w), lambda i: (i, 0)),
                  pl.BlockSpec(memory_space=pl.ANY)],
        out_specs=pl.BlockSpec(memory_space=pl.ANY),
        out_shape=jax.ShapeDtypeStruct(xs0.shape, U32),
        scratch_shapes=[pltpu.SMEM((1, 2 * td), I32), pltpu.SemaphoreType.DMA,
                        pltpu.SemaphoreType.DMA],
        input_output_aliases={2: 0},
        compiler_params=_cparams(1),
        name="dispatch",
    )(dest2, v, xs0)


def _experts_body(te_ref, nused_ref, xs_ref, w1_ref, w3_ref, w2_ref, o_ref, *, half):
    del te_ref
    i = pl.program_id(0)

    @pl.when(i < nused_ref[0])
    def _():
        a, b = _unpack_bf16_pair(xs_ref[...])
        a, b = a.astype(BF16), b.astype(BF16)

        def up(w_ref):
            return (jnp.dot(a, w_ref[0:half, :], preferred_element_type=F32)
                    + jnp.dot(b, w_ref[half:2 * half, :], preferred_element_type=F32))
        h1 = up(w1_ref)
        act = (h1 * _sigmoid(h1) * up(w3_ref)).astype(BF16)
        o = jnp.dot(act, w2_ref[...], preferred_element_type=F32)
        o_ref[...] = _pack_bf16_pair(o[:, 0:half], o[:, half:2 * half])

    @pl.when(i >= nused_ref[0])
    def _():
        o_ref[...] = jnp.zeros_like(o_ref)


def _experts(tile_expert, n_used, xs, w1, w3, w2, *, tme):
    rows, half = xs.shape
    n_e, d_model, d_exp = w1.shape
    grid_spec = pltpu.PrefetchScalarGridSpec(
        num_scalar_prefetch=2,
        grid=(rows // tme,),
        in_specs=[pl.BlockSpec((tme, half), lambda i, te, nu: (i, 0)),
                  pl.BlockSpec((None, d_model, d_exp), lambda i, te, nu: (te[i], 0, 0)),
                  pl.BlockSpec((None, d_model, d_exp), lambda i, te, nu: (te[i], 0, 0)),
                  pl.BlockSpec((None, d_exp, d_model), lambda i, te, nu: (te[i], 0, 0))],
        out_specs=pl.BlockSpec((tme, half), lambda i, te, nu: (i, 0)),
    )
    return pl.pallas_call(
        functools.partial(_experts_body, half=half),
        grid_spec=grid_spec,
        out_shape=jax.ShapeDtypeStruct((rows, half), U32),
        compiler_params=_cparams(1),
        name="experts",
    )(tile_expert, n_used, xs, w1, w3, w2)


def _combine_body(dest_hbm, o_hbm, h1_ref, route_ref, g_ref, y_ref, dsm, gbuf, sem, dsem, *,
                  tf, tile0, half):
    i = pl.program_id(0)
    cp = pltpu.make_async_copy(dest_hbm.at[pl.ds(tile0 + i, 1), :], dsm, dsem)
    cp.start()
    cp.wait()

    def issue(r, _):
        _row_copy(o_hbm, dsm[0, 2 * r], gbuf.at[0], r, sem).start()
        _row_copy(o_hbm, dsm[0, 2 * r + 1], gbuf.at[1], r, sem).start()
        return 0
    lax.fori_loop(0, tf, issue, 0)

    def drain(r, _):
        _row_copy(o_hbm, 0, gbuf.at[0], 0, sem).wait()
        _row_copy(o_hbm, 0, gbuf.at[0], 0, sem).wait()
        return 0
    lax.fori_loop(0, tf, drain, 0)

    route = route_ref[...]
    g1, g2 = route[:, 2:3], route[:, 3:4]
    a1, b1 = _unpack_bf16_pair(gbuf[0])
    a2, b2 = _unpack_bf16_pair(gbuf[1])
    h1 = h1_ref[...]
    ha = h1[:, 0:half] + (a1 * g1 + a2 * g2)
    hb = h1[:, half:2 * half] + (b1 * g1 + b2 * g2)
    ms = (jnp.sum(ha * ha, axis=-1, keepdims=True)
          + jnp.sum(hb * hb, axis=-1, keepdims=True)) / (2 * half)
    inv = lax.rsqrt(ms + EPS)
    y_ref[:, 0:half] = ha * inv * g_ref[:, 0:half]
    y_ref[:, half:2 * half] = hb * inv * g_ref[:, half:2 * half]


def _combine(dest2, o, h1, route, g, *, tf, row0, n_rows):
    d_model = h1.shape[1]
    half = d_model // 2
    tile0 = row0 // tf
    return pl.pallas_call(
        functools.partial(_combine_body, tf=tf, tile0=tile0, half=half),
        grid=(n_rows // tf,),
        in_specs=[pl.BlockSpec(memory_space=pl.ANY), pl.BlockSpec(memory_space=pl.ANY),
                  pl.BlockSpec((tf, d_model), lambda i: (tile0 + i, 0)),
                  pl.BlockSpec((tf, LANES), lambda i: (tile0 + i, 0)),
                  pl.BlockSpec((1, d_model), lambda i: (0, 0))],
        out_specs=pl.BlockSpec((tf, d_model), lambda i: (i, 0)),
        out_shape=jax.ShapeDtypeStruct((n_rows, d_model), F32),
        scratch_shapes=[pltpu.SMEM((1, 2 * tf), I32), pltpu.VMEM((2, tf, half), U32),
                        pltpu.SemaphoreType.DMA, pltpu.SemaphoreType.DMA],
        compiler_params=_cparams(1),
        name="combine",
    )(dest2, o, h1, route, g)


def kernel(x_prompt, x_sample, meta, norm_mix_g, w_in, ssm_lam_re, ssm_lam_im, ssm_log_step, ssm_b_re, ssm_b_im, ssm_c_re, ssm_c_im, ssm_d, ssm_w_glu, conv_w_dw, conv_b_dw, conv_ln_g, conv_ln_b, conv_w_pw, w_out, norm_ffn_g, router_group_w, router_group_b, router_expert_w, router_expert_b, expert_w1, expert_w3, expert_w2, final_g):
    assert w_in.shape[0] == 1, "single-layer trunk"
    bp, lp, d_model = x_prompt.shape
    bs, ls, _ = x_sample.shape
    n_meta = meta.shape[0]
    d_ssm = ssm_d.shape[-1]
    d_conv = conv_b_dw.shape[-1]
    n_ch = ssm_b_re.shape[-1]
    n_g = ssm_b_re.shape[2]
    n_grp = router_group_w.shape[-1]
    n_exp = router_expert_w.shape[-1]
    assert n_meta == CHUNK and lp % CHUNK == 0 and ls % CHUNK == 0
    n_p, n_s = bp * lp, bs * ls
    t = n_p + n_s
    tm = min(256, lp, ls)
    assert lp % tm == 0 and ls % tm == 0

    xp2 = x_prompt.reshape(n_p, d_model)
    xs2 = x_sample.reshape(n_s, d_model)
    row = lambda a: a.reshape(1, -1)
    w_in_bf = w_in[0].astype(BF16)

    us, c, sga, sgb = _inproj(xp2, xs2, row(norm_mix_g[0]), w_in_bf, d_ssm=d_ssm, d_conv=d_conv, tm=tm)
    us_meta, c_meta = _meta_inproj(meta, row(norm_mix_g[0]), w_in_bf, d_ssm=d_ssm, d_conv=d_conv)

    w1, cst, aq, init = _ssm_prep(ssm_lam_re[0], ssm_lam_im[0], ssm_log_step[0], ssm_b_re[0],
                                  ssm_b_im[0], ssm_c_re[0], ssm_c_im[0], ssm_d[0], us_meta)
    nc = t // CHUNK
    width = CHUNK * n_ch
    xt = us.reshape(nc, CHUNK, n_g, n_ch).transpose(2, 0, 1, 3).reshape(n_g, nc, width)
    gb = min(4, n_g)
    yi, s = _ssm_mm1(xt, w1, gb=gb)
    sw = w1.shape[-1] - width
    hw = sw // 4
    lane_even = (jnp.arange(sw) % hw) < hw // 2
    aq_pair = jnp.where(lane_even, aq[0::2, 0], aq[1::2, 0])
    init_pair = init[0::2, 0] + init[1::2, 0]
    cb = min(128, lp // CHUNK, ls // CHUNK)
    xf, xb = _ssm_scan(s.reshape(nc, n_g // 2, sw), aq_pair, init_pair, cb=cb,
                       chunks_pseq=lp // CHUNK, chunks_sseq=ls // CHUNK, n_pchunks=n_p // CHUNK)
    zt = _ssm_mm2(yi, xf.reshape(nc, n_g // 2 * 2 * hw), xb.reshape(nc, n_g // 2 * 2 * hw), cst, gb=gb)
    z = zt.reshape(n_g, nc, CHUNK, n_ch).transpose(1, 2, 0, 3).reshape(t, d_ssm)

    rw = jnp.zeros((d_model, LANES), F32).at[:, 0:n_grp].set(router_group_w[0]).at[
        :, n_grp:n_grp + n_exp].set(router_expert_w[0])
    rb = jnp.zeros((1, LANES), F32).at[0, 0:n_grp].set(router_group_b[0]).at[
        0, n_grp:n_grp + n_exp].set(router_expert_b[0])
    h1, v, route, cnt = _mix(
        xp2, xs2, z, c, c_meta, sga, sgb, ssm_w_glu[0].astype(BF16), conv_w_pw[0].astype(BF16),
        w_out[0].astype(BF16), conv_w_dw[0], row(conv_b_dw[0]), row(conv_ln_g[0]),
        row(conv_ln_b[0]), row(norm_ffn_g[0]), rw, rb,
        tm=tm, p_seq=lp, s_seq=ls, n_grp=n_grp, n_exp=n_exp)

    tme = 256
    counts = cnt[0, 0:n_exp].astype(I32)
    tiles_e = (counts + tme - 1) // tme
    tile_end = jnp.cumsum(tiles_e)
    n_used = tile_end[-1]
    offs = jnp.zeros((1, LANES), F32).at[0, 0:n_exp].set(((tile_end - tiles_e) * tme).astype(F32))
    n_tiles = (2 * t) // tme + n_exp
    ids = jnp.arange(n_tiles, dtype=I32)
    te_map = jnp.searchsorted(tile_end, jnp.minimum(ids, n_used - 1), side="right").astype(I32)
    te_map = jnp.minimum(te_map, n_exp - 1)

    td = min(512, t)
    dest = _positions(route, offs, te=td)
    dest2 = dest[:, 0:2].reshape(t // td, 2 * td)
    xs0 = jnp.zeros((n_tiles * tme, d_model // 2), U32)
    xs = _dispatch(dest2, v, xs0, td=td)
    o = _experts(te_map, n_used.reshape(1), xs, expert_w1[0].astype(BF16),
                 expert_w3[0].astype(BF16), expert_w2[0].astype(BF16), tme=tme)

    tf = min(256, n_p, n_s)
    dest2f = dest[:, 0:2].reshape(t // tf, 2 * tf)
    fg = row(final_g)
    y_p = _combine(dest2f, o, h1, route, fg, tf=tf, row0=0, n_rows=n_p)
    y_s = _combine(dest2f, o, h1, route, fg, tf=tf, row0=n_p, n_rows=n_s)
    return (y_p.reshape(bp, lp, d_model), y_s.reshape(bs, ls, d_model))
```

```python
import functools

import jax
import jax.numpy as jnp
from jax import lax
from jax.experimental import pallas as pl
from jax.experimental.pallas import tpu as pltpu

F32 = jnp.float32
BF16 = jnp.bfloat16
U32 = jnp.uint32
I32 = jnp.int32

EPS = 1e-6
LAM_RE_MAX = -1e-4
CHUNK = 16
LANES = 128
VMEM_LIMIT = 56 << 20
HIGHEST = lax.Precision.HIGHEST


def _cparams(n_axes):
    return pltpu.CompilerParams(dimension_semantics=("arbitrary",) * n_axes,
                                vmem_limit_bytes=VMEM_LIMIT)


def _resident(shape):
    nd = len(shape)
    return pl.BlockSpec(shape, lambda *_: (0,) * nd, pipeline_mode=pl.Buffered(1))


def _sigmoid(x):
    return 1.0 / (1.0 + jnp.exp(-x))


def _rms(x, g):
    return x * lax.rsqrt(jnp.mean(x * x, axis=-1, keepdims=True) + EPS) * g


def _pack_bf16_pair(a, b):
    def rnd(x):
        u = pltpu.bitcast(x, U32)
        return (u + jnp.uint32(0x7FFF) + ((u >> 16) & jnp.uint32(1))) >> 16
    return (rnd(a) << 16) | rnd(b)


def _unpack_bf16_pair(p):
    a = pltpu.bitcast(p & jnp.uint32(0xFFFF0000), F32)
    b = pltpu.bitcast(p << 16, F32)
    return a, b


def _inproj_body(xp_ref, xs_ref, g_ref, w_ref, us_ref, c_ref, sga_ref, sgb_ref, *,
                 n_p_tiles, d_ssm, d_conv, d_model, col):
    i = pl.program_id(0)
    x = jnp.where(i < n_p_tiles, xp_ref[...], xs_ref[...])
    y = _rms(x, g_ref[...]).astype(BF16)

    def proj(lo, n):
        return jnp.dot(y, w_ref[:, lo:lo + n], preferred_element_type=F32)

    for k in range(d_ssm // col):
        us_ref[:, k * col:(k + 1) * col] = proj(k * col, col).astype(BF16)
    for k in range(d_conv // col):
        cv = proj(d_ssm + k * col, col)
        cg = proj(d_ssm + d_conv + k * col, col)
        c_ref[:, k * col:(k + 1) * col] = (cv * _sigmoid(cg)).astype(BF16)
    base = d_ssm + 2 * d_conv
    for k in range(d_model // col):
        sga_ref[:, k * col:(k + 1) * col] = _sigmoid(proj(base + k * col, col)).astype(BF16)
        sgb_ref[:, k * col:(k + 1) * col] = _sigmoid(
            proj(base + d_model + k * col, col)).astype(BF16)


def _inproj(xp2, xs2, g, w_bf, *, d_ssm, d_conv, tm):
    n_p, d_model = xp2.shape
    n_s = xs2.shape[0]
    n_p_tiles, n_s_tiles = n_p // tm, n_s // tm
    t = n_p + n_s
    col = min(1024, d_ssm, d_conv, d_model)
    body = functools.partial(_inproj_body, n_p_tiles=n_p_tiles, d_ssm=d_ssm, d_conv=d_conv,
                             d_model=d_model, col=col)
    row = lambda i: (i, 0)
    return pl.pallas_call(
        body,
        grid=(n_p_tiles + n_s_tiles,),
        in_specs=[
            pl.BlockSpec((tm, d_model), lambda i: (jnp.minimum(i, n_p_tiles - 1), 0)),
            pl.BlockSpec((tm, d_model), lambda i: (jnp.maximum(i - n_p_tiles, 0), 0)),
            _resident((1, d_model)),
            _resident(w_bf.shape),
        ],
        out_specs=[pl.BlockSpec((tm, d_ssm), row), pl.BlockSpec((tm, d_conv), row),
                   pl.BlockSpec((tm, d_model), row), pl.BlockSpec((tm, d_model), row)],
        out_shape=[jax.ShapeDtypeStruct((t, d_ssm), BF16), jax.ShapeDtypeStruct((t, d_conv), BF16),
                   jax.ShapeDtypeStruct((t, d_model), BF16), jax.ShapeDtypeStruct((t, d_model), BF16)],
        compiler_params=_cparams(1),
        name="inproj",
    )(xp2, xs2, g, w_bf)


def _meta_body(m_ref, g_ref, w_ref, us_ref, c_ref, *, d_ssm, d_conv):
    y = _rms(m_ref[...], g_ref[...]).astype(BF16)
    us_ref[...] = jnp.dot(y, w_ref[:, 0:d_ssm], preferred_element_type=F32).astype(BF16)
    cv = jnp.dot(y, w_ref[:, d_ssm:d_ssm + d_conv], preferred_element_type=F32)
    cg = jnp.dot(y, w_ref[:, d_ssm + d_conv:d_ssm + 2 * d_conv], preferred_element_type=F32)
    c_ref[...] = (cv * _sigmoid(cg)).astype(BF16)


def _meta_inproj(meta, g, w_bf, *, d_ssm, d_conv):
    n_meta, d_model = meta.shape
    ncol = d_ssm + 2 * d_conv
    return pl.pallas_call(
        functools.partial(_meta_body, d_ssm=d_ssm, d_conv=d_conv),
        grid=(1,),
        in_specs=[pl.BlockSpec((n_meta, d_model), lambda i: (0, 0)),
                  pl.BlockSpec((1, d_model), lambda i: (0, 0)),
                  pl.BlockSpec((d_model, ncol), lambda i: (0, 0))],
        out_specs=[pl.BlockSpec((n_meta, d_ssm), lambda i: (0, 0)),
                   pl.BlockSpec((n_meta, d_conv), lambda i: (0, 0))],
        out_shape=[jax.ShapeDtypeStruct((n_meta, d_ssm), BF16),
                   jax.ShapeDtypeStruct((n_meta, d_conv), BF16)],
        compiler_params=_cparams(1),
        name="meta_inproj",
    )(meta, g, w_bf)


def _cmul(ar, ai, br, bi):
    return ar * br - ai * bi, ar * bi + ai * br


def _discretize(lam_re, lam_im, log_step):
    lr = jnp.minimum(lam_re, LAM_RE_MAX)
    dt = jnp.exp(log_step)
    mag = jnp.exp(lr * dt)
    ar = mag * jnp.cos(lam_im * dt)
    ai = mag * jnp.sin(lam_im * dt)
    den = lr * lr + lam_im * lam_im
    nr = ar - 1.0
    fr = (nr * lr + ai * lam_im) / den
    fi = (ai * lr - nr * lam_im) / den
    return ar, ai, fr, fi


def _cpow(ar, ai, k, nbits, shape):
    pr = jnp.ones(shape, F32)
    pi = jnp.zeros(shape, F32)
    br = jnp.broadcast_to(ar, shape)
    bi = jnp.broadcast_to(ai, shape)
    kk = jnp.broadcast_to(k, shape)
    for b in range(nbits):
        sel = ((kk >> b) & 1) == 1
        nr, ni = _cmul(pr, pi, br, bi)
        pr = jnp.where(sel, nr, pr)
        pi = jnp.where(sel, ni, pi)
        br, bi = _cmul(br, bi, br, bi)
    return pr, pi


def _ssm_prep_body(lam_re_c, lam_im_c, ls_c, ct_re, ct_im, bt_re, bt_im,
                   lam_re_r, lam_im_r, ls_r, btr_re, btr_im, dpad, xmeta,
                   w1_ref, cst_ref, aq_ref, init_ref, *, n_state, n_ch):
    q = CHUNK
    width = q * n_ch
    par = pl.program_id(0) % 2
    kblk = lax.broadcasted_iota(I32, (1, width), 1) // n_ch
    strips = []
    for d in range(2):
        ar, ai, fr, fi = _discretize(lam_re_c[d], lam_im_c[d], ls_c[d])
        kexp = kblk if d == 0 else (q - 1) - kblk
        wr, wi = _cpow(ar, ai, kexp, 4, (n_state, width))
        gcr, gci = _cmul(ct_re[d], ct_im[d], wr, wi)
        gfr, gfi = _cmul(gcr, gci, fr, fi)
        strips.append(jnp.dot(bt_re[d], gfr, precision=HIGHEST, preferred_element_type=F32)
                      - jnp.dot(bt_im[d], gfi, precision=HIGHEST, preferred_element_type=F32))
        g1r, g1i = _cmul(gcr, gci, ar, ai)
        rowmask = (lax.broadcasted_iota(I32, (2 * n_state, width), 0) // n_state) == par
        rep_r = jnp.concatenate([g1r, g1r], axis=0)
        rep_i = jnp.concatenate([g1i, g1i], axis=0)
        r0 = 4 * n_state * d
        cst_ref[r0:r0 + 2 * n_state, :] = jnp.where(rowmask, rep_r, 0.0).astype(BF16)
        cst_ref[r0 + 2 * n_state:r0 + 4 * n_state, :] = jnp.where(rowmask, -rep_i, 0.0).astype(BF16)

    zf, zb = strips
    zero = jnp.zeros((n_ch, width), F32)
    z512 = jnp.concatenate([zb, zero], axis=1) + pltpu.roll(
        jnp.concatenate([zf, zero], axis=1), (q - 1) * n_ch, 1)
    row = lax.broadcasted_iota(I32, (n_ch, 2 * width), 0)
    lane = lax.broadcasted_iota(I32, (n_ch, 2 * width), 1)
    z512 = z512 + jnp.where(lane - (q - 1) * n_ch == row, dpad[...], 0.0)
    for s in range(q):
        sh = (q - 1 - s) * n_ch
        blk = z512 if sh == 0 else pltpu.roll(z512, 2 * width - sh, 1)
        w1_ref[n_ch * s:n_ch * (s + 1), 0:width] = blk[:, 0:width].astype(BF16)

    parmask = (lax.broadcasted_iota(I32, (1, 2 * n_state), 1) // n_state) == par
    for d in range(2):
        ar, ai, fr, fi = _discretize(lam_re_r[d], lam_im_r[d], ls_r[d])
        pw = [(jnp.ones_like(ar), jnp.zeros_like(ar))]
        for _ in range(q):
            pw.append(_cmul(pw[-1][0], pw[-1][1], ar, ai))
        col = width + 4 * n_state * d
        for s in range(q):
            e = (q - 1 - s) if d == 0 else s
            cr, ci = _cmul(fr, fi, pw[e][0], pw[e][1])
            br, bi = _cmul(btr_re[d], btr_im[d], cr, ci)
            w1_ref[n_ch * s:n_ch * (s + 1), col:col + 2 * n_state] = (
                jnp.where(parmask, br, 0.0).astype(BF16))
            w1_ref[n_ch * s:n_ch * (s + 1), col + 2 * n_state:col + 4 * n_state] = (
                jnp.where(parmask, bi, 0.0).astype(BF16))
        aq_ref[:, col - width:col - width + 2 * n_state] = pw[q][0]
        aq_ref[:, col - width + 2 * n_state:col - width + 4 * n_state] = pw[q][1]

    init_ref[...] = jnp.dot(xmeta[...].astype(BF16), w1_ref[:, width:width + 4 * n_state],
                            preferred_element_type=F32)


def _ssm_prep(lam_re, lam_im, log_step, b_re, b_im, c_re, c_im, d_skip, us_meta):
    _, n_g, n_state, n_ch = b_re.shape
    q = CHUNK
    width = q * n_ch
    dup = lambda x: jnp.concatenate([x, x], axis=-1)
    lam_re_c, lam_im_c = lam_re[..., None], lam_im[..., None]
    ls_c = log_step[..., None, None]
    ct_re = jnp.tile(jnp.swapaxes(c_re, -1, -2), (1, 1, 1, q))
    ct_im = jnp.tile(jnp.swapaxes(c_im, -1, -2), (1, 1, 1, q))
    bt_re, bt_im = jnp.swapaxes(b_re, -1, -2), jnp.swapaxes(b_im, -1, -2)
    lam_re_r, lam_im_r = dup(lam_re)[:, :, None, :], dup(lam_im)[:, :, None, :]
    ls_r = jnp.broadcast_to(log_step[..., None, None], (2, n_g, 1, 2 * n_state))
    btr_re, btr_im = dup(bt_re), dup(bt_im)
    dpad = jnp.pad(d_skip.reshape(n_g, 1, n_ch), ((0, 0), (0, 0), ((q - 1) * n_ch, width)))
    xm = us_meta.astype(F32).reshape(q, n_g, n_ch).transpose(1, 0, 2).reshape(n_g, 1, width)
    xm = jnp.pad(xm, ((0, 0), (0, 7), (0, 0)))

    def dspec(shape):
        return pl.BlockSpec((2, None) + shape, lambda g: (0, g, 0, 0))

    def gspec(shape):
        return pl.BlockSpec((None,) + shape, lambda g: (g, 0, 0))

    body = functools.partial(_ssm_prep_body, n_state=n_state, n_ch=n_ch)
    return pl.pallas_call(
        body,
        grid=(n_g,),
        in_specs=[dspec((n_state, 1)), dspec((n_state, 1)), dspec((1, 1)),
                  dspec((n_state, width)), dspec((n_state, width)),
                  dspec((n_ch, n_state)), dspec((n_ch, n_state)),
                  dspec((1, 2 * n_state)), dspec((1, 2 * n_state)), dspec((1, 2 * n_state)),
                  dspec((n_ch, 2 * n_state)), dspec((n_ch, 2 * n_state)),
                  gspec((1, 2 * width)), gspec((8, width))],
        out_specs=[gspec((width, width + 8 * n_state)), gspec((8 * n_state, width)),
                   gspec((1, 8 * n_state)), gspec((8, 4 * n_state))],
        out_shape=[jax.ShapeDtypeStruct((n_g, width, width + 8 * n_state), BF16),
                   jax.ShapeDtypeStruct((n_g, 8 * n_state, width), BF16),
                   jax.ShapeDtypeStruct((n_g, 1, 8 * n_state), F32),
                   jax.ShapeDtypeStruct((n_g, 8, 4 * n_state), F32)],
        compiler_params=_cparams(1),
        name="ssm_prep",
    )(lam_re_c, lam_im_c, ls_c, ct_re, ct_im, bt_re, bt_im,
      lam_re_r, lam_im_r, ls_r, btr_re, btr_im, dpad, xm)


def _ssm_mm1_body(x_ref, w1_ref, yi_ref, s_ref, *, gb, width, sw):
    for j in range(0, gb, 2):
        r0 = jnp.dot(x_ref[j], w1_ref[j], preferred_element_type=F32)
        r1 = jnp.dot(x_ref[j + 1], w1_ref[j + 1], preferred_element_type=F32)
        yi_ref[j] = r0[:, 0:width]
        yi_ref[j + 1] = r1[:, 0:width]
        s_ref[:, (j // 2) * sw:(j // 2 + 1) * sw] = r0[:, width:] + r1[:, width:]


def _ssm_mm1(xt, w1, *, gb):
    n_g, nc, width = xt.shape
    sw = w1.shape[-1] - width
    body = functools.partial(_ssm_mm1_body, gb=gb, width=width, sw=sw)
    return pl.pallas_call(
        body,
        grid=(n_g // gb,),
        in_specs=[pl.BlockSpec((gb, nc, width), lambda i: (i, 0, 0)),
                  pl.BlockSpec((gb, width, width + sw), lambda i: (i, 0, 0))],
        out_specs=[pl.BlockSpec((gb, nc, width), lambda i: (i, 0, 0)),
                   pl.BlockSpec((nc, gb // 2 * sw), lambda i: (0, i))],
        out_shape=[jax.ShapeDtypeStruct((n_g, nc, width), F32),
                   jax.ShapeDtypeStruct((nc, n_g // 2 * sw), F32)],
        compiler_params=_cparams(1),
        name="ssm_mm1",
    )(xt, w1)


def _ssm_scan_body(sf_ref, sb_ref, aq_ref, init_ref, xf_ref, xb_ref, stf, stb, *,
                   cb, hw, n_pblk, blk_pseq, blk_sseq):
    j = pl.program_id(1)
    pos = jnp.where(j < n_pblk, lax.rem(j, blk_pseq), lax.rem(jnp.maximum(j - n_pblk, 0), blk_sseq))

    @pl.when(pos == 0)
    def _():
        stf[...] = init_ref[...]
        stb[...] = jnp.zeros_like(stb)

    afr, afi = aq_ref[:, 0:hw], aq_ref[:, hw:2 * hw]
    abr, abi = aq_ref[:, 2 * hw:3 * hw], aq_ref[:, 3 * hw:4 * hw]

    def body(i, carry):
        fr, fi, br, bi = carry
        ib = cb - 1 - i
        xf_ref[i, :, 0:hw] = fr
        xf_ref[i, :, hw:2 * hw] = fi
        xb_ref[ib, :, 0:hw] = br
        xb_ref[ib, :, hw:2 * hw] = bi
        nfr = afr * fr - afi * fi + sf_ref[i, :, 0:hw]
        nfi = afr * fi + afi * fr + sf_ref[i, :, hw:2 * hw]
        nbr = abr * br - abi * bi + sb_ref[ib, :, 0:hw]
        nbi = abr * bi + abi * br + sb_ref[ib, :, hw:2 * hw]
        return nfr, nfi, nbr, nbi

    fr, fi, br, bi = lax.fori_loop(
        0, cb, body, (stf[:, 0:hw], stf[:, hw:2 * hw], stb[:, 0:hw], stb[:, hw:2 * hw]))
    stf[:, 0:hw] = fr
    stf[:, hw:2 * hw] = fi
    stb[:, 0:hw] = br
    stb[:, hw:2 * hw] = bi


def _ssm_scan(s3, aq, init, *, cb, chunks_pseq, chunks_sseq, n_pchunks):
    nc, n_pair, sw = s3.shape
    hw = sw // 4
    pb = min(8, n_pair)
    n_pblk, blk_pseq, blk_sseq = n_pchunks // cb, chunks_pseq // cb, chunks_sseq // cb

    def bwd_block(j):
        in_p = j < n_pblk
        pos = jnp.where(in_p, lax.rem(j, blk_pseq), lax.rem(jnp.maximum(j - n_pblk, 0), blk_sseq))
        ln = jnp.where(in_p, blk_pseq, blk_sseq)
        return j - pos + ln - 1 - pos

    body = functools.partial(_ssm_scan_body, cb=cb, hw=hw, n_pblk=n_pblk, blk_pseq=blk_pseq,
                             blk_sseq=blk_sseq)
    return pl.pallas_call(
        body,
        grid=(n_pair // pb, nc // cb),
        in_specs=[pl.BlockSpec((cb, pb, 2 * hw), lambda p, j: (j, p, 0)),
                  pl.BlockSpec((cb, pb, 2 * hw), lambda p, j: (bwd_block(j), p, 1)),
                  pl.BlockSpec((pb, sw), lambda p, j: (p, 0)),
                  pl.BlockSpec((pb, 2 * hw), lambda p, j: (p, 0))],
        out_specs=[pl.BlockSpec((cb, pb, 2 * hw), lambda p, j: (j, p, 0)),
                   pl.BlockSpec((cb, pb, 2 * hw), lambda p, j: (bwd_block(j), p, 0))],
        out_shape=[jax.ShapeDtypeStruct((nc, n_pair, 2 * hw), F32),
                   jax.ShapeDtypeStruct((nc, n_pair, 2 * hw), F32)],
        scratch_shapes=[pltpu.VMEM((pb, 2 * hw), F32), pltpu.VMEM((pb, 2 * hw), F32)],
        compiler_params=_cparams(2),
        name="ssm_scan",
    )(s3, s3, aq, init)


def _gelu_tanh(x):
    return 0.5 * x * (1.0 + jnp.tanh(0.7978845608028654 * (x + 0.044715 * (x * x * x))))


def _ssm_mm2_body(yi_ref, xf_ref, xb_ref, cst_ref, z_ref, *, gb, hs):
    for j in range(gb):
        cols = slice((j // 2) * hs, (j // 2 + 1) * hs)
        y = (yi_ref[j]
             + jnp.dot(xf_ref[:, cols].astype(BF16), cst_ref[j, 0:hs, :], preferred_element_type=F32)
             + jnp.dot(xb_ref[:, cols].astype(BF16), cst_ref[j, hs:2 * hs, :],
                       preferred_element_type=F32))
        z_ref[j] = _gelu_tanh(y).astype(BF16)


def _ssm_mm2(yi, xf2, xb2, cst, *, gb):
    n_g, nc, width = yi.shape
    hs = cst.shape[1] // 2
    body = functools.partial(_ssm_mm2_body, gb=gb, hs=hs)
    return pl.pallas_call(
        body,
        grid=(n_g // gb,),
        in_specs=[pl.BlockSpec((gb, nc, width), lambda i: (i, 0, 0)),
                  pl.BlockSpec((nc, gb // 2 * hs), lambda i: (0, i)),
                  pl.BlockSpec((nc, gb // 2 * hs), lambda i: (0, i)),
                  pl.BlockSpec((gb, 2 * hs, width), lambda i: (i, 0, 0))],
        out_specs=pl.BlockSpec((gb, nc, width), lambda i: (i, 0, 0)),
        out_shape=jax.ShapeDtypeStruct((n_g, nc, width), BF16),
        compiler_params=_cparams(1),
        name="ssm_mm2",
    )(yi, xf2, xb2, cst)


def _mix_body(xp_ref, xs_ref, z_ref, c_ref, cprev_ref, cnext_ref, cmeta_ref, sga_ref, sgb_ref,
              wglu_ref, wpw_ref, wout_ref, wdw_ref, bdw_ref, lng_ref, lnb_ref, gffn_ref,
              rw_ref, rb_ref,
              h1_ref, v_ref, route_ref, cnt_ref,
              cw_ref, conv_ref, *,
              tm, n_p_tiles, tiles_per_pseq, tiles_per_sseq, d_model, d_conv, conv_w,
              n_grp, n_exp, exp_per_grp, rc):
    i = pl.program_id(0)
    in_prompt = i < n_p_tiles
    x = jnp.where(in_prompt, xp_ref[...], xs_ref[...])
    pos_p = lax.rem(i, tiles_per_pseq)
    pos_s = lax.rem(jnp.maximum(i - n_p_tiles, 0), tiles_per_sseq)
    is_start = jnp.where(in_prompt, pos_p == 0, pos_s == 0)
    is_end = jnp.where(in_prompt, pos_p == tiles_per_pseq - 1, pos_s == tiles_per_sseq - 1)

    halo = CHUNK
    pad = conv_w // 2
    cw_ref[0:halo, :] = jnp.where(is_start, cmeta_ref[...], cprev_ref[...]).astype(F32)
    cw_ref[halo:halo + tm, :] = c_ref[...].astype(F32)
    cw_ref[halo + tm:2 * halo + tm, :] = jnp.where(is_end, 0.0, cnext_ref[...].astype(F32))
    sub = 8
    for lc in range(d_conv // LANES):
        ls = slice(lc * LANES, (lc + 1) * LANES)
        for r0 in range(0, tm, rc):
            out = None
            for r in range(sub):
                part = None
                for q in range((conv_w + halo - pad) // sub + 1):
                    k = sub * q + r - (halo - pad)
                    if 0 <= k < conv_w:
                        term = wdw_ref[k:k + 1, ls] * cw_ref[r0 + sub * q:r0 + sub * q + rc + sub, ls]
                        part = term if part is None else part + term
                if part is not None:
                    shifted = part[r:r + rc]
                    out = shifted if out is None else out + shifted
            conv_ref[r0:r0 + rc, ls] = out
    cc = conv_ref[...] + bdw_ref[...]
    mu = jnp.mean(cc, axis=-1, keepdims=True)
    var = jnp.mean(jnp.square(cc - mu), axis=-1, keepdims=True)
    cc = (cc - mu) * lax.rsqrt(var + EPS) * lng_ref[...] + lnb_ref[...]
    cc = (cc * _sigmoid(cc)).astype(BF16)
    y_b = jnp.dot(cc, wpw_ref[...], preferred_element_type=F32)

    z = z_ref[...]
    va = jnp.dot(z, wglu_ref[:, 0:d_model], preferred_element_type=F32)
    ga = jnp.dot(z, wglu_ref[:, d_model:2 * d_model], preferred_element_type=F32)
    y_a = va * _sigmoid(ga)
    merged = (sga_ref[...].astype(F32) * y_a + sgb_ref[...].astype(F32) * y_b).astype(BF16)
    h1 = x + jnp.dot(merged, wout_ref[...], preferred_element_type=F32)
    h1_ref[...] = h1
    v = _rms(h1, gffn_ref[...])
    half = d_model // 2
    v_ref[...] = _pack_bf16_pair(v[:, 0:half], v[:, half:d_model])

    v_hi = v.astype(BF16)
    v_lo = (v - v_hi.astype(F32)).astype(BF16)
    acc = (jnp.dot(v_hi, rw_ref[...], preferred_element_type=F32)
           + jnp.dot(v_lo, rw_ref[...], preferred_element_type=F32))
    logits = acc + pltpu.roll(acc, LANES // 2, 1) + rb_ref[...]
    lane = lax.broadcasted_iota(I32, (tm, LANES), 1).astype(F32)
    big = jnp.float32(1e9)
    neg = jnp.float32(-jnp.inf)
    gmask = lane < n_grp
    lg = jnp.where(gmask, logits, neg)
    gmax = jnp.max(lg, axis=-1, keepdims=True)
    grp = jnp.min(jnp.where(lg == gmax, lane, big), axis=-1, keepdims=True)
    p_grp = 1.0 / jnp.sum(jnp.where(gmask, jnp.exp(logits - gmax), 0.0), axis=-1, keepdims=True)
    lo = n_grp + grp * exp_per_grp
    emask = (lane >= lo) & (lane < lo + exp_per_grp)
    le = jnp.where(emask, logits, neg)
    m1 = jnp.max(le, axis=-1, keepdims=True)
    i1 = jnp.min(jnp.where(le == m1, lane, big), axis=-1, keepdims=True)
    le2 = jnp.where(lane == i1, neg, le)
    m2 = jnp.max(le2, axis=-1, keepdims=True)
    i2 = jnp.min(jnp.where(le2 == m2, lane, big), axis=-1, keepdims=True)
    t = jnp.exp(m2 - m1)
    w1 = 1.0 / (1.0 + t)
    e1 = i1 - n_grp
    e2 = i2 - n_grp
    route_ref[...] = jnp.where(lane == 0, e1, jnp.where(lane == 1, e2, jnp.where(
        lane == 2, p_grp * w1, jnp.where(lane == 3, p_grp * (t * w1), 0.0))))

    @pl.when(i == 0)
    def _():
        cnt_ref[...] = jnp.zeros_like(cnt_ref)

    hot = jnp.where((lane == e1) | (lane == e2), 1.0, 0.0)
    cnt_ref[...] += jnp.sum(hot, axis=0, keepdims=True)


def _mix(xp2, xs2, z, c, c_meta, sga, sgb, wglu, wpw, wout, wdw, bdw, lng, lnb, gffn, rw, rb, *,
         tm, p_seq, s_seq, n_grp, n_exp):
    n_p, d_model = xp2.shape
    n_s = xs2.shape[0]
    t = n_p + n_s
    d_ssm, d_conv = z.shape[1], c.shape[1]
    conv_w = wdw.shape[0]
    n_p_tiles, n_s_tiles = n_p // tm, n_s // tm
    hpt = tm // CHUNK
    n_hblk = t // CHUNK
    body = functools.partial(
        _mix_body, tm=tm, n_p_tiles=n_p_tiles, tiles_per_pseq=p_seq // tm,
        tiles_per_sseq=s_seq // tm, d_model=d_model, d_conv=d_conv, conv_w=conv_w,
        n_grp=n_grp, n_exp=n_exp, exp_per_grp=n_exp // n_grp, rc=min(128, tm))
    row = lambda i: (i, 0)
    return pl.pallas_call(
        body,
        grid=(n_p_tiles + n_s_tiles,),
        in_specs=[
            pl.BlockSpec((tm, d_model), lambda i: (jnp.minimum(i, n_p_tiles - 1), 0)),
            pl.BlockSpec((tm, d_model), lambda i: (jnp.maximum(i - n_p_tiles, 0), 0)),
            pl.BlockSpec((tm, d_ssm), row),
            pl.BlockSpec((tm, d_conv), row),
            pl.BlockSpec((CHUNK, d_conv), lambda i: (jnp.maximum(i * hpt - 1, 0), 0)),
            pl.BlockSpec((CHUNK, d_conv), lambda i: (jnp.minimum((i + 1) * hpt, n_hblk - 1), 0)),
            _resident(c_meta.shape),
            pl.BlockSpec((tm, d_model), row),
            pl.BlockSpec((tm, d_model), row),
            _resident(wglu.shape), _resident(wpw.shape), _resident(wout.shape),
            _resident(wdw.shape), _resident(bdw.shape), _resident(lng.shape), _resident(lnb.shape),
            _resident(gffn.shape), _resident(rw.shape), _resident(rb.shape),
        ],
        out_specs=[pl.BlockSpec((tm, d_model), row), pl.BlockSpec((tm, d_model // 2), row),
                   pl.BlockSpec((tm, LANES), row), pl.BlockSpec((1, LANES), lambda i: (0, 0))],
        out_shape=[jax.ShapeDtypeStruct((t, d_model), F32),
                   jax.ShapeDtypeStruct((t, d_model // 2), U32),
                   jax.ShapeDtypeStruct((t, LANES), F32),
                   jax.ShapeDtypeStruct((1, LANES), F32)],
        scratch_shapes=[pltpu.VMEM((tm + 2 * CHUNK, d_conv), F32), pltpu.VMEM((tm, d_conv), F32)],
        compiler_params=_cparams(1),
        name="mix",
    )(xp2, xs2, z, c, c, c, c_meta, sga, sgb, wglu, wpw, wout, wdw, bdw, lng, lnb, gffn, rw, rb)


def _positions_body(route_ref, offs_ref, dest_ref, carry_ref, *, te):
    @pl.when(pl.program_id(0) == 0)
    def _():
        carry_ref[...] = jnp.zeros_like(carry_ref)

    lane = lax.broadcasted_iota(I32, (te, LANES), 1).astype(F32)
    r = route_ref[...]
    oh1 = lane == r[:, 0:1]
    oh2 = lane == r[:, 1:2]
    both = jnp.where(oh1 | oh2, 1.0, 0.0)
    tri = jnp.where(lax.broadcasted_iota(I32, (te, te), 0) > lax.broadcasted_iota(I32, (te, te), 1),
                    1.0, 0.0).astype(BF16)
    before = jnp.dot(tri, both.astype(BF16), preferred_element_type=F32)
    base = before + carry_ref[...] + offs_ref[...]
    d1 = jnp.sum(jnp.where(oh1, base, 0.0), axis=-1, keepdims=True)
    d2 = jnp.sum(jnp.where(oh2, base, 0.0), axis=-1, keepdims=True)
    dest_ref[...] = jnp.where(lane == 0, d1, jnp.where(lane == 1, d2, 0.0)).astype(I32)
    carry_ref[...] += jnp.sum(both, axis=0, keepdims=True)


def _positions(route, offs, *, te):
    t = route.shape[0]
    return pl.pallas_call(
        functools.partial(_positions_body, te=te),
        grid=(t // te,),
        in_specs=[pl.BlockSpec((te, LANES), lambda i: (i, 0)),
                  pl.BlockSpec((1, LANES), lambda i: (0, 0))],
        out_specs=pl.BlockSpec((te, LANES), lambda i: (i, 0)),
        out_shape=jax.ShapeDtypeStruct((t, LANES), I32),
        scratch_shapes=[pltpu.VMEM((1, LANES), F32)],
        compiler_params=_cparams(1),
        name="positions",
    )(route, offs)


def _row_copy(src_ref, src_row, dst_ref, dst_row, sem):
    return pltpu.make_async_copy(src_ref.at[pl.ds(src_row, 1), :],
                                 dst_ref.at[pl.ds(dst_row, 1), :], sem)


def _dispatch_body(dest_hbm, v_ref, xs_in, xs_hbm, dsm, sem, dsem, *, td):
    del xs_in
    i = pl.program_id(0)
    cp = pltpu.make_async_copy(dest_hbm.at[pl.ds(i, 1), :], dsm, dsem)
    cp.start()
    cp.wait()

    def issue(r, _):
        _row_copy(v_ref, r, xs_hbm, dsm[0, 2 * r], sem).start()
        _row_copy(v_ref, r, xs_hbm, dsm[0, 2 * r + 1], sem).start()
        return 0
    lax.fori_loop(0, td, issue, 0)

    def drain(r, _):
        _row_copy(v_ref, 0, xs_hbm, 0, sem).wait()
        _row_copy(v_ref, 0, xs_hbm, 0, sem).wait()
        return 0
    lax.fori_loop(0, td, drain, 0)


def _dispatch(dest2, v, xs0, *, td):
    t, w = v.shape
    return pl.pallas_call(
        functools.partial(_dispatch_body, td=td),
        grid=(t // td,),
        in_specs=[pl.BlockSpec(memory_space=pl.ANY),
                  pl.BlockSpec((td, w), lambda i: (i, 0)),
                  pl.BlockSpec(memory_space=pl.ANY)],
        out_specs=pl.BlockSpec(memory_space=pl.ANY),
        out_shape=jax.ShapeDtypeStruct(xs0.shape, U32),
        scratch_shapes=[pltpu.SMEM((1, 2 * td), I32), pltpu.SemaphoreType.DMA,
                        pltpu.SemaphoreType.DMA],
        input_output_aliases={2: 0},
        compiler_params=_cparams(1),
        name="dispatch",
    )(dest2, v, xs0)


def _experts_body(te_ref, nused_ref, xs_ref, w1_ref, w3_ref, w2_ref, o_ref, *, half):
    del te_ref
    i = pl.program_id(0)

    @pl.when(i < nused_ref[0])
    def _():
        a, b = _unpack_bf16_pair(xs_ref[...])
        a, b = a.astype(BF16), b.astype(BF16)

        def up(w_ref):
            return (jnp.dot(a, w_ref[0:half, :], preferred_element_type=F32)
                    + jnp.dot(b, w_ref[half:2 * half, :], preferred_element_type=F32))
        h1 = up(w1_ref)
        act = (h1 * _sigmoid(h1) * up(w3_ref)).astype(BF16)
        o = jnp.dot(act, w2_ref[...], preferred_element_type=F32)
        o_ref[...] = _pack_bf16_pair(o[:, 0:half], o[:, half:2 * half])

    @pl.when(i >= nused_ref[0])
    def _():
        o_ref[...] = jnp.zeros_like(o_ref)


def _experts(tile_expert, n_used, xs, w1, w3, w2, *, tme):
    rows, half = xs.shape
    n_e, d_model, d_exp = w1.shape
    grid_spec = pltpu.PrefetchScalarGridSpec(
        num_scalar_prefetch=2,
        grid=(rows // tme,),
        in_specs=[pl.BlockSpec((tme, half), lambda i, te, nu: (i, 0)),
                  pl.BlockSpec((None, d_model, d_exp), lambda i, te, nu: (te[i], 0, 0)),
                  pl.BlockSpec((None, d_model, d_exp), lambda i, te, nu: (te[i], 0, 0)),
                  pl.BlockSpec((None, d_exp, d_model), lambda i, te, nu: (te[i], 0, 0))],
        out_specs=pl.BlockSpec((tme, half), lambda i, te, nu: (i, 0)),
    )
    return pl.pallas_call(
        functools.partial(_experts_body, half=half),
        grid_spec=grid_spec,
        out_shape=jax.ShapeDtypeStruct((rows, half), U32),
        compiler_params=_cparams(1),
        name="experts",
    )(tile_expert, n_used, xs, w1, w3, w2)


def _combine_body(dest_hbm, o_hbm, h1_ref, route_ref, g_ref, y_ref, dsm, gbuf, sem, dsem, *,
                  tf, tile0, half):
    i = pl.program_id(0)
    cp = pltpu.make_async_copy(dest_hbm.at[pl.ds(tile0 + i, 1), :], dsm, dsem)
    cp.start()
    cp.wait()

    def issue(r, _):
        _row_copy(o_hbm, dsm[0, 2 * r], gbuf.at[0], r, sem).start()
        _row_copy(o_hbm, dsm[0, 2 * r + 1], gbuf.at[1], r, sem).start()
        return 0
    lax.fori_loop(0, tf, issue, 0)

    def drain(r, _):
        _row_copy(o_hbm, 0, gbuf.at[0], 0, sem).wait()
        _row_copy(o_hbm, 0, gbuf.at[0], 0, sem).wait()
        return 0
    lax.fori_loop(0, tf, drain, 0)

    route = route_ref[...]
    g1, g2 = route[:, 2:3], route[:, 3:4]
    a1, b1 = _unpack_bf16_pair(gbuf[0])
    a2, b2 = _unpack_bf16_pair(gbuf[1])
    h1 = h1_ref[...]
    ha = h1[:, 0:half] + (a1 * g1 + a2 * g2)
    hb = h1[:, half:2 * half] + (b1 * g1 + b2 * g2)
    ms = (jnp.sum(ha * ha, axis=-1, keepdims=True)
          + jnp.sum(hb * hb, axis=-1, keepdims=True)) / (2 * half)
    inv = lax.rsqrt(ms + EPS)
    y_ref[:, 0:half] = ha * inv * g_ref[:, 0:half]
    y_ref[:, half:2 * half] = hb * inv * g_ref[:, half:2 * half]


def _combine(dest2, o, h1, route, g, *, tf, row0, n_rows):
    d_model = h1.shape[1]
    half = d_model // 2
    tile0 = row0 // tf
    return pl.pallas_call(
        functools.partial(_combine_body, tf=tf, tile0=tile0, half=half),
        grid=(n_rows // tf,),
        in_specs=[pl.BlockSpec(memory_space=pl.ANY), pl.BlockSpec(memory_space=pl.ANY),
                  pl.BlockSpec((tf, d_model), lambda i: (tile0 + i, 0)),
                  pl.BlockSpec((tf, LANES), lambda i: (tile0 + i, 0)),
                  pl.BlockSpec((1, d_model), lambda i: (0, 0))],
        out_specs=pl.BlockSpec((tf, d_model), lambda i: (i, 0)),
        out_shape=jax.ShapeDtypeStruct((n_rows, d_model), F32),
        scratch_shapes=[pltpu.SMEM((1, 2 * tf), I32), pltpu.VMEM((2, tf, half), U32),
                        pltpu.SemaphoreType.DMA, pltpu.SemaphoreType.DMA],
        compiler_params=_cparams(1),
        name="combine",
    )(dest2, o, h1, route, g)


def kernel(x_prompt, x_sample, meta, norm_mix_g, w_in, ssm_lam_re, ssm_lam_im, ssm_log_step, ssm_b_re, ssm_b_im, ssm_c_re, ssm_c_im, ssm_d, ssm_w_glu, conv_w_dw, conv_b_dw, conv_ln_g, conv_ln_b, conv_w_pw, w_out, norm_ffn_g, router_group_w, router_group_b, router_expert_w, router_expert_b, expert_w1, expert_w3, expert_w2, final_g):
    assert w_in.shape[0] == 1, "single-layer trunk"
    bp, lp, d_model = x_prompt.shape
    bs, ls, _ = x_sample.shape
    n_meta = meta.shape[0]
    d_ssm = ssm_d.shape[-1]
    d_conv = conv_b_dw.shape[-1]
    n_ch = ssm_b_re.shape[-1]
    n_g = ssm_b_re.shape[2]
    n_grp = router_group_w.shape[-1]
    n_exp = router_expert_w.shape[-1]
    assert n_meta == CHUNK and lp % CHUNK == 0 and ls % CHUNK == 0
    n_p, n_s = bp * lp, bs * ls
    t = n_p + n_s
    tm = min(256, lp, ls)
    assert lp % tm == 0 and ls % tm == 0

    xp2 = x_prompt.reshape(n_p, d_model)
    xs2 = x_sample.reshape(n_s, d_model)
    row = lambda a: a.reshape(1, -1)
    w_in_bf = w_in[0].astype(BF16)

    us, c, sga, sgb = _inproj(xp2, xs2, row(norm_mix_g[0]), w_in_bf, d_ssm=d_ssm, d_conv=d_conv, tm=tm)
    us_meta, c_meta = _meta_inproj(meta, row(norm_mix_g[0]), w_in_bf, d_ssm=d_ssm, d_conv=d_conv)

    w1, cst, aq, init = _ssm_prep(ssm_lam_re[0], ssm_lam_im[0], ssm_log_step[0], ssm_b_re[0],
                                  ssm_b_im[0], ssm_c_re[0], ssm_c_im[0], ssm_d[0], us_meta)
    nc = t // CHUNK
    width = CHUNK * n_ch
    xt = us.reshape(nc, CHUNK, n_g, n_ch).transpose(2, 0, 1, 3).reshape(n_g, nc, width)
    gb = min(4, n_g)
    yi, s = _ssm_mm1(xt, w1, gb=gb)
    sw = w1.shape[-1] - width
    hw = sw // 4
    lane_even = (jnp.arange(sw) % hw) < hw // 2
    aq_pair = jnp.where(lane_even, aq[0::2, 0], aq[1::2, 0])
    init_pair = init[0::2, 0] + init[1::2, 0]
    cb = min(128, lp // CHUNK, ls // CHUNK)
    xf, xb = _ssm_scan(s.reshape(nc, n_g // 2, sw), aq_pair, init_pair, cb=cb,
                       chunks_pseq=lp // CHUNK, chunks_sseq=ls // CHUNK, n_pchunks=n_p // CHUNK)
    zt = _ssm_mm2(yi, xf.reshape(nc, n_g // 2 * 2 * hw), xb.reshape(nc, n_g // 2 * 2 * hw), cst, gb=gb)
    z = zt.reshape(n_g, nc, CHUNK, n_ch).transpose(1, 2, 0, 3).reshape(t, d_ssm)

    assert n_grp + n_exp <= LANES // 2
    rw32 = jnp.pad(jnp.concatenate([router_group_w[0], router_expert_w[0]], axis=1),
                   ((0, 0), (0, LANES // 2 - n_grp - n_exp)))
    rw_hi = rw32.astype(BF16)
    rw = jnp.concatenate([rw_hi, (rw32 - rw_hi.astype(F32)).astype(BF16)], axis=1)
    rb = jnp.zeros((1, LANES), F32).at[0, 0:n_grp].set(router_group_b[0]).at[
        0, n_grp:n_grp + n_exp].set(router_expert_b[0])
    h1, v, route, cnt = _mix(
        xp2, xs2, z, c, c_meta, sga, sgb, ssm_w_glu[0].astype(BF16), conv_w_pw[0].astype(BF16),
        w_out[0].astype(BF16), conv_w_dw[0], row(conv_b_dw[0]), row(conv_ln_g[0]),
        row(conv_ln_b[0]), row(norm_ffn_g[0]), rw, rb,
        tm=tm, p_seq=lp, s_seq=ls, n_grp=n_grp, n_exp=n_exp)

    tme = 256
    counts = cnt[0, 0:n_exp].astype(I32)
    tiles_e = (counts + tme - 1) // tme
    tile_end = jnp.cumsum(tiles_e)
    n_used = tile_end[-1]
    offs = jnp.zeros((1, LANES), F32).at[0, 0:n_exp].set(((tile_end - tiles_e) * tme).astype(F32))
    n_tiles = (2 * t) // tme + n_exp
    ids = jnp.arange(n_tiles, dtype=I32)
    te_map = jnp.sum((jnp.minimum(ids, n_used - 1)[:, None] >= tile_end[None, :]).astype(I32), axis=1)
    te_map = jnp.minimum(te_map, n_exp - 1)

    td = min(512, t)
    dest = _positions(route, offs, te=td)
    dest2 = dest[:, 0:2].reshape(t // td, 2 * td)
    xs0 = jnp.zeros((n_tiles * tme, d_model // 2), U32)
    xs = _dispatch(dest2, v, xs0, td=td)
    o = _experts(te_map, n_used.reshape(1), xs, expert_w1[0].astype(BF16),
                 expert_w3[0].astype(BF16), expert_w2[0].astype(BF16), tme=tme)

    tf = min(256, n_p, n_s)
    dest2f = dest[:, 0:2].reshape(t // tf, 2 * tf)
    fg = row(final_g)
    y_p = _combine(dest2f, o, h1, route, fg, tf=tf, row0=0, n_rows=n_p)
    y_s = _combine(dest2f, o, h1, route, fg, tf=tf, row0=n_p, n_rows=n_s)
    return (y_p.reshape(bp, lp, d_model), y_s.reshape(bs, ls, d_model))
```

```python
import functools

import jax
import jax.numpy as jnp
from jax import lax
from jax.experimental import pallas as pl
from jax.experimental.pallas import tpu as pltpu

F32 = jnp.float32
BF16 = jnp.bfloat16
U32 = jnp.uint32
I32 = jnp.int32

EPS = 1e-6
LAM_RE_MAX = -1e-4
CHUNK = 16
LANES = 128
VMEM_LIMIT = 56 << 20
HIGHEST = lax.Precision.HIGHEST


def _cparams(n_axes):
    return pltpu.CompilerParams(dimension_semantics=("arbitrary",) * n_axes,
                                vmem_limit_bytes=VMEM_LIMIT)


def _resident(shape):
    nd = len(shape)
    return pl.BlockSpec(shape, lambda *_: (0,) * nd, pipeline_mode=pl.Buffered(1))


def _sigmoid(x):
    return 1.0 / (1.0 + jnp.exp(-x))


def _rms(x, g):
    return x * lax.rsqrt(jnp.mean(x * x, axis=-1, keepdims=True) + EPS) * g


def _pack_bf16_pair(a, b):
    def rnd(x):
        u = pltpu.bitcast(x, U32)
        return (u + jnp.uint32(0x7FFF) + ((u >> 16) & jnp.uint32(1))) >> 16
    return (rnd(a) << 16) | rnd(b)


def _unpack_bf16_pair(p):
    a = pltpu.bitcast(p & jnp.uint32(0xFFFF0000), F32)
    b = pltpu.bitcast(p << 16, F32)
    return a, b


def _inproj_body(xp_ref, xs_ref, g_ref, w_ref, xg_ref, c_ref, sga_ref, sgb_ref, usc, *,
                 tm, n_p_tiles, d_ssm, d_conv, d_model, col):
    i = pl.program_id(0)
    x = jnp.where(i < n_p_tiles, xp_ref[...], xs_ref[...])
    y = _rms(x, g_ref[...]).astype(BF16)

    def proj(lo, n):
        return jnp.dot(y, w_ref[:, lo:lo + n], preferred_element_type=F32)

    for k in range(d_ssm // col):
        u = proj(k * col, col)
        for b in range(col // LANES):
            usc[k * (col // LANES) + b] = u[:, b * LANES:(b + 1) * LANES]
    for b in range(d_ssm // LANES):
        for s in range(CHUNK):
            c0 = (b * CHUNK + s) * LANES
            xg_ref[:, c0:c0 + LANES] = usc[b, pl.ds(s, tm // CHUNK, stride=CHUNK), :].astype(BF16)
    for k in range(d_conv // col):
        cv = proj(d_ssm + k * col, col)
        cg = proj(d_ssm + d_conv + k * col, col)
        c_ref[:, k * col:(k + 1) * col] = (cv * _sigmoid(cg)).astype(BF16)
    base = d_ssm + 2 * d_conv
    for k in range(d_model // col):
        sga_ref[:, k * col:(k + 1) * col] = _sigmoid(proj(base + k * col, col)).astype(BF16)
        sgb_ref[:, k * col:(k + 1) * col] = _sigmoid(
            proj(base + d_model + k * col, col)).astype(BF16)


def _inproj(xp2, xs2, g, w_bf, *, d_ssm, d_conv, tm):
    n_p, d_model = xp2.shape
    n_s = xs2.shape[0]
    n_p_tiles, n_s_tiles = n_p // tm, n_s // tm
    t = n_p + n_s
    col = min(1024, d_ssm, d_conv, d_model)
    body = functools.partial(_inproj_body, tm=tm, n_p_tiles=n_p_tiles, d_ssm=d_ssm, d_conv=d_conv,
                             d_model=d_model, col=col)
    row = lambda i: (i, 0)
    return pl.pallas_call(
        body,
        grid=(n_p_tiles + n_s_tiles,),
        in_specs=[
            pl.BlockSpec((tm, d_model), lambda i: (jnp.minimum(i, n_p_tiles - 1), 0)),
            pl.BlockSpec((tm, d_model), lambda i: (jnp.maximum(i - n_p_tiles, 0), 0)),
            _resident((1, d_model)),
            _resident(w_bf.shape),
        ],
        out_specs=[pl.BlockSpec((tm // CHUNK, CHUNK * d_ssm), row), pl.BlockSpec((tm, d_conv), row),
                   pl.BlockSpec((tm, d_model), row), pl.BlockSpec((tm, d_model), row)],
        out_shape=[jax.ShapeDtypeStruct((t // CHUNK, CHUNK * d_ssm), BF16),
                   jax.ShapeDtypeStruct((t, d_conv), BF16),
                   jax.ShapeDtypeStruct((t, d_model), BF16), jax.ShapeDtypeStruct((t, d_model), BF16)],
        scratch_shapes=[pltpu.VMEM((d_ssm // LANES, tm, LANES), F32)],
        compiler_params=_cparams(1),
        name="inproj",
    )(xp2, xs2, g, w_bf)


def _meta_body(m_ref, g_ref, w_ref, xg_ref, c_ref, *, d_ssm, d_conv):
    y = _rms(m_ref[...], g_ref[...]).astype(BF16)
    u = jnp.dot(y, w_ref[:, 0:d_ssm], preferred_element_type=F32)
    first = lax.broadcasted_iota(I32, (CHUNK, LANES), 0) == 0
    for b in range(d_ssm // LANES):
        for s in range(CHUNK):
            c0 = (b * CHUNK + s) * LANES
            piece = jnp.broadcast_to(u[s:s + 1, b * LANES:(b + 1) * LANES], (CHUNK, LANES))
            xg_ref[:, c0:c0 + LANES] = jnp.where(first, piece, 0.0).astype(BF16)
    cv = jnp.dot(y, w_ref[:, d_ssm:d_ssm + d_conv], preferred_element_type=F32)
    cg = jnp.dot(y, w_ref[:, d_ssm + d_conv:d_ssm + 2 * d_conv], preferred_element_type=F32)
    c_ref[...] = (cv * _sigmoid(cg)).astype(BF16)


def _meta_inproj(meta, g, w_bf, *, d_ssm, d_conv):
    n_meta, d_model = meta.shape
    ncol = d_ssm + 2 * d_conv
    return pl.pallas_call(
        functools.partial(_meta_body, d_ssm=d_ssm, d_conv=d_conv),
        grid=(1,),
        in_specs=[pl.BlockSpec((n_meta, d_model), lambda i: (0, 0)),
                  pl.BlockSpec((1, d_model), lambda i: (0, 0)),
                  pl.BlockSpec((d_model, ncol), lambda i: (0, 0))],
        out_specs=[pl.BlockSpec((CHUNK, CHUNK * d_ssm), lambda i: (0, 0)),
                   pl.BlockSpec((n_meta, d_conv), lambda i: (0, 0))],
        out_shape=[jax.ShapeDtypeStruct((CHUNK, CHUNK * d_ssm), BF16),
                   jax.ShapeDtypeStruct((n_meta, d_conv), BF16)],
        compiler_params=_cparams(1),
        name="meta_inproj",
    )(meta, g, w_bf)


def _cmul(ar, ai, br, bi):
    return ar * br - ai * bi, ar * bi + ai * br


def _discretize(lam_re, lam_im, log_step):
    lr = jnp.minimum(lam_re, LAM_RE_MAX)
    dt = jnp.exp(log_step)
    mag = jnp.exp(lr * dt)
    ar = mag * jnp.cos(lam_im * dt)
    ai = mag * jnp.sin(lam_im * dt)
    den = lr * lr + lam_im * lam_im
    nr = ar - 1.0
    fr = (nr * lr + ai * lam_im) / den
    fi = (ai * lr - nr * lam_im) / den
    return ar, ai, fr, fi


def _cpow(ar, ai, k, nbits, shape):
    pr = jnp.ones(shape, F32)
    pi = jnp.zeros(shape, F32)
    br = jnp.broadcast_to(ar, shape)
    bi = jnp.broadcast_to(ai, shape)
    kk = jnp.broadcast_to(k, shape)
    for b in range(nbits):
        sel = ((kk >> b) & 1) == 1
        nr, ni = _cmul(pr, pi, br, bi)
        pr = jnp.where(sel, nr, pr)
        pi = jnp.where(sel, ni, pi)
        br, bi = _cmul(br, bi, br, bi)
    return pr, pi


def _ssm_prep_body(lam_re_c, lam_im_c, ls_c, ct_re, ct_im, bt_re, bt_im,
                   lam_re_r, lam_im_r, ls_r, btr_re, btr_im, dpad, xmeta,
                   w1_ref, cst_ref, aq_ref, init_ref, *, n_state, n_ch):
    q = CHUNK
    width = q * n_ch
    par = pl.program_id(0) % 2
    kblk = lax.broadcasted_iota(I32, (1, width), 1) // n_ch
    strips = []
    for d in range(2):
        ar, ai, fr, fi = _discretize(lam_re_c[d], lam_im_c[d], ls_c[d])
        kexp = kblk if d == 0 else (q - 1) - kblk
        wr, wi = _cpow(ar, ai, kexp, 4, (n_state, width))
        gcr, gci = _cmul(ct_re[d], ct_im[d], wr, wi)
        gfr, gfi = _cmul(gcr, gci, fr, fi)
        strips.append(jnp.dot(bt_re[d], gfr, precision=HIGHEST, preferred_element_type=F32)
                      - jnp.dot(bt_im[d], gfi, precision=HIGHEST, preferred_element_type=F32))
        g1r, g1i = _cmul(gcr, gci, ar, ai)
        rowmask = (lax.broadcasted_iota(I32, (2 * n_state, width), 0) // n_state) == par
        rep_r = jnp.concatenate([g1r, g1r], axis=0)
        rep_i = jnp.concatenate([g1i, g1i], axis=0)
        r0 = 4 * n_state * d
        cst_ref[r0:r0 + 2 * n_state, :] = jnp.where(rowmask, rep_r, 0.0).astype(BF16)
        cst_ref[r0 + 2 * n_state:r0 + 4 * n_state, :] = jnp.where(rowmask, -rep_i, 0.0).astype(BF16)

    zf, zb = strips
    zero = jnp.zeros((n_ch, width), F32)
    z512 = jnp.concatenate([zb, zero], axis=1) + pltpu.roll(
        jnp.concatenate([zf, zero], axis=1), (q - 1) * n_ch, 1)
    row = lax.broadcasted_iota(I32, (n_ch, 2 * width), 0)
    lane = lax.broadcasted_iota(I32, (n_ch, 2 * width), 1)
    z512 = z512 + jnp.where(lane - (q - 1) * n_ch == row, dpad[...], 0.0)
    for s in range(q):
        sh = (q - 1 - s) * n_ch
        blk = z512 if sh == 0 else pltpu.roll(z512, 2 * width - sh, 1)
        w1_ref[n_ch * s:n_ch * (s + 1), 0:width] = blk[:, 0:width].astype(BF16)

    parmask = (lax.broadcasted_iota(I32, (1, 2 * n_state), 1) // n_state) == par
    for d in range(2):
        ar, ai, fr, fi = _discretize(lam_re_r[d], lam_im_r[d], ls_r[d])
        pw = [(jnp.ones_like(ar), jnp.zeros_like(ar))]
        for _ in range(q):
            pw.append(_cmul(pw[-1][0], pw[-1][1], ar, ai))
        col = width + 4 * n_state * d
        for s in range(q):
            e = (q - 1 - s) if d == 0 else s
            cr, ci = _cmul(fr, fi, pw[e][0], pw[e][1])
            br, bi = _cmul(btr_re[d], btr_im[d], cr, ci)
            w1_ref[n_ch * s:n_ch * (s + 1), col:col + 2 * n_state] = (
                jnp.where(parmask, br, 0.0).astype(BF16))
            w1_ref[n_ch * s:n_ch * (s + 1), col + 2 * n_state:col + 4 * n_state] = (
                jnp.where(parmask, bi, 0.0).astype(BF16))
        aq_ref[:, col - width:col - width + 2 * n_state] = pw[q][0]
        aq_ref[:, col - width + 2 * n_state:col - width + 4 * n_state] = pw[q][1]

    init_ref[...] = jnp.dot(xmeta[...].astype(BF16), w1_ref[:, width:width + 4 * n_state],
                            preferred_element_type=F32)


def _ssm_prep(lam_re, lam_im, log_step, b_re, b_im, c_re, c_im, d_skip, us_meta):
    _, n_g, n_state, n_ch = b_re.shape
    q = CHUNK
    width = q * n_ch
    dup = lambda x: jnp.concatenate([x, x], axis=-1)
    lam_re_c, lam_im_c = lam_re[..., None], lam_im[..., None]
    ls_c = log_step[..., None, None]
    ct_re = jnp.tile(jnp.swapaxes(c_re, -1, -2), (1, 1, 1, q))
    ct_im = jnp.tile(jnp.swapaxes(c_im, -1, -2), (1, 1, 1, q))
    bt_re, bt_im = jnp.swapaxes(b_re, -1, -2), jnp.swapaxes(b_im, -1, -2)
    lam_re_r, lam_im_r = dup(lam_re)[:, :, None, :], dup(lam_im)[:, :, None, :]
    ls_r = jnp.broadcast_to(log_step[..., None, None], (2, n_g, 1, 2 * n_state))
    btr_re, btr_im = dup(bt_re), dup(bt_im)
    dpad = jnp.pad(d_skip.reshape(n_g, 1, n_ch), ((0, 0), (0, 0), ((q - 1) * n_ch, width)))
    xm = us_meta.astype(F32).reshape(q, n_g, n_ch).transpose(1, 0, 2).reshape(n_g, 1, width)
    xm = jnp.pad(xm, ((0, 0), (0, 7), (0, 0)))

    def dspec(shape):
        return pl.BlockSpec((2, None) + shape, lambda g: (0, g, 0, 0))

    def gspec(shape):
        return pl.BlockSpec((None,) + shape, lambda g: (g, 0, 0))

    body = functools.partial(_ssm_prep_body, n_state=n_state, n_ch=n_ch)
    return pl.pallas_call(
        body,
        grid=(n_g,),
        in_specs=[dspec((n_state, 1)), dspec((n_state, 1)), dspec((1, 1)),
                  dspec((n_state, width)), dspec((n_state, width)),
                  dspec((n_ch, n_state)), dspec((n_ch, n_state)),
                  dspec((1, 2 * n_state)), dspec((1, 2 * n_state)), dspec((1, 2 * n_state)),
                  dspec((n_ch, 2 * n_state)), dspec((n_ch, 2 * n_state)),
                  gspec((1, 2 * width)), gspec((8, width))],
        out_specs=[gspec((width, width + 8 * n_state)), gspec((8 * n_state, width)),
                   gspec((1, 8 * n_state)), gspec((8, 4 * n_state))],
        out_shape=[jax.ShapeDtypeStruct((n_g, width, width + 8 * n_state), BF16),
                   jax.ShapeDtypeStruct((n_g, 8 * n_state, width), BF16),
                   jax.ShapeDtypeStruct((n_g, 1, 8 * n_state), F32),
                   jax.ShapeDtypeStruct((n_g, 8, 4 * n_state), F32)],
        compiler_params=_cparams(1),
        name="ssm_prep",
    )(lam_re_c, lam_im_c, ls_c, ct_re, ct_im, bt_re, bt_im,
      lam_re_r, lam_im_r, ls_r, btr_re, btr_im, dpad, xm)


def _ssm_mm1_body(x_ref, w1_ref, yi_ref, s_ref, *, gb, width, sw):
    for j in range(0, gb, 2):
        r0 = jnp.dot(x_ref[j], w1_ref[j], preferred_element_type=F32)
        r1 = jnp.dot(x_ref[j + 1], w1_ref[j + 1], preferred_element_type=F32)
        yi_ref[j] = r0[:, 0:width]
        yi_ref[j + 1] = r1[:, 0:width]
        s_ref[:, (j // 2) * sw:(j // 2 + 1) * sw] = r0[:, width:] + r1[:, width:]


def _ssm_mm1(xt, w1, *, gb):
    n_g, nc, width = xt.shape
    sw = w1.shape[-1] - width
    body = functools.partial(_ssm_mm1_body, gb=gb, width=width, sw=sw)
    return pl.pallas_call(
        body,
        grid=(n_g // gb,),
        in_specs=[pl.BlockSpec((gb, nc, width), lambda i: (i, 0, 0)),
                  pl.BlockSpec((gb, width, width + sw), lambda i: (i, 0, 0))],
        out_specs=[pl.BlockSpec((gb, nc, width), lambda i: (i, 0, 0)),
                   pl.BlockSpec((nc, gb // 2 * sw), lambda i: (0, i))],
        out_shape=[jax.ShapeDtypeStruct((n_g, nc, width), F32),
                   jax.ShapeDtypeStruct((nc, n_g // 2 * sw), F32)],
        compiler_params=_cparams(1),
        name="ssm_mm1",
    )(xt, w1)


def _ssm_scan_body(sf_ref, sb_ref, aq_ref, init_ref, xf_ref, xb_ref, stf, stb, *,
                   cb, hw, n_pblk, blk_pseq, blk_sseq):
    j = pl.program_id(1)
    pos = jnp.where(j < n_pblk, lax.rem(j, blk_pseq), lax.rem(jnp.maximum(j - n_pblk, 0), blk_sseq))

    @pl.when(pos == 0)
    def _():
        stf[...] = init_ref[...]
        stb[...] = jnp.zeros_like(stb)

    afr, afi = aq_ref[:, 0:hw], aq_ref[:, hw:2 * hw]
    abr, abi = aq_ref[:, 2 * hw:3 * hw], aq_ref[:, 3 * hw:4 * hw]

    def body(i, carry):
        fr, fi, br, bi = carry
        ib = cb - 1 - i
        xf_ref[i, :, 0:hw] = fr
        xf_ref[i, :, hw:2 * hw] = fi
        xb_ref[ib, :, 0:hw] = br
        xb_ref[ib, :, hw:2 * hw] = bi
        nfr = afr * fr - afi * fi + sf_ref[i, :, 0:hw]
        nfi = afr * fi + afi * fr + sf_ref[i, :, hw:2 * hw]
        nbr = abr * br - abi * bi + sb_ref[ib, :, 0:hw]
        nbi = abr * bi + abi * br + sb_ref[ib, :, hw:2 * hw]
        return nfr, nfi, nbr, nbi

    fr, fi, br, bi = lax.fori_loop(
        0, cb, body, (stf[:, 0:hw], stf[:, hw:2 * hw], stb[:, 0:hw], stb[:, hw:2 * hw]))
    stf[:, 0:hw] = fr
    stf[:, hw:2 * hw] = fi
    stb[:, 0:hw] = br
    stb[:, hw:2 * hw] = bi


def _ssm_scan(s3, aq, init, *, cb, chunks_pseq, chunks_sseq, n_pchunks):
    nc, n_pair, sw = s3.shape
    hw = sw // 4
    pb = min(8, n_pair)
    n_pblk, blk_pseq, blk_sseq = n_pchunks // cb, chunks_pseq // cb, chunks_sseq // cb

    def bwd_block(j):
        in_p = j < n_pblk
        pos = jnp.where(in_p, lax.rem(j, blk_pseq), lax.rem(jnp.maximum(j - n_pblk, 0), blk_sseq))
        ln = jnp.where(in_p, blk_pseq, blk_sseq)
        return j - pos + ln - 1 - pos

    body = functools.partial(_ssm_scan_body, cb=cb, hw=hw, n_pblk=n_pblk, blk_pseq=blk_pseq,
                             blk_sseq=blk_sseq)
    return pl.pallas_call(
        body,
        grid=(n_pair // pb, nc // cb),
        in_specs=[pl.BlockSpec((cb, pb, 2 * hw), lambda p, j: (j, p, 0)),
                  pl.BlockSpec((cb, pb, 2 * hw), lambda p, j: (bwd_block(j), p, 1)),
                  pl.BlockSpec((pb, sw), lambda p, j: (p, 0)),
                  pl.BlockSpec((pb, 2 * hw), lambda p, j: (p, 0))],
        out_specs=[pl.BlockSpec((cb, pb, 2 * hw), lambda p, j: (j, p, 0)),
                   pl.BlockSpec((cb, pb, 2 * hw), lambda p, j: (bwd_block(j), p, 0))],
        out_shape=[jax.ShapeDtypeStruct((nc, n_pair, 2 * hw), F32),
                   jax.ShapeDtypeStruct((nc, n_pair, 2 * hw), F32)],
        scratch_shapes=[pltpu.VMEM((pb, 2 * hw), F32), pltpu.VMEM((pb, 2 * hw), F32)],
        compiler_params=_cparams(2),
        name="ssm_scan",
    )(s3, s3, aq, init)


def _gelu_tanh(x):
    return 0.5 * x * (1.0 + jnp.tanh(0.7978845608028654 * (x + 0.044715 * (x * x * x))))


def _ssm_mm2_body(yi_ref, xf_ref, xb_ref, cst_ref, z_ref, *, gb, hs):
    for j in range(gb):
        cols = slice((j // 2) * hs, (j // 2 + 1) * hs)
        y = (yi_ref[j]
             + jnp.dot(xf_ref[:, cols].astype(BF16), cst_ref[j, 0:hs, :], preferred_element_type=F32)
             + jnp.dot(xb_ref[:, cols].astype(BF16), cst_ref[j, hs:2 * hs, :],
                       preferred_element_type=F32))
        z_ref[j] = _gelu_tanh(y).astype(BF16)


def _ssm_mm2(yi, xf2, xb2, cst, *, gb):
    n_g, nc, width = yi.shape
    hs = cst.shape[1] // 2
    body = functools.partial(_ssm_mm2_body, gb=gb, hs=hs)
    return pl.pallas_call(
        body,
        grid=(n_g // gb,),
        in_specs=[pl.BlockSpec((gb, nc, width), lambda i: (i, 0, 0)),
                  pl.BlockSpec((nc, gb // 2 * hs), lambda i: (0, i)),
                  pl.BlockSpec((nc, gb // 2 * hs), lambda i: (0, i)),
                  pl.BlockSpec((gb, 2 * hs, width), lambda i: (i, 0, 0))],
        out_specs=pl.BlockSpec((gb, nc, width), lambda i: (i, 0, 0)),
        out_shape=jax.ShapeDtypeStruct((n_g, nc, width), BF16),
        compiler_params=_cparams(1),
        name="ssm_mm2",
    )(yi, xf2, xb2, cst)


def _sg_prep_body(lam_re_c, lam_im_c, ls_c, ct_re, ct_im, bt_re, bt_im,
                  lam_re_r, lam_im_r, ls_r, btr_re, btr_im, dpad, exp_ref,
                  m_ref, b_ref, c_ref, aq_ref, *, n_state, n_ch):
    q = CHUNK
    width = q * n_ch
    per_blk = LANES // n_ch
    gl = pl.program_id(0) % per_blk
    shift = gl * n_ch
    kblk = lax.broadcasted_iota(I32, (1, width), 1) // n_ch
    strips = []
    for d in range(2):
        ar, ai, fr, fi = _discretize(lam_re_c[d], lam_im_c[d], ls_c[d])
        kexp = kblk if d == 0 else (q - 1) - kblk
        wr, wi = _cpow(ar, ai, kexp, 4, (n_state, width))
        gcr, gci = _cmul(ct_re[d], ct_im[d], wr, wi)
        gfr, gfi = _cmul(gcr, gci, fr, fi)
        strips.append(jnp.dot(bt_re[d], gfr, precision=HIGHEST, preferred_element_type=F32)
                      - jnp.dot(bt_im[d], gfi, precision=HIGHEST, preferred_element_type=F32))
        g1r, g1i = _cmul(gcr, gci, ar, ai)
        for comp, val in ((0, g1r), (1, -g1i)):
            wide = jnp.dot(val.astype(BF16), exp_ref[0:width, 0:q * LANES],
                           preferred_element_type=F32)
            c_ref[2 * d + comp] = pltpu.roll(wide, shift, 1).astype(BF16)

    zf, zb = strips
    zero = jnp.zeros((n_ch, width), F32)
    z512 = jnp.concatenate([zb, zero], axis=1) + pltpu.roll(
        jnp.concatenate([zf, zero], axis=1), (q - 1) * n_ch, 1)
    row = lax.broadcasted_iota(I32, (n_ch, 2 * width), 0)
    lane = lax.broadcasted_iota(I32, (n_ch, 2 * width), 1)
    z512 = z512 + jnp.where(lane - (q - 1) * n_ch == row, dpad[...], 0.0)
    zwide = pltpu.roll(jnp.dot(z512.astype(BF16), exp_ref[...], preferred_element_type=F32),
                       shift, 1)
    for s in range(q):
        lo = (q - 1 - s) * LANES
        m_ref[s] = zwide[:, lo:lo + q * LANES].astype(BF16)

    parmask = (lax.broadcasted_iota(I32, (1, 2 * n_state), 1) // n_state) == gl % 2
    pieces = [[None] * 4 for _ in range(q)]
    for d in range(2):
        ar, ai, fr, fi = _discretize(lam_re_r[d], lam_im_r[d], ls_r[d])
        pw = [(jnp.ones_like(ar), jnp.zeros_like(ar))]
        for _ in range(q):
            pw.append(_cmul(pw[-1][0], pw[-1][1], ar, ai))
        for s in range(q):
            e = (q - 1 - s) if d == 0 else s
            cr, ci = _cmul(fr, fi, pw[e][0], pw[e][1])
            br, bi = _cmul(btr_re[d], btr_im[d], cr, ci)
            pieces[s][2 * d] = jnp.where(parmask, br, 0.0)
            pieces[s][2 * d + 1] = jnp.where(parmask, bi, 0.0)
        aq_ref[:, 2 * d * n_state:(2 * d + 1) * n_state] = pw[q][0][:, 0:n_state]
        aq_ref[:, (2 * d + 1) * n_state:(2 * d + 2) * n_state] = pw[q][1][:, 0:n_state]
    n_pair = per_blk // 2
    for s in range(q):
        cols = []
        for seg in range(4):
            for blk in range(n_pair):
                cols.append(jnp.where(gl // 2 == blk, pieces[s][seg], 0.0))
        b_ref[s] = jnp.concatenate(cols, axis=1).astype(BF16)


def _sg_prep(lam_re, lam_im, log_step, b_re, b_im, c_re, c_im, d_skip):
    _, n_g, n_state, n_ch = b_re.shape
    q = CHUNK
    width = q * n_ch
    per_blk = LANES // n_ch
    n_blk = n_g // per_blk
    sw = 4 * per_blk * n_state
    dup = lambda x: jnp.concatenate([x, x], axis=-1)
    lam_re_c, lam_im_c = lam_re[..., None], lam_im[..., None]
    ls_c = log_step[..., None, None]
    ct_re = jnp.tile(jnp.swapaxes(c_re, -1, -2), (1, 1, 1, q))
    ct_im = jnp.tile(jnp.swapaxes(c_im, -1, -2), (1, 1, 1, q))
    bt_re, bt_im = jnp.swapaxes(b_re, -1, -2), jnp.swapaxes(b_im, -1, -2)
    lam_re_r, lam_im_r = dup(lam_re)[:, :, None, :], dup(lam_im)[:, :, None, :]
    ls_r = jnp.broadcast_to(log_step[..., None, None], (2, n_g, 1, 2 * n_state))
    btr_re, btr_im = dup(bt_re), dup(bt_im)
    dpad = jnp.pad(d_skip.reshape(n_g, 1, n_ch), ((0, 0), (0, 0), ((q - 1) * n_ch, width)))
    rr = jnp.arange(2 * width)[:, None]
    cc = jnp.arange(2 * q * LANES)[None, :]
    expand = ((cc // LANES == rr // n_ch) & (cc % LANES == rr % n_ch)).astype(BF16)

    def dspec(shape):
        return pl.BlockSpec((2, None) + shape, lambda g: (0, g, 0, 0))

    body = functools.partial(_sg_prep_body, n_state=n_state, n_ch=n_ch)
    m4, b4, c5, aq = pl.pallas_call(
        body,
        grid=(n_g,),
        in_specs=[dspec((n_state, 1)), dspec((n_state, 1)), dspec((1, 1)),
                  dspec((n_state, width)), dspec((n_state, width)),
                  dspec((n_ch, n_state)), dspec((n_ch, n_state)),
                  dspec((1, 2 * n_state)), dspec((1, 2 * n_state)), dspec((1, 2 * n_state)),
                  dspec((n_ch, 2 * n_state)), dspec((n_ch, 2 * n_state)),
                  pl.BlockSpec((None, 1, 2 * width), lambda g: (g, 0, 0)),
                  _resident(expand.shape)],
        out_specs=[
            pl.BlockSpec((None, q, n_ch, q * LANES), lambda g: (g // per_blk, 0, g % per_blk, 0)),
            pl.BlockSpec((None, q, n_ch, sw), lambda g: (g // per_blk, 0, g % per_blk, 0)),
            pl.BlockSpec((None, 4, None, n_state, q * LANES),
                         lambda g: (g // per_blk, 0, g % per_blk, 0, 0)),
            pl.BlockSpec((None, 1, 4 * n_state), lambda g: (g, 0, 0))],
        out_shape=[jax.ShapeDtypeStruct((n_blk, q, LANES, q * LANES), BF16),
                   jax.ShapeDtypeStruct((n_blk, q, LANES, sw), BF16),
                   jax.ShapeDtypeStruct((n_blk, 4, per_blk, n_state, q * LANES), BF16),
                   jax.ShapeDtypeStruct((n_g, 1, 4 * n_state), F32)],
        compiler_params=_cparams(1),
        name="ssm_prep",
    )(lam_re_c, lam_im_c, ls_c, ct_re, ct_im, bt_re, bt_im,
      lam_re_r, lam_im_r, ls_r, btr_re, btr_im, dpad, expand)
    intra = m4.reshape(n_blk, q * LANES, q * LANES)
    bst = b4.reshape(n_blk, q * LANES, sw)
    cst = c5.reshape(n_blk, sw, q * LANES)
    a = aq.reshape(n_blk, per_blk // 2, 2, 2, 2, n_state)
    a = a.transpose(3, 4, 0, 1, 2, 5).reshape(2, 2, n_blk, per_blk // 2, LANES)
    coef = []
    for d in range(2):
        re, im = a[d, 0], a[d, 1]
        coef.append((jnp.concatenate([re, re], axis=1), jnp.concatenate([-im, im], axis=1)))
    return intra, bst, cst, coef


def _sg_mm_intra_body(x_ref, m_ref, y_ref):
    y_ref[...] = jnp.dot(x_ref[...], m_ref[...], preferred_element_type=F32).astype(BF16)


def _sg_mm_intra(xg, intra, *, rt):
    nc = xg.shape[0]
    n_blk, kw, _ = intra.shape
    return pl.pallas_call(
        _sg_mm_intra_body,
        grid=(n_blk, nc // rt),
        in_specs=[pl.BlockSpec((rt, kw), lambda b, i: (i, b)),
                  pl.BlockSpec((None, kw, kw), lambda b, i: (b, 0, 0))],
        out_specs=pl.BlockSpec((rt, kw), lambda b, i: (i, b)),
        out_shape=jax.ShapeDtypeStruct((nc, n_blk * kw), BF16),
        compiler_params=_cparams(2),
        name="ssm_intra",
    )(xg, intra)


def _sg_mm_state_body(x_ref, xm_ref, b_ref, sf_ref, sb_ref, init_ref, *, rt, n_sub):
    r = jnp.dot(x_ref[...], b_ref[...], preferred_element_type=F32)
    for k in range(n_sub):
        sf_ref[pl.ds(k, rt, stride=n_sub), :] = r[:, k * LANES:(k + 1) * LANES]
        sb_ref[pl.ds(k, rt, stride=n_sub), :] = r[:, (n_sub + k) * LANES:(n_sub + k + 1) * LANES]

    @pl.when(pl.program_id(1) == 0)
    def _():
        r0 = jnp.dot(xm_ref[...], b_ref[...], preferred_element_type=F32)
        for k in range(n_sub):
            init_ref[k:k + 1, :] = r0[0:1, k * LANES:(k + 1) * LANES]


def _sg_mm_state(xg, xg_meta, bst, *, rt):
    nc = xg.shape[0]
    n_blk, kw, sw = bst.shape
    n_sub = sw // (2 * LANES)
    n_rt = nc // rt
    body = functools.partial(_sg_mm_state_body, rt=rt, n_sub=n_sub)
    dense = jax.ShapeDtypeStruct((n_blk * nc * n_sub, LANES), F32)
    dspec = pl.BlockSpec((rt * n_sub, LANES), lambda b, i: (b * n_rt + i, 0))
    return pl.pallas_call(
        body,
        grid=(n_blk, n_rt),
        in_specs=[pl.BlockSpec((rt, kw), lambda b, i: (i, b)),
                  pl.BlockSpec((CHUNK, kw), lambda b, i: (0, b)),
                  pl.BlockSpec((None, kw, sw), lambda b, i: (b, 0, 0))],
        out_specs=[dspec, dspec, pl.BlockSpec((None, n_sub, LANES), lambda b, i: (b, 0, 0))],
        out_shape=[dense, dense, jax.ShapeDtypeStruct((n_blk, n_sub, LANES), F32)],
        compiler_params=_cparams(2),
        name="ssm_state",
    )(xg, xg_meta, bst)


def _sg_scan_body(sf_ref, sb_ref, a1f_ref, a2f_ref, a1b_ref, a2b_ref, init_ref, xf_ref, xb_ref,
                  stf, stb, *, cb, n_blk, half, n_pblk, blk_pseq, blk_sseq):
    j = pl.program_id(0)
    pos = jnp.where(j < n_pblk, lax.rem(j, blk_pseq), lax.rem(jnp.maximum(j - n_pblk, 0), blk_sseq))

    @pl.when(pos == 0)
    def _():
        stf[...] = init_ref[...]
        stb[...] = jnp.zeros_like(stb)

    def body(i, carry):
        ef, eb = carry
        ib = cb - 1 - i
        nf, nb = [], []
        for g in range(n_blk):
            xf_ref[g, i] = ef[g]
            xb_ref[g, ib] = eb[g]
            nf.append(a1f_ref[g] * ef[g] + a2f_ref[g] * pltpu.roll(ef[g], half, 0) + sf_ref[g, i])
            nb.append(a1b_ref[g] * eb[g] + a2b_ref[g] * pltpu.roll(eb[g], half, 0) + sb_ref[g, ib])
        return tuple(nf), tuple(nb)

    ef, eb = lax.fori_loop(0, cb, body, (tuple(stf[g] for g in range(n_blk)),
                                         tuple(stb[g] for g in range(n_blk))))
    for g in range(n_blk):
        stf[g] = ef[g]
        stb[g] = eb[g]


def _sg_scan(sf, sb, coef, init, *, cb, chunks_pseq, chunks_sseq, n_pchunks):
    n_blk, nc, n_sub, _ = sf.shape
    n_pblk, blk_pseq, blk_sseq = n_pchunks // cb, chunks_pseq // cb, chunks_sseq // cb

    def bwd_block(j):
        in_p = j < n_pblk
        pos = jnp.where(in_p, lax.rem(j, blk_pseq), lax.rem(jnp.maximum(j - n_pblk, 0), blk_sseq))
        ln = jnp.where(in_p, blk_pseq, blk_sseq)
        return j - pos + ln - 1 - pos

    body = functools.partial(_sg_scan_body, cb=cb, n_blk=n_blk, half=n_sub // 2, n_pblk=n_pblk,
                             blk_pseq=blk_pseq, blk_sseq=blk_sseq)
    fwd = pl.BlockSpec((n_blk, cb, n_sub, LANES), lambda j: (0, j, 0, 0))
    bwd = pl.BlockSpec((n_blk, cb, n_sub, LANES), lambda j: (0, bwd_block(j), 0, 0))
    small = pl.BlockSpec((n_blk, n_sub, LANES), lambda j: (0, 0, 0))
    return pl.pallas_call(
        body,
        grid=(nc // cb,),
        in_specs=[fwd, bwd, small, small, small, small, small],
        out_specs=[fwd, bwd],
        out_shape=[jax.ShapeDtypeStruct(sf.shape, F32), jax.ShapeDtypeStruct(sb.shape, F32)],
        scratch_shapes=[pltpu.VMEM((n_blk, n_sub, LANES), F32), pltpu.VMEM((n_blk, n_sub, LANES), F32)],
        compiler_params=_cparams(1),
        name="ssm_scan",
    )(sf, sb, coef[0][0], coef[0][1], coef[1][0], coef[1][1], init)


def _gelu_tanh(x):
    return 0.5 * x * (1.0 + jnp.tanh(0.7978845608028654 * (x + 0.044715 * (x * x * x))))


def _sg_mm_out_body(y_ref, xf_ref, xb_ref, c_ref, z_ref, *, rt, n_sub):
    cols = [xf_ref[pl.ds(k, rt, stride=n_sub), :] for k in range(n_sub)]
    cols += [xb_ref[pl.ds(k, rt, stride=n_sub), :] for k in range(n_sub)]
    state = jnp.concatenate(cols, axis=1).astype(BF16)
    y = y_ref[...].astype(F32) + jnp.dot(state, c_ref[...], preferred_element_type=F32)
    z_ref[...] = _gelu_tanh(y).astype(BF16)


def _sg_mm_out(yi, xf, xb, cst, *, rt):
    nc = yi.shape[0]
    n_blk, sw, kw = cst.shape
    n_sub = sw // (2 * LANES)
    n_rt = nc // rt
    dspec = pl.BlockSpec((rt * n_sub, LANES), lambda b, i: (b * n_rt + i, 0))
    return pl.pallas_call(
        functools.partial(_sg_mm_out_body, rt=rt, n_sub=n_sub),
        grid=(n_blk, n_rt),
        in_specs=[pl.BlockSpec((rt, kw), lambda b, i: (i, b)), dspec, dspec,
                  pl.BlockSpec((None, sw, kw), lambda b, i: (b, 0, 0))],
        out_specs=pl.BlockSpec((rt, kw), lambda b, i: (i, b)),
        out_shape=jax.ShapeDtypeStruct((nc, n_blk * kw), BF16),
        compiler_params=_cparams(2),
        name="ssm_out",
    )(yi, xf, xb, cst)


def _mix_body(xp_ref, xs_ref, z_ref, c_ref, cprev_ref, cnext_ref, cmeta_ref, sga_ref, sgb_ref,
              wglu_ref, wpw_ref, wout_ref, wdw_ref, bdw_ref, lng_ref, lnb_ref, gffn_ref,
              rw_ref, rb_ref,
              h1_ref, v_ref, route_ref, cnt_ref,
              cw_ref, conv_ref, zsc_ref, *,
              tm, n_p_tiles, tiles_per_pseq, tiles_per_sseq, d_model, d_ssm, d_conv, conv_w,
              n_grp, n_exp, exp_per_grp, rc):
    i = pl.program_id(0)
    in_prompt = i < n_p_tiles
    x = jnp.where(in_prompt, xp_ref[...], xs_ref[...])
    pos_p = lax.rem(i, tiles_per_pseq)
    pos_s = lax.rem(jnp.maximum(i - n_p_tiles, 0), tiles_per_sseq)
    is_start = jnp.where(in_prompt, pos_p == 0, pos_s == 0)
    is_end = jnp.where(in_prompt, pos_p == tiles_per_pseq - 1, pos_s == tiles_per_sseq - 1)

    halo = CHUNK
    pad = conv_w // 2
    cw_ref[0:halo, :] = jnp.where(is_start, cmeta_ref[...], cprev_ref[...]).astype(F32)
    cw_ref[halo:halo + tm, :] = c_ref[...].astype(F32)
    cw_ref[halo + tm:2 * halo + tm, :] = jnp.where(is_end, 0.0, cnext_ref[...].astype(F32))
    sub = 8
    for lc in range(d_conv // LANES):
        ls = slice(lc * LANES, (lc + 1) * LANES)
        for r0 in range(0, tm, rc):
            out = None
            for r in range(sub):
                part = None
                for q in range((conv_w + halo - pad) // sub + 1):
                    k = sub * q + r - (halo - pad)
                    if 0 <= k < conv_w:
                        term = wdw_ref[k:k + 1, ls] * cw_ref[r0 + sub * q:r0 + sub * q + rc + sub, ls]
                        part = term if part is None else part + term
                if part is not None:
                    shifted = part[r:r + rc]
                    out = shifted if out is None else out + shifted
            conv_ref[r0:r0 + rc, ls] = out
    cc = conv_ref[...] + bdw_ref[...]
    mu = jnp.mean(cc, axis=-1, keepdims=True)
    var = jnp.mean(jnp.square(cc - mu), axis=-1, keepdims=True)
    cc = (cc - mu) * lax.rsqrt(var + EPS) * lng_ref[...] + lnb_ref[...]
    cc = (cc * _sigmoid(cc)).astype(BF16)
    y_b = jnp.dot(cc, wpw_ref[...], preferred_element_type=F32)

    for b in range(d_ssm // LANES):
        for s in range(CHUNK):
            c0 = (b * CHUNK + s) * LANES
            zsc_ref[b, pl.ds(s, tm // CHUNK, stride=CHUNK), :] = z_ref[:, c0:c0 + LANES].astype(F32)
    z = jnp.concatenate([zsc_ref[b] for b in range(d_ssm // LANES)], axis=1).astype(BF16)
    va = jnp.dot(z, wglu_ref[:, 0:d_model], preferred_element_type=F32)
    ga = jnp.dot(z, wglu_ref[:, d_model:2 * d_model], preferred_element_type=F32)
    y_a = va * _sigmoid(ga)
    merged = (sga_ref[...].astype(F32) * y_a + sgb_ref[...].astype(F32) * y_b).astype(BF16)
    h1 = x + jnp.dot(merged, wout_ref[...], preferred_element_type=F32)
    h1_ref[...] = h1
    v = _rms(h1, gffn_ref[...])
    half = d_model // 2
    v_ref[...] = _pack_bf16_pair(v[:, 0:half], v[:, half:d_model])

    v_hi = v.astype(BF16)
    v_lo = (v - v_hi.astype(F32)).astype(BF16)
    acc = (jnp.dot(v_hi, rw_ref[...], preferred_element_type=F32)
           + jnp.dot(v_lo, rw_ref[...], preferred_element_type=F32))
    logits = acc + pltpu.roll(acc, LANES // 2, 1) + rb_ref[...]
    lane = lax.broadcasted_iota(I32, (tm, LANES), 1).astype(F32)
    big = jnp.float32(1e9)
    neg = jnp.float32(-jnp.inf)
    gmask = lane < n_grp
    lg = jnp.where(gmask, logits, neg)
    gmax = jnp.max(lg, axis=-1, keepdims=True)
    grp = jnp.min(jnp.where(lg == gmax, lane, big), axis=-1, keepdims=True)
    p_grp = 1.0 / jnp.sum(jnp.where(gmask, jnp.exp(logits - gmax), 0.0), axis=-1, keepdims=True)
    lo = n_grp + grp * exp_per_grp
    emask = (lane >= lo) & (lane < lo + exp_per_grp)
    le = jnp.where(emask, logits, neg)
    m1 = jnp.max(le, axis=-1, keepdims=True)
    i1 = jnp.min(jnp.where(le == m1, lane, big), axis=-1, keepdims=True)
    le2 = jnp.where(lane == i1, neg, le)
    m2 = jnp.max(le2, axis=-1, keepdims=True)
    i2 = jnp.min(jnp.where(le2 == m2, lane, big), axis=-1, keepdims=True)
    t = jnp.exp(m2 - m1)
    w1 = 1.0 / (1.0 + t)
    e1 = i1 - n_grp
    e2 = i2 - n_grp
    route_ref[...] = jnp.where(lane == 0, e1, jnp.where(lane == 1, e2, jnp.where(
        lane == 2, p_grp * w1, jnp.where(lane == 3, p_grp * (t * w1), 0.0))))

    @pl.when(i == 0)
    def _():
        cnt_ref[...] = jnp.zeros_like(cnt_ref)

    hot = jnp.where((lane == e1) | (lane == e2), 1.0, 0.0)
    cnt_ref[...] += jnp.sum(hot, axis=0, keepdims=True)


def _mix(xp2, xs2, z, c, c_meta, sga, sgb, wglu, wpw, wout, wdw, bdw, lng, lnb, gffn, rw, rb, *,
         tm, p_seq, s_seq, n_grp, n_exp):
    n_p, d_model = xp2.shape
    n_s = xs2.shape[0]
    t = n_p + n_s
    d_ssm, d_conv = z.shape[1] // CHUNK, c.shape[1]
    conv_w = wdw.shape[0]
    n_p_tiles, n_s_tiles = n_p // tm, n_s // tm
    hpt = tm // CHUNK
    n_hblk = t // CHUNK
    body = functools.partial(
        _mix_body, tm=tm, n_p_tiles=n_p_tiles, tiles_per_pseq=p_seq // tm,
        tiles_per_sseq=s_seq // tm, d_model=d_model, d_ssm=d_ssm, d_conv=d_conv, conv_w=conv_w,
        n_grp=n_grp, n_exp=n_exp, exp_per_grp=n_exp // n_grp, rc=min(128, tm))
    row = lambda i: (i, 0)
    return pl.pallas_call(
        body,
        grid=(n_p_tiles + n_s_tiles,),
        in_specs=[
            pl.BlockSpec((tm, d_model), lambda i: (jnp.minimum(i, n_p_tiles - 1), 0)),
            pl.BlockSpec((tm, d_model), lambda i: (jnp.maximum(i - n_p_tiles, 0), 0)),
            pl.BlockSpec((tm // CHUNK, CHUNK * d_ssm), row),
            pl.BlockSpec((tm, d_conv), row),
            pl.BlockSpec((CHUNK, d_conv), lambda i: (jnp.maximum(i * hpt - 1, 0), 0)),
            pl.BlockSpec((CHUNK, d_conv), lambda i: (jnp.minimum((i + 1) * hpt, n_hblk - 1), 0)),
            _resident(c_meta.shape),
            pl.BlockSpec((tm, d_model), row),
            pl.BlockSpec((tm, d_model), row),
            _resident(wglu.shape), _resident(wpw.shape), _resident(wout.shape),
            _resident(wdw.shape), _resident(bdw.shape), _resident(lng.shape), _resident(lnb.shape),
            _resident(gffn.shape), _resident(rw.shape), _resident(rb.shape),
        ],
        out_specs=[pl.BlockSpec((tm, d_model), row), pl.BlockSpec((tm, d_model // 2), row),
                   pl.BlockSpec((tm, LANES), row), pl.BlockSpec((1, LANES), lambda i: (0, 0))],
        out_shape=[jax.ShapeDtypeStruct((t, d_model), F32),
                   jax.ShapeDtypeStruct((t, d_model // 2), U32),
                   jax.ShapeDtypeStruct((t, LANES), F32),
                   jax.ShapeDtypeStruct((1, LANES), F32)],
        scratch_shapes=[pltpu.VMEM((tm + 2 * CHUNK, d_conv), F32), pltpu.VMEM((tm, d_conv), F32),
                        pltpu.VMEM((d_ssm // LANES, tm, LANES), F32)],
        compiler_params=_cparams(1),
        name="mix",
    )(xp2, xs2, z, c, c, c, c_meta, sga, sgb, wglu, wpw, wout, wdw, bdw, lng, lnb, gffn, rw, rb)


def _positions_body(route_ref, offs_ref, dest_ref, carry_ref, *, te):
    @pl.when(pl.program_id(0) == 0)
    def _():
        carry_ref[...] = jnp.zeros_like(carry_ref)

    lane = lax.broadcasted_iota(I32, (te, LANES), 1).astype(F32)
    r = route_ref[...]
    oh1 = lane == r[:, 0:1]
    oh2 = lane == r[:, 1:2]
    both = jnp.where(oh1 | oh2, 1.0, 0.0)
    tri = jnp.where(lax.broadcasted_iota(I32, (te, te), 0) > lax.broadcasted_iota(I32, (te, te), 1),
                    1.0, 0.0).astype(BF16)
    before = jnp.dot(tri, both.astype(BF16), preferred_element_type=F32)
    base = before + carry_ref[...] + offs_ref[...]
    d1 = jnp.sum(jnp.where(oh1, base, 0.0), axis=-1, keepdims=True)
    d2 = jnp.sum(jnp.where(oh2, base, 0.0), axis=-1, keepdims=True)
    dest_ref[...] = jnp.where(lane == 0, d1, jnp.where(lane == 1, d2, 0.0)).astype(I32)
    carry_ref[...] += jnp.sum(both, axis=0, keepdims=True)


def _positions(route, offs, *, te):
    t = route.shape[0]
    return pl.pallas_call(
        functools.partial(_positions_body, te=te),
        grid=(t // te,),
        in_specs=[pl.BlockSpec((te, LANES), lambda i: (i, 0)),
                  pl.BlockSpec((1, LANES), lambda i: (0, 0))],
        out_specs=pl.BlockSpec((te, LANES), lambda i: (i, 0)),
        out_shape=jax.ShapeDtypeStruct((t, LANES), I32),
        scratch_shapes=[pltpu.VMEM((1, LANES), F32)],
        compiler_params=_cparams(1),
        name="positions",
    )(route, offs)


def _row_copy(src_ref, src_row, dst_ref, dst_row, sem):
    return pltpu.make_async_copy(src_ref.at[pl.ds(src_row, 1), :],
                                 dst_ref.at[pl.ds(dst_row, 1), :], sem)


def _dispatch_body(dest_hbm, v_ref, xs_in, xs_hbm, dsm, sem, dsem, *, td):
    del xs_in
    i = pl.program_id(0)
    cp = pltpu.make_async_copy(dest_hbm.at[pl.ds(i, 1), :], dsm, dsem)
    cp.start()
    cp.wait()

    def issue(r, _):
        _row_copy(v_ref, r, xs_hbm, dsm[0, 2 * r], sem).start()
        _row_copy(v_ref, r, xs_hbm, dsm[0, 2 * r + 1], sem).start()
        return 0
    lax.fori_loop(0, td, issue, 0)

    def drain(r, _):
        _row_copy(v_ref, 0, xs_hbm, 0, sem).wait()
        _row_copy(v_ref, 0, xs_hbm, 0, sem).wait()
        return 0
    lax.fori_loop(0, td, drain, 0)


def _dispatch(dest2, v, xs0, *, td):
    t, w = v.shape
    return pl.pallas_call(
        functools.partial(_dispatch_body, td=td),
        grid=(t // td,),
        in_specs=[pl.BlockSpec(memory_space=pl.ANY),
                  pl.BlockSpec((td, w), lambda i: (i, 0)),
                  pl.BlockSpec(memory_space=pl.ANY)],
        out_specs=pl.BlockSpec(memory_space=pl.ANY),
        out_shape=jax.ShapeDtypeStruct(xs0.shape, U32),
        scratch_shapes=[pltpu.SMEM((1, 2 * td), I32), pltpu.SemaphoreType.DMA,
                        pltpu.SemaphoreType.DMA],
        input_output_aliases={2: 0},
        compiler_params=_cparams(1),
        name="dispatch",
    )(dest2, v, xs0)


def _experts_body(te_ref, nused_ref, xs_ref, w1_ref, w3_ref, w2_ref, o_ref, *, half):
    del te_ref
    i = pl.program_id(0)

    @pl.when(i < nused_ref[0])
    def _():
        a, b = _unpack_bf16_pair(xs_ref[...])
        a, b = a.astype(BF16), b.astype(BF16)

        def up(w_ref):
            return (jnp.dot(a, w_ref[0:half, :], preferred_element_type=F32)
                    + jnp.dot(b, w_ref[half:2 * half, :], preferred_element_type=F32))
        h1 = up(w1_ref)
        act = (h1 * _sigmoid(h1) * up(w3_ref)).astype(BF16)
        o = jnp.dot(act, w2_ref[...], preferred_element_type=F32)
        o_ref[...] = _pack_bf16_pair(o[:, 0:half], o[:, half:2 * half])

    @pl.when(i >= nused_ref[0])
    def _():
        o_ref[...] = jnp.zeros_like(o_ref)


def _experts(tile_expert, n_used, xs, w1, w3, w2, *, tme):
    rows, half = xs.shape
    n_e, d_model, d_exp = w1.shape
    grid_spec = pltpu.PrefetchScalarGridSpec(
        num_scalar_prefetch=2,
        grid=(rows // tme,),
        in_specs=[pl.BlockSpec((tme, half), lambda i, te, nu: (i, 0)),
                  pl.BlockSpec((None, d_model, d_exp), lambda i, te, nu: (te[i], 0, 0)),
                  pl.BlockSpec((None, d_model, d_exp), lambda i, te, nu: (te[i], 0, 0)),
                  pl.BlockSpec((None, d_exp, d_model), lambda i, te, nu: (te[i], 0, 0))],
        out_specs=pl.BlockSpec((tme, half), lambda i, te, nu: (i, 0)),
    )
    return pl.pallas_call(
        functools.partial(_experts_body, half=half),
        grid_spec=grid_spec,
        out_shape=jax.ShapeDtypeStruct((rows, half), U32),
        compiler_params=_cparams(1),
        name="experts",
    )(tile_expert, n_used, xs, w1, w3, w2)


def _combine_body(dest_hbm, o_hbm, h1_ref, route_ref, g_ref, y_ref, dsm, gbuf, sem, dsem, *,
                  tf, tile0, half):
    i = pl.program_id(0)
    cp = pltpu.make_async_copy(dest_hbm.at[pl.ds(tile0 + i, 1), :], dsm, dsem)
    cp.start()
    cp.wait()

    def issue(r, _):
        _row_copy(o_hbm, dsm[0, 2 * r], gbuf.at[0], r, sem).start()
        _row_copy(o_hbm, dsm[0, 2 * r + 1], gbuf.at[1], r, sem).start()
        return 0
    lax.fori_loop(0, tf, issue, 0)

    def drain(r, _):
        _row_copy(o_hbm, 0, gbuf.at[0], 0, sem).wait()
        _row_copy(o_hbm, 0, gbuf.at[0], 0, sem).wait()
        return 0
    lax.fori_loop(0, tf, drain, 0)

    route = route_ref[...]
    g1, g2 = route[:, 2:3], route[:, 3:4]
    a1, b1 = _unpack_bf16_pair(gbuf[0])
    a2, b2 = _unpack_bf16_pair(gbuf[1])
    h1 = h1_ref[...]
    ha = h1[:, 0:half] + (a1 * g1 + a2 * g2)
    hb = h1[:, half:2 * half] + (b1 * g1 + b2 * g2)
    ms = (jnp.sum(ha * ha, axis=-1, keepdims=True)
          + jnp.sum(hb * hb, axis=-1, keepdims=True)) / (2 * half)
    inv = lax.rsqrt(ms + EPS)
    y_ref[:, 0:half] = ha * inv * g_ref[:, 0:half]
    y_ref[:, half:2 * half] = hb * inv * g_ref[:, half:2 * half]


def _combine(dest2, o, h1, route, g, *, tf, row0, n_rows):
    d_model = h1.shape[1]
    half = d_model // 2
    tile0 = row0 // tf
    return pl.pallas_call(
        functools.partial(_combine_body, tf=tf, tile0=tile0, half=half),
        grid=(n_rows // tf,),
        in_specs=[pl.BlockSpec(memory_space=pl.ANY), pl.BlockSpec(memory_space=pl.ANY),
                  pl.BlockSpec((tf, d_model), lambda i: (tile0 + i, 0)),
                  pl.BlockSpec((tf, LANES), lambda i: (tile0 + i, 0)),
                  pl.BlockSpec((1, d_model), lambda i: (0, 0))],
        out_specs=pl.BlockSpec((tf, d_model), lambda i: (i, 0)),
        out_shape=jax.ShapeDtypeStruct((n_rows, d_model), F32),
        scratch_shapes=[pltpu.SMEM((1, 2 * tf), I32), pltpu.VMEM((2, tf, half), U32),
                        pltpu.SemaphoreType.DMA, pltpu.SemaphoreType.DMA],
        compiler_params=_cparams(1),
        name="combine",
    )(dest2, o, h1, route, g)


def kernel(x_prompt, x_sample, meta, norm_mix_g, w_in, ssm_lam_re, ssm_lam_im, ssm_log_step, ssm_b_re, ssm_b_im, ssm_c_re, ssm_c_im, ssm_d, ssm_w_glu, conv_w_dw, conv_b_dw, conv_ln_g, conv_ln_b, conv_w_pw, w_out, norm_ffn_g, router_group_w, router_group_b, router_expert_w, router_expert_b, expert_w1, expert_w3, expert_w2, final_g):
    assert w_in.shape[0] == 1, "single-layer trunk"
    bp, lp, d_model = x_prompt.shape
    bs, ls, _ = x_sample.shape
    n_meta = meta.shape[0]
    d_ssm = ssm_d.shape[-1]
    d_conv = conv_b_dw.shape[-1]
    n_ch = ssm_b_re.shape[-1]
    n_g = ssm_b_re.shape[2]
    n_grp = router_group_w.shape[-1]
    n_exp = router_expert_w.shape[-1]
    assert n_meta == CHUNK and lp % CHUNK == 0 and ls % CHUNK == 0
    n_p, n_s = bp * lp, bs * ls
    t = n_p + n_s
    tm = min(256, lp, ls)
    assert lp % tm == 0 and ls % tm == 0

    xp2 = x_prompt.reshape(n_p, d_model)
    xs2 = x_sample.reshape(n_s, d_model)
    row = lambda a: a.reshape(1, -1)
    w_in_bf = w_in[0].astype(BF16)

    xg, c, sga, sgb = _inproj(xp2, xs2, row(norm_mix_g[0]), w_in_bf, d_ssm=d_ssm, d_conv=d_conv, tm=tm)
    xg_meta, c_meta = _meta_inproj(meta, row(norm_mix_g[0]), w_in_bf, d_ssm=d_ssm, d_conv=d_conv)

    assert n_ch == CHUNK and LANES % n_ch == 0 and n_g % (LANES // n_ch) == 0
    intra, bst, cst, coef = _sg_prep(ssm_lam_re[0], ssm_lam_im[0], ssm_log_step[0], ssm_b_re[0],
                                     ssm_b_im[0], ssm_c_re[0], ssm_c_im[0], ssm_d[0])
    nc = t // CHUNK
    n_blk = d_ssm // LANES
    rt = min(512, nc)
    yi = _sg_mm_intra(xg, intra, rt=rt)
    sf, sb, init = _sg_mm_state(xg, xg_meta, bst, rt=rt)
    n_sub = sf.shape[0] // (n_blk * nc)
    cb = min(64, lp // CHUNK, ls // CHUNK)
    xf, xb = _sg_scan(sf.reshape(n_blk, nc, n_sub, LANES), sb.reshape(n_blk, nc, n_sub, LANES),
                      coef, init, cb=cb, chunks_pseq=lp // CHUNK, chunks_sseq=ls // CHUNK,
                      n_pchunks=n_p // CHUNK)
    z = _sg_mm_out(yi, xf.reshape(sf.shape), xb.reshape(sb.shape), cst, rt=rt)

    assert n_grp + n_exp <= LANES // 2
    rw32 = jnp.pad(jnp.concatenate([router_group_w[0], router_expert_w[0]], axis=1),
                   ((0, 0), (0, LANES // 2 - n_grp - n_exp)))
    rw_hi = rw32.astype(BF16)
    rw = jnp.concatenate([rw_hi, (rw32 - rw_hi.astype(F32)).astype(BF16)], axis=1)
    rb = jnp.zeros((1, LANES), F32).at[0, 0:n_grp].set(router_group_b[0]).at[
        0, n_grp:n_grp + n_exp].set(router_expert_b[0])
    h1, v, route, cnt = _mix(
        xp2, xs2, z, c, c_meta, sga, sgb, ssm_w_glu[0].astype(BF16), conv_w_pw[0].astype(BF16),
        w_out[0].astype(BF16), conv_w_dw[0], row(conv_b_dw[0]), row(conv_ln_g[0]),
        row(conv_ln_b[0]), row(norm_ffn_g[0]), rw, rb,
        tm=tm, p_seq=lp, s_seq=ls, n_grp=n_grp, n_exp=n_exp)

    tme = 256
    counts = cnt[0, 0:n_exp].astype(I32)
    tiles_e = (counts + tme - 1) // tme
    tile_end = jnp.cumsum(tiles_e)
    n_used = tile_end[-1]
    offs = jnp.zeros((1, LANES), F32).at[0, 0:n_exp].set(((tile_end - tiles_e) * tme).astype(F32))
    n_tiles = (2 * t) // tme + n_exp
    ids = jnp.arange(n_tiles, dtype=I32)
    te_map = jnp.sum((jnp.minimum(ids, n_used - 1)[:, None] >= tile_end[None, :]).astype(I32), axis=1)
    te_map = jnp.minimum(te_map, n_exp - 1)

    td = min(512, t)
    dest = _positions(route, offs, te=td)
    dest2 = dest[:, 0:2].reshape(t // td, 2 * td)
    xs0 = jnp.zeros((n_tiles * tme, d_model // 2), U32)
    xs = _dispatch(dest2, v, xs0, td=td)
    o = _experts(te_map, n_used.reshape(1), xs, expert_w1[0].astype(BF16),
                 expert_w3[0].astype(BF16), expert_w2[0].astype(BF16), tme=tme)

    tf = min(256, n_p, n_s)
    dest2f = dest[:, 0:2].reshape(t // tf, 2 * tf)
    fg = row(final_g)
    y_p = _combine(dest2f, o, h1, route, fg, tf=tf, row0=0, n_rows=n_p)
    y_s = _combine(dest2f, o, h1, route, fg, tf=tf, row0=n_p, n_rows=n_s)
    return (y_p.reshape(bp, lp, d_model), y_s.reshape(bs, ls, d_model))
```

```python
import functools

import jax
import jax.numpy as jnp
from jax import lax
from jax.experimental import pallas as pl
from jax.experimental.pallas import tpu as pltpu

F32 = jnp.float32
BF16 = jnp.bfloat16
U32 = jnp.uint32
I32 = jnp.int32

EPS = 1e-6
LAM_RE_MAX = -1e-4
CHUNK = 16
LANES = 128
VMEM_LIMIT = 56 << 20
HIGHEST = lax.Precision.HIGHEST


def _cparams(n_axes):
    return pltpu.CompilerParams(dimension_semantics=("arbitrary",) * n_axes,
                                vmem_limit_bytes=VMEM_LIMIT)


def _resident(shape):
    nd = len(shape)
    return pl.BlockSpec(shape, lambda *_: (0,) * nd, pipeline_mode=pl.Buffered(1))


def _sigmoid(x):
    return 1.0 / (1.0 + jnp.exp(-x))


def _rms(x, g):
    return x * lax.rsqrt(jnp.mean(x * x, axis=-1, keepdims=True) + EPS) * g


def _pack_bf16_pair(a, b):
    def rnd(x):
        u = pltpu.bitcast(x, U32)
        return (u + jnp.uint32(0x7FFF) + ((u >> 16) & jnp.uint32(1))) >> 16
    return (rnd(a) << 16) | rnd(b)


def _unpack_bf16_pair(p):
    a = pltpu.bitcast(p & jnp.uint32(0xFFFF0000), F32)
    b = pltpu.bitcast(p << 16, F32)
    return a, b


def _inproj_body(xp_ref, xs_ref, g_ref, w_ref, xg_ref, c_ref, sga_ref, sgb_ref, usc, *,
                 tm, n_p_tiles, d_ssm, d_conv, d_model, col):
    i = pl.program_id(0)
    x = jnp.where(i < n_p_tiles, xp_ref[...], xs_ref[...])
    y = _rms(x, g_ref[...]).astype(BF16)

    def proj(lo, n):
        return jnp.dot(y, w_ref[:, lo:lo + n], preferred_element_type=F32)

    for k in range(d_ssm // col):
        u = proj(k * col, col)
        for b in range(col // LANES):
            usc[k * (col // LANES) + b] = u[:, b * LANES:(b + 1) * LANES]
    for b in range(d_ssm // LANES):
        for s in range(CHUNK):
            c0 = (b * CHUNK + s) * LANES
            xg_ref[:, c0:c0 + LANES] = usc[b, pl.ds(s, tm // CHUNK, stride=CHUNK), :].astype(BF16)
    for k in range(d_conv // col):
        cv = proj(d_ssm + k * col, col)
        cg = proj(d_ssm + d_conv + k * col, col)
        c_ref[:, k * col:(k + 1) * col] = (cv * _sigmoid(cg)).astype(BF16)
    base = d_ssm + 2 * d_conv
    for k in range(d_model // col):
        sga_ref[:, k * col:(k + 1) * col] = _sigmoid(proj(base + k * col, col)).astype(BF16)
        sgb_ref[:, k * col:(k + 1) * col] = _sigmoid(
            proj(base + d_model + k * col, col)).astype(BF16)


def _inproj(xp2, xs2, g, w_bf, *, d_ssm, d_conv, tm):
    n_p, d_model = xp2.shape
    n_s = xs2.shape[0]
    n_p_tiles, n_s_tiles = n_p // tm, n_s // tm
    t = n_p + n_s
    col = min(1024, d_ssm, d_conv, d_model)
    body = functools.partial(_inproj_body, tm=tm, n_p_tiles=n_p_tiles, d_ssm=d_ssm, d_conv=d_conv,
                             d_model=d_model, col=col)
    row = lambda i: (i, 0)
    return pl.pallas_call(
        body,
        grid=(n_p_tiles + n_s_tiles,),
        in_specs=[
            pl.BlockSpec((tm, d_model), lambda i: (jnp.minimum(i, n_p_tiles - 1), 0)),
            pl.BlockSpec((tm, d_model), lambda i: (jnp.maximum(i - n_p_tiles, 0), 0)),
            _resident((1, d_model)),
            _resident(w_bf.shape),
        ],
        out_specs=[pl.BlockSpec((tm // CHUNK, CHUNK * d_ssm), row), pl.BlockSpec((tm, d_conv), row),
                   pl.BlockSpec((tm, d_model), row), pl.BlockSpec((tm, d_model), row)],
        out_shape=[jax.ShapeDtypeStruct((t // CHUNK, CHUNK * d_ssm), BF16),
                   jax.ShapeDtypeStruct((t, d_conv), BF16),
                   jax.ShapeDtypeStruct((t, d_model), BF16), jax.ShapeDtypeStruct((t, d_model), BF16)],
        scratch_shapes=[pltpu.VMEM((d_ssm // LANES, tm, LANES), F32)],
        compiler_params=_cparams(1),
        name="inproj",
    )(xp2, xs2, g, w_bf)


def _meta_body(m_ref, g_ref, w_ref, xg_ref, c_ref, *, d_ssm, d_conv):
    y = _rms(m_ref[...], g_ref[...]).astype(BF16)
    u = jnp.dot(y, w_ref[:, 0:d_ssm], preferred_element_type=F32)
    first = lax.broadcasted_iota(I32, (CHUNK, LANES), 0) == 0
    for b in range(d_ssm // LANES):
        for s in range(CHUNK):
            c0 = (b * CHUNK + s) * LANES
            piece = jnp.broadcast_to(u[s:s + 1, b * LANES:(b + 1) * LANES], (CHUNK, LANES))
            xg_ref[:, c0:c0 + LANES] = jnp.where(first, piece, 0.0).astype(BF16)
    cv = jnp.dot(y, w_ref[:, d_ssm:d_ssm + d_conv], preferred_element_type=F32)
    cg = jnp.dot(y, w_ref[:, d_ssm + d_conv:d_ssm + 2 * d_conv], preferred_element_type=F32)
    c_ref[...] = (cv * _sigmoid(cg)).astype(BF16)


def _meta_inproj(meta, g, w_bf, *, d_ssm, d_conv):
    n_meta, d_model = meta.shape
    ncol = d_ssm + 2 * d_conv
    return pl.pallas_call(
        functools.partial(_meta_body, d_ssm=d_ssm, d_conv=d_conv),
        grid=(1,),
        in_specs=[pl.BlockSpec((n_meta, d_model), lambda i: (0, 0)),
                  pl.BlockSpec((1, d_model), lambda i: (0, 0)),
                  pl.BlockSpec((d_model, ncol), lambda i: (0, 0))],
        out_specs=[pl.BlockSpec((CHUNK, CHUNK * d_ssm), lambda i: (0, 0)),
                   pl.BlockSpec((n_meta, d_conv), lambda i: (0, 0))],
        out_shape=[jax.ShapeDtypeStruct((CHUNK, CHUNK * d_ssm), BF16),
                   jax.ShapeDtypeStruct((n_meta, d_conv), BF16)],
        compiler_params=_cparams(1),
        name="meta_inproj",
    )(meta, g, w_bf)


def _cmul(ar, ai, br, bi):
    return ar * br - ai * bi, ar * bi + ai * br


def _discretize(lam_re, lam_im, log_step):
    lr = jnp.minimum(lam_re, LAM_RE_MAX)
    dt = jnp.exp(log_step)
    mag = jnp.exp(lr * dt)
    ar = mag * jnp.cos(lam_im * dt)
    ai = mag * jnp.sin(lam_im * dt)
    den = lr * lr + lam_im * lam_im
    nr = ar - 1.0
    fr = (nr * lr + ai * lam_im) / den
    fi = (ai * lr - nr * lam_im) / den
    return ar, ai, fr, fi


def _cpow(ar, ai, k, nbits, shape):
    pr = jnp.ones(shape, F32)
    pi = jnp.zeros(shape, F32)
    br = jnp.broadcast_to(ar, shape)
    bi = jnp.broadcast_to(ai, shape)
    kk = jnp.broadcast_to(k, shape)
    for b in range(nbits):
        sel = ((kk >> b) & 1) == 1
        nr, ni = _cmul(pr, pi, br, bi)
        pr = jnp.where(sel, nr, pr)
        pi = jnp.where(sel, ni, pi)
        br, bi = _cmul(br, bi, br, bi)
    return pr, pi


def _ssm_prep_body(lam_re_c, lam_im_c, ls_c, ct_re, ct_im, bt_re, bt_im,
                   lam_re_r, lam_im_r, ls_r, btr_re, btr_im, dpad, xmeta,
                   w1_ref, cst_ref, aq_ref, init_ref, *, n_state, n_ch):
    q = CHUNK
    width = q * n_ch
    par = pl.program_id(0) % 2
    kblk = lax.broadcasted_iota(I32, (1, width), 1) // n_ch
    strips = []
    for d in range(2):
        ar, ai, fr, fi = _discretize(lam_re_c[d], lam_im_c[d], ls_c[d])
        kexp = kblk if d == 0 else (q - 1) - kblk
        wr, wi = _cpow(ar, ai, kexp, 4, (n_state, width))
        gcr, gci = _cmul(ct_re[d], ct_im[d], wr, wi)
        gfr, gfi = _cmul(gcr, gci, fr, fi)
        strips.append(jnp.dot(bt_re[d], gfr, precision=HIGHEST, preferred_element_type=F32)
                      - jnp.dot(bt_im[d], gfi, precision=HIGHEST, preferred_element_type=F32))
        g1r, g1i = _cmul(gcr, gci, ar, ai)
        rowmask = (lax.broadcasted_iota(I32, (2 * n_state, width), 0) // n_state) == par
        rep_r = jnp.concatenate([g1r, g1r], axis=0)
        rep_i = jnp.concatenate([g1i, g1i], axis=0)
        r0 = 4 * n_state * d
        cst_ref[r0:r0 + 2 * n_state, :] = jnp.where(rowmask, rep_r, 0.0).astype(BF16)
        cst_ref[r0 + 2 * n_state:r0 + 4 * n_state, :] = jnp.where(rowmask, -rep_i, 0.0).astype(BF16)

    zf, zb = strips
    zero = jnp.zeros((n_ch, width), F32)
    z512 = jnp.concatenate([zb, zero], axis=1) + pltpu.roll(
        jnp.concatenate([zf, zero], axis=1), (q - 1) * n_ch, 1)
    row = lax.broadcasted_iota(I32, (n_ch, 2 * width), 0)
    lane = lax.broadcasted_iota(I32, (n_ch, 2 * width), 1)
    z512 = z512 + jnp.where(lane - (q - 1) * n_ch == row, dpad[...], 0.0)
    for s in range(q):
        sh = (q - 1 - s) * n_ch
        blk = z512 if sh == 0 else pltpu.roll(z512, 2 * width - sh, 1)
        w1_ref[n_ch * s:n_ch * (s + 1), 0:width] = blk[:, 0:width].astype(BF16)

    parmask = (lax.broadcasted_iota(I32, (1, 2 * n_state), 1) // n_state) == par
    for d in range(2):
        ar, ai, fr, fi = _discretize(lam_re_r[d], lam_im_r[d], ls_r[d])
        pw = [(jnp.ones_like(ar), jnp.zeros_like(ar))]
        for _ in range(q):
            pw.append(_cmul(pw[-1][0], pw[-1][1], ar, ai))
        col = width + 4 * n_state * d
        for s in range(q):
            e = (q - 1 - s) if d == 0 else s
            cr, ci = _cmul(fr, fi, pw[e][0], pw[e][1])
            br, bi = _cmul(btr_re[d], btr_im[d], cr, ci)
            w1_ref[n_ch * s:n_ch * (s + 1), col:col + 2 * n_state] = (
                jnp.where(parmask, br, 0.0).astype(BF16))
            w1_ref[n_ch * s:n_ch * (s + 1), col + 2 * n_state:col + 4 * n_state] = (
                jnp.where(parmask, bi, 0.0).astype(BF16))
        aq_ref[:, col - width:col - width + 2 * n_state] = pw[q][0]
        aq_ref[:, col - width + 2 * n_state:col - width + 4 * n_state] = pw[q][1]

    init_ref[...] = jnp.dot(xmeta[...].astype(BF16), w1_ref[:, width:width + 4 * n_state],
                            preferred_element_type=F32)


def _ssm_prep(lam_re, lam_im, log_step, b_re, b_im, c_re, c_im, d_skip, us_meta):
    _, n_g, n_state, n_ch = b_re.shape
    q = CHUNK
    width = q * n_ch
    dup = lambda x: jnp.concatenate([x, x], axis=-1)
    lam_re_c, lam_im_c = lam_re[..., None], lam_im[..., None]
    ls_c = log_step[..., None, None]
    ct_re = jnp.tile(jnp.swapaxes(c_re, -1, -2), (1, 1, 1, q))
    ct_im = jnp.tile(jnp.swapaxes(c_im, -1, -2), (1, 1, 1, q))
    bt_re, bt_im = jnp.swapaxes(b_re, -1, -2), jnp.swapaxes(b_im, -1, -2)
    lam_re_r, lam_im_r = dup(lam_re)[:, :, None, :], dup(lam_im)[:, :, None, :]
    ls_r = jnp.broadcast_to(log_step[..., None, None], (2, n_g, 1, 2 * n_state))
    btr_re, btr_im = dup(bt_re), dup(bt_im)
    dpad = jnp.pad(d_skip.reshape(n_g, 1, n_ch), ((0, 0), (0, 0), ((q - 1) * n_ch, width)))
    xm = us_meta.astype(F32).reshape(q, n_g, n_ch).transpose(1, 0, 2).reshape(n_g, 1, width)
    xm = jnp.pad(xm, ((0, 0), (0, 7), (0, 0)))

    def dspec(shape):
        return pl.BlockSpec((2, None) + shape, lambda g: (0, g, 0, 0))

    def gspec(shape):
        return pl.BlockSpec((None,) + shape, lambda g: (g, 0, 0))

    body = functools.partial(_ssm_prep_body, n_state=n_state, n_ch=n_ch)
    return pl.pallas_call(
        body,
        grid=(n_g,),
        in_specs=[dspec((n_state, 1)), dspec((n_state, 1)), dspec((1, 1)),
                  dspec((n_state, width)), dspec((n_state, width)),
                  dspec((n_ch, n_state)), dspec((n_ch, n_state)),
                  dspec((1, 2 * n_state)), dspec((1, 2 * n_state)), dspec((1, 2 * n_state)),
                  dspec((n_ch, 2 * n_state)), dspec((n_ch, 2 * n_state)),
                  gspec((1, 2 * width)), gspec((8, width))],
        out_specs=[gspec((width, width + 8 * n_state)), gspec((8 * n_state, width)),
                   gspec((1, 8 * n_state)), gspec((8, 4 * n_state))],
        out_shape=[jax.ShapeDtypeStruct((n_g, width, width + 8 * n_state), BF16),
                   jax.ShapeDtypeStruct((n_g, 8 * n_state, width), BF16),
                   jax.ShapeDtypeStruct((n_g, 1, 8 * n_state), F32),
                   jax.ShapeDtypeStruct((n_g, 8, 4 * n_state), F32)],
        compiler_params=_cparams(1),
        name="ssm_prep",
    )(lam_re_c, lam_im_c, ls_c, ct_re, ct_im, bt_re, bt_im,
      lam_re_r, lam_im_r, ls_r, btr_re, btr_im, dpad, xm)


def _ssm_mm1_body(x_ref, w1_ref, yi_ref, s_ref, *, gb, width, sw):
    for j in range(0, gb, 2):
        r0 = jnp.dot(x_ref[j], w1_ref[j], preferred_element_type=F32)
        r1 = jnp.dot(x_ref[j + 1], w1_ref[j + 1], preferred_element_type=F32)
        yi_ref[j] = r0[:, 0:width]
        yi_ref[j + 1] = r1[:, 0:width]
        s_ref[:, (j // 2) * sw:(j // 2 + 1) * sw] = r0[:, width:] + r1[:, width:]


def _ssm_mm1(xt, w1, *, gb):
    n_g, nc, width = xt.shape
    sw = w1.shape[-1] - width
    body = functools.partial(_ssm_mm1_body, gb=gb, width=width, sw=sw)
    return pl.pallas_call(
        body,
        grid=(n_g // gb,),
        in_specs=[pl.BlockSpec((gb, nc, width), lambda i: (i, 0, 0)),
                  pl.BlockSpec((gb, width, width + sw), lambda i: (i, 0, 0))],
        out_specs=[pl.BlockSpec((gb, nc, width), lambda i: (i, 0, 0)),
                   pl.BlockSpec((nc, gb // 2 * sw), lambda i: (0, i))],
        out_shape=[jax.ShapeDtypeStruct((n_g, nc, width), F32),
                   jax.ShapeDtypeStruct((nc, n_g // 2 * sw), F32)],
        compiler_params=_cparams(1),
        name="ssm_mm1",
    )(xt, w1)


def _ssm_scan_body(sf_ref, sb_ref, aq_ref, init_ref, xf_ref, xb_ref, stf, stb, *,
                   cb, hw, n_pblk, blk_pseq, blk_sseq):
    j = pl.program_id(1)
    pos = jnp.where(j < n_pblk, lax.rem(j, blk_pseq), lax.rem(jnp.maximum(j - n_pblk, 0), blk_sseq))

    @pl.when(pos == 0)
    def _():
        stf[...] = init_ref[...]
        stb[...] = jnp.zeros_like(stb)

    afr, afi = aq_ref[:, 0:hw], aq_ref[:, hw:2 * hw]
    abr, abi = aq_ref[:, 2 * hw:3 * hw], aq_ref[:, 3 * hw:4 * hw]

    def body(i, carry):
        fr, fi, br, bi = carry
        ib = cb - 1 - i
        xf_ref[i, :, 0:hw] = fr
        xf_ref[i, :, hw:2 * hw] = fi
        xb_ref[ib, :, 0:hw] = br
        xb_ref[ib, :, hw:2 * hw] = bi
        nfr = afr * fr - afi * fi + sf_ref[i, :, 0:hw]
        nfi = afr * fi + afi * fr + sf_ref[i, :, hw:2 * hw]
        nbr = abr * br - abi * bi + sb_ref[ib, :, 0:hw]
        nbi = abr * bi + abi * br + sb_ref[ib, :, hw:2 * hw]
        return nfr, nfi, nbr, nbi

    fr, fi, br, bi = lax.fori_loop(
        0, cb, body, (stf[:, 0:hw], stf[:, hw:2 * hw], stb[:, 0:hw], stb[:, hw:2 * hw]))
    stf[:, 0:hw] = fr
    stf[:, hw:2 * hw] = fi
    stb[:, 0:hw] = br
    stb[:, hw:2 * hw] = bi


def _ssm_scan(s3, aq, init, *, cb, chunks_pseq, chunks_sseq, n_pchunks):
    nc, n_pair, sw = s3.shape
    hw = sw // 4
    pb = min(8, n_pair)
    n_pblk, blk_pseq, blk_sseq = n_pchunks // cb, chunks_pseq // cb, chunks_sseq // cb

    def bwd_block(j):
        in_p = j < n_pblk
        pos = jnp.where(in_p, lax.rem(j, blk_pseq), lax.rem(jnp.maximum(j - n_pblk, 0), blk_sseq))
        ln = jnp.where(in_p, blk_pseq, blk_sseq)
        return j - pos + ln - 1 - pos

    body = functools.partial(_ssm_scan_body, cb=cb, hw=hw, n_pblk=n_pblk, blk_pseq=blk_pseq,
                             blk_sseq=blk_sseq)
    return pl.pallas_call(
        body,
        grid=(n_pair // pb, nc // cb),
        in_specs=[pl.BlockSpec((cb, pb, 2 * hw), lambda p, j: (j, p, 0)),
                  pl.BlockSpec((cb, pb, 2 * hw), lambda p, j: (bwd_block(j), p, 1)),
                  pl.BlockSpec((pb, sw), lambda p, j: (p, 0)),
                  pl.BlockSpec((pb, 2 * hw), lambda p, j: (p, 0))],
        out_specs=[pl.BlockSpec((cb, pb, 2 * hw), lambda p, j: (j, p, 0)),
                   pl.BlockSpec((cb, pb, 2 * hw), lambda p, j: (bwd_block(j), p, 0))],
        out_shape=[jax.ShapeDtypeStruct((nc, n_pair, 2 * hw), F32),
                   jax.ShapeDtypeStruct((nc, n_pair, 2 * hw), F32)],
        scratch_shapes=[pltpu.VMEM((pb, 2 * hw), F32), pltpu.VMEM((pb, 2 * hw), F32)],
        compiler_params=_cparams(2),
        name="ssm_scan",
    )(s3, s3, aq, init)


def _gelu_tanh(x):
    return 0.5 * x * (1.0 + jnp.tanh(0.7978845608028654 * (x + 0.044715 * (x * x * x))))


def _ssm_mm2_body(yi_ref, xf_ref, xb_ref, cst_ref, z_ref, *, gb, hs):
    for j in range(gb):
        cols = slice((j // 2) * hs, (j // 2 + 1) * hs)
        y = (yi_ref[j]
             + jnp.dot(xf_ref[:, cols].astype(BF16), cst_ref[j, 0:hs, :], preferred_element_type=F32)
             + jnp.dot(xb_ref[:, cols].astype(BF16), cst_ref[j, hs:2 * hs, :],
                       preferred_element_type=F32))
        z_ref[j] = _gelu_tanh(y).astype(BF16)


def _ssm_mm2(yi, xf2, xb2, cst, *, gb):
    n_g, nc, width = yi.shape
    hs = cst.shape[1] // 2
    body = functools.partial(_ssm_mm2_body, gb=gb, hs=hs)
    return pl.pallas_call(
        body,
        grid=(n_g // gb,),
        in_specs=[pl.BlockSpec((gb, nc, width), lambda i: (i, 0, 0)),
                  pl.BlockSpec((nc, gb // 2 * hs), lambda i: (0, i)),
                  pl.BlockSpec((nc, gb // 2 * hs), lambda i: (0, i)),
                  pl.BlockSpec((gb, 2 * hs, width), lambda i: (i, 0, 0))],
        out_specs=pl.BlockSpec((gb, nc, width), lambda i: (i, 0, 0)),
        out_shape=jax.ShapeDtypeStruct((n_g, nc, width), BF16),
        compiler_params=_cparams(1),
        name="ssm_mm2",
    )(yi, xf2, xb2, cst)


def _spread(x, n_ch, shift):
    rows, w = x.shape
    lane = lax.broadcasted_iota(I32, (rows, LANES), 1)
    keep = (lane >= shift) & (lane < shift + n_ch)
    out = []
    for b in range(w // n_ch):
        src = (b * n_ch // LANES) * LANES
        amount = lax.rem(shift + LANES - (b * n_ch) % LANES, LANES)
        out.append(jnp.where(keep, pltpu.roll(x[:, src:src + LANES], amount, 1), 0.0))
    return jnp.concatenate(out, axis=1)


def _sg_prep_body(lam_re_c, lam_im_c, ls_c, ct_re, ct_im, bt_re, bt_im,
                  lam_re_r, lam_im_r, ls_r, btr_re, btr_im, dpad,
                  m_ref, b_ref, c_ref, aq_ref, *, n_state, n_ch):
    q = CHUNK
    width = q * n_ch
    per_blk = LANES // n_ch
    gl = pl.program_id(0) % per_blk
    shift = gl * n_ch
    kblk = lax.broadcasted_iota(I32, (1, width), 1) // n_ch
    strips = []
    for d in range(2):
        ar, ai, fr, fi = _discretize(lam_re_c[d], lam_im_c[d], ls_c[d])
        kexp = kblk if d == 0 else (q - 1) - kblk
        wr, wi = _cpow(ar, ai, kexp, 4, (n_state, width))
        gcr, gci = _cmul(ct_re[d], ct_im[d], wr, wi)
        gfr, gfi = _cmul(gcr, gci, fr, fi)
        strips.append(jnp.dot(bt_re[d], gfr, precision=HIGHEST, preferred_element_type=F32)
                      - jnp.dot(bt_im[d], gfi, precision=HIGHEST, preferred_element_type=F32))
        g1r, g1i = _cmul(gcr, gci, ar, ai)
        for comp, val in ((0, g1r), (1, -g1i)):
            c_ref[2 * d + comp] = _spread(val, n_ch, shift).astype(BF16)

    zf, zb = strips
    zero = jnp.zeros((n_ch, width), F32)
    z512 = jnp.concatenate([zb, zero], axis=1) + pltpu.roll(
        jnp.concatenate([zf, zero], axis=1), (q - 1) * n_ch, 1)
    row = lax.broadcasted_iota(I32, (n_ch, 2 * width), 0)
    lane = lax.broadcasted_iota(I32, (n_ch, 2 * width), 1)
    z512 = z512 + jnp.where(lane - (q - 1) * n_ch == row, dpad[...], 0.0)
    zwide = _spread(z512, n_ch, shift)
    for s in range(q):
        lo = (q - 1 - s) * LANES
        m_ref[s] = zwide[:, lo:lo + q * LANES].astype(BF16)

    parmask = (lax.broadcasted_iota(I32, (1, 2 * n_state), 1) // n_state) == gl % 2
    pieces = [[None] * 4 for _ in range(q)]
    for d in range(2):
        ar, ai, fr, fi = _discretize(lam_re_r[d], lam_im_r[d], ls_r[d])
        pw = [(jnp.ones_like(ar), jnp.zeros_like(ar))]
        for _ in range(q):
            pw.append(_cmul(pw[-1][0], pw[-1][1], ar, ai))
        for s in range(q):
            e = (q - 1 - s) if d == 0 else s
            cr, ci = _cmul(fr, fi, pw[e][0], pw[e][1])
            br, bi = _cmul(btr_re[d], btr_im[d], cr, ci)
            pieces[s][2 * d] = jnp.where(parmask, br, 0.0)
            pieces[s][2 * d + 1] = jnp.where(parmask, bi, 0.0)
        aq_ref[:, 2 * d * n_state:(2 * d + 1) * n_state] = pw[q][0][:, 0:n_state]
        aq_ref[:, (2 * d + 1) * n_state:(2 * d + 2) * n_state] = pw[q][1][:, 0:n_state]
    n_pair = per_blk // 2
    for s in range(q):
        cols = []
        for seg in range(4):
            for blk in range(n_pair):
                cols.append(jnp.where(gl // 2 == blk, pieces[s][seg], 0.0))
        b_ref[s] = jnp.concatenate(cols, axis=1).astype(BF16)


def _sg_prep(lam_re, lam_im, log_step, b_re, b_im, c_re, c_im, d_skip):
    _, n_g, n_state, n_ch = b_re.shape
    q = CHUNK
    width = q * n_ch
    per_blk = LANES // n_ch
    n_blk = n_g // per_blk
    sw = 4 * per_blk * n_state
    dup = lambda x: jnp.concatenate([x, x], axis=-1)
    lam_re_c, lam_im_c = lam_re[..., None], lam_im[..., None]
    ls_c = log_step[..., None, None]
    ct_re = jnp.tile(jnp.swapaxes(c_re, -1, -2), (1, 1, 1, q))
    ct_im = jnp.tile(jnp.swapaxes(c_im, -1, -2), (1, 1, 1, q))
    bt_re, bt_im = jnp.swapaxes(b_re, -1, -2), jnp.swapaxes(b_im, -1, -2)
    lam_re_r, lam_im_r = dup(lam_re)[:, :, None, :], dup(lam_im)[:, :, None, :]
    ls_r = jnp.broadcast_to(log_step[..., None, None], (2, n_g, 1, 2 * n_state))
    btr_re, btr_im = dup(bt_re), dup(bt_im)
    dpad = jnp.pad(d_skip.reshape(n_g, 1, n_ch), ((0, 0), (0, 0), ((q - 1) * n_ch, width)))

    def dspec(shape):
        return pl.BlockSpec((2, None) + shape, lambda g: (0, g, 0, 0))

    body = functools.partial(_sg_prep_body, n_state=n_state, n_ch=n_ch)
    m4, b4, c5, aq = pl.pallas_call(
        body,
        grid=(n_g,),
        in_specs=[dspec((n_state, 1)), dspec((n_state, 1)), dspec((1, 1)),
                  dspec((n_state, width)), dspec((n_state, width)),
                  dspec((n_ch, n_state)), dspec((n_ch, n_state)),
                  dspec((1, 2 * n_state)), dspec((1, 2 * n_state)), dspec((1, 2 * n_state)),
                  dspec((n_ch, 2 * n_state)), dspec((n_ch, 2 * n_state)),
                  pl.BlockSpec((None, 1, 2 * width), lambda g: (g, 0, 0))],
        out_specs=[
            pl.BlockSpec((None, q, n_ch, q * LANES), lambda g: (g // per_blk, 0, g % per_blk, 0)),
            pl.BlockSpec((None, q, n_ch, sw), lambda g: (g // per_blk, 0, g % per_blk, 0)),
            pl.BlockSpec((None, 4, None, n_state, q * LANES),
                         lambda g: (g // per_blk, 0, g % per_blk, 0, 0)),
            pl.BlockSpec((None, 1, 4 * n_state), lambda g: (g, 0, 0))],
        out_shape=[jax.ShapeDtypeStruct((n_blk, q, LANES, q * LANES), BF16),
                   jax.ShapeDtypeStruct((n_blk, q, LANES, sw), BF16),
                   jax.ShapeDtypeStruct((n_blk, 4, per_blk, n_state, q * LANES), BF16),
                   jax.ShapeDtypeStruct((n_g, 1, 4 * n_state), F32)],
        compiler_params=_cparams(1),
        name="ssm_prep",
    )(lam_re_c, lam_im_c, ls_c, ct_re, ct_im, bt_re, bt_im,
      lam_re_r, lam_im_r, ls_r, btr_re, btr_im, dpad)
    intra = m4.reshape(n_blk, q * LANES, q * LANES)
    bst = b4.reshape(n_blk, q * LANES, sw)
    cst = c5.reshape(n_blk, sw, q * LANES)
    a = aq.reshape(n_blk, per_blk // 2, 2, 2, 2, n_state)
    a = a.transpose(3, 4, 0, 1, 2, 5).reshape(2, 2, n_blk, per_blk // 2, LANES)
    coef = []
    for d in range(2):
        re, im = a[d, 0], a[d, 1]
        coef.append((jnp.concatenate([re, re], axis=1), jnp.concatenate([-im, im], axis=1)))
    return intra, bst, cst, coef


def _sg_mm_intra_body(x_ref, m_ref, y_ref):
    y_ref[...] = jnp.dot(x_ref[...], m_ref[...], preferred_element_type=F32).astype(BF16)


def _sg_mm_intra(xg, intra, *, rt):
    nc = xg.shape[0]
    n_blk, kw, _ = intra.shape
    return pl.pallas_call(
        _sg_mm_intra_body,
        grid=(n_blk, nc // rt),
        in_specs=[pl.BlockSpec((rt, kw), lambda b, i: (i, b)),
                  pl.BlockSpec((None, kw, kw), lambda b, i: (b, 0, 0))],
        out_specs=pl.BlockSpec((rt, kw), lambda b, i: (i, b)),
        out_shape=jax.ShapeDtypeStruct((nc, n_blk * kw), BF16),
        compiler_params=_cparams(2),
        name="ssm_intra",
    )(xg, intra)


def _sg_mm_state_body(x_ref, xm_ref, b_ref, sf_ref, sb_ref, init_ref, *, rt, n_sub):
    r = jnp.dot(x_ref[...], b_ref[...], preferred_element_type=F32)
    for k in range(n_sub):
        sf_ref[pl.ds(k, rt, stride=n_sub), :] = r[:, k * LANES:(k + 1) * LANES]
        sb_ref[pl.ds(k, rt, stride=n_sub), :] = r[:, (n_sub + k) * LANES:(n_sub + k + 1) * LANES]

    @pl.when(pl.program_id(1) == 0)
    def _():
        r0 = jnp.dot(xm_ref[...], b_ref[...], preferred_element_type=F32)
        for k in range(n_sub):
            init_ref[k:k + 1, :] = r0[0:1, k * LANES:(k + 1) * LANES]


def _sg_mm_state(xg, xg_meta, bst, *, rt):
    nc = xg.shape[0]
    n_blk, kw, sw = bst.shape
    n_sub = sw // (2 * LANES)
    n_rt = nc // rt
    body = functools.partial(_sg_mm_state_body, rt=rt, n_sub=n_sub)
    dense = jax.ShapeDtypeStruct((n_blk * nc * n_sub, LANES), F32)
    dspec = pl.BlockSpec((rt * n_sub, LANES), lambda b, i: (b * n_rt + i, 0))
    return pl.pallas_call(
        body,
        grid=(n_blk, n_rt),
        in_specs=[pl.BlockSpec((rt, kw), lambda b, i: (i, b)),
                  pl.BlockSpec((CHUNK, kw), lambda b, i: (0, b)),
                  pl.BlockSpec((None, kw, sw), lambda b, i: (b, 0, 0))],
        out_specs=[dspec, dspec, pl.BlockSpec((None, n_sub, LANES), lambda b, i: (b, 0, 0))],
        out_shape=[dense, dense, jax.ShapeDtypeStruct((n_blk, n_sub, LANES), F32)],
        compiler_params=_cparams(2),
        name="ssm_state",
    )(xg, xg_meta, bst)


def _sg_scan_body(sf_ref, sb_ref, a1f_ref, a2f_ref, a1b_ref, a2b_ref, init_ref, xf_ref, xb_ref,
                  stf, stb, *, cb, n_blk, half, n_pblk, blk_pseq, blk_sseq):
    j = pl.program_id(0)
    pos = jnp.where(j < n_pblk, lax.rem(j, blk_pseq), lax.rem(jnp.maximum(j - n_pblk, 0), blk_sseq))

    @pl.when(pos == 0)
    def _():
        stf[...] = init_ref[...]
        stb[...] = jnp.zeros_like(stb)

    def body(i, carry):
        ef, eb = carry
        ib = cb - 1 - i
        nf, nb = [], []
        for g in range(n_blk):
            xf_ref[g, i] = ef[g]
            xb_ref[g, ib] = eb[g]
            nf.append(a1f_ref[g] * ef[g] + a2f_ref[g] * pltpu.roll(ef[g], half, 0) + sf_ref[g, i])
            nb.append(a1b_ref[g] * eb[g] + a2b_ref[g] * pltpu.roll(eb[g], half, 0) + sb_ref[g, ib])
        return tuple(nf), tuple(nb)

    ef, eb = lax.fori_loop(0, cb, body, (tuple(stf[g] for g in range(n_blk)),
                                         tuple(stb[g] for g in range(n_blk))))
    for g in range(n_blk):
        stf[g] = ef[g]
        stb[g] = eb[g]


def _sg_scan(sf, sb, coef, init, *, cb, chunks_pseq, chunks_sseq, n_pchunks):
    n_blk, nc, n_sub, _ = sf.shape
    n_pblk, blk_pseq, blk_sseq = n_pchunks // cb, chunks_pseq // cb, chunks_sseq // cb

    def bwd_block(j):
        in_p = j < n_pblk
        pos = jnp.where(in_p, lax.rem(j, blk_pseq), lax.rem(jnp.maximum(j - n_pblk, 0), blk_sseq))
        ln = jnp.where(in_p, blk_pseq, blk_sseq)
        return j - pos + ln - 1 - pos

    body = functools.partial(_sg_scan_body, cb=cb, n_blk=n_blk, half=n_sub // 2, n_pblk=n_pblk,
                             blk_pseq=blk_pseq, blk_sseq=blk_sseq)
    fwd = pl.BlockSpec((n_blk, cb, n_sub, LANES), lambda j: (0, j, 0, 0))
    bwd = pl.BlockSpec((n_blk, cb, n_sub, LANES), lambda j: (0, bwd_block(j), 0, 0))
    small = pl.BlockSpec((n_blk, n_sub, LANES), lambda j: (0, 0, 0))
    return pl.pallas_call(
        body,
        grid=(nc // cb,),
        in_specs=[fwd, bwd, small, small, small, small, small],
        out_specs=[fwd, bwd],
        out_shape=[jax.ShapeDtypeStruct(sf.shape, F32), jax.ShapeDtypeStruct(sb.shape, F32)],
        scratch_shapes=[pltpu.VMEM((n_blk, n_sub, LANES), F32), pltpu.VMEM((n_blk, n_sub, LANES), F32)],
        compiler_params=_cparams(1),
        name="ssm_scan",
    )(sf, sb, coef[0][0], coef[0][1], coef[1][0], coef[1][1], init)


def _gelu_tanh(x):
    return 0.5 * x * (1.0 + jnp.tanh(0.7978845608028654 * (x + 0.044715 * (x * x * x))))


def _sg_mm_out_body(y_ref, xf_ref, xb_ref, c_ref, z_ref, *, rt, n_sub):
    cols = [xf_ref[pl.ds(k, rt, stride=n_sub), :] for k in range(n_sub)]
    cols += [xb_ref[pl.ds(k, rt, stride=n_sub), :] for k in range(n_sub)]
    state = jnp.concatenate(cols, axis=1).astype(BF16)
    y = y_ref[...].astype(F32) + jnp.dot(state, c_ref[...], preferred_element_type=F32)
    z_ref[...] = _gelu_tanh(y).astype(BF16)


def _sg_mm_out(yi, xf, xb, cst, *, rt):
    nc = yi.shape[0]
    n_blk, sw, kw = cst.shape
    n_sub = sw // (2 * LANES)
    n_rt = nc // rt
    dspec = pl.BlockSpec((rt * n_sub, LANES), lambda b, i: (b * n_rt + i, 0))
    return pl.pallas_call(
        functools.partial(_sg_mm_out_body, rt=rt, n_sub=n_sub),
        grid=(n_blk, n_rt),
        in_specs=[pl.BlockSpec((rt, kw), lambda b, i: (i, b)), dspec, dspec,
                  pl.BlockSpec((None, sw, kw), lambda b, i: (b, 0, 0))],
        out_specs=pl.BlockSpec((rt, kw), lambda b, i: (i, b)),
        out_shape=jax.ShapeDtypeStruct((nc, n_blk * kw), BF16),
        compiler_params=_cparams(2),
        name="ssm_out",
    )(yi, xf, xb, cst)


def _mix_body(xp_ref, xs_ref, z_ref, c_ref, cprev_ref, cnext_ref, cmeta_ref, sga_ref, sgb_ref,
              wglu_ref, wpw_ref, wout_ref, wdw_ref, bdw_ref, lng_ref, lnb_ref, gffn_ref,
              rw_ref, rb_ref,
              h1_ref, v_ref, route_ref, cnt_ref,
              cw_ref, conv_ref, zsc_ref, *,
              tm, n_p_tiles, tiles_per_pseq, tiles_per_sseq, d_model, d_ssm, d_conv, conv_w,
              n_grp, n_exp, exp_per_grp, rc):
    i = pl.program_id(0)
    in_prompt = i < n_p_tiles
    x = jnp.where(in_prompt, xp_ref[...], xs_ref[...])
    pos_p = lax.rem(i, tiles_per_pseq)
    pos_s = lax.rem(jnp.maximum(i - n_p_tiles, 0), tiles_per_sseq)
    is_start = jnp.where(in_prompt, pos_p == 0, pos_s == 0)
    is_end = jnp.where(in_prompt, pos_p == tiles_per_pseq - 1, pos_s == tiles_per_sseq - 1)

    halo = CHUNK
    pad = conv_w // 2
    cw_ref[0:halo, :] = jnp.where(is_start, cmeta_ref[...], cprev_ref[...]).astype(F32)
    cw_ref[halo:halo + tm, :] = c_ref[...].astype(F32)
    cw_ref[halo + tm:2 * halo + tm, :] = jnp.where(is_end, 0.0, cnext_ref[...].astype(F32))
    sub = 8
    for lc in range(d_conv // LANES):
        ls = slice(lc * LANES, (lc + 1) * LANES)
        for r0 in range(0, tm, rc):
            out = None
            for r in range(sub):
                part = None
                for q in range((conv_w + halo - pad) // sub + 1):
                    k = sub * q + r - (halo - pad)
                    if 0 <= k < conv_w:
                        term = wdw_ref[k:k + 1, ls] * cw_ref[r0 + sub * q:r0 + sub * q + rc + sub, ls]
                        part = term if part is None else part + term
                if part is not None:
                    shifted = part[r:r + rc]
                    out = shifted if out is None else out + shifted
            conv_ref[r0:r0 + rc, ls] = out
    cc = conv_ref[...] + bdw_ref[...]
    mu = jnp.mean(cc, axis=-1, keepdims=True)
    var = jnp.mean(jnp.square(cc - mu), axis=-1, keepdims=True)
    cc = (cc - mu) * lax.rsqrt(var + EPS) * lng_ref[...] + lnb_ref[...]
    cc = (cc * _sigmoid(cc)).astype(BF16)
    y_b = jnp.dot(cc, wpw_ref[...], preferred_element_type=F32)

    for b in range(d_ssm // LANES):
        for s in range(CHUNK):
            c0 = (b * CHUNK + s) * LANES
            zsc_ref[b, pl.ds(s, tm // CHUNK, stride=CHUNK), :] = z_ref[:, c0:c0 + LANES].astype(F32)
    z = jnp.concatenate([zsc_ref[b] for b in range(d_ssm // LANES)], axis=1).astype(BF16)
    va = jnp.dot(z, wglu_ref[:, 0:d_model], preferred_element_type=F32)
    ga = jnp.dot(z, wglu_ref[:, d_model:2 * d_model], preferred_element_type=F32)
    y_a = va * _sigmoid(ga)
    merged = (sga_ref[...].astype(F32) * y_a + sgb_ref[...].astype(F32) * y_b).astype(BF16)
    h1 = x + jnp.dot(merged, wout_ref[...], preferred_element_type=F32)
    h1_ref[...] = h1
    v = _rms(h1, gffn_ref[...])
    half = d_model // 2
    v_ref[...] = _pack_bf16_pair(v[:, 0:half], v[:, half:d_model])

    v_hi = v.astype(BF16)
    v_lo = (v - v_hi.astype(F32)).astype(BF16)
    acc = (jnp.dot(v_hi, rw_ref[...], preferred_element_type=F32)
           + jnp.dot(v_lo, rw_ref[...], preferred_element_type=F32))
    logits = acc + pltpu.roll(acc, LANES // 2, 1) + rb_ref[...]
    lane = lax.broadcasted_iota(I32, (tm, LANES), 1).astype(F32)
    big = jnp.float32(1e9)
    neg = jnp.float32(-jnp.inf)
    gmask = lane < n_grp
    lg = jnp.where(gmask, logits, neg)
    gmax = jnp.max(lg, axis=-1, keepdims=True)
    grp = jnp.min(jnp.where(lg == gmax, lane, big), axis=-1, keepdims=True)
    p_grp = 1.0 / jnp.sum(jnp.where(gmask, jnp.exp(logits - gmax), 0.0), axis=-1, keepdims=True)
    lo = n_grp + grp * exp_per_grp
    emask = (lane >= lo) & (lane < lo + exp_per_grp)
    le = jnp.where(emask, logits, neg)
    m1 = jnp.max(le, axis=-1, keepdims=True)
    i1 = jnp.min(jnp.where(le == m1, lane, big), axis=-1, keepdims=True)
    le2 = jnp.where(lane == i1, neg, le)
    m2 = jnp.max(le2, axis=-1, keepdims=True)
    i2 = jnp.min(jnp.where(le2 == m2, lane, big), axis=-1, keepdims=True)
    t = jnp.exp(m2 - m1)
    w1 = 1.0 / (1.0 + t)
    e1 = i1 - n_grp
    e2 = i2 - n_grp
    route_ref[...] = jnp.where(lane == 0, e1, jnp.where(lane == 1, e2, jnp.where(
        lane == 2, p_grp * w1, jnp.where(lane == 3, p_grp * (t * w1), 0.0))))

    @pl.when(i == 0)
    def _():
        cnt_ref[...] = jnp.zeros_like(cnt_ref)

    hot = jnp.where((lane == e1) | (lane == e2), 1.0, 0.0)
    cnt_ref[...] += jnp.sum(hot, axis=0, keepdims=True)


def _mix(xp2, xs2, z, c, c_meta, sga, sgb, wglu, wpw, wout, wdw, bdw, lng, lnb, gffn, rw, rb, *,
         tm, p_seq, s_seq, n_grp, n_exp):
    n_p, d_model = xp2.shape
    n_s = xs2.shape[0]
    t = n_p + n_s
    d_ssm, d_conv = z.shape[1] // CHUNK, c.shape[1]
    conv_w = wdw.shape[0]
    n_p_tiles, n_s_tiles = n_p // tm, n_s // tm
    hpt = tm // CHUNK
    n_hblk = t // CHUNK
    body = functools.partial(
        _mix_body, tm=tm, n_p_tiles=n_p_tiles, tiles_per_pseq=p_seq // tm,
        tiles_per_sseq=s_seq // tm, d_model=d_model, d_ssm=d_ssm, d_conv=d_conv, conv_w=conv_w,
        n_grp=n_grp, n_exp=n_exp, exp_per_grp=n_exp // n_grp, rc=min(128, tm))
    row = lambda i: (i, 0)
    return pl.pallas_call(
        body,
        grid=(n_p_tiles + n_s_tiles,),
        in_specs=[
            pl.BlockSpec((tm, d_model), lambda i: (jnp.minimum(i, n_p_tiles - 1), 0)),
            pl.BlockSpec((tm, d_model), lambda i: (jnp.maximum(i - n_p_tiles, 0), 0)),
            pl.BlockSpec((tm // CHUNK, CHUNK * d_ssm), row),
            pl.BlockSpec((tm, d_conv), row),
            pl.BlockSpec((CHUNK, d_conv), lambda i: (jnp.maximum(i * hpt - 1, 0), 0)),
            pl.BlockSpec((CHUNK, d_conv), lambda i: (jnp.minimum((i + 1) * hpt, n_hblk - 1), 0)),
            _resident(c_meta.shape),
            pl.BlockSpec((tm, d_model), row),
            pl.BlockSpec((tm, d_model), row),
            _resident(wglu.shape), _resident(wpw.shape), _resident(wout.shape),
            _resident(wdw.shape), _resident(bdw.shape), _resident(lng.shape), _resident(lnb.shape),
            _resident(gffn.shape), _resident(rw.shape), _resident(rb.shape),
        ],
        out_specs=[pl.BlockSpec((tm, d_model), row), pl.BlockSpec((tm, d_model // 2), row),
                   pl.BlockSpec((tm, LANES), row), pl.BlockSpec((1, LANES), lambda i: (0, 0))],
        out_shape=[jax.ShapeDtypeStruct((t, d_model), F32),
                   jax.ShapeDtypeStruct((t, d_model // 2), U32),
                   jax.ShapeDtypeStruct((t, LANES), F32),
                   jax.ShapeDtypeStruct((1, LANES), F32)],
        scratch_shapes=[pltpu.VMEM((tm + 2 * CHUNK, d_conv), F32), pltpu.VMEM((tm, d_conv), F32),
                        pltpu.VMEM((d_ssm // LANES, tm, LANES), F32)],
        compiler_params=_cparams(1),
        name="mix",
    )(xp2, xs2, z, c, c, c, c_meta, sga, sgb, wglu, wpw, wout, wdw, bdw, lng, lnb, gffn, rw, rb)


def _positions_body(route_ref, offs_ref, dest_ref, carry_ref, *, te):
    @pl.when(pl.program_id(0) == 0)
    def _():
        carry_ref[...] = jnp.zeros_like(carry_ref)

    lane = lax.broadcasted_iota(I32, (te, LANES), 1).astype(F32)
    r = route_ref[...]
    oh1 = lane == r[:, 0:1]
    oh2 = lane == r[:, 1:2]
    both = jnp.where(oh1 | oh2, 1.0, 0.0)
    tri = jnp.where(lax.broadcasted_iota(I32, (te, te), 0) > lax.broadcasted_iota(I32, (te, te), 1),
                    1.0, 0.0).astype(BF16)
    before = jnp.dot(tri, both.astype(BF16), preferred_element_type=F32)
    base = before + carry_ref[...] + offs_ref[...]
    d1 = jnp.sum(jnp.where(oh1, base, 0.0), axis=-1, keepdims=True)
    d2 = jnp.sum(jnp.where(oh2, base, 0.0), axis=-1, keepdims=True)
    dest_ref[...] = jnp.where(lane == 0, d1, jnp.where(lane == 1, d2, 0.0)).astype(I32)
    carry_ref[...] += jnp.sum(both, axis=0, keepdims=True)


def _positions(route, offs, *, te):
    t = route.shape[0]
    return pl.pallas_call(
        functools.partial(_positions_body, te=te),
        grid=(t // te,),
        in_specs=[pl.BlockSpec((te, LANES), lambda i: (i, 0)),
                  pl.BlockSpec((1, LANES), lambda i: (0, 0))],
        out_specs=pl.BlockSpec((te, LANES), lambda i: (i, 0)),
        out_shape=jax.ShapeDtypeStruct((t, LANES), I32),
        scratch_shapes=[pltpu.VMEM((1, LANES), F32)],
        compiler_params=_cparams(1),
        name="positions",
    )(route, offs)


def _row_copy(src_ref, src_row, dst_ref, dst_row, sem):
    return pltpu.make_async_copy(src_ref.at[pl.ds(src_row, 1), :],
                                 dst_ref.at[pl.ds(dst_row, 1), :], sem)


def _dispatch_body(dest_hbm, v_ref, xs_in, xs_hbm, dsm, sem, dsem, *, td):
    del xs_in
    i = pl.program_id(0)
    cp = pltpu.make_async_copy(dest_hbm.at[pl.ds(i, 1), :], dsm, dsem)
    cp.start()
    cp.wait()

    def issue(r, _):
        _row_copy(v_ref, r, xs_hbm, dsm[0, 2 * r], sem).start(priority=0)
        _row_copy(v_ref, r, xs_hbm, dsm[0, 2 * r + 1], sem).start(priority=1)
        return 0
    lax.fori_loop(0, td, issue, 0, unroll=8)

    def drain(r, _):
        _row_copy(v_ref, 0, xs_hbm, 0, sem).wait()
        _row_copy(v_ref, 0, xs_hbm, 0, sem).wait()
        return 0
    lax.fori_loop(0, td, drain, 0)


def _dispatch(dest2, v, xs0, *, td):
    t, w = v.shape
    return pl.pallas_call(
        functools.partial(_dispatch_body, td=td),
        grid=(t // td,),
        in_specs=[pl.BlockSpec(memory_space=pl.ANY),
                  pl.BlockSpec((td, w), lambda i: (i, 0)),
                  pl.BlockSpec(memory_space=pl.ANY)],
        out_specs=pl.BlockSpec(memory_space=pl.ANY),
        out_shape=jax.ShapeDtypeStruct(xs0.shape, U32),
        scratch_shapes=[pltpu.SMEM((1, 2 * td), I32), pltpu.SemaphoreType.DMA,
                        pltpu.SemaphoreType.DMA],
        input_output_aliases={2: 0},
        compiler_params=_cparams(1),
        name="dispatch",
    )(dest2, v, xs0)


def _experts_body(te_ref, first_ref, last_ref, nxt_ref, nused_ref,
                  xs_ref, w1_hbm, w3_hbm, w2_hbm, o_ref,
                  w1b, w3b, w2b, st1, st3, st2, sem, *, half, n_conv):
    i = pl.program_id(0)
    used = i < nused_ref[0]
    nxt = nxt_ref[i]

    def copies(e):
        return (pltpu.make_async_copy(w1_hbm.at[e], st1, sem.at[0]),
                pltpu.make_async_copy(w3_hbm.at[e], st3, sem.at[1]),
                pltpu.make_async_copy(w2_hbm.at[e], st2, sem.at[2]))

    def convert():
        r13 = st1.shape[0] // n_conv
        r2 = st2.shape[0] // n_conv

        def body(c, _):
            rows = pl.ds(pl.multiple_of(c * r13, r13), r13)
            w1b[rows, :] = st1[rows, :].astype(BF16)
            w3b[rows, :] = st3[rows, :].astype(BF16)
            rows2 = pl.ds(pl.multiple_of(c * r2, r2), r2)
            w2b[rows2, :] = st2[rows2, :].astype(BF16)
            return 0
        lax.fori_loop(0, n_conv, body, 0)

    @pl.when(i == 0)
    def _():
        for cp in copies(te_ref[0]):
            cp.start()
        for cp in copies(te_ref[0]):
            cp.wait()
        convert()

    @pl.when(used & (first_ref[i] == 1) & (nxt >= 0))
    def _():
        for cp in copies(nxt):
            cp.start()

    @pl.when(used)
    def _():
        a, b = _unpack_bf16_pair(xs_ref[...])
        a, b = a.astype(BF16), b.astype(BF16)

        def up(w_ref):
            return (jnp.dot(a, w_ref[0:half, :], preferred_element_type=F32)
                    + jnp.dot(b, w_ref[half:2 * half, :], preferred_element_type=F32))
        h1 = up(w1b)
        act = (h1 * _sigmoid(h1) * up(w3b)).astype(BF16)
        o = jnp.dot(act, w2b[...], preferred_element_type=F32)
        o_ref[...] = _pack_bf16_pair(o[:, 0:half], o[:, half:2 * half])

    @pl.when(jnp.logical_not(used))
    def _():
        o_ref[...] = jnp.zeros_like(o_ref)

    @pl.when(used & (last_ref[i] == 1) & (nxt >= 0))
    def _():
        for cp in copies(nxt):
            cp.wait()
        convert()


def _experts(tile_expert, first, last, nxt, n_used, xs, w1, w3, w2, *, tme):
    rows, half = xs.shape
    n_e, d_model, d_exp = w1.shape
    grid_spec = pltpu.PrefetchScalarGridSpec(
        num_scalar_prefetch=5,
        grid=(rows // tme,),
        in_specs=[pl.BlockSpec((tme, half), lambda i, *_: (i, 0)),
                  pl.BlockSpec(memory_space=pl.ANY), pl.BlockSpec(memory_space=pl.ANY),
                  pl.BlockSpec(memory_space=pl.ANY)],
        out_specs=pl.BlockSpec((tme, half), lambda i, *_: (i, 0)),
        scratch_shapes=[pltpu.VMEM((d_model, d_exp), BF16), pltpu.VMEM((d_model, d_exp), BF16),
                        pltpu.VMEM((d_exp, d_model), BF16),
                        pltpu.VMEM((d_model, d_exp), F32), pltpu.VMEM((d_model, d_exp), F32),
                        pltpu.VMEM((d_exp, d_model), F32),
                        pltpu.SemaphoreType.DMA((3,))],
    )
    return pl.pallas_call(
        functools.partial(_experts_body, half=half, n_conv=8),
        grid_spec=grid_spec,
        out_shape=jax.ShapeDtypeStruct((rows, half), U32),
        compiler_params=_cparams(1),
        name="experts",
    )(tile_expert, first, last, nxt, n_used, xs, w1, w3, w2)


def _combine_body(dest_hbm, o_hbm, h1_ref, route_ref, g_ref, y_ref, dsm, gbuf, sem, dsem, *,
                  tf, tile0, half):
    i = pl.program_id(0)
    cp = pltpu.make_async_copy(dest_hbm.at[pl.ds(tile0 + i, 1), :], dsm, dsem)
    cp.start()
    cp.wait()

    def issue(r, _):
        _row_copy(o_hbm, dsm[0, 2 * r], gbuf.at[0], r, sem).start(priority=0)
        _row_copy(o_hbm, dsm[0, 2 * r + 1], gbuf.at[1], r, sem).start(priority=1)
        return 0
    lax.fori_loop(0, tf, issue, 0, unroll=8)

    def drain(r, _):
        _row_copy(o_hbm, 0, gbuf.at[0], 0, sem).wait()
        _row_copy(o_hbm, 0, gbuf.at[0], 0, sem).wait()
        return 0
    lax.fori_loop(0, tf, drain, 0)

    route = route_ref[...]
    g1, g2 = route[:, 2:3], route[:, 3:4]
    a1, b1 = _unpack_bf16_pair(gbuf[0])
    a2, b2 = _unpack_bf16_pair(gbuf[1])
    h1 = h1_ref[...]
    ha = h1[:, 0:half] + (a1 * g1 + a2 * g2)
    hb = h1[:, half:2 * half] + (b1 * g1 + b2 * g2)
    ms = (jnp.sum(ha * ha, axis=-1, keepdims=True)
          + jnp.sum(hb * hb, axis=-1, keepdims=True)) / (2 * half)
    inv = lax.rsqrt(ms + EPS)
    y_ref[:, 0:half] = ha * inv * g_ref[:, 0:half]
    y_ref[:, half:2 * half] = hb * inv * g_ref[:, half:2 * half]


def _combine(dest2, o, h1, route, g, *, tf, row0, n_rows):
    d_model = h1.shape[1]
    half = d_model // 2
    tile0 = row0 // tf
    return pl.pallas_call(
        functools.partial(_combine_body, tf=tf, tile0=tile0, half=half),
        grid=(n_rows // tf,),
        in_specs=[pl.BlockSpec(memory_space=pl.ANY), pl.BlockSpec(memory_space=pl.ANY),
                  pl.BlockSpec((tf, d_model), lambda i: (tile0 + i, 0)),
                  pl.BlockSpec((tf, LANES), lambda i: (tile0 + i, 0)),
                  pl.BlockSpec((1, d_model), lambda i: (0, 0))],
        out_specs=pl.BlockSpec((tf, d_model), lambda i: (i, 0)),
        out_shape=jax.ShapeDtypeStruct((n_rows, d_model), F32),
        scratch_shapes=[pltpu.SMEM((1, 2 * tf), I32), pltpu.VMEM((2, tf, half), U32),
                        pltpu.SemaphoreType.DMA, pltpu.SemaphoreType.DMA],
        compiler_params=_cparams(1),
        name="combine",
    )(dest2, o, h1, route, g)


def kernel(x_prompt, x_sample, meta, norm_mix_g, w_in, ssm_lam_re, ssm_lam_im, ssm_log_step, ssm_b_re, ssm_b_im, ssm_c_re, ssm_c_im, ssm_d, ssm_w_glu, conv_w_dw, conv_b_dw, conv_ln_g, conv_ln_b, conv_w_pw, w_out, norm_ffn_g, router_group_w, router_group_b, router_expert_w, router_expert_b, expert_w1, expert_w3, expert_w2, final_g):
    assert w_in.shape[0] == 1, "single-layer trunk"
    bp, lp, d_model = x_prompt.shape
    bs, ls, _ = x_sample.shape
    n_meta = meta.shape[0]
    d_ssm = ssm_d.shape[-1]
    d_conv = conv_b_dw.shape[-1]
    n_ch = ssm_b_re.shape[-1]
    n_g = ssm_b_re.shape[2]
    n_grp = router_group_w.shape[-1]
    n_exp = router_expert_w.shape[-1]
    assert n_meta == CHUNK and lp % CHUNK == 0 and ls % CHUNK == 0
    n_p, n_s = bp * lp, bs * ls
    t = n_p + n_s
    tm = min(256, lp, ls)
    assert lp % tm == 0 and ls % tm == 0

    xp2 = x_prompt.reshape(n_p, d_model)
    xs2 = x_sample.reshape(n_s, d_model)
    row = lambda a: a.reshape(1, -1)
    w_in_bf = w_in[0].astype(BF16)

    xg, c, sga, sgb = _inproj(xp2, xs2, row(norm_mix_g[0]), w_in_bf, d_ssm=d_ssm, d_conv=d_conv, tm=tm)
    xg_meta, c_meta = _meta_inproj(meta, row(norm_mix_g[0]), w_in_bf, d_ssm=d_ssm, d_conv=d_conv)

    assert n_ch == CHUNK and LANES % n_ch == 0 and n_g % (LANES // n_ch) == 0
    intra, bst, cst, coef = _sg_prep(ssm_lam_re[0], ssm_lam_im[0], ssm_log_step[0], ssm_b_re[0],
                                     ssm_b_im[0], ssm_c_re[0], ssm_c_im[0], ssm_d[0])
    nc = t // CHUNK
    n_blk = d_ssm // LANES
    rt = min(512, nc)
    yi = _sg_mm_intra(xg, intra, rt=rt)
    sf, sb, init = _sg_mm_state(xg, xg_meta, bst, rt=rt)
    n_sub = sf.shape[0] // (n_blk * nc)
    cb = min(64, lp // CHUNK, ls // CHUNK)
    xf, xb = _sg_scan(sf.reshape(n_blk, nc, n_sub, LANES), sb.reshape(n_blk, nc, n_sub, LANES),
                      coef, init, cb=cb, chunks_pseq=lp // CHUNK, chunks_sseq=ls // CHUNK,
                      n_pchunks=n_p // CHUNK)
    z = _sg_mm_out(yi, xf.reshape(sf.shape), xb.reshape(sb.shape), cst, rt=rt)

    assert n_grp + n_exp <= LANES // 2
    rw32 = jnp.pad(jnp.concatenate([router_group_w[0], router_expert_w[0]], axis=1),
                   ((0, 0), (0, LANES // 2 - n_grp - n_exp)))
    rw_hi = rw32.astype(BF16)
    rw = jnp.concatenate([rw_hi, (rw32 - rw_hi.astype(F32)).astype(BF16)], axis=1)
    rb = jnp.zeros((1, LANES), F32).at[0, 0:n_grp].set(router_group_b[0]).at[
        0, n_grp:n_grp + n_exp].set(router_expert_b[0])
    h1, v, route, cnt = _mix(
        xp2, xs2, z, c, c_meta, sga, sgb, ssm_w_glu[0].astype(BF16), conv_w_pw[0].astype(BF16),
        w_out[0].astype(BF16), conv_w_dw[0], row(conv_b_dw[0]), row(conv_ln_g[0]),
        row(conv_ln_b[0]), row(norm_ffn_g[0]), rw, rb,
        tm=tm, p_seq=lp, s_seq=ls, n_grp=n_grp, n_exp=n_exp)

    tme = 256
    counts = cnt[0, 0:n_exp].astype(I32)
    tiles_e = (counts + tme - 1) // tme
    tile_end = jnp.cumsum(tiles_e)
    n_used = tile_end[-1]
    offs = jnp.zeros((1, LANES), F32).at[0, 0:n_exp].set(((tile_end - tiles_e) * tme).astype(F32))
    n_tiles = (2 * t) // tme + n_exp
    ids = jnp.arange(n_tiles, dtype=I32)
    te_map = jnp.sum((jnp.minimum(ids, n_used - 1)[:, None] >= tile_end[None, :]).astype(I32), axis=1)
    te_map = jnp.minimum(te_map, n_exp - 1)
    run_start = (tile_end - tiles_e)[te_map]
    run_end = tile_end[te_map]
    valid = ids < n_used
    first = (valid & (ids == run_start)).astype(I32)
    last = (valid & (ids == run_end - 1)).astype(I32)
    nxt = jnp.where(valid & (run_end < n_used), te_map[jnp.minimum(run_end, n_tiles - 1)], -1)

    td = min(512, t)
    dest = _positions(route, offs, te=td)
    dest2 = dest[:, 0:2].reshape(t // td, 2 * td)
    xs0 = jnp.zeros((n_tiles * tme, d_model // 2), U32)
    xs = _dispatch(dest2, v, xs0, td=td)
    o = _experts(te_map, first, last, nxt, n_used.reshape(1), xs, expert_w1[0], expert_w3[0],
                 expert_w2[0], tme=tme)

    tf = min(256, n_p, n_s)
    dest2f = dest[:, 0:2].reshape(t // tf, 2 * tf)
    fg = row(final_g)
    y_p = _combine(dest2f, o, h1, route, fg, tf=tf, row0=0, n_rows=n_p)
    y_s = _combine(dest2f, o, h1, route, fg, tf=tf, row0=n_p, n_rows=n_s)
    return (y_p.reshape(bp, lp, d_model), y_s.reshape(bs, ls, d_model))
```

```python
import functools

import jax
import jax.numpy as jnp
from jax import lax
from jax.experimental import pallas as pl
from jax.experimental.pallas import tpu as pltpu
from jax.experimental.pallas import tpu_sc as plsc

F32 = jnp.float32
BF16 = jnp.bfloat16
U32 = jnp.uint32
I32 = jnp.int32

EPS = 1e-6
LAM_RE_MAX = -1e-4
CHUNK = 16
LANES = 128
VMEM_LIMIT = 56 << 20
SC_CORES = 2
SC_SUBCORES = 16
HIGHEST = lax.Precision.HIGHEST


def _cparams(n_axes):
    return pltpu.CompilerParams(dimension_semantics=("arbitrary",) * n_axes,
                                vmem_limit_bytes=VMEM_LIMIT)


def _resident(shape):
    nd = len(shape)
    return pl.BlockSpec(shape, lambda *_: (0,) * nd, pipeline_mode=pl.Buffered(1))


def _sigmoid(x):
    return 1.0 / (1.0 + jnp.exp(-x))


def _rms(x, g):
    return x * lax.rsqrt(jnp.mean(x * x, axis=-1, keepdims=True) + EPS) * g


def _pack_bf16_pair(a, b):
    def rnd(x):
        u = pltpu.bitcast(x, U32)
        return (u + jnp.uint32(0x7FFF) + ((u >> 16) & jnp.uint32(1))) >> 16
    return (rnd(a) << 16) | rnd(b)


def _unpack_bf16_pair(p):
    a = pltpu.bitcast(p & jnp.uint32(0xFFFF0000), F32)
    b = pltpu.bitcast(p << 16, F32)
    return a, b


def _inproj_body(xp_ref, xs_ref, g_ref, w_ref, xg_ref, c_ref, sga_ref, sgb_ref, usc, *,
                 tm, n_p_tiles, d_ssm, d_conv, d_model, col):
    i = pl.program_id(0)
    x = jnp.where(i < n_p_tiles, xp_ref[...], xs_ref[...])
    y = _rms(x, g_ref[...]).astype(BF16)

    def proj(lo, n):
        return jnp.dot(y, w_ref[:, lo:lo + n], preferred_element_type=F32)

    for k in range(d_ssm // col):
        u = proj(k * col, col)
        for b in range(col // LANES):
            usc[k * (col // LANES) + b] = u[:, b * LANES:(b + 1) * LANES]
    for b in range(d_ssm // LANES):
        for s in range(CHUNK):
            c0 = (b * CHUNK + s) * LANES
            xg_ref[:, c0:c0 + LANES] = usc[b, pl.ds(s, tm // CHUNK, stride=CHUNK), :].astype(BF16)
    for k in range(d_conv // col):
        cv = proj(d_ssm + k * col, col)
        cg = proj(d_ssm + d_conv + k * col, col)
        c_ref[:, k * col:(k + 1) * col] = (cv * _sigmoid(cg)).astype(BF16)
    base = d_ssm + 2 * d_conv
    for k in range(d_model // col):
        sga_ref[:, k * col:(k + 1) * col] = _sigmoid(proj(base + k * col, col)).astype(BF16)
        sgb_ref[:, k * col:(k + 1) * col] = _sigmoid(
            proj(base + d_model + k * col, col)).astype(BF16)


def _inproj(xp2, xs2, g, w_bf, *, d_ssm, d_conv, tm):
    n_p, d_model = xp2.shape
    n_s = xs2.shape[0]
    n_p_tiles, n_s_tiles = n_p // tm, n_s // tm
    t = n_p + n_s
    col = min(1024, d_ssm, d_conv, d_model)
    body = functools.partial(_inproj_body, tm=tm, n_p_tiles=n_p_tiles, d_ssm=d_ssm, d_conv=d_conv,
                             d_model=d_model, col=col)
    row = lambda i: (i, 0)
    return pl.pallas_call(
        body,
        grid=(n_p_tiles + n_s_tiles,),
        in_specs=[
            pl.BlockSpec((tm, d_model), lambda i: (jnp.minimum(i, n_p_tiles - 1), 0)),
            pl.BlockSpec((tm, d_model), lambda i: (jnp.maximum(i - n_p_tiles, 0), 0)),
            _resident((1, d_model)),
            _resident(w_bf.shape),
        ],
        out_specs=[pl.BlockSpec((tm // CHUNK, CHUNK * d_ssm), row), pl.BlockSpec((tm, d_conv), row),
                   pl.BlockSpec((tm, d_model), row), pl.BlockSpec((tm, d_model), row)],
        out_shape=[jax.ShapeDtypeStruct((t // CHUNK, CHUNK * d_ssm), BF16),
                   jax.ShapeDtypeStruct((t, d_conv), BF16),
                   jax.ShapeDtypeStruct((t, d_model), BF16), jax.ShapeDtypeStruct((t, d_model), BF16)],
        scratch_shapes=[pltpu.VMEM((d_ssm // LANES, tm, LANES), F32)],
        compiler_params=_cparams(1),
        name="inproj",
    )(xp2, xs2, g, w_bf)


def _meta_body(m_ref, g_ref, w_ref, xg_ref, c_ref, *, d_ssm, d_conv):
    y = _rms(m_ref[...], g_ref[...]).astype(BF16)
    u = jnp.dot(y, w_ref[:, 0:d_ssm], preferred_element_type=F32)
    first = lax.broadcasted_iota(I32, (CHUNK, LANES), 0) == 0
    for b in range(d_ssm // LANES):
        for s in range(CHUNK):
            c0 = (b * CHUNK + s) * LANES
            piece = jnp.broadcast_to(u[s:s + 1, b * LANES:(b + 1) * LANES], (CHUNK, LANES))
            xg_ref[:, c0:c0 + LANES] = jnp.where(first, piece, 0.0).astype(BF16)
    cv = jnp.dot(y, w_ref[:, d_ssm:d_ssm + d_conv], preferred_element_type=F32)
    cg = jnp.dot(y, w_ref[:, d_ssm + d_conv:d_ssm + 2 * d_conv], preferred_element_type=F32)
    c_ref[...] = (cv * _sigmoid(cg)).astype(BF16)


def _meta_inproj(meta, g, w_bf, *, d_ssm, d_conv):
    n_meta, d_model = meta.shape
    ncol = d_ssm + 2 * d_conv
    return pl.pallas_call(
        functools.partial(_meta_body, d_ssm=d_ssm, d_conv=d_conv),
        grid=(1,),
        in_specs=[pl.BlockSpec((n_meta, d_model), lambda i: (0, 0)),
                  pl.BlockSpec((1, d_model), lambda i: (0, 0)),
                  pl.BlockSpec((d_model, ncol), lambda i: (0, 0))],
        out_specs=[pl.BlockSpec((CHUNK, CHUNK * d_ssm), lambda i: (0, 0)),
                   pl.BlockSpec((n_meta, d_conv), lambda i: (0, 0))],
        out_shape=[jax.ShapeDtypeStruct((CHUNK, CHUNK * d_ssm), BF16),
                   jax.ShapeDtypeStruct((n_meta, d_conv), BF16)],
        compiler_params=_cparams(1),
        name="meta_inproj",
    )(meta, g, w_bf)


def _cmul(ar, ai, br, bi):
    return ar * br - ai * bi, ar * bi + ai * br


def _discretize(lam_re, lam_im, log_step):
    lr = jnp.minimum(lam_re, LAM_RE_MAX)
    dt = jnp.exp(log_step)
    mag = jnp.exp(lr * dt)
    ar = mag * jnp.cos(lam_im * dt)
    ai = mag * jnp.sin(lam_im * dt)
    den = lr * lr + lam_im * lam_im
    nr = ar - 1.0
    fr = (nr * lr + ai * lam_im) / den
    fi = (ai * lr - nr * lam_im) / den
    return ar, ai, fr, fi


def _cpow(ar, ai, k, nbits, shape):
    pr = jnp.ones(shape, F32)
    pi = jnp.zeros(shape, F32)
    br = jnp.broadcast_to(ar, shape)
    bi = jnp.broadcast_to(ai, shape)
    kk = jnp.broadcast_to(k, shape)
    for b in range(nbits):
        sel = ((kk >> b) & 1) == 1
        nr, ni = _cmul(pr, pi, br, bi)
        pr = jnp.where(sel, nr, pr)
        pi = jnp.where(sel, ni, pi)
        br, bi = _cmul(br, bi, br, bi)
    return pr, pi


def _ssm_prep_body(lam_re_c, lam_im_c, ls_c, ct_re, ct_im, bt_re, bt_im,
                   lam_re_r, lam_im_r, ls_r, btr_re, btr_im, dpad, xmeta,
                   w1_ref, cst_ref, aq_ref, init_ref, *, n_state, n_ch):
    q = CHUNK
    width = q * n_ch
    par = pl.program_id(0) % 2
    kblk = lax.broadcasted_iota(I32, (1, width), 1) // n_ch
    strips = []
    for d in range(2):
        ar, ai, fr, fi = _discretize(lam_re_c[d], lam_im_c[d], ls_c[d])
        kexp = kblk if d == 0 else (q - 1) - kblk
        wr, wi = _cpow(ar, ai, kexp, 4, (n_state, width))
        gcr, gci = _cmul(ct_re[d], ct_im[d], wr, wi)
        gfr, gfi = _cmul(gcr, gci, fr, fi)
        strips.append(jnp.dot(bt_re[d], gfr, precision=HIGHEST, preferred_element_type=F32)
                      - jnp.dot(bt_im[d], gfi, precision=HIGHEST, preferred_element_type=F32))
        g1r, g1i = _cmul(gcr, gci, ar, ai)
        rowmask = (lax.broadcasted_iota(I32, (2 * n_state, width), 0) // n_state) == par
        rep_r = jnp.concatenate([g1r, g1r], axis=0)
        rep_i = jnp.concatenate([g1i, g1i], axis=0)
        r0 = 4 * n_state * d
        cst_ref[r0:r0 + 2 * n_state, :] = jnp.where(rowmask, rep_r, 0.0).astype(BF16)
        cst_ref[r0 + 2 * n_state:r0 + 4 * n_state, :] = jnp.where(rowmask, -rep_i, 0.0).astype(BF16)

    zf, zb = strips
    zero = jnp.zeros((n_ch, width), F32)
    z512 = jnp.concatenate([zb, zero], axis=1) + pltpu.roll(
        jnp.concatenate([zf, zero], axis=1), (q - 1) * n_ch, 1)
    row = lax.broadcasted_iota(I32, (n_ch, 2 * width), 0)
    lane = lax.broadcasted_iota(I32, (n_ch, 2 * width), 1)
    z512 = z512 + jnp.where(lane - (q - 1) * n_ch == row, dpad[...], 0.0)
    for s in range(q):
        sh = (q - 1 - s) * n_ch
        blk = z512 if sh == 0 else pltpu.roll(z512, 2 * width - sh, 1)
        w1_ref[n_ch * s:n_ch * (s + 1), 0:width] = blk[:, 0:width].astype(BF16)

    parmask = (lax.broadcasted_iota(I32, (1, 2 * n_state), 1) // n_state) == par
    for d in range(2):
        ar, ai, fr, fi = _discretize(lam_re_r[d], lam_im_r[d], ls_r[d])
        pw = [(jnp.ones_like(ar), jnp.zeros_like(ar))]
        for _ in range(q):
            pw.append(_cmul(pw[-1][0], pw[-1][1], ar, ai))
        col = width + 4 * n_state * d
        for s in range(q):
            e = (q - 1 - s) if d == 0 else s
            cr, ci = _cmul(fr, fi, pw[e][0], pw[e][1])
            br, bi = _cmul(btr_re[d], btr_im[d], cr, ci)
            w1_ref[n_ch * s:n_ch * (s + 1), col:col + 2 * n_state] = (
                jnp.where(parmask, br, 0.0).astype(BF16))
            w1_ref[n_ch * s:n_ch * (s + 1), col + 2 * n_state:col + 4 * n_state] = (
                jnp.where(parmask, bi, 0.0).astype(BF16))
        aq_ref[:, col - width:col - width + 2 * n_state] = pw[q][0]
        aq_ref[:, col - width + 2 * n_state:col - width + 4 * n_state] = pw[q][1]

    init_ref[...] = jnp.dot(xmeta[...].astype(BF16), w1_ref[:, width:width + 4 * n_state],
                            preferred_element_type=F32)


def _ssm_prep(lam_re, lam_im, log_step, b_re, b_im, c_re, c_im, d_skip, us_meta):
    _, n_g, n_state, n_ch = b_re.shape
    q = CHUNK
    width = q * n_ch
    dup = lambda x: jnp.concatenate([x, x], axis=-1)
    lam_re_c, lam_im_c = lam_re[..., None], lam_im[..., None]
    ls_c = log_step[..., None, None]
    ct_re = jnp.tile(jnp.swapaxes(c_re, -1, -2), (1, 1, 1, q))
    ct_im = jnp.tile(jnp.swapaxes(c_im, -1, -2), (1, 1, 1, q))
    bt_re, bt_im = jnp.swapaxes(b_re, -1, -2), jnp.swapaxes(b_im, -1, -2)
    lam_re_r, lam_im_r = dup(lam_re)[:, :, None, :], dup(lam_im)[:, :, None, :]
    ls_r = jnp.broadcast_to(log_step[..., None, None], (2, n_g, 1, 2 * n_state))
    btr_re, btr_im = dup(bt_re), dup(bt_im)
    dpad = jnp.pad(d_skip.reshape(n_g, 1, n_ch), ((0, 0), (0, 0), ((q - 1) * n_ch, width)))
    xm = us_meta.astype(F32).reshape(q, n_g, n_ch).transpose(1, 0, 2).reshape(n_g, 1, width)
    xm = jnp.pad(xm, ((0, 0), (0, 7), (0, 0)))

    def dspec(shape):
        return pl.BlockSpec((2, None) + shape, lambda g: (0, g, 0, 0))

    def gspec(shape):
        return pl.BlockSpec((None,) + shape, lambda g: (g, 0, 0))

    body = functools.partial(_ssm_prep_body, n_state=n_state, n_ch=n_ch)
    return pl.pallas_call(
        body,
        grid=(n_g,),
        in_specs=[dspec((n_state, 1)), dspec((n_state, 1)), dspec((1, 1)),
                  dspec((n_state, width)), dspec((n_state, width)),
                  dspec((n_ch, n_state)), dspec((n_ch, n_state)),
                  dspec((1, 2 * n_state)), dspec((1, 2 * n_state)), dspec((1, 2 * n_state)),
                  dspec((n_ch, 2 * n_state)), dspec((n_ch, 2 * n_state)),
                  gspec((1, 2 * width)), gspec((8, width))],
        out_specs=[gspec((width, width + 8 * n_state)), gspec((8 * n_state, width)),
                   gspec((1, 8 * n_state)), gspec((8, 4 * n_state))],
        out_shape=[jax.ShapeDtypeStruct((n_g, width, width + 8 * n_state), BF16),
                   jax.ShapeDtypeStruct((n_g, 8 * n_state, width), BF16),
                   jax.ShapeDtypeStruct((n_g, 1, 8 * n_state), F32),
                   jax.ShapeDtypeStruct((n_g, 8, 4 * n_state), F32)],
        compiler_params=_cparams(1),
        name="ssm_prep",
    )(lam_re_c, lam_im_c, ls_c, ct_re, ct_im, bt_re, bt_im,
      lam_re_r, lam_im_r, ls_r, btr_re, btr_im, dpad, xm)


def _ssm_mm1_body(x_ref, w1_ref, yi_ref, s_ref, *, gb, width, sw):
    for j in range(0, gb, 2):
        r0 = jnp.dot(x_ref[j], w1_ref[j], preferred_element_type=F32)
        r1 = jnp.dot(x_ref[j + 1], w1_ref[j + 1], preferred_element_type=F32)
        yi_ref[j] = r0[:, 0:width]
        yi_ref[j + 1] = r1[:, 0:width]
        s_ref[:, (j // 2) * sw:(j // 2 + 1) * sw] = r0[:, width:] + r1[:, width:]


def _ssm_mm1(xt, w1, *, gb):
    n_g, nc, width = xt.shape
    sw = w1.shape[-1] - width
    body = functools.partial(_ssm_mm1_body, gb=gb, width=width, sw=sw)
    return pl.pallas_call(
        body,
        grid=(n_g // gb,),
        in_specs=[pl.BlockSpec((gb, nc, width), lambda i: (i, 0, 0)),
                  pl.BlockSpec((gb, width, width + sw), lambda i: (i, 0, 0))],
        out_specs=[pl.BlockSpec((gb, nc, width), lambda i: (i, 0, 0)),
                   pl.BlockSpec((nc, gb // 2 * sw), lambda i: (0, i))],
        out_shape=[jax.ShapeDtypeStruct((n_g, nc, width), F32),
                   jax.ShapeDtypeStruct((nc, n_g // 2 * sw), F32)],
        compiler_params=_cparams(1),
        name="ssm_mm1",
    )(xt, w1)


def _ssm_scan_body(sf_ref, sb_ref, aq_ref, init_ref, xf_ref, xb_ref, stf, stb, *,
                   cb, hw, n_pblk, blk_pseq, blk_sseq):
    j = pl.program_id(1)
    pos = jnp.where(j < n_pblk, lax.rem(j, blk_pseq), lax.rem(jnp.maximum(j - n_pblk, 0), blk_sseq))

    @pl.when(pos == 0)
    def _():
        stf[...] = init_ref[...]
        stb[...] = jnp.zeros_like(stb)

    afr, afi = aq_ref[:, 0:hw], aq_ref[:, hw:2 * hw]
    abr, abi = aq_ref[:, 2 * hw:3 * hw], aq_ref[:, 3 * hw:4 * hw]

    def body(i, carry):
        fr, fi, br, bi = carry
        ib = cb - 1 - i
        xf_ref[i, :, 0:hw] = fr
        xf_ref[i, :, hw:2 * hw] = fi
        xb_ref[ib, :, 0:hw] = br
        xb_ref[ib, :, hw:2 * hw] = bi
        nfr = afr * fr - afi * fi + sf_ref[i, :, 0:hw]
        nfi = afr * fi + afi * fr + sf_ref[i, :, hw:2 * hw]
        nbr = abr * br - abi * bi + sb_ref[ib, :, 0:hw]
        nbi = abr * bi + abi * br + sb_ref[ib, :, hw:2 * hw]
        return nfr, nfi, nbr, nbi

    fr, fi, br, bi = lax.fori_loop(
        0, cb, body, (stf[:, 0:hw], stf[:, hw:2 * hw], stb[:, 0:hw], stb[:, hw:2 * hw]))
    stf[:, 0:hw] = fr
    stf[:, hw:2 * hw] = fi
    stb[:, 0:hw] = br
    stb[:, hw:2 * hw] = bi


def _ssm_scan(s3, aq, init, *, cb, chunks_pseq, chunks_sseq, n_pchunks):
    nc, n_pair, sw = s3.shape
    hw = sw // 4
    pb = min(8, n_pair)
    n_pblk, blk_pseq, blk_sseq = n_pchunks // cb, chunks_pseq // cb, chunks_sseq // cb

    def bwd_block(j):
        in_p = j < n_pblk
        pos = jnp.where(in_p, lax.rem(j, blk_pseq), lax.rem(jnp.maximum(j - n_pblk, 0), blk_sseq))
        ln = jnp.where(in_p, blk_pseq, blk_sseq)
        return j - pos + ln - 1 - pos

    body = functools.partial(_ssm_scan_body, cb=cb, hw=hw, n_pblk=n_pblk, blk_pseq=blk_pseq,
                             blk_sseq=blk_sseq)
    return pl.pallas_call(
        body,
        grid=(n_pair // pb, nc // cb),
        in_specs=[pl.BlockSpec((cb, pb, 2 * hw), lambda p, j: (j, p, 0)),
                  pl.BlockSpec((cb, pb, 2 * hw), lambda p, j: (bwd_block(j), p, 1)),
                  pl.BlockSpec((pb, sw), lambda p, j: (p, 0)),
                  pl.BlockSpec((pb, 2 * hw), lambda p, j: (p, 0))],
        out_specs=[pl.BlockSpec((cb, pb, 2 * hw), lambda p, j: (j, p, 0)),
                   pl.BlockSpec((cb, pb, 2 * hw), lambda p, j: (bwd_block(j), p, 0))],
        out_shape=[jax.ShapeDtypeStruct((nc, n_pair, 2 * hw), F32),
                   jax.ShapeDtypeStruct((nc, n_pair, 2 * hw), F32)],
        scratch_shapes=[pltpu.VMEM((pb, 2 * hw), F32), pltpu.VMEM((pb, 2 * hw), F32)],
        compiler_params=_cparams(2),
        name="ssm_scan",
    )(s3, s3, aq, init)


def _gelu_tanh(x):
    return 0.5 * x * (1.0 + jnp.tanh(0.7978845608028654 * (x + 0.044715 * (x * x * x))))


def _ssm_mm2_body(yi_ref, xf_ref, xb_ref, cst_ref, z_ref, *, gb, hs):
    for j in range(gb):
        cols = slice((j // 2) * hs, (j // 2 + 1) * hs)
        y = (yi_ref[j]
             + jnp.dot(xf_ref[:, cols].astype(BF16), cst_ref[j, 0:hs, :], preferred_element_type=F32)
             + jnp.dot(xb_ref[:, cols].astype(BF16), cst_ref[j, hs:2 * hs, :],
                       preferred_element_type=F32))
        z_ref[j] = _gelu_tanh(y).astype(BF16)


def _ssm_mm2(yi, xf2, xb2, cst, *, gb):
    n_g, nc, width = yi.shape
    hs = cst.shape[1] // 2
    body = functools.partial(_ssm_mm2_body, gb=gb, hs=hs)
    return pl.pallas_call(
        body,
        grid=(n_g // gb,),
        in_specs=[pl.BlockSpec((gb, nc, width), lambda i: (i, 0, 0)),
                  pl.BlockSpec((nc, gb // 2 * hs), lambda i: (0, i)),
                  pl.BlockSpec((nc, gb // 2 * hs), lambda i: (0, i)),
                  pl.BlockSpec((gb, 2 * hs, width), lambda i: (i, 0, 0))],
        out_specs=pl.BlockSpec((gb, nc, width), lambda i: (i, 0, 0)),
        out_shape=jax.ShapeDtypeStruct((n_g, nc, width), BF16),
        compiler_params=_cparams(1),
        name="ssm_mm2",
    )(yi, xf2, xb2, cst)


def _spread(x, n_ch, shift):
    rows, w = x.shape
    lane = lax.broadcasted_iota(I32, (rows, LANES), 1)
    keep = (lane >= shift) & (lane < shift + n_ch)
    out = []
    for b in range(w // n_ch):
        src = (b * n_ch // LANES) * LANES
        amount = lax.rem(shift + LANES - (b * n_ch) % LANES, LANES)
        out.append(jnp.where(keep, pltpu.roll(x[:, src:src + LANES], amount, 1), 0.0))
    return jnp.concatenate(out, axis=1)


def _tile_lanes(x, n_ch, width):
    span = n_ch
    while span < LANES:
        x = x + pltpu.roll(x, span, 1)
        span *= 2
    return jnp.concatenate([x] * (width // LANES), axis=1)


def _sg_prep_body(lam_re_c, lam_im_c, ls_c, ct_re, ct_im, bt_re, bt_im,
                  lam_re_r, lam_im_r, ls_r, btr_re, btr_im, dpad,
                  m_ref, b_ref, c_ref, aq_ref, *, n_state, n_ch):
    q = CHUNK
    width = q * n_ch
    per_blk = LANES // n_ch
    gl = pl.program_id(0) % per_blk
    shift = gl * n_ch
    kblk = lax.broadcasted_iota(I32, (1, width), 1) // n_ch
    strips = []
    for d in range(2):
        ar, ai, fr, fi = _discretize(lam_re_c[d], lam_im_c[d], ls_c[d])
        kexp = kblk if d == 0 else (q - 1) - kblk
        wr, wi = _cpow(ar, ai, kexp, 4, (n_state, width))
        gcr, gci = _cmul(_tile_lanes(ct_re[d], n_ch, width), _tile_lanes(ct_im[d], n_ch, width),
                         wr, wi)
        gfr, gfi = _cmul(gcr, gci, fr, fi)
        strips.append(jnp.dot(bt_re[d], gfr, precision=HIGHEST, preferred_element_type=F32)
                      - jnp.dot(bt_im[d], gfi, precision=HIGHEST, preferred_element_type=F32))
        g1r, g1i = _cmul(gcr, gci, ar, ai)
        for comp, val in ((0, g1r), (1, -g1i)):
            c_ref[2 * d + comp] = _spread(val, n_ch, shift).astype(BF16)

    zf, zb = strips
    zero = jnp.zeros((n_ch, width), F32)
    z512 = jnp.concatenate([zb, zero], axis=1) + pltpu.roll(
        jnp.concatenate([zf, zero], axis=1), (q - 1) * n_ch, 1)
    row = lax.broadcasted_iota(I32, (n_ch, 2 * width), 0)
    lane = lax.broadcasted_iota(I32, (n_ch, 2 * width), 1)
    z512 = z512 + jnp.where(lane - (q - 1) * n_ch == row, dpad[...], 0.0)
    zwide = _spread(z512, n_ch, shift)
    for s in range(q):
        lo = (q - 1 - s) * LANES
        m_ref[s] = zwide[:, lo:lo + q * LANES].astype(BF16)

    parmask = (lax.broadcasted_iota(I32, (1, 2 * n_state), 1) // n_state) == gl % 2
    pieces = [[None] * 4 for _ in range(q)]
    for d in range(2):
        ar, ai, fr, fi = _discretize(lam_re_r[d], lam_im_r[d], ls_r[d])
        pw = [(jnp.ones_like(ar), jnp.zeros_like(ar))]
        for _ in range(q):
            pw.append(_cmul(pw[-1][0], pw[-1][1], ar, ai))
        for s in range(q):
            e = (q - 1 - s) if d == 0 else s
            cr, ci = _cmul(fr, fi, pw[e][0], pw[e][1])
            br, bi = _cmul(btr_re[d], btr_im[d], cr, ci)
            pieces[s][2 * d] = jnp.where(parmask, br, 0.0)
            pieces[s][2 * d + 1] = jnp.where(parmask, bi, 0.0)
        aq_ref[:, 2 * d * n_state:(2 * d + 1) * n_state] = pw[q][0][:, 0:n_state]
        aq_ref[:, (2 * d + 1) * n_state:(2 * d + 2) * n_state] = pw[q][1][:, 0:n_state]
    n_pair = per_blk // 2
    for s in range(q):
        cols = []
        for seg in range(4):
            for blk in range(n_pair):
                cols.append(jnp.where(gl // 2 == blk, pieces[s][seg], 0.0))
        b_ref[s] = jnp.concatenate(cols, axis=1).astype(BF16)


def _sg_prep(lam_re, lam_im, log_step, b_re, b_im, c_re, c_im, d_skip):
    _, n_g, n_state, n_ch = b_re.shape
    q = CHUNK
    width = q * n_ch
    per_blk = LANES // n_ch
    n_blk = n_g // per_blk
    sw = 4 * per_blk * n_state
    dup = lambda x: jnp.concatenate([x, x], axis=-1)
    lane_pad = lambda x: jnp.pad(x, ((0, 0),) * 3 + ((0, LANES - n_ch),))
    lam_re_c, lam_im_c = lam_re[..., None], lam_im[..., None]
    ls_c = log_step[..., None, None]
    ct_re = lane_pad(jnp.swapaxes(c_re, -1, -2))
    ct_im = lane_pad(jnp.swapaxes(c_im, -1, -2))
    bt_re, bt_im = jnp.swapaxes(b_re, -1, -2), jnp.swapaxes(b_im, -1, -2)
    lam_re_r, lam_im_r = dup(lam_re)[:, :, None, :], dup(lam_im)[:, :, None, :]
    ls_r = jnp.broadcast_to(log_step[..., None, None], (2, n_g, 1, 2 * n_state))
    btr_re, btr_im = dup(bt_re), dup(bt_im)
    dpad = jnp.pad(d_skip.reshape(n_g, 1, n_ch), ((0, 0), (0, 0), ((q - 1) * n_ch, width)))

    def dspec(shape):
        return pl.BlockSpec((2, None) + shape, lambda g: (0, g, 0, 0))

    body = functools.partial(_sg_prep_body, n_state=n_state, n_ch=n_ch)
    m4, b4, c5, aq = pl.pallas_call(
        body,
        grid=(n_g,),
        in_specs=[dspec((n_state, 1)), dspec((n_state, 1)), dspec((1, 1)),
                  dspec((n_state, LANES)), dspec((n_state, LANES)),
                  dspec((n_ch, n_state)), dspec((n_ch, n_state)),
                  dspec((1, 2 * n_state)), dspec((1, 2 * n_state)), dspec((1, 2 * n_state)),
                  dspec((n_ch, 2 * n_state)), dspec((n_ch, 2 * n_state)),
                  pl.BlockSpec((None, 1, 2 * width), lambda g: (g, 0, 0))],
        out_specs=[
            pl.BlockSpec((None, q, n_ch, q * LANES), lambda g: (g // per_blk, 0, g % per_blk, 0)),
            pl.BlockSpec((None, q, n_ch, sw), lambda g: (g // per_blk, 0, g % per_blk, 0)),
            pl.BlockSpec((None, 4, None, n_state, q * LANES),
                         lambda g: (g // per_blk, 0, g % per_blk, 0, 0)),
            pl.BlockSpec((None, 1, 4 * n_state), lambda g: (g, 0, 0))],
        out_shape=[jax.ShapeDtypeStruct((n_blk, q, LANES, q * LANES), BF16),
                   jax.ShapeDtypeStruct((n_blk, q, LANES, sw), BF16),
                   jax.ShapeDtypeStruct((n_blk, 4, per_blk, n_state, q * LANES), BF16),
                   jax.ShapeDtypeStruct((n_g, 1, 4 * n_state), F32)],
        compiler_params=_cparams(1),
        name="ssm_prep",
    )(lam_re_c, lam_im_c, ls_c, ct_re, ct_im, bt_re, bt_im,
      lam_re_r, lam_im_r, ls_r, btr_re, btr_im, dpad)
    intra = m4.reshape(n_blk, q * LANES, q * LANES)
    bst = b4.reshape(n_blk, q * LANES, sw)
    cst = c5.reshape(n_blk, sw, q * LANES)
    a = aq.reshape(n_blk, per_blk // 2, 2, 2, 2, n_state)
    a = a.transpose(3, 4, 0, 1, 2, 5).reshape(2, 2, n_blk, per_blk // 2, LANES)
    coef = []
    for d in range(2):
        re, im = a[d, 0], a[d, 1]
        coef.append((jnp.concatenate([re, re], axis=1), jnp.concatenate([-im, im], axis=1)))
    return intra, bst, cst, coef


def _sg_mm_intra_body(x_ref, m_ref, y_ref):
    y_ref[...] = jnp.dot(x_ref[...], m_ref[...], preferred_element_type=F32).astype(BF16)


def _sg_mm_intra(xg, intra, *, rt):
    nc = xg.shape[0]
    n_blk, kw, _ = intra.shape
    return pl.pallas_call(
        _sg_mm_intra_body,
        grid=(n_blk, nc // rt),
        in_specs=[pl.BlockSpec((rt, kw), lambda b, i: (i, b)),
                  pl.BlockSpec((None, kw, kw), lambda b, i: (b, 0, 0))],
        out_specs=pl.BlockSpec((rt, kw), lambda b, i: (i, b)),
        out_shape=jax.ShapeDtypeStruct((nc, n_blk * kw), BF16),
        compiler_params=_cparams(2),
        name="ssm_intra",
    )(xg, intra)


def _sg_mm_state_body(x_ref, xm_ref, b_ref, sf_ref, sb_ref, init_ref, *, rt, n_sub):
    r = jnp.dot(x_ref[...], b_ref[...], preferred_element_type=F32)
    for k in range(n_sub):
        sf_ref[pl.ds(k, rt, stride=n_sub), :] = r[:, k * LANES:(k + 1) * LANES]
        sb_ref[pl.ds(k, rt, stride=n_sub), :] = r[:, (n_sub + k) * LANES:(n_sub + k + 1) * LANES]

    @pl.when(pl.program_id(1) == 0)
    def _():
        r0 = jnp.dot(xm_ref[...], b_ref[...], preferred_element_type=F32)
        for k in range(n_sub):
            init_ref[k:k + 1, :] = r0[0:1, k * LANES:(k + 1) * LANES]


def _sg_mm_state(xg, xg_meta, bst, *, rt):
    nc = xg.shape[0]
    n_blk, kw, sw = bst.shape
    n_sub = sw // (2 * LANES)
    n_rt = nc // rt
    body = functools.partial(_sg_mm_state_body, rt=rt, n_sub=n_sub)
    dense = jax.ShapeDtypeStruct((n_blk * nc * n_sub, LANES), F32)
    dspec = pl.BlockSpec((rt * n_sub, LANES), lambda b, i: (b * n_rt + i, 0))
    return pl.pallas_call(
        body,
        grid=(n_blk, n_rt),
        in_specs=[pl.BlockSpec((rt, kw), lambda b, i: (i, b)),
                  pl.BlockSpec((CHUNK, kw), lambda b, i: (0, b)),
                  pl.BlockSpec((None, kw, sw), lambda b, i: (b, 0, 0))],
        out_specs=[dspec, dspec, pl.BlockSpec((None, n_sub, LANES), lambda b, i: (b, 0, 0))],
        out_shape=[dense, dense, jax.ShapeDtypeStruct((n_blk, n_sub, LANES), F32)],
        compiler_params=_cparams(2),
        name="ssm_state",
    )(xg, xg_meta, bst)


def _sg_scan_body(sf_ref, sb_ref, a1f_ref, a2f_ref, a1b_ref, a2b_ref, init_ref, xf_ref, xb_ref,
                  stf, stb, *, cb, n_blk, half, n_pblk, blk_pseq, blk_sseq):
    j = pl.program_id(0)
    pos = jnp.where(j < n_pblk, lax.rem(j, blk_pseq), lax.rem(jnp.maximum(j - n_pblk, 0), blk_sseq))

    @pl.when(pos == 0)
    def _():
        stf[...] = init_ref[...]
        stb[...] = jnp.zeros_like(stb)

    def body(i, carry):
        ef, eb = carry
        ib = cb - 1 - i
        nf, nb = [], []
        for g in range(n_blk):
            xf_ref[g, i] = ef[g]
            xb_ref[g, ib] = eb[g]
            nf.append(a1f_ref[g] * ef[g] + a2f_ref[g] * pltpu.roll(ef[g], half, 0) + sf_ref[g, i])
            nb.append(a1b_ref[g] * eb[g] + a2b_ref[g] * pltpu.roll(eb[g], half, 0) + sb_ref[g, ib])
        return tuple(nf), tuple(nb)

    ef, eb = lax.fori_loop(0, cb, body, (tuple(stf[g] for g in range(n_blk)),
                                         tuple(stb[g] for g in range(n_blk))))
    for g in range(n_blk):
        stf[g] = ef[g]
        stb[g] = eb[g]


def _sg_scan(sf, sb, coef, init, *, cb, chunks_pseq, chunks_sseq, n_pchunks):
    n_blk, nc, n_sub, _ = sf.shape
    n_pblk, blk_pseq, blk_sseq = n_pchunks // cb, chunks_pseq // cb, chunks_sseq // cb

    def bwd_block(j):
        in_p = j < n_pblk
        pos = jnp.where(in_p, lax.rem(j, blk_pseq), lax.rem(jnp.maximum(j - n_pblk, 0), blk_sseq))
        ln = jnp.where(in_p, blk_pseq, blk_sseq)
        return j - pos + ln - 1 - pos

    body = functools.partial(_sg_scan_body, cb=cb, n_blk=n_blk, half=n_sub // 2, n_pblk=n_pblk,
                             blk_pseq=blk_pseq, blk_sseq=blk_sseq)
    fwd = pl.BlockSpec((n_blk, cb, n_sub, LANES), lambda j: (0, j, 0, 0))
    bwd = pl.BlockSpec((n_blk, cb, n_sub, LANES), lambda j: (0, bwd_block(j), 0, 0))
    small = pl.BlockSpec((n_blk, n_sub, LANES), lambda j: (0, 0, 0))
    return pl.pallas_call(
        body,
        grid=(nc // cb,),
        in_specs=[fwd, bwd, small, small, small, small, small],
        out_specs=[fwd, bwd],
        out_shape=[jax.ShapeDtypeStruct(sf.shape, F32), jax.ShapeDtypeStruct(sb.shape, F32)],
        scratch_shapes=[pltpu.VMEM((n_blk, n_sub, LANES), F32), pltpu.VMEM((n_blk, n_sub, LANES), F32)],
        compiler_params=_cparams(1),
        name="ssm_scan",
    )(sf, sb, coef[0][0], coef[0][1], coef[1][0], coef[1][1], init)


def _gelu_tanh(x):
    return 0.5 * x * (1.0 + jnp.tanh(0.7978845608028654 * (x + 0.044715 * (x * x * x))))


def _sg_mm_out_body(y_ref, xf_ref, xb_ref, c_ref, z_ref, *, rt, n_sub):
    cols = [xf_ref[pl.ds(k, rt, stride=n_sub), :] for k in range(n_sub)]
    cols += [xb_ref[pl.ds(k, rt, stride=n_sub), :] for k in range(n_sub)]
    state = jnp.concatenate(cols, axis=1).astype(BF16)
    y = y_ref[...].astype(F32) + jnp.dot(state, c_ref[...], preferred_element_type=F32)
    z_ref[...] = _gelu_tanh(y).astype(BF16)


def _sg_mm_out(yi, xf, xb, cst, *, rt):
    nc = yi.shape[0]
    n_blk, sw, kw = cst.shape
    n_sub = sw // (2 * LANES)
    n_rt = nc // rt
    dspec = pl.BlockSpec((rt * n_sub, LANES), lambda b, i: (b * n_rt + i, 0))
    return pl.pallas_call(
        functools.partial(_sg_mm_out_body, rt=rt, n_sub=n_sub),
        grid=(n_blk, n_rt),
        in_specs=[pl.BlockSpec((rt, kw), lambda b, i: (i, b)), dspec, dspec,
                  pl.BlockSpec((None, sw, kw), lambda b, i: (b, 0, 0))],
        out_specs=pl.BlockSpec((rt, kw), lambda b, i: (i, b)),
        out_shape=jax.ShapeDtypeStruct((nc, n_blk * kw), BF16),
        compiler_params=_cparams(2),
        name="ssm_out",
    )(yi, xf, xb, cst)


def _mix_body(xp_ref, xs_ref, z_ref, c_ref, cprev_ref, cnext_ref, cmeta_ref, sga_ref, sgb_ref,
              wglu_ref, wpw_ref, wout_ref, wdw_ref, bdw_ref, lng_ref, lnb_ref, gffn_ref,
              rw_ref, rb_ref,
              h1_ref, v_ref, route_ref, cnt_ref,
              cw_ref, conv_ref, zsc_ref, *,
              tm, n_p_tiles, tiles_per_pseq, tiles_per_sseq, d_model, d_ssm, d_conv, conv_w,
              n_grp, n_exp, exp_per_grp, rc):
    i = pl.program_id(0)
    in_prompt = i < n_p_tiles
    x = jnp.where(in_prompt, xp_ref[...], xs_ref[...])
    pos_p = lax.rem(i, tiles_per_pseq)
    pos_s = lax.rem(jnp.maximum(i - n_p_tiles, 0), tiles_per_sseq)
    is_start = jnp.where(in_prompt, pos_p == 0, pos_s == 0)
    is_end = jnp.where(in_prompt, pos_p == tiles_per_pseq - 1, pos_s == tiles_per_sseq - 1)

    halo = CHUNK
    pad = conv_w // 2
    cw_ref[0:halo, :] = jnp.where(is_start, cmeta_ref[...], cprev_ref[...]).astype(F32)
    cw_ref[halo:halo + tm, :] = c_ref[...].astype(F32)
    cw_ref[halo + tm:2 * halo + tm, :] = jnp.where(is_end, 0.0, cnext_ref[...].astype(F32))
    sub = 8
    for lc in range(d_conv // LANES):
        ls = slice(lc * LANES, (lc + 1) * LANES)
        for r0 in range(0, tm, rc):
            out = None
            for r in range(sub):
                part = None
                for q in range((conv_w + halo - pad) // sub + 1):
                    k = sub * q + r - (halo - pad)
                    if 0 <= k < conv_w:
                        term = wdw_ref[k:k + 1, ls] * cw_ref[r0 + sub * q:r0 + sub * q + rc + sub, ls]
                        part = term if part is None else part + term
                if part is not None:
                    shifted = part[r:r + rc]
                    out = shifted if out is None else out + shifted
            conv_ref[r0:r0 + rc, ls] = out
    cc = conv_ref[...] + bdw_ref[...]
    mu = jnp.mean(cc, axis=-1, keepdims=True)
    var = jnp.mean(jnp.square(cc - mu), axis=-1, keepdims=True)
    cc = (cc - mu) * lax.rsqrt(var + EPS) * lng_ref[...] + lnb_ref[...]
    cc = (cc * _sigmoid(cc)).astype(BF16)
    y_b = jnp.dot(cc, wpw_ref[...], preferred_element_type=F32)

    for b in range(d_ssm // LANES):
        for s in range(CHUNK):
            c0 = (b * CHUNK + s) * LANES
            zsc_ref[b, pl.ds(s, tm // CHUNK, stride=CHUNK), :] = z_ref[:, c0:c0 + LANES].astype(F32)
    z = jnp.concatenate([zsc_ref[b] for b in range(d_ssm // LANES)], axis=1).astype(BF16)
    va = jnp.dot(z, wglu_ref[:, 0:d_model], preferred_element_type=F32)
    ga = jnp.dot(z, wglu_ref[:, d_model:2 * d_model], preferred_element_type=F32)
    y_a = va * _sigmoid(ga)
    merged = (sga_ref[...].astype(F32) * y_a + sgb_ref[...].astype(F32) * y_b).astype(BF16)
    h1 = x + jnp.dot(merged, wout_ref[...], preferred_element_type=F32)
    h1_ref[...] = h1
    v = _rms(h1, gffn_ref[...])
    half = d_model // 2
    v_ref[...] = _pack_bf16_pair(v[:, 0:half], v[:, half:d_model])

    v_hi = v.astype(BF16)
    v_lo = (v - v_hi.astype(F32)).astype(BF16)
    acc = (jnp.dot(v_hi, rw_ref[...], preferred_element_type=F32)
           + jnp.dot(v_lo, rw_ref[...], preferred_element_type=F32))
    logits = acc + pltpu.roll(acc, LANES // 2, 1) + rb_ref[...]
    lane = lax.broadcasted_iota(I32, (tm, LANES), 1).astype(F32)
    big = jnp.float32(1e9)
    neg = jnp.float32(-jnp.inf)
    gmask = lane < n_grp
    lg = jnp.where(gmask, logits, neg)
    gmax = jnp.max(lg, axis=-1, keepdims=True)
    grp = jnp.min(jnp.where(lg == gmax, lane, big), axis=-1, keepdims=True)
    p_grp = 1.0 / jnp.sum(jnp.where(gmask, jnp.exp(logits - gmax), 0.0), axis=-1, keepdims=True)
    lo = n_grp + grp * exp_per_grp
    emask = (lane >= lo) & (lane < lo + exp_per_grp)
    le = jnp.where(emask, logits, neg)
    m1 = jnp.max(le, axis=-1, keepdims=True)
    i1 = jnp.min(jnp.where(le == m1, lane, big), axis=-1, keepdims=True)
    le2 = jnp.where(lane == i1, neg, le)
    m2 = jnp.max(le2, axis=-1, keepdims=True)
    i2 = jnp.min(jnp.where(le2 == m2, lane, big), axis=-1, keepdims=True)
    t = jnp.exp(m2 - m1)
    w1 = 1.0 / (1.0 + t)
    e1 = i1 - n_grp
    e2 = i2 - n_grp
    route_ref[...] = jnp.where(lane == 0, e1, jnp.where(lane == 1, e2, jnp.where(
        lane == 2, p_grp * w1, jnp.where(lane == 3, p_grp * (t * w1), 0.0))))

    @pl.when(i == 0)
    def _():
        cnt_ref[...] = jnp.zeros_like(cnt_ref)

    hot = jnp.where((lane == e1) | (lane == e2), 1.0, 0.0)
    cnt_ref[...] += jnp.sum(hot, axis=0, keepdims=True)


def _mix(xp2, xs2, z, c, c_meta, sga, sgb, wglu, wpw, wout, wdw, bdw, lng, lnb, gffn, rw, rb, *,
         tm, p_seq, s_seq, n_grp, n_exp):
    n_p, d_model = xp2.shape
    n_s = xs2.shape[0]
    t = n_p + n_s
    d_ssm, d_conv = z.shape[1] // CHUNK, c.shape[1]
    conv_w = wdw.shape[0]
    n_p_tiles, n_s_tiles = n_p // tm, n_s // tm
    hpt = tm // CHUNK
    n_hblk = t // CHUNK
    body = functools.partial(
        _mix_body, tm=tm, n_p_tiles=n_p_tiles, tiles_per_pseq=p_seq // tm,
        tiles_per_sseq=s_seq // tm, d_model=d_model, d_ssm=d_ssm, d_conv=d_conv, conv_w=conv_w,
        n_grp=n_grp, n_exp=n_exp, exp_per_grp=n_exp // n_grp, rc=min(128, tm))
    row = lambda i: (i, 0)
    return pl.pallas_call(
        body,
        grid=(n_p_tiles + n_s_tiles,),
        in_specs=[
            pl.BlockSpec((tm, d_model), lambda i: (jnp.minimum(i, n_p_tiles - 1), 0)),
            pl.BlockSpec((tm, d_model), lambda i: (jnp.maximum(i - n_p_tiles, 0), 0)),
            pl.BlockSpec((tm // CHUNK, CHUNK * d_ssm), row),
            pl.BlockSpec((tm, d_conv), row),
            pl.BlockSpec((CHUNK, d_conv), lambda i: (jnp.maximum(i * hpt - 1, 0), 0)),
            pl.BlockSpec((CHUNK, d_conv), lambda i: (jnp.minimum((i + 1) * hpt, n_hblk - 1), 0)),
            _resident(c_meta.shape),
            pl.BlockSpec((tm, d_model), row),
            pl.BlockSpec((tm, d_model), row),
            _resident(wglu.shape), _resident(wpw.shape), _resident(wout.shape),
            _resident(wdw.shape), _resident(bdw.shape), _resident(lng.shape), _resident(lnb.shape),
            _resident(gffn.shape), _resident(rw.shape), _resident(rb.shape),
        ],
        out_specs=[pl.BlockSpec((tm, d_model), row), pl.BlockSpec((tm, d_model // 2), row),
                   pl.BlockSpec((tm, LANES), row), pl.BlockSpec((1, LANES), lambda i: (0, 0))],
        out_shape=[jax.ShapeDtypeStruct((t, d_model), F32),
                   jax.ShapeDtypeStruct((t, d_model // 2), U32),
                   jax.ShapeDtypeStruct((t, LANES), F32),
                   jax.ShapeDtypeStruct((1, LANES), F32)],
        scratch_shapes=[pltpu.VMEM((tm + 2 * CHUNK, d_conv), F32), pltpu.VMEM((tm, d_conv), F32),
                        pltpu.VMEM((d_ssm // LANES, tm, LANES), F32)],
        compiler_params=_cparams(1),
        name="mix",
    )(xp2, xs2, z, c, c, c, c_meta, sga, sgb, wglu, wpw, wout, wdw, bdw, lng, lnb, gffn, rw, rb)


def _positions_body(route_ref, offs_ref, dest_ref, carry_ref, *, te):
    @pl.when(pl.program_id(0) == 0)
    def _():
        carry_ref[...] = jnp.zeros_like(carry_ref)

    lane = lax.broadcasted_iota(I32, (te, LANES), 1).astype(F32)
    r = route_ref[...]
    oh1 = lane == r[:, 0:1]
    oh2 = lane == r[:, 1:2]
    both = jnp.where(oh1 | oh2, 1.0, 0.0)
    tri = jnp.where(lax.broadcasted_iota(I32, (te, te), 0) > lax.broadcasted_iota(I32, (te, te), 1),
                    1.0, 0.0).astype(BF16)
    before = jnp.dot(tri, both.astype(BF16), preferred_element_type=F32)
    base = before + carry_ref[...] + offs_ref[...]
    d1 = jnp.sum(jnp.where(oh1, base, 0.0), axis=-1, keepdims=True)
    d2 = jnp.sum(jnp.where(oh2, base, 0.0), axis=-1, keepdims=True)
    dest_ref[...] = jnp.where(lane == 0, d1, jnp.where(lane == 1, d2, 0.0)).astype(I32)
    carry_ref[...] += jnp.sum(both, axis=0, keepdims=True)


def _positions(route, offs, *, te):
    t = route.shape[0]
    return pl.pallas_call(
        functools.partial(_positions_body, te=te),
        grid=(t // te,),
        in_specs=[pl.BlockSpec((te, LANES), lambda i: (i, 0)),
                  pl.BlockSpec((1, LANES), lambda i: (0, 0))],
        out_specs=pl.BlockSpec((te, LANES), lambda i: (i, 0)),
        out_shape=jax.ShapeDtypeStruct((t, LANES), I32),
        scratch_shapes=[pltpu.VMEM((1, LANES), F32)],
        compiler_params=_cparams(1),
        name="positions",
    )(route, offs)


def _row_copy(src_ref, src_row, dst_ref, dst_row, sem):
    return pltpu.make_async_copy(src_ref.at[pl.ds(src_row, 1), :],
                                 dst_ref.at[pl.ds(dst_row, 1), :], sem)


def _dispatch_body(dest_hbm, pad_tile_ref, v_ref, xs_hbm, dsm, zbuf, sem, dsem, zsem, *,
                   td, tme, n_exp):
    i = pl.program_id(0)

    @pl.when(i == 0)
    def _():
        zbuf[...] = jnp.zeros_like(zbuf)

        def zero_tile(e):
            row0 = pl.multiple_of(pad_tile_ref[e] * tme, tme)
            return pltpu.make_async_copy(zbuf, xs_hbm.at[pl.ds(row0, tme), :], zsem)
        for e in range(n_exp):
            @pl.when(pad_tile_ref[e] >= 0)
            def _(e=e):
                zero_tile(e).start()
        for e in range(n_exp):
            @pl.when(pad_tile_ref[e] >= 0)
            def _(e=e):
                zero_tile(e).wait()

    cp = pltpu.make_async_copy(dest_hbm.at[pl.ds(i, 1), :], dsm, dsem)
    cp.start()
    cp.wait()

    def issue(r, _):
        _row_copy(v_ref, r, xs_hbm, dsm[0, 2 * r], sem).start(priority=0)
        _row_copy(v_ref, r, xs_hbm, dsm[0, 2 * r + 1], sem).start(priority=1)
        return 0
    lax.fori_loop(0, td, issue, 0, unroll=8)

    def drain(r, _):
        _row_copy(v_ref, 0, xs_hbm, 0, sem).wait()
        _row_copy(v_ref, 0, xs_hbm, 0, sem).wait()
        return 0
    lax.fori_loop(0, td, drain, 0)


def _dispatch(dest2, pad_tile, v, *, td, tme, n_rows):
    t, w = v.shape
    n_exp = pad_tile.shape[0]
    return pl.pallas_call(
        functools.partial(_dispatch_body, td=td, tme=tme, n_exp=n_exp),
        grid=(t // td,),
        in_specs=[pl.BlockSpec(memory_space=pl.ANY),
                  pl.BlockSpec(memory_space=pltpu.SMEM),
                  pl.BlockSpec((td, w), lambda i: (i, 0))],
        out_specs=pl.BlockSpec(memory_space=pl.ANY),
        out_shape=jax.ShapeDtypeStruct((n_rows, w), U32),
        scratch_shapes=[pltpu.SMEM((1, 2 * td), I32), pltpu.VMEM((tme, w), U32),
                        pltpu.SemaphoreType.DMA, pltpu.SemaphoreType.DMA, pltpu.SemaphoreType.DMA],
        compiler_params=_cparams(1),
        name="dispatch",
    )(dest2, pad_tile, v)


def _experts_body(te_ref, first_ref, last_ref, nxt_ref, nused_ref,
                  xs_ref, w1_hbm, w3_hbm, w2_hbm, o_ref,
                  w1b, w3b, w2b, st1, st3, st2, sem, *, half, n_conv):
    i = pl.program_id(0)
    used = i < nused_ref[0]
    nxt = nxt_ref[i]

    def copies(e):
        return (pltpu.make_async_copy(w1_hbm.at[e], st1, sem.at[0]),
                pltpu.make_async_copy(w3_hbm.at[e], st3, sem.at[1]),
                pltpu.make_async_copy(w2_hbm.at[e], st2, sem.at[2]))

    def convert():
        r13 = st1.shape[0] // n_conv
        r2 = st2.shape[0] // n_conv

        def body(c, _):
            rows = pl.ds(pl.multiple_of(c * r13, r13), r13)
            w1b[rows, :] = st1[rows, :].astype(BF16)
            w3b[rows, :] = st3[rows, :].astype(BF16)
            rows2 = pl.ds(pl.multiple_of(c * r2, r2), r2)
            w2b[rows2, :] = st2[rows2, :].astype(BF16)
            return 0
        lax.fori_loop(0, n_conv, body, 0)

    @pl.when(i == 0)
    def _():
        for cp in copies(te_ref[0]):
            cp.start()
        for cp in copies(te_ref[0]):
            cp.wait()
        convert()

    @pl.when(used & (first_ref[i] == 1) & (nxt >= 0))
    def _():
        for cp in copies(nxt):
            cp.start()

    @pl.when(used)
    def _():
        a, b = _unpack_bf16_pair(xs_ref[...])
        a, b = a.astype(BF16), b.astype(BF16)

        def up(w_ref):
            return (jnp.dot(a, w_ref[0:half, :], preferred_element_type=F32)
                    + jnp.dot(b, w_ref[half:2 * half, :], preferred_element_type=F32))
        h1 = up(w1b)
        act = (h1 * _sigmoid(h1) * up(w3b)).astype(BF16)
        o = jnp.dot(act, w2b[...], preferred_element_type=F32)
        o_ref[...] = _pack_bf16_pair(o[:, 0:half], o[:, half:2 * half])

    @pl.when(jnp.logical_not(used))
    def _():
        o_ref[...] = jnp.zeros_like(o_ref)

    @pl.when(used & (last_ref[i] == 1) & (nxt >= 0))
    def _():
        for cp in copies(nxt):
            cp.wait()
        convert()


def _experts(tile_expert, first, last, nxt, n_used, xs, w1, w3, w2, *, tme):
    rows, half = xs.shape
    n_e, d_model, d_exp = w1.shape
    grid_spec = pltpu.PrefetchScalarGridSpec(
        num_scalar_prefetch=5,
        grid=(rows // tme,),
        in_specs=[pl.BlockSpec((tme, half), lambda i, te, fi, la, nx, nu: (jnp.minimum(i, nu[0] - 1), 0)),
                  pl.BlockSpec(memory_space=pl.ANY), pl.BlockSpec(memory_space=pl.ANY),
                  pl.BlockSpec(memory_space=pl.ANY)],
        out_specs=pl.BlockSpec((tme, half), lambda i, *_: (i, 0)),
        scratch_shapes=[pltpu.VMEM((d_model, d_exp), BF16), pltpu.VMEM((d_model, d_exp), BF16),
                        pltpu.VMEM((d_exp, d_model), BF16),
                        pltpu.VMEM((d_model, d_exp), F32), pltpu.VMEM((d_model, d_exp), F32),
                        pltpu.VMEM((d_exp, d_model), F32),
                        pltpu.SemaphoreType.DMA((3,))],
    )
    return pl.pallas_call(
        functools.partial(_experts_body, half=half, n_conv=8),
        grid_spec=grid_spec,
        out_shape=jax.ShapeDtypeStruct((rows, half), U32),
        compiler_params=_cparams(1),
        name="experts",
    )(tile_expert, first, last, nxt, n_used, xs, w1, w3, w2)


def _sc_gather(table, idx, *, chunk):
    n_rows, width = idx.shape[0], table.shape[1]
    n_workers = SC_CORES * SC_SUBCORES
    per_w = n_rows // n_workers
    n_chunks = per_w // chunk
    assert per_w * n_workers == n_rows and n_chunks * chunk == per_w and n_chunks % 2 == 0
    mesh = plsc.VectorSubcoreMesh(core_axis_name="c", subcore_axis_name="s",
                                  num_cores=SC_CORES, num_subcores=SC_SUBCORES)

    @functools.partial(
        pl.kernel, mesh=mesh,
        out_type=jax.ShapeDtypeStruct((n_rows, width), table.dtype),
        scratch_types=[pltpu.VMEM((per_w,), I32), pltpu.VMEM((2, chunk, width), table.dtype),
                       pltpu.SemaphoreType.DMA((2,))],
    )
    def gather_kernel(table_hbm, idx_hbm, out_hbm, idx_v, rows_v, sem):
        wid = lax.axis_index("s") * SC_CORES + lax.axis_index("c")
        base = pl.multiple_of(wid * per_w, per_w)
        pltpu.sync_copy(idx_hbm.at[pl.ds(base, per_w)], idx_v)

        def gather(j, slot):
            off = pl.multiple_of(j * chunk, chunk)
            return pltpu.make_async_copy(table_hbm.at[idx_v.at[pl.ds(off, chunk)]],
                                         rows_v.at[slot], sem.at[slot])

        gather(0, 0).start()

        @pl.loop(0, n_chunks, step=2)
        def _(j):
            for slot in range(2):
                jj = j + slot
                gather(jj, slot).wait()

                @pl.when(jj + 1 < n_chunks)
                def _():
                    gather(jj + 1, 1 - slot).start()
                off = pl.multiple_of(base + jj * chunk, chunk)
                pltpu.sync_copy(rows_v.at[slot], out_hbm.at[pl.ds(off, chunk)])

    return gather_kernel(table, idx)


def _final_body(g1_ref, g2_ref, h1_ref, route_ref, g_ref, y_ref, *, half):
    route = route_ref[...]
    w1, w2 = route[:, 2:3], route[:, 3:4]
    a1, b1 = _unpack_bf16_pair(pltpu.bitcast(g1_ref[...], U32))
    a2, b2 = _unpack_bf16_pair(pltpu.bitcast(g2_ref[...], U32))
    h1 = h1_ref[...]
    ha = h1[:, 0:half] + (a1 * w1 + a2 * w2)
    hb = h1[:, half:2 * half] + (b1 * w1 + b2 * w2)
    ms = (jnp.sum(ha * ha, axis=-1, keepdims=True)
          + jnp.sum(hb * hb, axis=-1, keepdims=True)) / (2 * half)
    inv = lax.rsqrt(ms + EPS)
    y_ref[:, 0:half] = ha * inv * g_ref[:, 0:half]
    y_ref[:, half:2 * half] = hb * inv * g_ref[:, half:2 * half]


def _final(gathered, h1, route, g, *, tf, row0, n_rows):
    t, d_model = h1.shape
    half = d_model // 2
    tile0, tiles_t = row0 // tf, t // tf
    return pl.pallas_call(
        functools.partial(_final_body, half=half),
        grid=(n_rows // tf,),
        in_specs=[pl.BlockSpec((tf, half), lambda i: (tile0 + i, 0)),
                  pl.BlockSpec((tf, half), lambda i: (tiles_t + tile0 + i, 0)),
                  pl.BlockSpec((tf, d_model), lambda i: (tile0 + i, 0)),
                  pl.BlockSpec((tf, LANES), lambda i: (tile0 + i, 0)),
                  pl.BlockSpec((1, d_model), lambda i: (0, 0))],
        out_specs=pl.BlockSpec((tf, d_model), lambda i: (i, 0)),
        out_shape=jax.ShapeDtypeStruct((n_rows, d_model), F32),
        compiler_params=_cparams(1),
        name="final",
    )(gathered, gathered, h1, route, g)


def _combine_body(dest_hbm, o_hbm, h1_ref, route_ref, g_ref, y_ref, dsm, gbuf, sem, dsem, *,
                  tf, tile0, half):
    i = pl.program_id(0)
    cp = pltpu.make_async_copy(dest_hbm.at[pl.ds(tile0 + i, 1), :], dsm, dsem)
    cp.start()
    cp.wait()

    def issue(r, _):
        _row_copy(o_hbm, dsm[0, 2 * r], gbuf.at[0], r, sem).start(priority=0)
        _row_copy(o_hbm, dsm[0, 2 * r + 1], gbuf.at[1], r, sem).start(priority=1)
        return 0
    lax.fori_loop(0, tf, issue, 0, unroll=8)

    def drain(r, _):
        _row_copy(o_hbm, 0, gbuf.at[0], 0, sem).wait()
        _row_copy(o_hbm, 0, gbuf.at[0], 0, sem).wait()
        return 0
    lax.fori_loop(0, tf, drain, 0)

    route = route_ref[...]
    g1, g2 = route[:, 2:3], route[:, 3:4]
    a1, b1 = _unpack_bf16_pair(gbuf[0])
    a2, b2 = _unpack_bf16_pair(gbuf[1])
    h1 = h1_ref[...]
    ha = h1[:, 0:half] + (a1 * g1 + a2 * g2)
    hb = h1[:, half:2 * half] + (b1 * g1 + b2 * g2)
    ms = (jnp.sum(ha * ha, axis=-1, keepdims=True)
          + jnp.sum(hb * hb, axis=-1, keepdims=True)) / (2 * half)
    inv = lax.rsqrt(ms + EPS)
    y_ref[:, 0:half] = ha * inv * g_ref[:, 0:half]
    y_ref[:, half:2 * half] = hb * inv * g_ref[:, half:2 * half]


def _combine(dest2, o, h1, route, g, *, tf, row0, n_rows):
    d_model = h1.shape[1]
    half = d_model // 2
    tile0 = row0 // tf
    return pl.pallas_call(
        functools.partial(_combine_body, tf=tf, tile0=tile0, half=half),
        grid=(n_rows // tf,),
        in_specs=[pl.BlockSpec(memory_space=pl.ANY), pl.BlockSpec(memory_space=pl.ANY),
                  pl.BlockSpec((tf, d_model), lambda i: (tile0 + i, 0)),
                  pl.BlockSpec((tf, LANES), lambda i: (tile0 + i, 0)),
                  pl.BlockSpec((1, d_model), lambda i: (0, 0))],
        out_specs=pl.BlockSpec((tf, d_model), lambda i: (i, 0)),
        out_shape=jax.ShapeDtypeStruct((n_rows, d_model), F32),
        scratch_shapes=[pltpu.SMEM((1, 2 * tf), I32), pltpu.VMEM((2, tf, half), U32),
                        pltpu.SemaphoreType.DMA, pltpu.SemaphoreType.DMA],
        compiler_params=_cparams(1),
        name="combine",
    )(dest2, o, h1, route, g)


def kernel(x_prompt, x_sample, meta, norm_mix_g, w_in, ssm_lam_re, ssm_lam_im, ssm_log_step, ssm_b_re, ssm_b_im, ssm_c_re, ssm_c_im, ssm_d, ssm_w_glu, conv_w_dw, conv_b_dw, conv_ln_g, conv_ln_b, conv_w_pw, w_out, norm_ffn_g, router_group_w, router_group_b, router_expert_w, router_expert_b, expert_w1, expert_w3, expert_w2, final_g):
    assert w_in.shape[0] == 1, "single-layer trunk"
    bp, lp, d_model = x_prompt.shape
    bs, ls, _ = x_sample.shape
    n_meta = meta.shape[0]
    d_ssm = ssm_d.shape[-1]
    d_conv = conv_b_dw.shape[-1]
    n_ch = ssm_b_re.shape[-1]
    n_g = ssm_b_re.shape[2]
    n_grp = router_group_w.shape[-1]
    n_exp = router_expert_w.shape[-1]
    assert n_meta == CHUNK and lp % CHUNK == 0 and ls % CHUNK == 0
    n_p, n_s = bp * lp, bs * ls
    t = n_p + n_s
    tm = min(256, lp, ls)
    assert lp % tm == 0 and ls % tm == 0

    xp2 = x_prompt.reshape(n_p, d_model)
    xs2 = x_sample.reshape(n_s, d_model)
    row = lambda a: a.reshape(1, -1)
    w_in_bf = w_in[0].astype(BF16)

    xg, c, sga, sgb = _inproj(xp2, xs2, row(norm_mix_g[0]), w_in_bf, d_ssm=d_ssm, d_conv=d_conv, tm=tm)
    xg_meta, c_meta = _meta_inproj(meta, row(norm_mix_g[0]), w_in_bf, d_ssm=d_ssm, d_conv=d_conv)

    assert n_ch == CHUNK and LANES % n_ch == 0 and n_g % (LANES // n_ch) == 0
    intra, bst, cst, coef = _sg_prep(ssm_lam_re[0], ssm_lam_im[0], ssm_log_step[0], ssm_b_re[0],
                                     ssm_b_im[0], ssm_c_re[0], ssm_c_im[0], ssm_d[0])
    nc = t // CHUNK
    n_blk = d_ssm // LANES
    rt = min(512, nc)
    yi = _sg_mm_intra(xg, intra, rt=rt)
    sf, sb, init = _sg_mm_state(xg, xg_meta, bst, rt=rt)
    n_sub = sf.shape[0] // (n_blk * nc)
    cb = min(64, lp // CHUNK, ls // CHUNK)
    xf, xb = _sg_scan(sf.reshape(n_blk, nc, n_sub, LANES), sb.reshape(n_blk, nc, n_sub, LANES),
                      coef, init, cb=cb, chunks_pseq=lp // CHUNK, chunks_sseq=ls // CHUNK,
                      n_pchunks=n_p // CHUNK)
    z = _sg_mm_out(yi, xf.reshape(sf.shape), xb.reshape(sb.shape), cst, rt=rt)

    assert n_grp + n_exp <= LANES // 2
    rw32 = jnp.pad(jnp.concatenate([router_group_w[0], router_expert_w[0]], axis=1),
                   ((0, 0), (0, LANES // 2 - n_grp - n_exp)))
    rw_hi = rw32.astype(BF16)
    rw = jnp.concatenate([rw_hi, (rw32 - rw_hi.astype(F32)).astype(BF16)], axis=1)
    rb = jnp.zeros((1, LANES), F32).at[0, 0:n_grp].set(router_group_b[0]).at[
        0, n_grp:n_grp + n_exp].set(router_expert_b[0])
    h1, v, route, cnt = _mix(
        xp2, xs2, z, c, c_meta, sga, sgb, ssm_w_glu[0].astype(BF16), conv_w_pw[0].astype(BF16),
        w_out[0].astype(BF16), conv_w_dw[0], row(conv_b_dw[0]), row(conv_ln_g[0]),
        row(conv_ln_b[0]), row(norm_ffn_g[0]), rw, rb,
        tm=tm, p_seq=lp, s_seq=ls, n_grp=n_grp, n_exp=n_exp)

    tme = 256
    counts = cnt[0, 0:n_exp].astype(I32)
    tiles_e = (counts + tme - 1) // tme
    tile_end = jnp.cumsum(tiles_e)
    n_used = tile_end[-1]
    offs = jnp.zeros((1, LANES), F32).at[0, 0:n_exp].set(((tile_end - tiles_e) * tme).astype(F32))
    n_tiles = (2 * t) // tme + n_exp
    ids = jnp.arange(n_tiles, dtype=I32)
    te_map = jnp.sum((jnp.minimum(ids, n_used - 1)[:, None] >= tile_end[None, :]).astype(I32), axis=1)
    te_map = jnp.minimum(te_map, n_exp - 1)
    run_start = (tile_end - tiles_e)[te_map]
    run_end = tile_end[te_map]
    valid = ids < n_used
    first = (valid & (ids == run_start)).astype(I32)
    last = (valid & (ids == run_end - 1)).astype(I32)
    nxt = jnp.where(valid & (run_end < n_used), te_map[jnp.minimum(run_end, n_tiles - 1)], -1)

    td = min(512, t)
    dest = _positions(route, offs, te=td)
    dest2 = dest[:, 0:2].reshape(t // td, 2 * td)
    pad_tile = jnp.where(tiles_e > 0, tile_end - 1, -1).astype(I32)
    xs = _dispatch(dest2, pad_tile, v, td=td, tme=tme, n_rows=n_tiles * tme)
    o = _experts(te_map, first, last, nxt, n_used.reshape(1), xs, expert_w1[0], expert_w3[0],
                 expert_w2[0], tme=tme)

    tf = min(256, n_p, n_s)
    idx = jnp.concatenate([dest[:, 0], dest[:, 1]])
    gathered = _sc_gather(lax.bitcast_convert_type(o, I32), idx, chunk=32)
    fg = row(final_g)
    y_p = _final(gathered, h1, route, fg, tf=tf, row0=0, n_rows=n_p)
    y_s = _final(gathered, h1, route, fg, tf=tf, row0=n_p, n_rows=n_s)
    return (y_p.reshape(bp, lp, d_model), y_s.reshape(bs, ls, d_model))
```

```python
import functools

import jax
import jax.numpy as jnp
from jax import lax
from jax.experimental import pallas as pl
from jax.experimental.pallas import tpu as pltpu
from jax.experimental.pallas import tpu_sc as plsc

F32 = jnp.float32
BF16 = jnp.bfloat16
U32 = jnp.uint32
I32 = jnp.int32

EPS = 1e-6
LAM_RE_MAX = -1e-4
CHUNK = 16
LANES = 128
VMEM_LIMIT = 56 << 20
SC_CORES = 2
SC_SUBCORES = 16
SC_LANES = 16
HIGHEST = lax.Precision.HIGHEST


def _cparams(n_axes):
    return pltpu.CompilerParams(dimension_semantics=("arbitrary",) * n_axes,
                                vmem_limit_bytes=VMEM_LIMIT)


def _resident(shape):
    nd = len(shape)
    return pl.BlockSpec(shape, lambda *_: (0,) * nd, pipeline_mode=pl.Buffered(1))


def _sigmoid(x):
    return 1.0 / (1.0 + jnp.exp(-x))


def _rms(x, g):
    return x * lax.rsqrt(jnp.mean(x * x, axis=-1, keepdims=True) + EPS) * g


def _pack_bf16_pair(a, b):
    def rnd(x):
        u = pltpu.bitcast(x, U32)
        return (u + jnp.uint32(0x7FFF) + ((u >> 16) & jnp.uint32(1))) >> 16
    return pltpu.bitcast((rnd(a) << 16) | rnd(b), I32)


def _unpack_bf16_pair(p):
    p = pltpu.bitcast(p, U32)
    a = pltpu.bitcast(p & jnp.uint32(0xFFFF0000), F32)
    b = pltpu.bitcast(p << 16, F32)
    return a, b


def _inproj_body(xp_ref, xs_ref, g_ref, w_ref, xg_ref, c_ref, sga_ref, sgb_ref, usc, *,
                 tm, n_p_tiles, d_ssm, d_conv, d_model, col):
    i = pl.program_id(0)
    x = jnp.where(i < n_p_tiles, xp_ref[...], xs_ref[...])
    y = _rms(x, g_ref[...]).astype(BF16)

    def proj(lo, n):
        return jnp.dot(y, w_ref[:, lo:lo + n], preferred_element_type=F32)

    for k in range(d_ssm // col):
        u = proj(k * col, col)
        for b in range(col // LANES):
            usc[k * (col // LANES) + b] = u[:, b * LANES:(b + 1) * LANES]
    for b in range(d_ssm // LANES):
        for s in range(CHUNK):
            c0 = (b * CHUNK + s) * LANES
            xg_ref[:, c0:c0 + LANES] = usc[b, pl.ds(s, tm // CHUNK, stride=CHUNK), :].astype(BF16)
    for k in range(d_conv // col):
        cv = proj(d_ssm + k * col, col)
        cg = proj(d_ssm + d_conv + k * col, col)
        c_ref[:, k * col:(k + 1) * col] = (cv * _sigmoid(cg)).astype(BF16)
    base = d_ssm + 2 * d_conv
    for k in range(d_model // col):
        sga_ref[:, k * col:(k + 1) * col] = _sigmoid(proj(base + k * col, col)).astype(BF16)
        sgb_ref[:, k * col:(k + 1) * col] = _sigmoid(
            proj(base + d_model + k * col, col)).astype(BF16)


def _inproj(xp2, xs2, g, w_bf, *, d_ssm, d_conv, tm):
    n_p, d_model = xp2.shape
    n_s = xs2.shape[0]
    n_p_tiles, n_s_tiles = n_p // tm, n_s // tm
    t = n_p + n_s
    col = min(1024, d_ssm, d_conv, d_model)
    body = functools.partial(_inproj_body, tm=tm, n_p_tiles=n_p_tiles, d_ssm=d_ssm, d_conv=d_conv,
                             d_model=d_model, col=col)
    row = lambda i: (i, 0)
    return pl.pallas_call(
        body,
        grid=(n_p_tiles + n_s_tiles,),
        in_specs=[
            pl.BlockSpec((tm, d_model), lambda i: (jnp.minimum(i, n_p_tiles - 1), 0)),
            pl.BlockSpec((tm, d_model), lambda i: (jnp.maximum(i - n_p_tiles, 0), 0)),
            _resident((1, d_model)),
            _resident(w_bf.shape),
        ],
        out_specs=[pl.BlockSpec((tm // CHUNK, CHUNK * d_ssm), row), pl.BlockSpec((tm, d_conv), row),
                   pl.BlockSpec((tm, d_model), row), pl.BlockSpec((tm, d_model), row)],
        out_shape=[jax.ShapeDtypeStruct((t // CHUNK, CHUNK * d_ssm), BF16),
                   jax.ShapeDtypeStruct((t, d_conv), BF16),
                   jax.ShapeDtypeStruct((t, d_model), BF16), jax.ShapeDtypeStruct((t, d_model), BF16)],
        scratch_shapes=[pltpu.VMEM((d_ssm // LANES, tm, LANES), F32)],
        compiler_params=_cparams(1),
        name="inproj",
    )(xp2, xs2, g, w_bf)


def _meta_body(m_ref, g_ref, w_ref, xg_ref, c_ref, *, d_ssm, d_conv):
    y = _rms(m_ref[...], g_ref[...]).astype(BF16)
    u = jnp.dot(y, w_ref[:, 0:d_ssm], preferred_element_type=F32)
    first = lax.broadcasted_iota(I32, (CHUNK, LANES), 0) == 0
    for b in range(d_ssm // LANES):
        for s in range(CHUNK):
            c0 = (b * CHUNK + s) * LANES
            piece = jnp.broadcast_to(u[s:s + 1, b * LANES:(b + 1) * LANES], (CHUNK, LANES))
            xg_ref[:, c0:c0 + LANES] = jnp.where(first, piece, 0.0).astype(BF16)
    cv = jnp.dot(y, w_ref[:, d_ssm:d_ssm + d_conv], preferred_element_type=F32)
    cg = jnp.dot(y, w_ref[:, d_ssm + d_conv:d_ssm + 2 * d_conv], preferred_element_type=F32)
    c_ref[...] = (cv * _sigmoid(cg)).astype(BF16)


def _meta_inproj(meta, g, w_bf, *, d_ssm, d_conv):
    n_meta, d_model = meta.shape
    ncol = d_ssm + 2 * d_conv
    return pl.pallas_call(
        functools.partial(_meta_body, d_ssm=d_ssm, d_conv=d_conv),
        grid=(1,),
        in_specs=[pl.BlockSpec((n_meta, d_model), lambda i: (0, 0)),
                  pl.BlockSpec((1, d_model), lambda i: (0, 0)),
                  pl.BlockSpec((d_model, ncol), lambda i: (0, 0))],
        out_specs=[pl.BlockSpec((CHUNK, CHUNK * d_ssm), lambda i: (0, 0)),
                   pl.BlockSpec((n_meta, d_conv), lambda i: (0, 0))],
        out_shape=[jax.ShapeDtypeStruct((CHUNK, CHUNK * d_ssm), BF16),
                   jax.ShapeDtypeStruct((n_meta, d_conv), BF16)],
        compiler_params=_cparams(1),
        name="meta_inproj",
    )(meta, g, w_bf)


def _cmul(ar, ai, br, bi):
    return ar * br - ai * bi, ar * bi + ai * br


def _discretize(lam_re, lam_im, log_step):
    lr = jnp.minimum(lam_re, LAM_RE_MAX)
    dt = jnp.exp(log_step)
    mag = jnp.exp(lr * dt)
    ar = mag * jnp.cos(lam_im * dt)
    ai = mag * jnp.sin(lam_im * dt)
    den = lr * lr + lam_im * lam_im
    nr = ar - 1.0
    fr = (nr * lr + ai * lam_im) / den
    fi = (ai * lr - nr * lam_im) / den
    return ar, ai, fr, fi


def _cpow(ar, ai, k, nbits, shape):
    pr = jnp.ones(shape, F32)
    pi = jnp.zeros(shape, F32)
    br = jnp.broadcast_to(ar, shape)
    bi = jnp.broadcast_to(ai, shape)
    kk = jnp.broadcast_to(k, shape)
    for b in range(nbits):
        sel = ((kk >> b) & 1) == 1
        nr, ni = _cmul(pr, pi, br, bi)
        pr = jnp.where(sel, nr, pr)
        pi = jnp.where(sel, ni, pi)
        br, bi = _cmul(br, bi, br, bi)
    return pr, pi


def _ssm_prep_body(lam_re_c, lam_im_c, ls_c, ct_re, ct_im, bt_re, bt_im,
                   lam_re_r, lam_im_r, ls_r, btr_re, btr_im, dpad, xmeta,
                   w1_ref, cst_ref, aq_ref, init_ref, *, n_state, n_ch):
    q = CHUNK
    width = q * n_ch
    par = pl.program_id(0) % 2
    kblk = lax.broadcasted_iota(I32, (1, width), 1) // n_ch
    strips = []
    for d in range(2):
        ar, ai, fr, fi = _discretize(lam_re_c[d], lam_im_c[d], ls_c[d])
        kexp = kblk if d == 0 else (q - 1) - kblk
        wr, wi = _cpow(ar, ai, kexp, 4, (n_state, width))
        gcr, gci = _cmul(ct_re[d], ct_im[d], wr, wi)
        gfr, gfi = _cmul(gcr, gci, fr, fi)
        strips.append(jnp.dot(bt_re[d], gfr, precision=HIGHEST, preferred_element_type=F32)
                      - jnp.dot(bt_im[d], gfi, precision=HIGHEST, preferred_element_type=F32))
        g1r, g1i = _cmul(gcr, gci, ar, ai)
        rowmask = (lax.broadcasted_iota(I32, (2 * n_state, width), 0) // n_state) == par
        rep_r = jnp.concatenate([g1r, g1r], axis=0)
        rep_i = jnp.concatenate([g1i, g1i], axis=0)
        r0 = 4 * n_state * d
        cst_ref[r0:r0 + 2 * n_state, :] = jnp.where(rowmask, rep_r, 0.0).astype(BF16)
        cst_ref[r0 + 2 * n_state:r0 + 4 * n_state, :] = jnp.where(rowmask, -rep_i, 0.0).astype(BF16)

    zf, zb = strips
    zero = jnp.zeros((n_ch, width), F32)
    z512 = jnp.concatenate([zb, zero], axis=1) + pltpu.roll(
        jnp.concatenate([zf, zero], axis=1), (q - 1) * n_ch, 1)
    row = lax.broadcasted_iota(I32, (n_ch, 2 * width), 0)
    lane = lax.broadcasted_iota(I32, (n_ch, 2 * width), 1)
    z512 = z512 + jnp.where(lane - (q - 1) * n_ch == row, dpad[...], 0.0)
    for s in range(q):
        sh = (q - 1 - s) * n_ch
        blk = z512 if sh == 0 else pltpu.roll(z512, 2 * width - sh, 1)
        w1_ref[n_ch * s:n_ch * (s + 1), 0:width] = blk[:, 0:width].astype(BF16)

    parmask = (lax.broadcasted_iota(I32, (1, 2 * n_state), 1) // n_state) == par
    for d in range(2):
        ar, ai, fr, fi = _discretize(lam_re_r[d], lam_im_r[d], ls_r[d])
        pw = [(jnp.ones_like(ar), jnp.zeros_like(ar))]
        for _ in range(q):
            pw.append(_cmul(pw[-1][0], pw[-1][1], ar, ai))
        col = width + 4 * n_state * d
        for s in range(q):
            e = (q - 1 - s) if d == 0 else s
            cr, ci = _cmul(fr, fi, pw[e][0], pw[e][1])
            br, bi = _cmul(btr_re[d], btr_im[d], cr, ci)
            w1_ref[n_ch * s:n_ch * (s + 1), col:col + 2 * n_state] = (
                jnp.where(parmask, br, 0.0).astype(BF16))
            w1_ref[n_ch * s:n_ch * (s + 1), col + 2 * n_state:col + 4 * n_state] = (
                jnp.where(parmask, bi, 0.0).astype(BF16))
        aq_ref[:, col - width:col - width + 2 * n_state] = pw[q][0]
        aq_ref[:, col - width + 2 * n_state:col - width + 4 * n_state] = pw[q][1]

    init_ref[...] = jnp.dot(xmeta[...].astype(BF16), w1_ref[:, width:width + 4 * n_state],
                            preferred_element_type=F32)


def _ssm_prep(lam_re, lam_im, log_step, b_re, b_im, c_re, c_im, d_skip, us_meta):
    _, n_g, n_state, n_ch = b_re.shape
    q = CHUNK
    width = q * n_ch
    dup = lambda x: jnp.concatenate([x, x], axis=-1)
    lam_re_c, lam_im_c = lam_re[..., None], lam_im[..., None]
    ls_c = log_step[..., None, None]
    ct_re = jnp.tile(jnp.swapaxes(c_re, -1, -2), (1, 1, 1, q))
    ct_im = jnp.tile(jnp.swapaxes(c_im, -1, -2), (1, 1, 1, q))
    bt_re, bt_im = jnp.swapaxes(b_re, -1, -2), jnp.swapaxes(b_im, -1, -2)
    lam_re_r, lam_im_r = dup(lam_re)[:, :, None, :], dup(lam_im)[:, :, None, :]
    ls_r = jnp.broadcast_to(log_step[..., None, None], (2, n_g, 1, 2 * n_state))
    btr_re, btr_im = dup(bt_re), dup(bt_im)
    dpad = jnp.pad(d_skip.reshape(n_g, 1, n_ch), ((0, 0), (0, 0), ((q - 1) * n_ch, width)))
    xm = us_meta.astype(F32).reshape(q, n_g, n_ch).transpose(1, 0, 2).reshape(n_g, 1, width)
    xm = jnp.pad(xm, ((0, 0), (0, 7), (0, 0)))

    def dspec(shape):
        return pl.BlockSpec((2, None) + shape, lambda g: (0, g, 0, 0))

    def gspec(shape):
        return pl.BlockSpec((None,) + shape, lambda g: (g, 0, 0))

    body = functools.partial(_ssm_prep_body, n_state=n_state, n_ch=n_ch)
    return pl.pallas_call(
        body,
        grid=(n_g,),
        in_specs=[dspec((n_state, 1)), dspec((n_state, 1)), dspec((1, 1)),
                  dspec((n_state, width)), dspec((n_state, width)),
                  dspec((n_ch, n_state)), dspec((n_ch, n_state)),
                  dspec((1, 2 * n_state)), dspec((1, 2 * n_state)), dspec((1, 2 * n_state)),
                  dspec((n_ch, 2 * n_state)), dspec((n_ch, 2 * n_state)),
                  gspec((1, 2 * width)), gspec((8, width))],
        out_specs=[gspec((width, width + 8 * n_state)), gspec((8 * n_state, width)),
                   gspec((1, 8 * n_state)), gspec((8, 4 * n_state))],
        out_shape=[jax.ShapeDtypeStruct((n_g, width, width + 8 * n_state), BF16),
                   jax.ShapeDtypeStruct((n_g, 8 * n_state, width), BF16),
                   jax.ShapeDtypeStruct((n_g, 1, 8 * n_state), F32),
                   jax.ShapeDtypeStruct((n_g, 8, 4 * n_state), F32)],
        compiler_params=_cparams(1),
        name="ssm_prep",
    )(lam_re_c, lam_im_c, ls_c, ct_re, ct_im, bt_re, bt_im,
      lam_re_r, lam_im_r, ls_r, btr_re, btr_im, dpad, xm)


def _ssm_mm1_body(x_ref, w1_ref, yi_ref, s_ref, *, gb, width, sw):
    for j in range(0, gb, 2):
        r0 = jnp.dot(x_ref[j], w1_ref[j], preferred_element_type=F32)
        r1 = jnp.dot(x_ref[j + 1], w1_ref[j + 1], preferred_element_type=F32)
        yi_ref[j] = r0[:, 0:width]
        yi_ref[j + 1] = r1[:, 0:width]
        s_ref[:, (j // 2) * sw:(j // 2 + 1) * sw] = r0[:, width:] + r1[:, width:]


def _ssm_mm1(xt, w1, *, gb):
    n_g, nc, width = xt.shape
    sw = w1.shape[-1] - width
    body = functools.partial(_ssm_mm1_body, gb=gb, width=width, sw=sw)
    return pl.pallas_call(
        body,
        grid=(n_g // gb,),
        in_specs=[pl.BlockSpec((gb, nc, width), lambda i: (i, 0, 0)),
                  pl.BlockSpec((gb, width, width + sw), lambda i: (i, 0, 0))],
        out_specs=[pl.BlockSpec((gb, nc, width), lambda i: (i, 0, 0)),
                   pl.BlockSpec((nc, gb // 2 * sw), lambda i: (0, i))],
        out_shape=[jax.ShapeDtypeStruct((n_g, nc, width), F32),
                   jax.ShapeDtypeStruct((nc, n_g // 2 * sw), F32)],
        compiler_params=_cparams(1),
        name="ssm_mm1",
    )(xt, w1)


def _ssm_scan_body(sf_ref, sb_ref, aq_ref, init_ref, xf_ref, xb_ref, stf, stb, *,
                   cb, hw, n_pblk, blk_pseq, blk_sseq):
    j = pl.program_id(1)
    pos = jnp.where(j < n_pblk, lax.rem(j, blk_pseq), lax.rem(jnp.maximum(j - n_pblk, 0), blk_sseq))

    @pl.when(pos == 0)
    def _():
        stf[...] = init_ref[...]
        stb[...] = jnp.zeros_like(stb)

    afr, afi = aq_ref[:, 0:hw], aq_ref[:, hw:2 * hw]
    abr, abi = aq_ref[:, 2 * hw:3 * hw], aq_ref[:, 3 * hw:4 * hw]

    def body(i, carry):
        fr, fi, br, bi = carry
        ib = cb - 1 - i
        xf_ref[i, :, 0:hw] = fr
        xf_ref[i, :, hw:2 * hw] = fi
        xb_ref[ib, :, 0:hw] = br
        xb_ref[ib, :, hw:2 * hw] = bi
        nfr = afr * fr - afi * fi + sf_ref[i, :, 0:hw]
        nfi = afr * fi + afi * fr + sf_ref[i, :, hw:2 * hw]
        nbr = abr * br - abi * bi + sb_ref[ib, :, 0:hw]
        nbi = abr * bi + abi * br + sb_ref[ib, :, hw:2 * hw]
        return nfr, nfi, nbr, nbi

    fr, fi, br, bi = lax.fori_loop(
        0, cb, body, (stf[:, 0:hw], stf[:, hw:2 * hw], stb[:, 0:hw], stb[:, hw:2 * hw]))
    stf[:, 0:hw] = fr
    stf[:, hw:2 * hw] = fi
    stb[:, 0:hw] = br
    stb[:, hw:2 * hw] = bi


def _ssm_scan(s3, aq, init, *, cb, chunks_pseq, chunks_sseq, n_pchunks):
    nc, n_pair, sw = s3.shape
    hw = sw // 4
    pb = min(8, n_pair)
    n_pblk, blk_pseq, blk_sseq = n_pchunks // cb, chunks_pseq // cb, chunks_sseq // cb

    def bwd_block(j):
        in_p = j < n_pblk
        pos = jnp.where(in_p, lax.rem(j, blk_pseq), lax.rem(jnp.maximum(j - n_pblk, 0), blk_sseq))
        ln = jnp.where(in_p, blk_pseq, blk_sseq)
        return j - pos + ln - 1 - pos

    body = functools.partial(_ssm_scan_body, cb=cb, hw=hw, n_pblk=n_pblk, blk_pseq=blk_pseq,
                             blk_sseq=blk_sseq)
    return pl.pallas_call(
        body,
        grid=(n_pair // pb, nc // cb),
        in_specs=[pl.BlockSpec((cb, pb, 2 * hw), lambda p, j: (j, p, 0)),
                  pl.BlockSpec((cb, pb, 2 * hw), lambda p, j: (bwd_block(j), p, 1)),
                  pl.BlockSpec((pb, sw), lambda p, j: (p, 0)),
                  pl.BlockSpec((pb, 2 * hw), lambda p, j: (p, 0))],
        out_specs=[pl.BlockSpec((cb, pb, 2 * hw), lambda p, j: (j, p, 0)),
                   pl.BlockSpec((cb, pb, 2 * hw), lambda p, j: (bwd_block(j), p, 0))],
        out_shape=[jax.ShapeDtypeStruct((nc, n_pair, 2 * hw), F32),
                   jax.ShapeDtypeStruct((nc, n_pair, 2 * hw), F32)],
        scratch_shapes=[pltpu.VMEM((pb, 2 * hw), F32), pltpu.VMEM((pb, 2 * hw), F32)],
        compiler_params=_cparams(2),
        name="ssm_scan",
    )(s3, s3, aq, init)


def _gelu_tanh(x):
    return 0.5 * x * (1.0 + jnp.tanh(0.7978845608028654 * (x + 0.044715 * (x * x * x))))


def _ssm_mm2_body(yi_ref, xf_ref, xb_ref, cst_ref, z_ref, *, gb, hs):
    for j in range(gb):
        cols = slice((j // 2) * hs, (j // 2 + 1) * hs)
        y = (yi_ref[j]
             + jnp.dot(xf_ref[:, cols].astype(BF16), cst_ref[j, 0:hs, :], preferred_element_type=F32)
             + jnp.dot(xb_ref[:, cols].astype(BF16), cst_ref[j, hs:2 * hs, :],
                       preferred_element_type=F32))
        z_ref[j] = _gelu_tanh(y).astype(BF16)


def _ssm_mm2(yi, xf2, xb2, cst, *, gb):
    n_g, nc, width = yi.shape
    hs = cst.shape[1] // 2
    body = functools.partial(_ssm_mm2_body, gb=gb, hs=hs)
    return pl.pallas_call(
        body,
        grid=(n_g // gb,),
        in_specs=[pl.BlockSpec((gb, nc, width), lambda i: (i, 0, 0)),
                  pl.BlockSpec((nc, gb // 2 * hs), lambda i: (0, i)),
                  pl.BlockSpec((nc, gb // 2 * hs), lambda i: (0, i)),
                  pl.BlockSpec((gb, 2 * hs, width), lambda i: (i, 0, 0))],
        out_specs=pl.BlockSpec((gb, nc, width), lambda i: (i, 0, 0)),
        out_shape=jax.ShapeDtypeStruct((n_g, nc, width), BF16),
        compiler_params=_cparams(1),
        name="ssm_mm2",
    )(yi, xf2, xb2, cst)


def _spread(x, n_ch, shift):
    rows, w = x.shape
    lane = lax.broadcasted_iota(I32, (rows, LANES), 1)
    keep = (lane >= shift) & (lane < shift + n_ch)
    out = []
    for b in range(w // n_ch):
        src = (b * n_ch // LANES) * LANES
        amount = lax.rem(shift + LANES - (b * n_ch) % LANES, LANES)
        out.append(jnp.where(keep, pltpu.roll(x[:, src:src + LANES], amount, 1), 0.0))
    return jnp.concatenate(out, axis=1)


def _tile_lanes(x, n_ch, width):
    span = n_ch
    while span < LANES:
        x = x + pltpu.roll(x, span, 1)
        span *= 2
    return jnp.concatenate([x] * (width // LANES), axis=1)


def _sg_prep_body(lam_re_c, lam_im_c, ls_c, ct_re, ct_im, bt_re, bt_im,
                  lam_re_r, lam_im_r, ls_r, btr_re, btr_im, dpad,
                  m_ref, b_ref, c_ref, aq_ref, *, n_state, n_ch):
    q = CHUNK
    width = q * n_ch
    per_blk = LANES // n_ch
    gl = pl.program_id(0) % per_blk
    shift = gl * n_ch
    kblk = lax.broadcasted_iota(I32, (1, width), 1) // n_ch
    strips = []
    for d in range(2):
        ar, ai, fr, fi = _discretize(lam_re_c[d], lam_im_c[d], ls_c[d])
        kexp = kblk if d == 0 else (q - 1) - kblk
        wr, wi = _cpow(ar, ai, kexp, 4, (n_state, width))
        gcr, gci = _cmul(_tile_lanes(ct_re[d], n_ch, width), _tile_lanes(ct_im[d], n_ch, width),
                         wr, wi)
        gfr, gfi = _cmul(gcr, gci, fr, fi)
        strips.append(jnp.dot(bt_re[d], gfr, precision=HIGHEST, preferred_element_type=F32)
                      - jnp.dot(bt_im[d], gfi, precision=HIGHEST, preferred_element_type=F32))
        g1r, g1i = _cmul(gcr, gci, ar, ai)
        for comp, val in ((0, g1r), (1, -g1i)):
            c_ref[2 * d + comp] = _spread(val, n_ch, shift).astype(BF16)

    zf, zb = strips
    zero = jnp.zeros((n_ch, width), F32)
    z512 = jnp.concatenate([zb, zero], axis=1) + pltpu.roll(
        jnp.concatenate([zf, zero], axis=1), (q - 1) * n_ch, 1)
    row = lax.broadcasted_iota(I32, (n_ch, 2 * width), 0)
    lane = lax.broadcasted_iota(I32, (n_ch, 2 * width), 1)
    z512 = z512 + jnp.where(lane - (q - 1) * n_ch == row, dpad[...], 0.0)
    zwide = _spread(z512, n_ch, shift)
    for s in range(q):
        lo = (q - 1 - s) * LANES
        m_ref[s] = zwide[:, lo:lo + q * LANES].astype(BF16)

    parmask = (lax.broadcasted_iota(I32, (1, 2 * n_state), 1) // n_state) == gl % 2
    pieces = [[None] * 4 for _ in range(q)]
    for d in range(2):
        ar, ai, fr, fi = _discretize(lam_re_r[d], lam_im_r[d], ls_r[d])
        pw = [(jnp.ones_like(ar), jnp.zeros_like(ar))]
        for _ in range(q):
            pw.append(_cmul(pw[-1][0], pw[-1][1], ar, ai))
        for s in range(q):
            e = (q - 1 - s) if d == 0 else s
            cr, ci = _cmul(fr, fi, pw[e][0], pw[e][1])
            br, bi = _cmul(btr_re[d], btr_im[d], cr, ci)
            pieces[s][2 * d] = jnp.where(parmask, br, 0.0)
            pieces[s][2 * d + 1] = jnp.where(parmask, bi, 0.0)
        aq_ref[:, 2 * d * n_state:(2 * d + 1) * n_state] = pw[q][0][:, 0:n_state]
        aq_ref[:, (2 * d + 1) * n_state:(2 * d + 2) * n_state] = pw[q][1][:, 0:n_state]
    n_pair = per_blk // 2
    for s in range(q):
        cols = []
        for seg in range(4):
            for blk in range(n_pair):
                cols.append(jnp.where(gl // 2 == blk, pieces[s][seg], 0.0))
        b_ref[s] = jnp.concatenate(cols, axis=1).astype(BF16)


def _sg_prep(lam_re, lam_im, log_step, b_re, b_im, c_re, c_im, d_skip):
    _, n_g, n_state, n_ch = b_re.shape
    q = CHUNK
    width = q * n_ch
    per_blk = LANES // n_ch
    n_blk = n_g // per_blk
    sw = 4 * per_blk * n_state
    dup = lambda x: jnp.concatenate([x, x], axis=-1)
    lane_pad = lambda x: jnp.pad(x, ((0, 0),) * 3 + ((0, LANES - n_ch),))
    lam_re_c, lam_im_c = lam_re[..., None], lam_im[..., None]
    ls_c = log_step[..., None, None]
    ct_re = lane_pad(jnp.swapaxes(c_re, -1, -2))
    ct_im = lane_pad(jnp.swapaxes(c_im, -1, -2))
    bt_re, bt_im = jnp.swapaxes(b_re, -1, -2), jnp.swapaxes(b_im, -1, -2)
    lam_re_r, lam_im_r = dup(lam_re)[:, :, None, :], dup(lam_im)[:, :, None, :]
    ls_r = jnp.broadcast_to(log_step[..., None, None], (2, n_g, 1, 2 * n_state))
    btr_re, btr_im = dup(bt_re), dup(bt_im)
    dpad = jnp.pad(d_skip.reshape(n_g, 1, n_ch), ((0, 0), (0, 0), ((q - 1) * n_ch, width)))

    def dspec(shape):
        return pl.BlockSpec((2, None) + shape, lambda g: (0, g, 0, 0))

    body = functools.partial(_sg_prep_body, n_state=n_state, n_ch=n_ch)
    m4, b4, c5, aq = pl.pallas_call(
        body,
        grid=(n_g,),
        in_specs=[dspec((n_state, 1)), dspec((n_state, 1)), dspec((1, 1)),
                  dspec((n_state, LANES)), dspec((n_state, LANES)),
                  dspec((n_ch, n_state)), dspec((n_ch, n_state)),
                  dspec((1, 2 * n_state)), dspec((1, 2 * n_state)), dspec((1, 2 * n_state)),
                  dspec((n_ch, 2 * n_state)), dspec((n_ch, 2 * n_state)),
                  pl.BlockSpec((None, 1, 2 * width), lambda g: (g, 0, 0))],
        out_specs=[
            pl.BlockSpec((None, q, n_ch, q * LANES), lambda g: (g // per_blk, 0, g % per_blk, 0)),
            pl.BlockSpec((None, q, n_ch, sw), lambda g: (g // per_blk, 0, g % per_blk, 0)),
            pl.BlockSpec((None, 4, None, n_state, q * LANES),
                         lambda g: (g // per_blk, 0, g % per_blk, 0, 0)),
            pl.BlockSpec((None, 1, 4 * n_state), lambda g: (g, 0, 0))],
        out_shape=[jax.ShapeDtypeStruct((n_blk, q, LANES, q * LANES), BF16),
                   jax.ShapeDtypeStruct((n_blk, q, LANES, sw), BF16),
                   jax.ShapeDtypeStruct((n_blk, 4, per_blk, n_state, q * LANES), BF16),
                   jax.ShapeDtypeStruct((n_g, 1, 4 * n_state), F32)],
        compiler_params=_cparams(1),
        name="ssm_prep",
    )(lam_re_c, lam_im_c, ls_c, ct_re, ct_im, bt_re, bt_im,
      lam_re_r, lam_im_r, ls_r, btr_re, btr_im, dpad)
    intra = m4.reshape(n_blk, q * LANES, q * LANES)
    bst = b4.reshape(n_blk, q * LANES, sw)
    cst = c5.reshape(n_blk, sw, q * LANES)
    a = aq.reshape(n_blk, per_blk // 2, 2, 2, 2, n_state)
    a = a.transpose(3, 4, 0, 1, 2, 5).reshape(2, 2, n_blk, per_blk // 2, LANES)
    coef = []
    for d in range(2):
        re, im = a[d, 0], a[d, 1]
        coef.append((jnp.concatenate([re, re], axis=1), jnp.concatenate([-im, im], axis=1)))
    return intra, bst, cst, coef


def _sg_mm_intra_body(x_ref, m_ref, y_ref):
    y_ref[...] = jnp.dot(x_ref[...], m_ref[...], preferred_element_type=F32).astype(BF16)


def _sg_mm_intra(xg, intra, *, rt):
    nc = xg.shape[0]
    n_blk, kw, _ = intra.shape
    return pl.pallas_call(
        _sg_mm_intra_body,
        grid=(n_blk, nc // rt),
        in_specs=[pl.BlockSpec((rt, kw), lambda b, i: (i, b)),
                  pl.BlockSpec((None, kw, kw), lambda b, i: (b, 0, 0))],
        out_specs=pl.BlockSpec((rt, kw), lambda b, i: (i, b)),
        out_shape=jax.ShapeDtypeStruct((nc, n_blk * kw), BF16),
        compiler_params=_cparams(2),
        name="ssm_intra",
    )(xg, intra)


def _sg_mm_state_body(x_ref, xm_ref, b_ref, sf_ref, sb_ref, init_ref, *, rt, n_sub):
    r = jnp.dot(x_ref[...], b_ref[...], preferred_element_type=F32)
    for k in range(n_sub):
        sf_ref[pl.ds(k, rt, stride=n_sub), :] = r[:, k * LANES:(k + 1) * LANES]
        sb_ref[pl.ds(k, rt, stride=n_sub), :] = r[:, (n_sub + k) * LANES:(n_sub + k + 1) * LANES]

    @pl.when(pl.program_id(1) == 0)
    def _():
        r0 = jnp.dot(xm_ref[...], b_ref[...], preferred_element_type=F32)
        for k in range(n_sub):
            init_ref[k:k + 1, :] = r0[0:1, k * LANES:(k + 1) * LANES]


def _sg_mm_state(xg, xg_meta, bst, *, rt):
    nc = xg.shape[0]
    n_blk, kw, sw = bst.shape
    n_sub = sw // (2 * LANES)
    n_rt = nc // rt
    body = functools.partial(_sg_mm_state_body, rt=rt, n_sub=n_sub)
    dense = jax.ShapeDtypeStruct((n_blk * nc * n_sub, LANES), F32)
    dspec = pl.BlockSpec((rt * n_sub, LANES), lambda b, i: (b * n_rt + i, 0))
    return pl.pallas_call(
        body,
        grid=(n_blk, n_rt),
        in_specs=[pl.BlockSpec((rt, kw), lambda b, i: (i, b)),
                  pl.BlockSpec((CHUNK, kw), lambda b, i: (0, b)),
                  pl.BlockSpec((None, kw, sw), lambda b, i: (b, 0, 0))],
        out_specs=[dspec, dspec, pl.BlockSpec((None, n_sub, LANES), lambda b, i: (b, 0, 0))],
        out_shape=[dense, dense, jax.ShapeDtypeStruct((n_blk, n_sub, LANES), F32)],
        compiler_params=_cparams(2),
        name="ssm_state",
    )(xg, xg_meta, bst)


def _sg_scan_body(sf_ref, sb_ref, a1f_ref, a2f_ref, a1b_ref, a2b_ref, init_ref, xf_ref, xb_ref,
                  stf, stb, *, cb, n_blk, half, n_pblk, blk_pseq, blk_sseq):
    j = pl.program_id(0)
    pos = jnp.where(j < n_pblk, lax.rem(j, blk_pseq), lax.rem(jnp.maximum(j - n_pblk, 0), blk_sseq))

    @pl.when(pos == 0)
    def _():
        stf[...] = init_ref[...]
        stb[...] = jnp.zeros_like(stb)

    def body(i, carry):
        ef, eb = carry
        ib = cb - 1 - i
        nf, nb = [], []
        for g in range(n_blk):
            xf_ref[g, i] = ef[g]
            xb_ref[g, ib] = eb[g]
            nf.append(a1f_ref[g] * ef[g] + a2f_ref[g] * pltpu.roll(ef[g], half, 0) + sf_ref[g, i])
            nb.append(a1b_ref[g] * eb[g] + a2b_ref[g] * pltpu.roll(eb[g], half, 0) + sb_ref[g, ib])
        return tuple(nf), tuple(nb)

    ef, eb = lax.fori_loop(0, cb, body, (tuple(stf[g] for g in range(n_blk)),
                                         tuple(stb[g] for g in range(n_blk))))
    for g in range(n_blk):
        stf[g] = ef[g]
        stb[g] = eb[g]


def _sg_scan(sf, sb, coef, init, *, cb, chunks_pseq, chunks_sseq, n_pchunks):
    n_blk, nc, n_sub, _ = sf.shape
    n_pblk, blk_pseq, blk_sseq = n_pchunks // cb, chunks_pseq // cb, chunks_sseq // cb

    def bwd_block(j):
        in_p = j < n_pblk
        pos = jnp.where(in_p, lax.rem(j, blk_pseq), lax.rem(jnp.maximum(j - n_pblk, 0), blk_sseq))
        ln = jnp.where(in_p, blk_pseq, blk_sseq)
        return j - pos + ln - 1 - pos

    body = functools.partial(_sg_scan_body, cb=cb, n_blk=n_blk, half=n_sub // 2, n_pblk=n_pblk,
                             blk_pseq=blk_pseq, blk_sseq=blk_sseq)
    fwd = pl.BlockSpec((n_blk, cb, n_sub, LANES), lambda j: (0, j, 0, 0))
    bwd = pl.BlockSpec((n_blk, cb, n_sub, LANES), lambda j: (0, bwd_block(j), 0, 0))
    small = pl.BlockSpec((n_blk, n_sub, LANES), lambda j: (0, 0, 0))
    return pl.pallas_call(
        body,
        grid=(nc // cb,),
        in_specs=[fwd, bwd, small, small, small, small, small],
        out_specs=[fwd, bwd],
        out_shape=[jax.ShapeDtypeStruct(sf.shape, F32), jax.ShapeDtypeStruct(sb.shape, F32)],
        scratch_shapes=[pltpu.VMEM((n_blk, n_sub, LANES), F32), pltpu.VMEM((n_blk, n_sub, LANES), F32)],
        compiler_params=_cparams(1),
        name="ssm_scan",
    )(sf, sb, coef[0][0], coef[0][1], coef[1][0], coef[1][1], init)


def _gelu_tanh(x):
    return 0.5 * x * (1.0 + jnp.tanh(0.7978845608028654 * (x + 0.044715 * (x * x * x))))


def _sg_mm_out_body(y_ref, xf_ref, xb_ref, c_ref, z_ref, *, rt, n_sub):
    cols = [xf_ref[pl.ds(k, rt, stride=n_sub), :] for k in range(n_sub)]
    cols += [xb_ref[pl.ds(k, rt, stride=n_sub), :] for k in range(n_sub)]
    state = jnp.concatenate(cols, axis=1).astype(BF16)
    y = y_ref[...].astype(F32) + jnp.dot(state, c_ref[...], preferred_element_type=F32)
    z_ref[...] = _gelu_tanh(y).astype(BF16)


def _sg_mm_out(yi, xf, xb, cst, *, rt):
    nc = yi.shape[0]
    n_blk, sw, kw = cst.shape
    n_sub = sw // (2 * LANES)
    n_rt = nc // rt
    dspec = pl.BlockSpec((rt * n_sub, LANES), lambda b, i: (b * n_rt + i, 0))
    return pl.pallas_call(
        functools.partial(_sg_mm_out_body, rt=rt, n_sub=n_sub),
        grid=(n_blk, n_rt),
        in_specs=[pl.BlockSpec((rt, kw), lambda b, i: (i, b)), dspec, dspec,
                  pl.BlockSpec((None, sw, kw), lambda b, i: (b, 0, 0))],
        out_specs=pl.BlockSpec((rt, kw), lambda b, i: (i, b)),
        out_shape=jax.ShapeDtypeStruct((nc, n_blk * kw), BF16),
        compiler_params=_cparams(2),
        name="ssm_out",
    )(yi, xf, xb, cst)


def _mix_body(xp_ref, xs_ref, z_ref, c_ref, cprev_ref, cnext_ref, cmeta_ref, sga_ref, sgb_ref,
              wglu_ref, wpw_ref, wout_ref, wdw_ref, bdw_ref, lng_ref, lnb_ref, gffn_ref,
              rw_ref, rb_ref,
              h1_ref, v_ref, route_ref, cnt_ref,
              cw_ref, conv_ref, zsc_ref, *,
              tm, n_p_tiles, tiles_per_pseq, tiles_per_sseq, d_model, d_ssm, d_conv, conv_w,
              n_grp, n_exp, exp_per_grp, rc):
    i = pl.program_id(0)
    in_prompt = i < n_p_tiles
    x = jnp.where(in_prompt, xp_ref[...], xs_ref[...])
    pos_p = lax.rem(i, tiles_per_pseq)
    pos_s = lax.rem(jnp.maximum(i - n_p_tiles, 0), tiles_per_sseq)
    is_start = jnp.where(in_prompt, pos_p == 0, pos_s == 0)
    is_end = jnp.where(in_prompt, pos_p == tiles_per_pseq - 1, pos_s == tiles_per_sseq - 1)

    halo = CHUNK
    pad = conv_w // 2
    cw_ref[0:halo, :] = jnp.where(is_start, cmeta_ref[...], cprev_ref[...]).astype(F32)
    cw_ref[halo:halo + tm, :] = c_ref[...].astype(F32)
    cw_ref[halo + tm:2 * halo + tm, :] = jnp.where(is_end, 0.0, cnext_ref[...].astype(F32))
    sub = 8
    for lc in range(d_conv // LANES):
        ls = slice(lc * LANES, (lc + 1) * LANES)
        for r0 in range(0, tm, rc):
            out = None
            for r in range(sub):
                part = None
                for q in range((conv_w + halo - pad) // sub + 1):
                    k = sub * q + r - (halo - pad)
                    if 0 <= k < conv_w:
                        term = wdw_ref[k:k + 1, ls] * cw_ref[r0 + sub * q:r0 + sub * q + rc + sub, ls]
                        part = term if part is None else part + term
                if part is not None:
                    shifted = part[r:r + rc]
                    out = shifted if out is None else out + shifted
            conv_ref[r0:r0 + rc, ls] = out
    cc = conv_ref[...] + bdw_ref[...]
    mu = jnp.mean(cc, axis=-1, keepdims=True)
    var = jnp.mean(jnp.square(cc - mu), axis=-1, keepdims=True)
    cc = (cc - mu) * lax.rsqrt(var + EPS) * lng_ref[...] + lnb_ref[...]
    cc = (cc * _sigmoid(cc)).astype(BF16)
    y_b = jnp.dot(cc, wpw_ref[...], preferred_element_type=F32)

    for b in range(d_ssm // LANES):
        for s in range(CHUNK):
            c0 = (b * CHUNK + s) * LANES
            zsc_ref[b, pl.ds(s, tm // CHUNK, stride=CHUNK), :] = z_ref[:, c0:c0 + LANES].astype(F32)
    z = jnp.concatenate([zsc_ref[b] for b in range(d_ssm // LANES)], axis=1).astype(BF16)
    va = jnp.dot(z, wglu_ref[:, 0:d_model], preferred_element_type=F32)
    ga = jnp.dot(z, wglu_ref[:, d_model:2 * d_model], preferred_element_type=F32)
    y_a = va * _sigmoid(ga)
    merged = (sga_ref[...].astype(F32) * y_a + sgb_ref[...].astype(F32) * y_b).astype(BF16)
    h1 = x + jnp.dot(merged, wout_ref[...], preferred_element_type=F32)
    h1_ref[...] = h1
    v = _rms(h1, gffn_ref[...])
    half = d_model // 2
    v_ref[...] = _pack_bf16_pair(v[:, 0:half], v[:, half:d_model])

    v_hi = v.astype(BF16)
    v_lo = (v - v_hi.astype(F32)).astype(BF16)
    acc = (jnp.dot(v_hi, rw_ref[...], preferred_element_type=F32)
           + jnp.dot(v_lo, rw_ref[...], preferred_element_type=F32))
    logits = acc + pltpu.roll(acc, LANES // 2, 1) + rb_ref[...]
    lane = lax.broadcasted_iota(I32, (tm, LANES), 1).astype(F32)
    big = jnp.float32(1e9)
    neg = jnp.float32(-jnp.inf)
    gmask = lane < n_grp
    lg = jnp.where(gmask, logits, neg)
    gmax = jnp.max(lg, axis=-1, keepdims=True)
    grp = jnp.min(jnp.where(lg == gmax, lane, big), axis=-1, keepdims=True)
    p_grp = 1.0 / jnp.sum(jnp.where(gmask, jnp.exp(logits - gmax), 0.0), axis=-1, keepdims=True)
    lo = n_grp + grp * exp_per_grp
    emask = (lane >= lo) & (lane < lo + exp_per_grp)
    le = jnp.where(emask, logits, neg)
    m1 = jnp.max(le, axis=-1, keepdims=True)
    i1 = jnp.min(jnp.where(le == m1, lane, big), axis=-1, keepdims=True)
    le2 = jnp.where(lane == i1, neg, le)
    m2 = jnp.max(le2, axis=-1, keepdims=True)
    i2 = jnp.min(jnp.where(le2 == m2, lane, big), axis=-1, keepdims=True)
    t = jnp.exp(m2 - m1)
    w1 = 1.0 / (1.0 + t)
    e1 = i1 - n_grp
    e2 = i2 - n_grp
    route_ref[...] = jnp.where(lane == 0, e1, jnp.where(lane == 1, e2, jnp.where(
        lane == 2, p_grp * w1, jnp.where(lane == 3, p_grp * (t * w1), 0.0))))

    @pl.when(i == 0)
    def _():
        cnt_ref[...] = jnp.zeros_like(cnt_ref)

    hot = jnp.where((lane == e1) | (lane == e2), 1.0, 0.0)
    cnt_ref[...] += jnp.sum(hot, axis=0, keepdims=True)


def _mix(xp2, xs2, z, c, c_meta, sga, sgb, wglu, wpw, wout, wdw, bdw, lng, lnb, gffn, rw, rb, *,
         tm, p_seq, s_seq, n_grp, n_exp):
    n_p, d_model = xp2.shape
    n_s = xs2.shape[0]
    t = n_p + n_s
    d_ssm, d_conv = z.shape[1] // CHUNK, c.shape[1]
    conv_w = wdw.shape[0]
    n_p_tiles, n_s_tiles = n_p // tm, n_s // tm
    hpt = tm // CHUNK
    n_hblk = t // CHUNK
    body = functools.partial(
        _mix_body, tm=tm, n_p_tiles=n_p_tiles, tiles_per_pseq=p_seq // tm,
        tiles_per_sseq=s_seq // tm, d_model=d_model, d_ssm=d_ssm, d_conv=d_conv, conv_w=conv_w,
        n_grp=n_grp, n_exp=n_exp, exp_per_grp=n_exp // n_grp, rc=min(128, tm))
    row = lambda i: (i, 0)
    return pl.pallas_call(
        body,
        grid=(n_p_tiles + n_s_tiles,),
        in_specs=[
            pl.BlockSpec((tm, d_model), lambda i: (jnp.minimum(i, n_p_tiles - 1), 0)),
            pl.BlockSpec((tm, d_model), lambda i: (jnp.maximum(i - n_p_tiles, 0), 0)),
            pl.BlockSpec((tm // CHUNK, CHUNK * d_ssm), row),
            pl.BlockSpec((tm, d_conv), row),
            pl.BlockSpec((CHUNK, d_conv), lambda i: (jnp.maximum(i * hpt - 1, 0), 0)),
            pl.BlockSpec((CHUNK, d_conv), lambda i: (jnp.minimum((i + 1) * hpt, n_hblk - 1), 0)),
            _resident(c_meta.shape),
            pl.BlockSpec((tm, d_model), row),
            pl.BlockSpec((tm, d_model), row),
            _resident(wglu.shape), _resident(wpw.shape), _resident(wout.shape),
            _resident(wdw.shape), _resident(bdw.shape), _resident(lng.shape), _resident(lnb.shape),
            _resident(gffn.shape), _resident(rw.shape), _resident(rb.shape),
        ],
        out_specs=[pl.BlockSpec((tm, d_model), row), pl.BlockSpec((tm, d_model // 2), row),
                   pl.BlockSpec((tm, LANES), row), pl.BlockSpec((1, LANES), lambda i: (0, 0))],
        out_shape=[jax.ShapeDtypeStruct((t, d_model), F32),
                   jax.ShapeDtypeStruct((t, d_model // 2), I32),
                   jax.ShapeDtypeStruct((t, LANES), F32),
                   jax.ShapeDtypeStruct((1, LANES), F32)],
        scratch_shapes=[pltpu.VMEM((tm + 2 * CHUNK, d_conv), F32), pltpu.VMEM((tm, d_conv), F32),
                        pltpu.VMEM((d_ssm // LANES, tm, LANES), F32)],
        compiler_params=_cparams(1),
        name="mix",
    )(xp2, xs2, z, c, c, c, c_meta, sga, sgb, wglu, wpw, wout, wdw, bdw, lng, lnb, gffn, rw, rb)


def _positions_body(route_ref, offs_ref, dest_ref, carry_ref, *, te):
    @pl.when(pl.program_id(0) == 0)
    def _():
        carry_ref[...] = jnp.zeros_like(carry_ref)

    lane = lax.broadcasted_iota(I32, (te, LANES), 1).astype(F32)
    r = route_ref[...]
    oh1 = lane == r[:, 0:1]
    oh2 = lane == r[:, 1:2]
    both = jnp.where(oh1 | oh2, 1.0, 0.0)
    tri = jnp.where(lax.broadcasted_iota(I32, (te, te), 0) > lax.broadcasted_iota(I32, (te, te), 1),
                    1.0, 0.0).astype(BF16)
    before = jnp.dot(tri, both.astype(BF16), preferred_element_type=F32)
    base = before + carry_ref[...] + offs_ref[...]
    d1 = jnp.sum(jnp.where(oh1, base, 0.0), axis=-1, keepdims=True)
    d2 = jnp.sum(jnp.where(oh2, base, 0.0), axis=-1, keepdims=True)
    dest_ref[...] = jnp.where(lane == 0, d1, jnp.where(lane == 1, d2, 0.0)).astype(I32)
    carry_ref[...] += jnp.sum(both, axis=0, keepdims=True)


def _positions(route, offs, *, te):
    t = route.shape[0]
    return pl.pallas_call(
        functools.partial(_positions_body, te=te),
        grid=(t // te,),
        in_specs=[pl.BlockSpec((te, LANES), lambda i: (i, 0)),
                  pl.BlockSpec((1, LANES), lambda i: (0, 0))],
        out_specs=pl.BlockSpec((te, LANES), lambda i: (i, 0)),
        out_shape=jax.ShapeDtypeStruct((t, LANES), I32),
        scratch_shapes=[pltpu.VMEM((1, LANES), F32)],
        compiler_params=_cparams(1),
        name="positions",
    )(route, offs)


def _row_copy(src_ref, src_row, dst_ref, dst_row, sem):
    return pltpu.make_async_copy(src_ref.at[pl.ds(src_row, 1), :],
                                 dst_ref.at[pl.ds(dst_row, 1), :], sem)


def _dispatch_body(dest_hbm, pad_tile_ref, v_ref, xs_hbm, dsm, zbuf, sem, dsem, zsem, *,
                   td, tme, n_exp):
    i = pl.program_id(0)

    @pl.when(i == 0)
    def _():
        zbuf[...] = jnp.zeros_like(zbuf)

        def zero_tile(e):
            row0 = pl.multiple_of(pad_tile_ref[e] * tme, tme)
            return pltpu.make_async_copy(zbuf, xs_hbm.at[pl.ds(row0, tme), :], zsem)
        for e in range(n_exp):
            @pl.when(pad_tile_ref[e] >= 0)
            def _(e=e):
                zero_tile(e).start()
        for e in range(n_exp):
            @pl.when(pad_tile_ref[e] >= 0)
            def _(e=e):
                zero_tile(e).wait()

    cp = pltpu.make_async_copy(dest_hbm.at[pl.ds(i, 1), :], dsm, dsem)
    cp.start()
    cp.wait()

    def issue(r, _):
        _row_copy(v_ref, r, xs_hbm, dsm[0, 2 * r], sem).start(priority=0)
        _row_copy(v_ref, r, xs_hbm, dsm[0, 2 * r + 1], sem).start(priority=1)
        return 0
    lax.fori_loop(0, td, issue, 0, unroll=8)

    def drain(r, _):
        _row_copy(v_ref, 0, xs_hbm, 0, sem).wait()
        _row_copy(v_ref, 0, xs_hbm, 0, sem).wait()
        return 0
    lax.fori_loop(0, td, drain, 0)


def _dispatch(dest2, pad_tile, v, *, td, tme, n_rows):
    t, w = v.shape
    n_exp = pad_tile.shape[0]
    return pl.pallas_call(
        functools.partial(_dispatch_body, td=td, tme=tme, n_exp=n_exp),
        grid=(t // td,),
        in_specs=[pl.BlockSpec(memory_space=pl.ANY),
                  pl.BlockSpec(memory_space=pltpu.SMEM),
                  pl.BlockSpec((td, w), lambda i: (i, 0))],
        out_specs=pl.BlockSpec(memory_space=pl.ANY),
        out_shape=jax.ShapeDtypeStruct((n_rows, w), U32),
        scratch_shapes=[pltpu.SMEM((1, 2 * td), I32), pltpu.VMEM((tme, w), U32),
                        pltpu.SemaphoreType.DMA, pltpu.SemaphoreType.DMA, pltpu.SemaphoreType.DMA],
        compiler_params=_cparams(1),
        name="dispatch",
    )(dest2, pad_tile, v)


def _experts_body(te_ref, first_ref, last_ref, nxt_ref, nused_ref,
                  xs_ref, w1_hbm, w3_hbm, w2_hbm, o_ref,
                  w1b, w3b, w2b, st1, st3, st2, sem, *, half, n_conv):
    i = pl.program_id(0)
    used = i < nused_ref[0]
    nxt = nxt_ref[i]

    def copies(e):
        return (pltpu.make_async_copy(w1_hbm.at[e], st1, sem.at[0]),
                pltpu.make_async_copy(w3_hbm.at[e], st3, sem.at[1]),
                pltpu.make_async_copy(w2_hbm.at[e], st2, sem.at[2]))

    def convert():
        r13 = st1.shape[0] // n_conv
        r2 = st2.shape[0] // n_conv

        def body(c, _):
            rows = pl.ds(pl.multiple_of(c * r13, r13), r13)
            w1b[rows, :] = st1[rows, :].astype(BF16)
            w3b[rows, :] = st3[rows, :].astype(BF16)
            rows2 = pl.ds(pl.multiple_of(c * r2, r2), r2)
            w2b[rows2, :] = st2[rows2, :].astype(BF16)
            return 0
        lax.fori_loop(0, n_conv, body, 0)

    @pl.when(i == 0)
    def _():
        for cp in copies(te_ref[0]):
            cp.start()
        for cp in copies(te_ref[0]):
            cp.wait()
        convert()

    @pl.when(used & (first_ref[i] == 1) & (nxt >= 0))
    def _():
        for cp in copies(nxt):
            cp.start()

    @pl.when(used)
    def _():
        a, b = _unpack_bf16_pair(xs_ref[...])
        a, b = a.astype(BF16), b.astype(BF16)

        def up(w_ref):
            return (jnp.dot(a, w_ref[0:half, :], preferred_element_type=F32)
                    + jnp.dot(b, w_ref[half:2 * half, :], preferred_element_type=F32))
        h1 = up(w1b)
        act = (h1 * _sigmoid(h1) * up(w3b)).astype(BF16)
        o = jnp.dot(act, w2b[...], preferred_element_type=F32)
        o_ref[...] = _pack_bf16_pair(o[:, 0:half], o[:, half:2 * half])

    @pl.when(jnp.logical_not(used))
    def _():
        o_ref[...] = jnp.zeros_like(o_ref)

    @pl.when(used & (last_ref[i] == 1) & (nxt >= 0))
    def _():
        for cp in copies(nxt):
            cp.wait()
        convert()


def _experts(tile_expert, first, last, nxt, n_used, xs, w1, w3, w2, *, tme):
    rows, half = xs.shape
    n_e, d_model, d_exp = w1.shape
    grid_spec = pltpu.PrefetchScalarGridSpec(
        num_scalar_prefetch=5,
        grid=(rows // tme,),
        in_specs=[pl.BlockSpec((tme, half), lambda i, te, fi, la, nx, nu: (jnp.minimum(i, nu[0] - 1), 0)),
                  pl.BlockSpec(memory_space=pl.ANY), pl.BlockSpec(memory_space=pl.ANY),
                  pl.BlockSpec(memory_space=pl.ANY)],
        out_specs=pl.BlockSpec((tme, half), lambda i, *_: (i, 0)),
        scratch_shapes=[pltpu.VMEM((d_model, d_exp), BF16), pltpu.VMEM((d_model, d_exp), BF16),
                        pltpu.VMEM((d_exp, d_model), BF16),
                        pltpu.VMEM((d_model, d_exp), F32), pltpu.VMEM((d_model, d_exp), F32),
                        pltpu.VMEM((d_exp, d_model), F32),
                        pltpu.SemaphoreType.DMA((3,))],
    )
    return pl.pallas_call(
        functools.partial(_experts_body, half=half, n_conv=8),
        grid_spec=grid_spec,
        out_shape=jax.ShapeDtypeStruct((rows, half), I32),
        compiler_params=_cparams(1),
        name="experts",
    )(tile_expert, first, last, nxt, n_used, xs, w1, w3, w2)


def _sc_gather(table, idx, *, chunk):
    n_rows, width = idx.shape[0], table.shape[1]
    n_workers = SC_CORES * SC_SUBCORES
    per_w = n_rows // n_workers
    n_chunks = per_w // chunk
    assert per_w * n_workers == n_rows and n_chunks * chunk == per_w and n_chunks % 2 == 0
    mesh = plsc.VectorSubcoreMesh(core_axis_name="c", subcore_axis_name="s",
                                  num_cores=SC_CORES, num_subcores=SC_SUBCORES)

    @functools.partial(
        pl.kernel, mesh=mesh,
        out_type=jax.ShapeDtypeStruct((n_rows, width), table.dtype),
        scratch_types=[pltpu.VMEM((per_w,), I32), pltpu.VMEM((2, chunk, width), table.dtype),
                       pltpu.SemaphoreType.DMA((2,))],
    )
    def gather_kernel(table_hbm, idx_hbm, out_hbm, idx_v, rows_v, sem):
        wid = lax.axis_index("s") * SC_CORES + lax.axis_index("c")
        base = pl.multiple_of(wid * per_w, per_w)
        pltpu.sync_copy(idx_hbm.at[pl.ds(base, per_w)], idx_v)

        def gather(j, slot):
            off = pl.multiple_of(j * chunk, chunk)
            return pltpu.make_async_copy(table_hbm.at[idx_v.at[pl.ds(off, chunk)]],
                                         rows_v.at[slot], sem.at[slot])

        gather(0, 0).start()

        @pl.loop(0, n_chunks, step=2)
        def _(j):
            for slot in range(2):
                jj = j + slot
                gather(jj, slot).wait()

                @pl.when(jj + 1 < n_chunks)
                def _():
                    gather(jj + 1, 1 - slot).start()
                off = pl.multiple_of(base + jj * chunk, chunk)
                pltpu.sync_copy(rows_v.at[slot], out_hbm.at[pl.ds(off, chunk)])

    return gather_kernel(table, idx)


def _sc_inverse(dest_flat, n_rows):
    n_assign = dest_flat.shape[0]
    n_tok = n_assign // 2
    n_workers = SC_CORES * SC_SUBCORES
    per_w = n_rows // n_workers
    assert per_w * n_workers == n_rows and per_w % SC_LANES == 0 and n_assign % SC_LANES == 0
    mesh = plsc.VectorSubcoreMesh(core_axis_name="c", subcore_axis_name="s",
                                  num_cores=SC_CORES, num_subcores=SC_SUBCORES)

    @functools.partial(
        pl.kernel, mesh=mesh,
        out_type=jax.ShapeDtypeStruct((n_rows,), I32),
        scratch_types=[pltpu.VMEM((n_assign,), I32), pltpu.VMEM((per_w,), I32)],
        compiler_params=pltpu.CompilerParams(needs_layout_passes=False),
    )
    def inverse_kernel(dest_hbm, src_hbm, dest_v, src_v):
        wid = lax.axis_index("s") * SC_CORES + lax.axis_index("c")
        lo = pl.multiple_of(wid * per_w, per_w)
        pltpu.sync_copy(dest_hbm, dest_v)

        @pl.loop(0, per_w, step=SC_LANES)
        def _(i):
            src_v[pl.ds(pl.multiple_of(i, SC_LANES), SC_LANES)] = jnp.zeros((SC_LANES,), I32)

        lane = lax.iota(I32, SC_LANES)

        @pl.loop(0, n_assign, step=SC_LANES)
        def _(i):
            d = dest_v[pl.ds(pl.multiple_of(i, SC_LANES), SC_LANES)]
            a = i + lane
            tok = jnp.where(a >= n_tok, a - n_tok, a)
            mine = (d >= lo) & (d < lo + per_w)
            plsc.store_scatter(src_v, [d - lo], tok, mask=mine)

        pltpu.sync_copy(src_v, src_hbm.at[pl.ds(lo, per_w)])

    return inverse_kernel(dest_flat)


def _final_body(g1_ref, g2_ref, h1_ref, route_ref, g_ref, y_ref, *, half):
    route = route_ref[...]
    w1, w2 = route[:, 2:3], route[:, 3:4]
    a1, b1 = _unpack_bf16_pair(g1_ref[...])
    a2, b2 = _unpack_bf16_pair(g2_ref[...])
    h1 = h1_ref[...]
    ha = h1[:, 0:half] + (a1 * w1 + a2 * w2)
    hb = h1[:, half:2 * half] + (b1 * w1 + b2 * w2)
    ms = (jnp.sum(ha * ha, axis=-1, keepdims=True)
          + jnp.sum(hb * hb, axis=-1, keepdims=True)) / (2 * half)
    inv = lax.rsqrt(ms + EPS)
    y_ref[:, 0:half] = ha * inv * g_ref[:, 0:half]
    y_ref[:, half:2 * half] = hb * inv * g_ref[:, half:2 * half]


def _final(gathered, h1, route, g, *, tf, row0, n_rows):
    t, d_model = h1.shape
    half = d_model // 2
    tile0, tiles_t = row0 // tf, t // tf
    return pl.pallas_call(
        functools.partial(_final_body, half=half),
        grid=(n_rows // tf,),
        in_specs=[pl.BlockSpec((tf, half), lambda i: (tile0 + i, 0)),
                  pl.BlockSpec((tf, half), lambda i: (tiles_t + tile0 + i, 0)),
                  pl.BlockSpec((tf, d_model), lambda i: (tile0 + i, 0)),
                  pl.BlockSpec((tf, LANES), lambda i: (tile0 + i, 0)),
                  pl.BlockSpec((1, d_model), lambda i: (0, 0))],
        out_specs=pl.BlockSpec((tf, d_model), lambda i: (i, 0)),
        out_shape=jax.ShapeDtypeStruct((n_rows, d_model), F32),
        compiler_params=_cparams(1),
        name="final",
    )(gathered, gathered, h1, route, g)


def _combine_body(dest_hbm, o_hbm, h1_ref, route_ref, g_ref, y_ref, dsm, gbuf, sem, dsem, *,
                  tf, tile0, half):
    i = pl.program_id(0)
    cp = pltpu.make_async_copy(dest_hbm.at[pl.ds(tile0 + i, 1), :], dsm, dsem)
    cp.start()
    cp.wait()

    def issue(r, _):
        _row_copy(o_hbm, dsm[0, 2 * r], gbuf.at[0], r, sem).start(priority=0)
        _row_copy(o_hbm, dsm[0, 2 * r + 1], gbuf.at[1], r, sem).start(priority=1)
        return 0
    lax.fori_loop(0, tf, issue, 0, unroll=8)

    def drain(r, _):
        _row_copy(o_hbm, 0, gbuf.at[0], 0, sem).wait()
        _row_copy(o_hbm, 0, gbuf.at[0], 0, sem).wait()
        return 0
    lax.fori_loop(0, tf, drain, 0)

    route = route_ref[...]
    g1, g2 = route[:, 2:3], route[:, 3:4]
    a1, b1 = _unpack_bf16_pair(gbuf[0])
    a2, b2 = _unpack_bf16_pair(gbuf[1])
    h1 = h1_ref[...]
    ha = h1[:, 0:half] + (a1 * g1 + a2 * g2)
    hb = h1[:, half:2 * half] + (b1 * g1 + b2 * g2)
    ms = (jnp.sum(ha * ha, axis=-1, keepdims=True)
          + jnp.sum(hb * hb, axis=-1, keepdims=True)) / (2 * half)
    inv = lax.rsqrt(ms + EPS)
    y_ref[:, 0:half] = ha * inv * g_ref[:, 0:half]
    y_ref[:, half:2 * half] = hb * inv * g_ref[:, half:2 * half]


def _combine(dest2, o, h1, route, g, *, tf, row0, n_rows):
    d_model = h1.shape[1]
    half = d_model // 2
    tile0 = row0 // tf
    return pl.pallas_call(
        functools.partial(_combine_body, tf=tf, tile0=tile0, half=half),
        grid=(n_rows // tf,),
        in_specs=[pl.BlockSpec(memory_space=pl.ANY), pl.BlockSpec(memory_space=pl.ANY),
                  pl.BlockSpec((tf, d_model), lambda i: (tile0 + i, 0)),
                  pl.BlockSpec((tf, LANES), lambda i: (tile0 + i, 0)),
                  pl.BlockSpec((1, d_model), lambda i: (0, 0))],
        out_specs=pl.BlockSpec((tf, d_model), lambda i: (i, 0)),
        out_shape=jax.ShapeDtypeStruct((n_rows, d_model), F32),
        scratch_shapes=[pltpu.SMEM((1, 2 * tf), I32), pltpu.VMEM((2, tf, half), U32),
                        pltpu.SemaphoreType.DMA, pltpu.SemaphoreType.DMA],
        compiler_params=_cparams(1),
        name="combine",
    )(dest2, o, h1, route, g)


def kernel(x_prompt, x_sample, meta, norm_mix_g, w_in, ssm_lam_re, ssm_lam_im, ssm_log_step, ssm_b_re, ssm_b_im, ssm_c_re, ssm_c_im, ssm_d, ssm_w_glu, conv_w_dw, conv_b_dw, conv_ln_g, conv_ln_b, conv_w_pw, w_out, norm_ffn_g, router_group_w, router_group_b, router_expert_w, router_expert_b, expert_w1, expert_w3, expert_w2, final_g):
    assert w_in.shape[0] == 1, "single-layer trunk"
    bp, lp, d_model = x_prompt.shape
    bs, ls, _ = x_sample.shape
    n_meta = meta.shape[0]
    d_ssm = ssm_d.shape[-1]
    d_conv = conv_b_dw.shape[-1]
    n_ch = ssm_b_re.shape[-1]
    n_g = ssm_b_re.shape[2]
    n_grp = router_group_w.shape[-1]
    n_exp = router_expert_w.shape[-1]
    assert n_meta == CHUNK and lp % CHUNK == 0 and ls % CHUNK == 0
    n_p, n_s = bp * lp, bs * ls
    t = n_p + n_s
    tm = min(256, lp, ls)
    assert lp % tm == 0 and ls % tm == 0

    xp2 = x_prompt.reshape(n_p, d_model)
    xs2 = x_sample.reshape(n_s, d_model)
    row = lambda a: a.reshape(1, -1)
    w_in_bf = w_in[0].astype(BF16)

    xg, c, sga, sgb = _inproj(xp2, xs2, row(norm_mix_g[0]), w_in_bf, d_ssm=d_ssm, d_conv=d_conv, tm=tm)
    xg_meta, c_meta = _meta_inproj(meta, row(norm_mix_g[0]), w_in_bf, d_ssm=d_ssm, d_conv=d_conv)

    assert n_ch == CHUNK and LANES % n_ch == 0 and n_g % (LANES // n_ch) == 0
    intra, bst, cst, coef = _sg_prep(ssm_lam_re[0], ssm_lam_im[0], ssm_log_step[0], ssm_b_re[0],
                                     ssm_b_im[0], ssm_c_re[0], ssm_c_im[0], ssm_d[0])
    nc = t // CHUNK
    n_blk = d_ssm // LANES
    rt = min(512, nc)
    yi = _sg_mm_intra(xg, intra, rt=rt)
    sf, sb, init = _sg_mm_state(xg, xg_meta, bst, rt=rt)
    n_sub = sf.shape[0] // (n_blk * nc)
    cb = min(64, lp // CHUNK, ls // CHUNK)
    xf, xb = _sg_scan(sf.reshape(n_blk, nc, n_sub, LANES), sb.reshape(n_blk, nc, n_sub, LANES),
                      coef, init, cb=cb, chunks_pseq=lp // CHUNK, chunks_sseq=ls // CHUNK,
                      n_pchunks=n_p // CHUNK)
    z = _sg_mm_out(yi, xf.reshape(sf.shape), xb.reshape(sb.shape), cst, rt=rt)

    assert n_grp + n_exp <= LANES // 2
    rw32 = jnp.pad(jnp.concatenate([router_group_w[0], router_expert_w[0]], axis=1),
                   ((0, 0), (0, LANES // 2 - n_grp - n_exp)))
    rw_hi = rw32.astype(BF16)
    rw = jnp.concatenate([rw_hi, (rw32 - rw_hi.astype(F32)).astype(BF16)], axis=1)
    rb = jnp.zeros((1, LANES), F32).at[0, 0:n_grp].set(router_group_b[0]).at[
        0, n_grp:n_grp + n_exp].set(router_expert_b[0])
    h1, v, route, cnt = _mix(
        xp2, xs2, z, c, c_meta, sga, sgb, ssm_w_glu[0].astype(BF16), conv_w_pw[0].astype(BF16),
        w_out[0].astype(BF16), conv_w_dw[0], row(conv_b_dw[0]), row(conv_ln_g[0]),
        row(conv_ln_b[0]), row(norm_ffn_g[0]), rw, rb,
        tm=tm, p_seq=lp, s_seq=ls, n_grp=n_grp, n_exp=n_exp)

    tme = 256
    counts = cnt[0, 0:n_exp].astype(I32)
    tiles_e = (counts + tme - 1) // tme
    tile_end = jnp.cumsum(tiles_e)
    n_used = tile_end[-1]
    offs = jnp.zeros((1, LANES), F32).at[0, 0:n_exp].set(((tile_end - tiles_e) * tme).astype(F32))
    n_tiles = (2 * t) // tme + n_exp
    ids = jnp.arange(n_tiles, dtype=I32)
    te_map = jnp.sum((jnp.minimum(ids, n_used - 1)[:, None] >= tile_end[None, :]).astype(I32), axis=1)
    te_map = jnp.minimum(te_map, n_exp - 1)
    run_start = (tile_end - tiles_e)[te_map]
    run_end = tile_end[te_map]
    valid = ids < n_used
    first = (valid & (ids == run_start)).astype(I32)
    last = (valid & (ids == run_end - 1)).astype(I32)
    nxt = jnp.where(valid & (run_end < n_used), te_map[jnp.minimum(run_end, n_tiles - 1)], -1)

    dest = _positions(route, offs, te=min(512, t))
    idx = jnp.concatenate([dest[:, 0], dest[:, 1]])
    src = _sc_inverse(idx, n_tiles * tme)
    xs = _sc_gather(v, src, chunk=32)
    o = _experts(te_map, first, last, nxt, n_used.reshape(1), xs, expert_w1[0], expert_w3[0],
                 expert_w2[0], tme=tme)

    tf = min(256, n_p, n_s)
    gathered = _sc_gather(o, idx, chunk=32)
    fg = row(final_g)
    y_p = _final(gathered, h1, route, fg, tf=tf, row0=0, n_rows=n_p)
    y_s = _final(gathered, h1, route, fg, tf=tf, row0=n_p, n_rows=n_s)
    return (y_p.reshape(bp, lp, d_model), y_s.reshape(bs, ls, d_model))
```

```python
import functools

import jax
import jax.numpy as jnp
from jax import lax
from jax.experimental import pallas as pl
from jax.experimental.pallas import tpu as pltpu
from jax.experimental.pallas import tpu_sc as plsc

F32 = jnp.float32
BF16 = jnp.bfloat16
U32 = jnp.uint32
I32 = jnp.int32

EPS = 1e-6
LAM_RE_MAX = -1e-4
CHUNK = 16
LANES = 128
VMEM_LIMIT = 56 << 20
SC_CORES = 2
SC_SUBCORES = 16
SC_LANES = 16
HIGHEST = lax.Precision.HIGHEST


def _cparams(n_axes):
    return pltpu.CompilerParams(dimension_semantics=("arbitrary",) * n_axes,
                                vmem_limit_bytes=VMEM_LIMIT)


def _resident(shape):
    nd = len(shape)
    return pl.BlockSpec(shape, lambda *_: (0,) * nd, pipeline_mode=pl.Buffered(1))


def _sigmoid(x):
    return 1.0 / (1.0 + jnp.exp(-x))


def _rms(x, g):
    return x * lax.rsqrt(jnp.mean(x * x, axis=-1, keepdims=True) + EPS) * g


def _pack_bf16_pair(a, b):
    def rnd(x):
        u = pltpu.bitcast(x, U32)
        return (u + jnp.uint32(0x7FFF) + ((u >> 16) & jnp.uint32(1))) >> 16
    return pltpu.bitcast((rnd(a) << 16) | rnd(b), I32)


def _unpack_bf16_pair(p):
    p = pltpu.bitcast(p, U32)
    a = pltpu.bitcast(p & jnp.uint32(0xFFFF0000), F32)
    b = pltpu.bitcast(p << 16, F32)
    return a, b


def _inproj_body(xp_ref, xs_ref, g_ref, w_ref, xg_ref, c_ref, sga_ref, sgb_ref, usc, *,
                 tm, n_p_tiles, d_ssm, d_conv, d_model, col):
    i = pl.program_id(0)
    x = jnp.where(i < n_p_tiles, xp_ref[...], xs_ref[...])
    y = _rms(x, g_ref[...]).astype(BF16)

    def proj(lo, n):
        return jnp.dot(y, w_ref[:, lo:lo + n], preferred_element_type=F32)

    for k in range(d_ssm // col):
        u = proj(k * col, col)
        for b in range(col // LANES):
            usc[k * (col // LANES) + b] = u[:, b * LANES:(b + 1) * LANES]
    for b in range(d_ssm // LANES):
        for s in range(CHUNK):
            c0 = (b * CHUNK + s) * LANES
            xg_ref[:, c0:c0 + LANES] = usc[b, pl.ds(s, tm // CHUNK, stride=CHUNK), :].astype(BF16)
    for k in range(d_conv // col):
        cv = proj(d_ssm + k * col, col)
        cg = proj(d_ssm + d_conv + k * col, col)
        c_ref[:, k * col:(k + 1) * col] = (cv * _sigmoid(cg)).astype(BF16)
    base = d_ssm + 2 * d_conv
    for k in range(d_model // col):
        sga_ref[:, k * col:(k + 1) * col] = _sigmoid(proj(base + k * col, col)).astype(BF16)
        sgb_ref[:, k * col:(k + 1) * col] = _sigmoid(
            proj(base + d_model + k * col, col)).astype(BF16)


def _inproj(xp2, xs2, g, w_bf, *, d_ssm, d_conv, tm):
    n_p, d_model = xp2.shape
    n_s = xs2.shape[0]
    n_p_tiles, n_s_tiles = n_p // tm, n_s // tm
    t = n_p + n_s
    col = min(1024, d_ssm, d_conv, d_model)
    body = functools.partial(_inproj_body, tm=tm, n_p_tiles=n_p_tiles, d_ssm=d_ssm, d_conv=d_conv,
                             d_model=d_model, col=col)
    row = lambda i: (i, 0)
    return pl.pallas_call(
        body,
        grid=(n_p_tiles + n_s_tiles,),
        in_specs=[
            pl.BlockSpec((tm, d_model), lambda i: (jnp.minimum(i, n_p_tiles - 1), 0)),
            pl.BlockSpec((tm, d_model), lambda i: (jnp.maximum(i - n_p_tiles, 0), 0)),
            _resident((1, d_model)),
            _resident(w_bf.shape),
        ],
        out_specs=[pl.BlockSpec((tm // CHUNK, CHUNK * d_ssm), row), pl.BlockSpec((tm, d_conv), row),
                   pl.BlockSpec((tm, d_model), row), pl.BlockSpec((tm, d_model), row)],
        out_shape=[jax.ShapeDtypeStruct((t // CHUNK, CHUNK * d_ssm), BF16),
                   jax.ShapeDtypeStruct((t, d_conv), BF16),
                   jax.ShapeDtypeStruct((t, d_model), BF16), jax.ShapeDtypeStruct((t, d_model), BF16)],
        scratch_shapes=[pltpu.VMEM((d_ssm // LANES, tm, LANES), F32)],
        compiler_params=_cparams(1),
        name="inproj",
    )(xp2, xs2, g, w_bf)


def _meta_body(m_ref, g_ref, w_ref, xg_ref, c_ref, *, d_ssm, d_conv):
    y = _rms(m_ref[...], g_ref[...]).astype(BF16)
    u = jnp.dot(y, w_ref[:, 0:d_ssm], preferred_element_type=F32)
    first = lax.broadcasted_iota(I32, (CHUNK, LANES), 0) == 0
    for b in range(d_ssm // LANES):
        for s in range(CHUNK):
            c0 = (b * CHUNK + s) * LANES
            piece = jnp.broadcast_to(u[s:s + 1, b * LANES:(b + 1) * LANES], (CHUNK, LANES))
            xg_ref[:, c0:c0 + LANES] = jnp.where(first, piece, 0.0).astype(BF16)
    cv = jnp.dot(y, w_ref[:, d_ssm:d_ssm + d_conv], preferred_element_type=F32)
    cg = jnp.dot(y, w_ref[:, d_ssm + d_conv:d_ssm + 2 * d_conv], preferred_element_type=F32)
    c_ref[...] = (cv * _sigmoid(cg)).astype(BF16)


def _meta_inproj(meta, g, w_bf, *, d_ssm, d_conv):
    n_meta, d_model = meta.shape
    ncol = d_ssm + 2 * d_conv
    return pl.pallas_call(
        functools.partial(_meta_body, d_ssm=d_ssm, d_conv=d_conv),
        grid=(1,),
        in_specs=[pl.BlockSpec((n_meta, d_model), lambda i: (0, 0)),
                  pl.BlockSpec((1, d_model), lambda i: (0, 0)),
                  pl.BlockSpec((d_model, ncol), lambda i: (0, 0))],
        out_specs=[pl.BlockSpec((CHUNK, CHUNK * d_ssm), lambda i: (0, 0)),
                   pl.BlockSpec((n_meta, d_conv), lambda i: (0, 0))],
        out_shape=[jax.ShapeDtypeStruct((CHUNK, CHUNK * d_ssm), BF16),
                   jax.ShapeDtypeStruct((n_meta, d_conv), BF16)],
        compiler_params=_cparams(1),
        name="meta_inproj",
    )(meta, g, w_bf)


def _cmul(ar, ai, br, bi):
    return ar * br - ai * bi, ar * bi + ai * br


def _discretize(lam_re, lam_im, log_step):
    lr = jnp.minimum(lam_re, LAM_RE_MAX)
    dt = jnp.exp(log_step)
    mag = jnp.exp(lr * dt)
    ar = mag * jnp.cos(lam_im * dt)
    ai = mag * jnp.sin(lam_im * dt)
    den = lr * lr + lam_im * lam_im
    nr = ar - 1.0
    fr = (nr * lr + ai * lam_im) / den
    fi = (ai * lr - nr * lam_im) / den
    return ar, ai, fr, fi


def _cpow(ar, ai, k, nbits, shape):
    pr = jnp.ones(shape, F32)
    pi = jnp.zeros(shape, F32)
    br = jnp.broadcast_to(ar, shape)
    bi = jnp.broadcast_to(ai, shape)
    kk = jnp.broadcast_to(k, shape)
    for b in range(nbits):
        sel = ((kk >> b) & 1) == 1
        nr, ni = _cmul(pr, pi, br, bi)
        pr = jnp.where(sel, nr, pr)
        pi = jnp.where(sel, ni, pi)
        br, bi = _cmul(br, bi, br, bi)
    return pr, pi


def _ssm_prep_body(lam_re_c, lam_im_c, ls_c, ct_re, ct_im, bt_re, bt_im,
                   lam_re_r, lam_im_r, ls_r, btr_re, btr_im, dpad, xmeta,
                   w1_ref, cst_ref, aq_ref, init_ref, *, n_state, n_ch):
    q = CHUNK
    width = q * n_ch
    par = pl.program_id(0) % 2
    kblk = lax.broadcasted_iota(I32, (1, width), 1) // n_ch
    strips = []
    for d in range(2):
        ar, ai, fr, fi = _discretize(lam_re_c[d], lam_im_c[d], ls_c[d])
        kexp = kblk if d == 0 else (q - 1) - kblk
        wr, wi = _cpow(ar, ai, kexp, 4, (n_state, width))
        gcr, gci = _cmul(ct_re[d], ct_im[d], wr, wi)
        gfr, gfi = _cmul(gcr, gci, fr, fi)
        strips.append(jnp.dot(bt_re[d], gfr, precision=HIGHEST, preferred_element_type=F32)
                      - jnp.dot(bt_im[d], gfi, precision=HIGHEST, preferred_element_type=F32))
        g1r, g1i = _cmul(gcr, gci, ar, ai)
        rowmask = (lax.broadcasted_iota(I32, (2 * n_state, width), 0) // n_state) == par
        rep_r = jnp.concatenate([g1r, g1r], axis=0)
        rep_i = jnp.concatenate([g1i, g1i], axis=0)
        r0 = 4 * n_state * d
        cst_ref[r0:r0 + 2 * n_state, :] = jnp.where(rowmask, rep_r, 0.0).astype(BF16)
        cst_ref[r0 + 2 * n_state:r0 + 4 * n_state, :] = jnp.where(rowmask, -rep_i, 0.0).astype(BF16)

    zf, zb = strips
    zero = jnp.zeros((n_ch, width), F32)
    z512 = jnp.concatenate([zb, zero], axis=1) + pltpu.roll(
        jnp.concatenate([zf, zero], axis=1), (q - 1) * n_ch, 1)
    row = lax.broadcasted_iota(I32, (n_ch, 2 * width), 0)
    lane = lax.broadcasted_iota(I32, (n_ch, 2 * width), 1)
    z512 = z512 + jnp.where(lane - (q - 1) * n_ch == row, dpad[...], 0.0)
    for s in range(q):
        sh = (q - 1 - s) * n_ch
        blk = z512 if sh == 0 else pltpu.roll(z512, 2 * width - sh, 1)
        w1_ref[n_ch * s:n_ch * (s + 1), 0:width] = blk[:, 0:width].astype(BF16)

    parmask = (lax.broadcasted_iota(I32, (1, 2 * n_state), 1) // n_state) == par
    for d in range(2):
        ar, ai, fr, fi = _discretize(lam_re_r[d], lam_im_r[d], ls_r[d])
        pw = [(jnp.ones_like(ar), jnp.zeros_like(ar))]
        for _ in range(q):
            pw.append(_cmul(pw[-1][0], pw[-1][1], ar, ai))
        col = width + 4 * n_state * d
        for s in range(q):
            e = (q - 1 - s) if d == 0 else s
            cr, ci = _cmul(fr, fi, pw[e][0], pw[e][1])
            br, bi = _cmul(btr_re[d], btr_im[d], cr, ci)
            w1_ref[n_ch * s:n_ch * (s + 1), col:col + 2 * n_state] = (
                jnp.where(parmask, br, 0.0).astype(BF16))
            w1_ref[n_ch * s:n_ch * (s + 1), col + 2 * n_state:col + 4 * n_state] = (
                jnp.where(parmask, bi, 0.0).astype(BF16))
        aq_ref[:, col - width:col - width + 2 * n_state] = pw[q][0]
        aq_ref[:, col - width + 2 * n_state:col - width + 4 * n_state] = pw[q][1]

    init_ref[...] = jnp.dot(xmeta[...].astype(BF16), w1_ref[:, width:width + 4 * n_state],
                            preferred_element_type=F32)


def _ssm_prep(lam_re, lam_im, log_step, b_re, b_im, c_re, c_im, d_skip, us_meta):
    _, n_g, n_state, n_ch = b_re.shape
    q = CHUNK
    width = q * n_ch
    dup = lambda x: jnp.concatenate([x, x], axis=-1)
    lam_re_c, lam_im_c = lam_re[..., None], lam_im[..., None]
    ls_c = log_step[..., None, None]
    ct_re = jnp.tile(jnp.swapaxes(c_re, -1, -2), (1, 1, 1, q))
    ct_im = jnp.tile(jnp.swapaxes(c_im, -1, -2), (1, 1, 1, q))
    bt_re, bt_im = jnp.swapaxes(b_re, -1, -2), jnp.swapaxes(b_im, -1, -2)
    lam_re_r, lam_im_r = dup(lam_re)[:, :, None, :], dup(lam_im)[:, :, None, :]
    ls_r = jnp.broadcast_to(log_step[..., None, None], (2, n_g, 1, 2 * n_state))
    btr_re, btr_im = dup(bt_re), dup(bt_im)
    dpad = jnp.pad(d_skip.reshape(n_g, 1, n_ch), ((0, 0), (0, 0), ((q - 1) * n_ch, width)))
    xm = us_meta.astype(F32).reshape(q, n_g, n_ch).transpose(1, 0, 2).reshape(n_g, 1, width)
    xm = jnp.pad(xm, ((0, 0), (0, 7), (0, 0)))

    def dspec(shape):
        return pl.BlockSpec((2, None) + shape, lambda g: (0, g, 0, 0))

    def gspec(shape):
        return pl.BlockSpec((None,) + shape, lambda g: (g, 0, 0))

    body = functools.partial(_ssm_prep_body, n_state=n_state, n_ch=n_ch)
    return pl.pallas_call(
        body,
        grid=(n_g,),
        in_specs=[dspec((n_state, 1)), dspec((n_state, 1)), dspec((1, 1)),
                  dspec((n_state, width)), dspec((n_state, width)),
                  dspec((n_ch, n_state)), dspec((n_ch, n_state)),
                  dspec((1, 2 * n_state)), dspec((1, 2 * n_state)), dspec((1, 2 * n_state)),
                  dspec((n_ch, 2 * n_state)), dspec((n_ch, 2 * n_state)),
                  gspec((1, 2 * width)), gspec((8, width))],
        out_specs=[gspec((width, width + 8 * n_state)), gspec((8 * n_state, width)),
                   gspec((1, 8 * n_state)), gspec((8, 4 * n_state))],
        out_shape=[jax.ShapeDtypeStruct((n_g, width, width + 8 * n_state), BF16),
                   jax.ShapeDtypeStruct((n_g, 8 * n_state, width), BF16),
                   jax.ShapeDtypeStruct((n_g, 1, 8 * n_state), F32),
                   jax.ShapeDtypeStruct((n_g, 8, 4 * n_state), F32)],
        compiler_params=_cparams(1),
        name="ssm_prep",
    )(lam_re_c, lam_im_c, ls_c, ct_re, ct_im, bt_re, bt_im,
      lam_re_r, lam_im_r, ls_r, btr_re, btr_im, dpad, xm)


def _ssm_mm1_body(x_ref, w1_ref, yi_ref, s_ref, *, gb, width, sw):
    for j in range(0, gb, 2):
        r0 = jnp.dot(x_ref[j], w1_ref[j], preferred_element_type=F32)
        r1 = jnp.dot(x_ref[j + 1], w1_ref[j + 1], preferred_element_type=F32)
        yi_ref[j] = r0[:, 0:width]
        yi_ref[j + 1] = r1[:, 0:width]
        s_ref[:, (j // 2) * sw:(j // 2 + 1) * sw] = r0[:, width:] + r1[:, width:]


def _ssm_mm1(xt, w1, *, gb):
    n_g, nc, width = xt.shape
    sw = w1.shape[-1] - width
    body = functools.partial(_ssm_mm1_body, gb=gb, width=width, sw=sw)
    return pl.pallas_call(
        body,
        grid=(n_g // gb,),
        in_specs=[pl.BlockSpec((gb, nc, width), lambda i: (i, 0, 0)),
                  pl.BlockSpec((gb, width, width + sw), lambda i: (i, 0, 0))],
        out_specs=[pl.BlockSpec((gb, nc, width), lambda i: (i, 0, 0)),
                   pl.BlockSpec((nc, gb // 2 * sw), lambda i: (0, i))],
        out_shape=[jax.ShapeDtypeStruct((n_g, nc, width), F32),
                   jax.ShapeDtypeStruct((nc, n_g // 2 * sw), F32)],
        compiler_params=_cparams(1),
        name="ssm_mm1",
    )(xt, w1)


def _ssm_scan_body(sf_ref, sb_ref, aq_ref, init_ref, xf_ref, xb_ref, stf, stb, *,
                   cb, hw, n_pblk, blk_pseq, blk_sseq):
    j = pl.program_id(1)
    pos = jnp.where(j < n_pblk, lax.rem(j, blk_pseq), lax.rem(jnp.maximum(j - n_pblk, 0), blk_sseq))

    @pl.when(pos == 0)
    def _():
        stf[...] = init_ref[...]
        stb[...] = jnp.zeros_like(stb)

    afr, afi = aq_ref[:, 0:hw], aq_ref[:, hw:2 * hw]
    abr, abi = aq_ref[:, 2 * hw:3 * hw], aq_ref[:, 3 * hw:4 * hw]

    def body(i, carry):
        fr, fi, br, bi = carry
        ib = cb - 1 - i
        xf_ref[i, :, 0:hw] = fr
        xf_ref[i, :, hw:2 * hw] = fi
        xb_ref[ib, :, 0:hw] = br
        xb_ref[ib, :, hw:2 * hw] = bi
        nfr = afr * fr - afi * fi + sf_ref[i, :, 0:hw]
        nfi = afr * fi + afi * fr + sf_ref[i, :, hw:2 * hw]
        nbr = abr * br - abi * bi + sb_ref[ib, :, 0:hw]
        nbi = abr * bi + abi * br + sb_ref[ib, :, hw:2 * hw]
        return nfr, nfi, nbr, nbi

    fr, fi, br, bi = lax.fori_loop(
        0, cb, body, (stf[:, 0:hw], stf[:, hw:2 * hw], stb[:, 0:hw], stb[:, hw:2 * hw]))
    stf[:, 0:hw] = fr
    stf[:, hw:2 * hw] = fi
    stb[:, 0:hw] = br
    stb[:, hw:2 * hw] = bi


def _ssm_scan(s3, aq, init, *, cb, chunks_pseq, chunks_sseq, n_pchunks):
    nc, n_pair, sw = s3.shape
    hw = sw // 4
    pb = min(8, n_pair)
    n_pblk, blk_pseq, blk_sseq = n_pchunks // cb, chunks_pseq // cb, chunks_sseq // cb

    def bwd_block(j):
        in_p = j < n_pblk
        pos = jnp.where(in_p, lax.rem(j, blk_pseq), lax.rem(jnp.maximum(j - n_pblk, 0), blk_sseq))
        ln = jnp.where(in_p, blk_pseq, blk_sseq)
        return j - pos + ln - 1 - pos

    body = functools.partial(_ssm_scan_body, cb=cb, hw=hw, n_pblk=n_pblk, blk_pseq=blk_pseq,
                             blk_sseq=blk_sseq)
    return pl.pallas_call(
        body,
        grid=(n_pair // pb, nc // cb),
        in_specs=[pl.BlockSpec((cb, pb, 2 * hw), lambda p, j: (j, p, 0)),
                  pl.BlockSpec((cb, pb, 2 * hw), lambda p, j: (bwd_block(j), p, 1)),
                  pl.BlockSpec((pb, sw), lambda p, j: (p, 0)),
                  pl.BlockSpec((pb, 2 * hw), lambda p, j: (p, 0))],
        out_specs=[pl.BlockSpec((cb, pb, 2 * hw), lambda p, j: (j, p, 0)),
                   pl.BlockSpec((cb, pb, 2 * hw), lambda p, j: (bwd_block(j), p, 0))],
        out_shape=[jax.ShapeDtypeStruct((nc, n_pair, 2 * hw), F32),
                   jax.ShapeDtypeStruct((nc, n_pair, 2 * hw), F32)],
        scratch_shapes=[pltpu.VMEM((pb, 2 * hw), F32), pltpu.VMEM((pb, 2 * hw), F32)],
        compiler_params=_cparams(2),
        name="ssm_scan",
    )(s3, s3, aq, init)


def _gelu_tanh(x):
    return 0.5 * x * (1.0 + jnp.tanh(0.7978845608028654 * (x + 0.044715 * (x * x * x))))


def _ssm_mm2_body(yi_ref, xf_ref, xb_ref, cst_ref, z_ref, *, gb, hs):
    for j in range(gb):
        cols = slice((j // 2) * hs, (j // 2 + 1) * hs)
        y = (yi_ref[j]
             + jnp.dot(xf_ref[:, cols].astype(BF16), cst_ref[j, 0:hs, :], preferred_element_type=F32)
             + jnp.dot(xb_ref[:, cols].astype(BF16), cst_ref[j, hs:2 * hs, :],
                       preferred_element_type=F32))
        z_ref[j] = _gelu_tanh(y).astype(BF16)


def _ssm_mm2(yi, xf2, xb2, cst, *, gb):
    n_g, nc, width = yi.shape
    hs = cst.shape[1] // 2
    body = functools.partial(_ssm_mm2_body, gb=gb, hs=hs)
    return pl.pallas_call(
        body,
        grid=(n_g // gb,),
        in_specs=[pl.BlockSpec((gb, nc, width), lambda i: (i, 0, 0)),
                  pl.BlockSpec((nc, gb // 2 * hs), lambda i: (0, i)),
                  pl.BlockSpec((nc, gb // 2 * hs), lambda i: (0, i)),
                  pl.BlockSpec((gb, 2 * hs, width), lambda i: (i, 0, 0))],
        out_specs=pl.BlockSpec((gb, nc, width), lambda i: (i, 0, 0)),
        out_shape=jax.ShapeDtypeStruct((n_g, nc, width), BF16),
        compiler_params=_cparams(1),
        name="ssm_mm2",
    )(yi, xf2, xb2, cst)


def _spread(x, n_ch, shift):
    rows, w = x.shape
    lane = lax.broadcasted_iota(I32, (rows, LANES), 1)
    keep = (lane >= shift) & (lane < shift + n_ch)
    out = []
    for b in range(w // n_ch):
        src = (b * n_ch // LANES) * LANES
        amount = lax.rem(shift + LANES - (b * n_ch) % LANES, LANES)
        out.append(jnp.where(keep, pltpu.roll(x[:, src:src + LANES], amount, 1), 0.0))
    return jnp.concatenate(out, axis=1)


def _tile_lanes(x, n_ch, width):
    span = n_ch
    while span < LANES:
        x = x + pltpu.roll(x, span, 1)
        span *= 2
    return jnp.concatenate([x] * (width // LANES), axis=1)


def _sg_prep_body(lam_re_c, lam_im_c, ls_c, ct_re, ct_im, bt_re, bt_im,
                  lam_re_r, lam_im_r, ls_r, btr_re, btr_im, dpad,
                  m_ref, b_ref, c_ref, aq_ref, *, n_state, n_ch):
    q = CHUNK
    width = q * n_ch
    per_blk = LANES // n_ch
    gl = pl.program_id(0) % per_blk
    shift = gl * n_ch
    kblk = lax.broadcasted_iota(I32, (1, width), 1) // n_ch
    strips = []
    for d in range(2):
        ar, ai, fr, fi = _discretize(lam_re_c[d], lam_im_c[d], ls_c[d])
        kexp = kblk if d == 0 else (q - 1) - kblk
        wr, wi = _cpow(ar, ai, kexp, 4, (n_state, width))
        gcr, gci = _cmul(_tile_lanes(ct_re[d], n_ch, width), _tile_lanes(ct_im[d], n_ch, width),
                         wr, wi)
        gfr, gfi = _cmul(gcr, gci, fr, fi)
        strips.append(jnp.dot(bt_re[d], gfr, precision=HIGHEST, preferred_element_type=F32)
                      - jnp.dot(bt_im[d], gfi, precision=HIGHEST, preferred_element_type=F32))
        g1r, g1i = _cmul(gcr, gci, ar, ai)
        for comp, val in ((0, g1r), (1, -g1i)):
            c_ref[2 * d + comp] = _spread(val, n_ch, shift).astype(BF16)

    zf, zb = strips
    zero = jnp.zeros((n_ch, width), F32)
    z512 = jnp.concatenate([zb, zero], axis=1) + pltpu.roll(
        jnp.concatenate([zf, zero], axis=1), (q - 1) * n_ch, 1)
    row = lax.broadcasted_iota(I32, (n_ch, 2 * width), 0)
    lane = lax.broadcasted_iota(I32, (n_ch, 2 * width), 1)
    z512 = z512 + jnp.where(lane - (q - 1) * n_ch == row, dpad[...], 0.0)
    zwide = _spread(z512, n_ch, shift)
    for s in range(q):
        lo = (q - 1 - s) * LANES
        m_ref[s] = zwide[:, lo:lo + q * LANES].astype(BF16)

    parmask = (lax.broadcasted_iota(I32, (1, 2 * n_state), 1) // n_state) == gl % 2
    pieces = [[None] * 4 for _ in range(q)]
    for d in range(2):
        ar, ai, fr, fi = _discretize(lam_re_r[d], lam_im_r[d], ls_r[d])
        pw = [(jnp.ones_like(ar), jnp.zeros_like(ar))]
        for _ in range(q):
            pw.append(_cmul(pw[-1][0], pw[-1][1], ar, ai))
        for s in range(q):
            e = (q - 1 - s) if d == 0 else s
            cr, ci = _cmul(fr, fi, pw[e][0], pw[e][1])
            br, bi = _cmul(btr_re[d], btr_im[d], cr, ci)
            pieces[s][2 * d] = jnp.where(parmask, br, 0.0)
            pieces[s][2 * d + 1] = jnp.where(parmask, bi, 0.0)
        aq_ref[:, 2 * d * n_state:(2 * d + 1) * n_state] = pw[q][0][:, 0:n_state]
        aq_ref[:, (2 * d + 1) * n_state:(2 * d + 2) * n_state] = pw[q][1][:, 0:n_state]
    n_pair = per_blk // 2
    for s in range(q):
        cols = []
        for seg in range(4):
            for blk in range(n_pair):
                cols.append(jnp.where(gl // 2 == blk, pieces[s][seg], 0.0))
        b_ref[s] = jnp.concatenate(cols, axis=1).astype(BF16)


def _sg_prep(lam_re, lam_im, log_step, b_re, b_im, c_re, c_im, d_skip):
    _, n_g, n_state, n_ch = b_re.shape
    q = CHUNK
    width = q * n_ch
    per_blk = LANES // n_ch
    n_blk = n_g // per_blk
    sw = 4 * per_blk * n_state
    dup = lambda x: jnp.concatenate([x, x], axis=-1)
    lane_pad = lambda x: jnp.pad(x, ((0, 0),) * 3 + ((0, LANES - n_ch),))
    lam_re_c, lam_im_c = lam_re[..., None], lam_im[..., None]
    ls_c = log_step[..., None, None]
    ct_re = lane_pad(jnp.swapaxes(c_re, -1, -2))
    ct_im = lane_pad(jnp.swapaxes(c_im, -1, -2))
    bt_re, bt_im = jnp.swapaxes(b_re, -1, -2), jnp.swapaxes(b_im, -1, -2)
    lam_re_r, lam_im_r = dup(lam_re)[:, :, None, :], dup(lam_im)[:, :, None, :]
    ls_r = jnp.broadcast_to(log_step[..., None, None], (2, n_g, 1, 2 * n_state))
    btr_re, btr_im = dup(bt_re), dup(bt_im)
    dpad = jnp.pad(d_skip.reshape(n_g, 1, n_ch), ((0, 0), (0, 0), ((q - 1) * n_ch, width)))

    def dspec(shape):
        return pl.BlockSpec((2, None) + shape, lambda g: (0, g, 0, 0))

    body = functools.partial(_sg_prep_body, n_state=n_state, n_ch=n_ch)
    m4, b4, c5, aq = pl.pallas_call(
        body,
        grid=(n_g,),
        in_specs=[dspec((n_state, 1)), dspec((n_state, 1)), dspec((1, 1)),
                  dspec((n_state, LANES)), dspec((n_state, LANES)),
                  dspec((n_ch, n_state)), dspec((n_ch, n_state)),
                  dspec((1, 2 * n_state)), dspec((1, 2 * n_state)), dspec((1, 2 * n_state)),
                  dspec((n_ch, 2 * n_state)), dspec((n_ch, 2 * n_state)),
                  pl.BlockSpec((None, 1, 2 * width), lambda g: (g, 0, 0))],
        out_specs=[
            pl.BlockSpec((None, q, n_ch, q * LANES), lambda g: (g // per_blk, 0, g % per_blk, 0)),
            pl.BlockSpec((None, q, n_ch, sw), lambda g: (g // per_blk, 0, g % per_blk, 0)),
            pl.BlockSpec((None, 4, None, n_state, q * LANES),
                         lambda g: (g // per_blk, 0, g % per_blk, 0, 0)),
            pl.BlockSpec((None, 1, 4 * n_state), lambda g: (g, 0, 0))],
        out_shape=[jax.ShapeDtypeStruct((n_blk, q, LANES, q * LANES), BF16),
                   jax.ShapeDtypeStruct((n_blk, q, LANES, sw), BF16),
                   jax.ShapeDtypeStruct((n_blk, 4, per_blk, n_state, q * LANES), BF16),
                   jax.ShapeDtypeStruct((n_g, 1, 4 * n_state), F32)],
        compiler_params=_cparams(1),
        name="ssm_prep",
    )(lam_re_c, lam_im_c, ls_c, ct_re, ct_im, bt_re, bt_im,
      lam_re_r, lam_im_r, ls_r, btr_re, btr_im, dpad)
    intra = m4.reshape(n_blk, q * LANES, q * LANES)
    bst = b4.reshape(n_blk, q * LANES, sw)
    cst = c5.reshape(n_blk, sw, q * LANES)
    a = aq.reshape(n_blk, per_blk // 2, 2, 2, 2, n_state)
    a = a.transpose(3, 4, 0, 1, 2, 5).reshape(2, 2, n_blk, per_blk // 2, LANES)
    coef = []
    for d in range(2):
        re, im = a[d, 0], a[d, 1]
        coef.append((jnp.concatenate([re, re], axis=1), jnp.concatenate([-im, im], axis=1)))
    return intra, bst, cst, coef


def _sg_mm_intra_body(x_ref, m_ref, y_ref):
    y_ref[...] = jnp.dot(x_ref[...], m_ref[...], preferred_element_type=F32).astype(BF16)


def _sg_mm_intra(xg, intra, *, rt):
    nc = xg.shape[0]
    n_blk, kw, _ = intra.shape
    return pl.pallas_call(
        _sg_mm_intra_body,
        grid=(n_blk, nc // rt),
        in_specs=[pl.BlockSpec((rt, kw), lambda b, i: (i, b)),
                  pl.BlockSpec((None, kw, kw), lambda b, i: (b, 0, 0))],
        out_specs=pl.BlockSpec((rt, kw), lambda b, i: (i, b)),
        out_shape=jax.ShapeDtypeStruct((nc, n_blk * kw), BF16),
        compiler_params=_cparams(2),
        name="ssm_intra",
    )(xg, intra)


def _sg_mm_state_body(x_ref, xm_ref, b_ref, sf_ref, sb_ref, init_ref, *, rt, n_sub):
    r = jnp.dot(x_ref[...], b_ref[...], preferred_element_type=F32)
    for k in range(n_sub):
        sf_ref[pl.ds(k, rt, stride=n_sub), :] = r[:, k * LANES:(k + 1) * LANES]
        sb_ref[pl.ds(k, rt, stride=n_sub), :] = r[:, (n_sub + k) * LANES:(n_sub + k + 1) * LANES]

    @pl.when(pl.program_id(1) == 0)
    def _():
        r0 = jnp.dot(xm_ref[...], b_ref[...], preferred_element_type=F32)
        for k in range(n_sub):
            init_ref[k:k + 1, :] = r0[0:1, k * LANES:(k + 1) * LANES]


def _sg_mm_state(xg, xg_meta, bst, *, rt):
    nc = xg.shape[0]
    n_blk, kw, sw = bst.shape
    n_sub = sw // (2 * LANES)
    n_rt = nc // rt
    body = functools.partial(_sg_mm_state_body, rt=rt, n_sub=n_sub)
    dense = jax.ShapeDtypeStruct((n_blk * nc * n_sub, LANES), F32)
    dspec = pl.BlockSpec((rt * n_sub, LANES), lambda b, i: (b * n_rt + i, 0))
    return pl.pallas_call(
        body,
        grid=(n_blk, n_rt),
        in_specs=[pl.BlockSpec((rt, kw), lambda b, i: (i, b)),
                  pl.BlockSpec((CHUNK, kw), lambda b, i: (0, b)),
                  pl.BlockSpec((None, kw, sw), lambda b, i: (b, 0, 0))],
        out_specs=[dspec, dspec, pl.BlockSpec((None, n_sub, LANES), lambda b, i: (b, 0, 0))],
        out_shape=[dense, dense, jax.ShapeDtypeStruct((n_blk, n_sub, LANES), F32)],
        compiler_params=_cparams(2),
        name="ssm_state",
    )(xg, xg_meta, bst)


def _sg_scan_body(sf_ref, sb_ref, a1f_ref, a2f_ref, a1b_ref, a2b_ref, init_ref, xf_ref, xb_ref,
                  stf, stb, *, cb, n_blk, half, n_pblk, blk_pseq, blk_sseq):
    j = pl.program_id(0)
    pos = jnp.where(j < n_pblk, lax.rem(j, blk_pseq), lax.rem(jnp.maximum(j - n_pblk, 0), blk_sseq))

    @pl.when(pos == 0)
    def _():
        stf[...] = init_ref[...]
        stb[...] = jnp.zeros_like(stb)

    def body(i, carry):
        ef, eb = carry
        ib = cb - 1 - i
        nf, nb = [], []
        for g in range(n_blk):
            xf_ref[g, i] = ef[g]
            xb_ref[g, ib] = eb[g]
            nf.append(a1f_ref[g] * ef[g] + a2f_ref[g] * pltpu.roll(ef[g], half, 0) + sf_ref[g, i])
            nb.append(a1b_ref[g] * eb[g] + a2b_ref[g] * pltpu.roll(eb[g], half, 0) + sb_ref[g, ib])
        return tuple(nf), tuple(nb)

    ef, eb = lax.fori_loop(0, cb, body, (tuple(stf[g] for g in range(n_blk)),
                                         tuple(stb[g] for g in range(n_blk))))
    for g in range(n_blk):
        stf[g] = ef[g]
        stb[g] = eb[g]


def _sg_scan(sf, sb, coef, init, *, cb, chunks_pseq, chunks_sseq, n_pchunks):
    n_blk, nc, n_sub, _ = sf.shape
    n_pblk, blk_pseq, blk_sseq = n_pchunks // cb, chunks_pseq // cb, chunks_sseq // cb

    def bwd_block(j):
        in_p = j < n_pblk
        pos = jnp.where(in_p, lax.rem(j, blk_pseq), lax.rem(jnp.maximum(j - n_pblk, 0), blk_sseq))
        ln = jnp.where(in_p, blk_pseq, blk_sseq)
        return j - pos + ln - 1 - pos

    body = functools.partial(_sg_scan_body, cb=cb, n_blk=n_blk, half=n_sub // 2, n_pblk=n_pblk,
                             blk_pseq=blk_pseq, blk_sseq=blk_sseq)
    fwd = pl.BlockSpec((n_blk, cb, n_sub, LANES), lambda j: (0, j, 0, 0))
    bwd = pl.BlockSpec((n_blk, cb, n_sub, LANES), lambda j: (0, bwd_block(j), 0, 0))
    small = pl.BlockSpec((n_blk, n_sub, LANES), lambda j: (0, 0, 0))
    return pl.pallas_call(
        body,
        grid=(nc // cb,),
        in_specs=[fwd, bwd, small, small, small, small, small],
        out_specs=[fwd, bwd],
        out_shape=[jax.ShapeDtypeStruct(sf.shape, F32), jax.ShapeDtypeStruct(sb.shape, F32)],
        scratch_shapes=[pltpu.VMEM((n_blk, n_sub, LANES), F32), pltpu.VMEM((n_blk, n_sub, LANES), F32)],
        compiler_params=_cparams(1),
        name="ssm_scan",
    )(sf, sb, coef[0][0], coef[0][1], coef[1][0], coef[1][1], init)


def _gelu_tanh(x):
    return 0.5 * x * (1.0 + jnp.tanh(0.7978845608028654 * (x + 0.044715 * (x * x * x))))


def _sg_mm_out_body(y_ref, xf_ref, xb_ref, c_ref, z_ref, *, rt, n_sub):
    cols = [xf_ref[pl.ds(k, rt, stride=n_sub), :] for k in range(n_sub)]
    cols += [xb_ref[pl.ds(k, rt, stride=n_sub), :] for k in range(n_sub)]
    state = jnp.concatenate(cols, axis=1).astype(BF16)
    y = y_ref[...].astype(F32) + jnp.dot(state, c_ref[...], preferred_element_type=F32)
    z_ref[...] = _gelu_tanh(y).astype(BF16)


def _sg_mm_out(yi, xf, xb, cst, *, rt):
    nc = yi.shape[0]
    n_blk, sw, kw = cst.shape
    n_sub = sw // (2 * LANES)
    n_rt = nc // rt
    dspec = pl.BlockSpec((rt * n_sub, LANES), lambda b, i: (b * n_rt + i, 0))
    return pl.pallas_call(
        functools.partial(_sg_mm_out_body, rt=rt, n_sub=n_sub),
        grid=(n_blk, n_rt),
        in_specs=[pl.BlockSpec((rt, kw), lambda b, i: (i, b)), dspec, dspec,
                  pl.BlockSpec((None, sw, kw), lambda b, i: (b, 0, 0))],
        out_specs=pl.BlockSpec((rt, kw), lambda b, i: (i, b)),
        out_shape=jax.ShapeDtypeStruct((nc, n_blk * kw), BF16),
        compiler_params=_cparams(2),
        name="ssm_out",
    )(yi, xf, xb, cst)


def _mix_body(xp_ref, xs_ref, z_ref, c_ref, cprev_ref, cnext_ref, cmeta_ref, sga_ref, sgb_ref,
              wglu_ref, wpw_ref, wout_ref, wdw_ref, bdw_ref, lng_ref, lnb_ref, gffn_ref,
              rw_ref, rb_ref,
              h1_ref, v_ref, route_ref, cnt_ref,
              cw_ref, conv_ref, zsc_ref, *,
              tm, n_p_tiles, tiles_per_pseq, tiles_per_sseq, d_model, d_ssm, d_conv, conv_w,
              n_grp, n_exp, exp_per_grp, rc):
    i = pl.program_id(0)
    in_prompt = i < n_p_tiles
    x = jnp.where(in_prompt, xp_ref[...], xs_ref[...])
    pos_p = lax.rem(i, tiles_per_pseq)
    pos_s = lax.rem(jnp.maximum(i - n_p_tiles, 0), tiles_per_sseq)
    is_start = jnp.where(in_prompt, pos_p == 0, pos_s == 0)
    is_end = jnp.where(in_prompt, pos_p == tiles_per_pseq - 1, pos_s == tiles_per_sseq - 1)

    halo = CHUNK
    pad = conv_w // 2
    cw_ref[0:halo, :] = jnp.where(is_start, cmeta_ref[...], cprev_ref[...]).astype(F32)
    cw_ref[halo:halo + tm, :] = c_ref[...].astype(F32)
    cw_ref[halo + tm:2 * halo + tm, :] = jnp.where(is_end, 0.0, cnext_ref[...].astype(F32))
    sub = 8
    for lc in range(d_conv // LANES):
        ls = slice(lc * LANES, (lc + 1) * LANES)
        for r0 in range(0, tm, rc):
            out = None
            for r in range(sub):
                part = None
                for q in range((conv_w + halo - pad) // sub + 1):
                    k = sub * q + r - (halo - pad)
                    if 0 <= k < conv_w:
                        term = wdw_ref[k:k + 1, ls] * cw_ref[r0 + sub * q:r0 + sub * q + rc + sub, ls]
                        part = term if part is None else part + term
                if part is not None:
                    shifted = part[r:r + rc]
                    out = shifted if out is None else out + shifted
            conv_ref[r0:r0 + rc, ls] = out
    cc = conv_ref[...] + bdw_ref[...]
    mu = jnp.mean(cc, axis=-1, keepdims=True)
    var = jnp.mean(jnp.square(cc - mu), axis=-1, keepdims=True)
    cc = (cc - mu) * lax.rsqrt(var + EPS) * lng_ref[...] + lnb_ref[...]
    cc = (cc * _sigmoid(cc)).astype(BF16)
    y_b = jnp.dot(cc, wpw_ref[...], preferred_element_type=F32)

    for b in range(d_ssm // LANES):
        for s in range(CHUNK):
            c0 = (b * CHUNK + s) * LANES
            zsc_ref[b, pl.ds(s, tm // CHUNK, stride=CHUNK), :] = z_ref[:, c0:c0 + LANES].astype(F32)
    z = jnp.concatenate([zsc_ref[b] for b in range(d_ssm // LANES)], axis=1).astype(BF16)
    va = jnp.dot(z, wglu_ref[:, 0:d_model], preferred_element_type=F32)
    ga = jnp.dot(z, wglu_ref[:, d_model:2 * d_model], preferred_element_type=F32)
    y_a = va * _sigmoid(ga)
    merged = (sga_ref[...].astype(F32) * y_a + sgb_ref[...].astype(F32) * y_b).astype(BF16)
    h1 = x + jnp.dot(merged, wout_ref[...], preferred_element_type=F32)
    h1_ref[...] = h1
    v = _rms(h1, gffn_ref[...])
    half = d_model // 2
    v_ref[...] = _pack_bf16_pair(v[:, 0:half], v[:, half:d_model])

    v_hi = v.astype(BF16)
    v_lo = (v - v_hi.astype(F32)).astype(BF16)
    acc = (jnp.dot(v_hi, rw_ref[...], preferred_element_type=F32)
           + jnp.dot(v_lo, rw_ref[...], preferred_element_type=F32))
    logits = acc + pltpu.roll(acc, LANES // 2, 1) + rb_ref[...]
    lane = lax.broadcasted_iota(I32, (tm, LANES), 1).astype(F32)
    big = jnp.float32(1e9)
    neg = jnp.float32(-jnp.inf)
    gmask = lane < n_grp
    lg = jnp.where(gmask, logits, neg)
    gmax = jnp.max(lg, axis=-1, keepdims=True)
    grp = jnp.min(jnp.where(lg == gmax, lane, big), axis=-1, keepdims=True)
    p_grp = 1.0 / jnp.sum(jnp.where(gmask, jnp.exp(logits - gmax), 0.0), axis=-1, keepdims=True)
    lo = n_grp + grp * exp_per_grp
    emask = (lane >= lo) & (lane < lo + exp_per_grp)
    le = jnp.where(emask, logits, neg)
    m1 = jnp.max(le, axis=-1, keepdims=True)
    i1 = jnp.min(jnp.where(le == m1, lane, big), axis=-1, keepdims=True)
    le2 = jnp.where(lane == i1, neg, le)
    m2 = jnp.max(le2, axis=-1, keepdims=True)
    i2 = jnp.min(jnp.where(le2 == m2, lane, big), axis=-1, keepdims=True)
    t = jnp.exp(m2 - m1)
    w1 = 1.0 / (1.0 + t)
    e1 = i1 - n_grp
    e2 = i2 - n_grp
    route_ref[...] = jnp.where(lane == 0, e1, jnp.where(lane == 1, e2, jnp.where(
        lane == 2, p_grp * w1, jnp.where(lane == 3, p_grp * (t * w1), 0.0))))

    @pl.when(i == 0)
    def _():
        cnt_ref[...] = jnp.zeros_like(cnt_ref)

    hot = jnp.where((lane == e1) | (lane == e2), 1.0, 0.0)
    cnt_ref[...] += jnp.sum(hot, axis=0, keepdims=True)


def _mix(xp2, xs2, z, c, c_meta, sga, sgb, wglu, wpw, wout, wdw, bdw, lng, lnb, gffn, rw, rb, *,
         tm, p_seq, s_seq, n_grp, n_exp):
    n_p, d_model = xp2.shape
    n_s = xs2.shape[0]
    t = n_p + n_s
    d_ssm, d_conv = z.shape[1] // CHUNK, c.shape[1]
    conv_w = wdw.shape[0]
    n_p_tiles, n_s_tiles = n_p // tm, n_s // tm
    hpt = tm // CHUNK
    n_hblk = t // CHUNK
    body = functools.partial(
        _mix_body, tm=tm, n_p_tiles=n_p_tiles, tiles_per_pseq=p_seq // tm,
        tiles_per_sseq=s_seq // tm, d_model=d_model, d_ssm=d_ssm, d_conv=d_conv, conv_w=conv_w,
        n_grp=n_grp, n_exp=n_exp, exp_per_grp=n_exp // n_grp, rc=min(128, tm))
    row = lambda i: (i, 0)
    return pl.pallas_call(
        body,
        grid=(n_p_tiles + n_s_tiles,),
        in_specs=[
            pl.BlockSpec((tm, d_model), lambda i: (jnp.minimum(i, n_p_tiles - 1), 0)),
            pl.BlockSpec((tm, d_model), lambda i: (jnp.maximum(i - n_p_tiles, 0), 0)),
            pl.BlockSpec((tm // CHUNK, CHUNK * d_ssm), row),
            pl.BlockSpec((tm, d_conv), row),
            pl.BlockSpec((CHUNK, d_conv), lambda i: (jnp.maximum(i * hpt - 1, 0), 0)),
            pl.BlockSpec((CHUNK, d_conv), lambda i: (jnp.minimum((i + 1) * hpt, n_hblk - 1), 0)),
            _resident(c_meta.shape),
            pl.BlockSpec((tm, d_model), row),
            pl.BlockSpec((tm, d_model), row),
            _resident(wglu.shape), _resident(wpw.shape), _resident(wout.shape),
            _resident(wdw.shape), _resident(bdw.shape), _resident(lng.shape), _resident(lnb.shape),
            _resident(gffn.shape), _resident(rw.shape), _resident(rb.shape),
        ],
        out_specs=[pl.BlockSpec((tm, d_model), row), pl.BlockSpec((tm, d_model // 2), row),
                   pl.BlockSpec((tm, LANES), row), pl.BlockSpec((1, LANES), lambda i: (0, 0))],
        out_shape=[jax.ShapeDtypeStruct((t, d_model), F32),
                   jax.ShapeDtypeStruct((t, d_model // 2), I32),
                   jax.ShapeDtypeStruct((t, LANES), F32),
                   jax.ShapeDtypeStruct((1, LANES), F32)],
        scratch_shapes=[pltpu.VMEM((tm + 2 * CHUNK, d_conv), F32), pltpu.VMEM((tm, d_conv), F32),
                        pltpu.VMEM((d_ssm // LANES, tm, LANES), F32)],
        compiler_params=_cparams(1),
        name="mix",
    )(xp2, xs2, z, c, c, c, c_meta, sga, sgb, wglu, wpw, wout, wdw, bdw, lng, lnb, gffn, rw, rb)


def _positions_body(route_ref, offs_ref, dest_ref, carry_ref, *, te):
    @pl.when(pl.program_id(0) == 0)
    def _():
        carry_ref[...] = jnp.zeros_like(carry_ref)

    lane = lax.broadcasted_iota(I32, (te, LANES), 1).astype(F32)
    r = route_ref[...]
    oh1 = lane == r[:, 0:1]
    oh2 = lane == r[:, 1:2]
    both = jnp.where(oh1 | oh2, 1.0, 0.0)
    tri = jnp.where(lax.broadcasted_iota(I32, (te, te), 0) > lax.broadcasted_iota(I32, (te, te), 1),
                    1.0, 0.0).astype(BF16)
    before = jnp.dot(tri, both.astype(BF16), preferred_element_type=F32)
    base = before + carry_ref[...] + offs_ref[...]
    d1 = jnp.sum(jnp.where(oh1, base, 0.0), axis=-1, keepdims=True)
    d2 = jnp.sum(jnp.where(oh2, base, 0.0), axis=-1, keepdims=True)
    dest_ref[...] = jnp.where(lane == 0, d1, jnp.where(lane == 1, d2, 0.0)).astype(I32)
    carry_ref[...] += jnp.sum(both, axis=0, keepdims=True)


def _positions(route, offs, *, te):
    t = route.shape[0]
    return pl.pallas_call(
        functools.partial(_positions_body, te=te),
        grid=(t // te,),
        in_specs=[pl.BlockSpec((te, LANES), lambda i: (i, 0)),
                  pl.BlockSpec((1, LANES), lambda i: (0, 0))],
        out_specs=pl.BlockSpec((te, LANES), lambda i: (i, 0)),
        out_shape=jax.ShapeDtypeStruct((t, LANES), I32),
        scratch_shapes=[pltpu.VMEM((1, LANES), F32)],
        compiler_params=_cparams(1),
        name="positions",
    )(route, offs)


def _row_copy(src_ref, src_row, dst_ref, dst_row, sem):
    return pltpu.make_async_copy(src_ref.at[pl.ds(src_row, 1), :],
                                 dst_ref.at[pl.ds(dst_row, 1), :], sem)


def _dispatch_body(dest_hbm, pad_tile_ref, v_ref, xs_hbm, dsm, zbuf, sem, dsem, zsem, *,
                   td, tme, n_exp):
    i = pl.program_id(0)

    @pl.when(i == 0)
    def _():
        zbuf[...] = jnp.zeros_like(zbuf)

        def zero_tile(e):
            row0 = pl.multiple_of(pad_tile_ref[e] * tme, tme)
            return pltpu.make_async_copy(zbuf, xs_hbm.at[pl.ds(row0, tme), :], zsem)
        for e in range(n_exp):
            @pl.when(pad_tile_ref[e] >= 0)
            def _(e=e):
                zero_tile(e).start()
        for e in range(n_exp):
            @pl.when(pad_tile_ref[e] >= 0)
            def _(e=e):
                zero_tile(e).wait()

    cp = pltpu.make_async_copy(dest_hbm.at[pl.ds(i, 1), :], dsm, dsem)
    cp.start()
    cp.wait()

    def issue(r, _):
        _row_copy(v_ref, r, xs_hbm, dsm[0, 2 * r], sem).start(priority=0)
        _row_copy(v_ref, r, xs_hbm, dsm[0, 2 * r + 1], sem).start(priority=1)
        return 0
    lax.fori_loop(0, td, issue, 0, unroll=8)

    def drain(r, _):
        _row_copy(v_ref, 0, xs_hbm, 0, sem).wait()
        _row_copy(v_ref, 0, xs_hbm, 0, sem).wait()
        return 0
    lax.fori_loop(0, td, drain, 0)


def _dispatch(dest2, pad_tile, v, *, td, tme, n_rows):
    t, w = v.shape
    n_exp = pad_tile.shape[0]
    return pl.pallas_call(
        functools.partial(_dispatch_body, td=td, tme=tme, n_exp=n_exp),
        grid=(t // td,),
        in_specs=[pl.BlockSpec(memory_space=pl.ANY),
                  pl.BlockSpec(memory_space=pltpu.SMEM),
                  pl.BlockSpec((td, w), lambda i: (i, 0))],
        out_specs=pl.BlockSpec(memory_space=pl.ANY),
        out_shape=jax.ShapeDtypeStruct((n_rows, w), U32),
        scratch_shapes=[pltpu.SMEM((1, 2 * td), I32), pltpu.VMEM((tme, w), U32),
                        pltpu.SemaphoreType.DMA, pltpu.SemaphoreType.DMA, pltpu.SemaphoreType.DMA],
        compiler_params=_cparams(1),
        name="dispatch",
    )(dest2, pad_tile, v)


def _experts_body(te_ref, first_ref, last_ref, nxt_ref, nused_ref,
                  xs_ref, w1_hbm, w3_hbm, w2_hbm, o_ref,
                  w1b, w3b, w2b, st1, st3, st2, sem, *, half, n_conv):
    i = pl.program_id(0)
    used = i < nused_ref[0]
    nxt = nxt_ref[i]

    def copies(e):
        return (pltpu.make_async_copy(w1_hbm.at[e], st1, sem.at[0]),
                pltpu.make_async_copy(w3_hbm.at[e], st3, sem.at[1]),
                pltpu.make_async_copy(w2_hbm.at[e], st2, sem.at[2]))

    def convert():
        r13 = st1.shape[0] // n_conv
        r2 = st2.shape[0] // n_conv

        def body(c, _):
            rows = pl.ds(pl.multiple_of(c * r13, r13), r13)
            w1b[rows, :] = st1[rows, :].astype(BF16)
            w3b[rows, :] = st3[rows, :].astype(BF16)
            rows2 = pl.ds(pl.multiple_of(c * r2, r2), r2)
            w2b[rows2, :] = st2[rows2, :].astype(BF16)
            return 0
        lax.fori_loop(0, n_conv, body, 0)

    @pl.when(i == 0)
    def _():
        for cp in copies(te_ref[0]):
            cp.start()
        for cp in copies(te_ref[0]):
            cp.wait()
        convert()

    @pl.when(used & (first_ref[i] == 1) & (nxt >= 0))
    def _():
        for cp in copies(nxt):
            cp.start()

    @pl.when(used)
    def _():
        a, b = _unpack_bf16_pair(xs_ref[...])
        a, b = a.astype(BF16), b.astype(BF16)

        def up(w_ref):
            return (jnp.dot(a, w_ref[0:half, :], preferred_element_type=F32)
                    + jnp.dot(b, w_ref[half:2 * half, :], preferred_element_type=F32))
        h1 = up(w1b)
        act = (h1 * _sigmoid(h1) * up(w3b)).astype(BF16)
        o = jnp.dot(act, w2b[...], preferred_element_type=F32)
        o_ref[...] = _pack_bf16_pair(o[:, 0:half], o[:, half:2 * half])

    @pl.when(jnp.logical_not(used))
    def _():
        o_ref[...] = jnp.zeros_like(o_ref)

    @pl.when(used & (last_ref[i] == 1) & (nxt >= 0))
    def _():
        for cp in copies(nxt):
            cp.wait()
        convert()


def _experts(tile_expert, first, last, nxt, n_used, xs, w1, w3, w2, *, tme):
    rows, half = xs.shape
    n_e, d_model, d_exp = w1.shape
    grid_spec = pltpu.PrefetchScalarGridSpec(
        num_scalar_prefetch=5,
        grid=(rows // tme,),
        in_specs=[pl.BlockSpec((tme, half), lambda i, te, fi, la, nx, nu: (jnp.minimum(i, nu[0] - 1), 0)),
                  pl.BlockSpec(memory_space=pl.ANY), pl.BlockSpec(memory_space=pl.ANY),
                  pl.BlockSpec(memory_space=pl.ANY)],
        out_specs=pl.BlockSpec((tme, half), lambda i, *_: (i, 0)),
        scratch_shapes=[pltpu.VMEM((d_model, d_exp), BF16), pltpu.VMEM((d_model, d_exp), BF16),
                        pltpu.VMEM((d_exp, d_model), BF16),
                        pltpu.VMEM((d_model, d_exp), F32), pltpu.VMEM((d_model, d_exp), F32),
                        pltpu.VMEM((d_exp, d_model), F32),
                        pltpu.SemaphoreType.DMA((3,))],
    )
    return pl.pallas_call(
        functools.partial(_experts_body, half=half, n_conv=8),
        grid_spec=grid_spec,
        out_shape=jax.ShapeDtypeStruct((rows, half), I32),
        compiler_params=_cparams(1),
        name="experts",
    )(tile_expert, first, last, nxt, n_used, xs, w1, w3, w2)


def _sc_gather(table, idx, *, chunk):
    n_rows, width = idx.shape[0], table.shape[1]
    n_workers = SC_CORES * SC_SUBCORES
    per_w = n_rows // n_workers
    n_chunks = per_w // chunk
    assert per_w * n_workers == n_rows and n_chunks * chunk == per_w and n_chunks % 2 == 0
    mesh = plsc.VectorSubcoreMesh(core_axis_name="c", subcore_axis_name="s",
                                  num_cores=SC_CORES, num_subcores=SC_SUBCORES)

    @functools.partial(
        pl.kernel, mesh=mesh,
        out_type=jax.ShapeDtypeStruct((n_rows, width), table.dtype),
        scratch_types=[pltpu.VMEM((per_w,), I32), pltpu.VMEM((2, chunk, width), table.dtype),
                       pltpu.SemaphoreType.DMA((2,))],
    )
    def gather_kernel(table_hbm, idx_hbm, out_hbm, idx_v, rows_v, sem):
        wid = lax.axis_index("s") * SC_CORES + lax.axis_index("c")
        base = pl.multiple_of(wid * per_w, per_w)
        pltpu.sync_copy(idx_hbm.at[pl.ds(base, per_w)], idx_v)

        def gather(j, slot):
            off = pl.multiple_of(j * chunk, chunk)
            return pltpu.make_async_copy(table_hbm.at[idx_v.at[pl.ds(off, chunk)]],
                                         rows_v.at[slot], sem.at[slot])

        gather(0, 0).start()

        @pl.loop(0, n_chunks, step=2)
        def _(j):
            for slot in range(2):
                jj = j + slot
                gather(jj, slot).wait()

                @pl.when(jj + 1 < n_chunks)
                def _():
                    gather(jj + 1, 1 - slot).start()
                off = pl.multiple_of(base + jj * chunk, chunk)
                pltpu.sync_copy(rows_v.at[slot], out_hbm.at[pl.ds(off, chunk)])

    return gather_kernel(table, idx)


def _sc_inverse(dest_flat, n_rows):
    n_assign = dest_flat.shape[0]
    n_tok = n_assign // 2
    n_workers = SC_CORES * SC_SUBCORES
    per_w = n_rows // n_workers
    assert per_w * n_workers == n_rows and per_w % SC_LANES == 0 and n_assign % SC_LANES == 0
    mesh = plsc.VectorSubcoreMesh(core_axis_name="c", subcore_axis_name="s",
                                  num_cores=SC_CORES, num_subcores=SC_SUBCORES)

    @functools.partial(
        pl.kernel, mesh=mesh,
        out_type=jax.ShapeDtypeStruct((n_rows,), I32),
        scratch_types=[pltpu.VMEM((n_assign,), I32), pltpu.VMEM((per_w,), I32)],
        compiler_params=pltpu.CompilerParams(needs_layout_passes=False),
    )
    def inverse_kernel(dest_hbm, src_hbm, dest_v, src_v):
        wid = lax.axis_index("s") * SC_CORES + lax.axis_index("c")
        lo = pl.multiple_of(wid * per_w, per_w)
        pltpu.sync_copy(dest_hbm, dest_v)
        lane = lax.iota(I32, SC_LANES)

        @pl.loop(0, per_w, step=SC_LANES)
        def _(i):
            src_v[pl.ds(pl.multiple_of(i, SC_LANES), SC_LANES)] = lax.rem(lo + i + lane, n_tok)

        @pl.loop(0, n_assign, step=SC_LANES)
        def _(i):
            d = dest_v[pl.ds(pl.multiple_of(i, SC_LANES), SC_LANES)]
            a = i + lane
            tok = jnp.where(a >= n_tok, a - n_tok, a)
            mine = (d >= lo) & (d < lo + per_w)
            plsc.store_scatter(src_v, [d - lo], tok, mask=mine)

        pltpu.sync_copy(src_v, src_hbm.at[pl.ds(lo, per_w)])

    return inverse_kernel(dest_flat)


def _final_body(g1_ref, g2_ref, h1_ref, route_ref, g_ref, y_ref, *, half):
    route = route_ref[...]
    w1, w2 = route[:, 2:3], route[:, 3:4]
    a1, b1 = _unpack_bf16_pair(g1_ref[...])
    a2, b2 = _unpack_bf16_pair(g2_ref[...])
    h1 = h1_ref[...]
    ha = h1[:, 0:half] + (a1 * w1 + a2 * w2)
    hb = h1[:, half:2 * half] + (b1 * w1 + b2 * w2)
    ms = (jnp.sum(ha * ha, axis=-1, keepdims=True)
          + jnp.sum(hb * hb, axis=-1, keepdims=True)) / (2 * half)
    inv = lax.rsqrt(ms + EPS)
    y_ref[:, 0:half] = ha * inv * g_ref[:, 0:half]
    y_ref[:, half:2 * half] = hb * inv * g_ref[:, half:2 * half]


def _final(gathered, h1, route, g, *, tf, row0, n_rows):
    t, d_model = h1.shape
    half = d_model // 2
    tile0, tiles_t = row0 // tf, t // tf
    return pl.pallas_call(
        functools.partial(_final_body, half=half),
        grid=(n_rows // tf,),
        in_specs=[pl.BlockSpec((tf, half), lambda i: (tile0 + i, 0)),
                  pl.BlockSpec((tf, half), lambda i: (tiles_t + tile0 + i, 0)),
                  pl.BlockSpec((tf, d_model), lambda i: (tile0 + i, 0)),
                  pl.BlockSpec((tf, LANES), lambda i: (tile0 + i, 0)),
                  pl.BlockSpec((1, d_model), lambda i: (0, 0))],
        out_specs=pl.BlockSpec((tf, d_model), lambda i: (i, 0)),
        out_shape=jax.ShapeDtypeStruct((n_rows, d_model), F32),
        compiler_params=_cparams(1),
        name="final",
    )(gathered, gathered, h1, route, g)


def _combine_body(dest_hbm, o_hbm, h1_ref, route_ref, g_ref, y_ref, dsm, gbuf, sem, dsem, *,
                  tf, tile0, half):
    i = pl.program_id(0)
    cp = pltpu.make_async_copy(dest_hbm.at[pl.ds(tile0 + i, 1), :], dsm, dsem)
    cp.start()
    cp.wait()

    def issue(r, _):
        _row_copy(o_hbm, dsm[0, 2 * r], gbuf.at[0], r, sem).start(priority=0)
        _row_copy(o_hbm, dsm[0, 2 * r + 1], gbuf.at[1], r, sem).start(priority=1)
        return 0
    lax.fori_loop(0, tf, issue, 0, unroll=8)

    def drain(r, _):
        _row_copy(o_hbm, 0, gbuf.at[0], 0, sem).wait()
        _row_copy(o_hbm, 0, gbuf.at[0], 0, sem).wait()
        return 0
    lax.fori_loop(0, tf, drain, 0)

    route = route_ref[...]
    g1, g2 = route[:, 2:3], route[:, 3:4]
    a1, b1 = _unpack_bf16_pair(gbuf[0])
    a2, b2 = _unpack_bf16_pair(gbuf[1])
    h1 = h1_ref[...]
    ha = h1[:, 0:half] + (a1 * g1 + a2 * g2)
    hb = h1[:, half:2 * half] + (b1 * g1 + b2 * g2)
    ms = (jnp.sum(ha * ha, axis=-1, keepdims=True)
          + jnp.sum(hb * hb, axis=-1, keepdims=True)) / (2 * half)
    inv = lax.rsqrt(ms + EPS)
    y_ref[:, 0:half] = ha * inv * g_ref[:, 0:half]
    y_ref[:, half:2 * half] = hb * inv * g_ref[:, half:2 * half]


def _combine(dest2, o, h1, route, g, *, tf, row0, n_rows):
    d_model = h1.shape[1]
    half = d_model // 2
    tile0 = row0 // tf
    return pl.pallas_call(
        functools.partial(_combine_body, tf=tf, tile0=tile0, half=half),
        grid=(n_rows // tf,),
        in_specs=[pl.BlockSpec(memory_space=pl.ANY), pl.BlockSpec(memory_space=pl.ANY),
                  pl.BlockSpec((tf, d_model), lambda i: (tile0 + i, 0)),
                  pl.BlockSpec((tf, LANES), lambda i: (tile0 + i, 0)),
                  pl.BlockSpec((1, d_model), lambda i: (0, 0))],
        out_specs=pl.BlockSpec((tf, d_model), lambda i: (i, 0)),
        out_shape=jax.ShapeDtypeStruct((n_rows, d_model), F32),
        scratch_shapes=[pltpu.SMEM((1, 2 * tf), I32), pltpu.VMEM((2, tf, half), U32),
                        pltpu.SemaphoreType.DMA, pltpu.SemaphoreType.DMA],
        compiler_params=_cparams(1),
        name="combine",
    )(dest2, o, h1, route, g)


def kernel(x_prompt, x_sample, meta, norm_mix_g, w_in, ssm_lam_re, ssm_lam_im, ssm_log_step, ssm_b_re, ssm_b_im, ssm_c_re, ssm_c_im, ssm_d, ssm_w_glu, conv_w_dw, conv_b_dw, conv_ln_g, conv_ln_b, conv_w_pw, w_out, norm_ffn_g, router_group_w, router_group_b, router_expert_w, router_expert_b, expert_w1, expert_w3, expert_w2, final_g):
    assert w_in.shape[0] == 1, "single-layer trunk"
    bp, lp, d_model = x_prompt.shape
    bs, ls, _ = x_sample.shape
    n_meta = meta.shape[0]
    d_ssm = ssm_d.shape[-1]
    d_conv = conv_b_dw.shape[-1]
    n_ch = ssm_b_re.shape[-1]
    n_g = ssm_b_re.shape[2]
    n_grp = router_group_w.shape[-1]
    n_exp = router_expert_w.shape[-1]
    assert n_meta == CHUNK and lp % CHUNK == 0 and ls % CHUNK == 0
    n_p, n_s = bp * lp, bs * ls
    t = n_p + n_s
    tm = min(256, lp, ls)
    assert lp % tm == 0 and ls % tm == 0

    xp2 = x_prompt.reshape(n_p, d_model)
    xs2 = x_sample.reshape(n_s, d_model)
    row = lambda a: a.reshape(1, -1)
    w_in_bf = w_in[0].astype(BF16)

    xg, c, sga, sgb = _inproj(xp2, xs2, row(norm_mix_g[0]), w_in_bf, d_ssm=d_ssm, d_conv=d_conv, tm=tm)
    xg_meta, c_meta = _meta_inproj(meta, row(norm_mix_g[0]), w_in_bf, d_ssm=d_ssm, d_conv=d_conv)

    assert n_ch == CHUNK and LANES % n_ch == 0 and n_g % (LANES // n_ch) == 0
    intra, bst, cst, coef = _sg_prep(ssm_lam_re[0], ssm_lam_im[0], ssm_log_step[0], ssm_b_re[0],
                                     ssm_b_im[0], ssm_c_re[0], ssm_c_im[0], ssm_d[0])
    nc = t // CHUNK
    n_blk = d_ssm // LANES
    rt = min(512, nc)
    yi = _sg_mm_intra(xg, intra, rt=rt)
    sf, sb, init = _sg_mm_state(xg, xg_meta, bst, rt=rt)
    n_sub = sf.shape[0] // (n_blk * nc)
    cb = min(64, lp // CHUNK, ls // CHUNK)
    xf, xb = _sg_scan(sf.reshape(n_blk, nc, n_sub, LANES), sb.reshape(n_blk, nc, n_sub, LANES),
                      coef, init, cb=cb, chunks_pseq=lp // CHUNK, chunks_sseq=ls // CHUNK,
                      n_pchunks=n_p // CHUNK)
    z = _sg_mm_out(yi, xf.reshape(sf.shape), xb.reshape(sb.shape), cst, rt=rt)

    assert n_grp + n_exp <= LANES // 2
    rw32 = jnp.pad(jnp.concatenate([router_group_w[0], router_expert_w[0]], axis=1),
                   ((0, 0), (0, LANES // 2 - n_grp - n_exp)))
    rw_hi = rw32.astype(BF16)
    rw = jnp.concatenate([rw_hi, (rw32 - rw_hi.astype(F32)).astype(BF16)], axis=1)
    rb = jnp.zeros((1, LANES), F32).at[0, 0:n_grp].set(router_group_b[0]).at[
        0, n_grp:n_grp + n_exp].set(router_expert_b[0])
    h1, v, route, cnt = _mix(
        xp2, xs2, z, c, c_meta, sga, sgb, ssm_w_glu[0].astype(BF16), conv_w_pw[0].astype(BF16),
        w_out[0].astype(BF16), conv_w_dw[0], row(conv_b_dw[0]), row(conv_ln_g[0]),
        row(conv_ln_b[0]), row(norm_ffn_g[0]), rw, rb,
        tm=tm, p_seq=lp, s_seq=ls, n_grp=n_grp, n_exp=n_exp)

    tme = 256
    counts = cnt[0, 0:n_exp].astype(I32)
    tiles_e = (counts + tme - 1) // tme
    tile_end = jnp.cumsum(tiles_e)
    n_used = tile_end[-1]
    offs = jnp.zeros((1, LANES), F32).at[0, 0:n_exp].set(((tile_end - tiles_e) * tme).astype(F32))
    n_tiles = (2 * t) // tme + n_exp
    ids = jnp.arange(n_tiles, dtype=I32)
    te_map = jnp.sum((jnp.minimum(ids, n_used - 1)[:, None] >= tile_end[None, :]).astype(I32), axis=1)
    te_map = jnp.minimum(te_map, n_exp - 1)
    run_start = (tile_end - tiles_e)[te_map]
    run_end = tile_end[te_map]
    valid = ids < n_used
    first = (valid & (ids == run_start)).astype(I32)
    last = (valid & (ids == run_end - 1)).astype(I32)
    nxt = jnp.where(valid & (run_end < n_used), te_map[jnp.minimum(run_end, n_tiles - 1)], -1)

    dest = _positions(route, offs, te=min(512, t))
    idx = jnp.concatenate([dest[:, 0], dest[:, 1]])
    src = _sc_inverse(idx, n_tiles * tme)
    xs = _sc_gather(v, src, chunk=32)
    o = _experts(te_map, first, last, nxt, n_used.reshape(1), xs, expert_w1[0], expert_w3[0],
                 expert_w2[0], tme=tme)

    tf = min(256, n_p, n_s)
    gathered = _sc_gather(o, idx, chunk=32)
    fg = row(final_g)
    y_p = _final(gathered, h1, route, fg, tf=tf, row0=0, n_rows=n_p)
    y_s = _final(gathered, h1, route, fg, tf=tf, row0=n_p, n_rows=n_s)
    return (y_p.reshape(bp, lp, d_model), y_s.reshape(bs, ls, d_model))
```

```python
import functools

import jax
import jax.numpy as jnp
from jax import lax
from jax.experimental import pallas as pl
from jax.experimental.pallas import tpu as pltpu
from jax.experimental.pallas import tpu_sc as plsc

F32 = jnp.float32
BF16 = jnp.bfloat16
U32 = jnp.uint32
I32 = jnp.int32

EPS = 1e-6
LAM_RE_MAX = -1e-4
CHUNK = 16
LANES = 128
VMEM_LIMIT = 56 << 20
SC_CORES = 2
SC_SUBCORES = 16
SC_LANES = 16
HIGHEST = lax.Precision.HIGHEST


def _cparams(n_axes):
    return pltpu.CompilerParams(dimension_semantics=("arbitrary",) * n_axes,
                                vmem_limit_bytes=VMEM_LIMIT)


def _resident(shape):
    nd = len(shape)
    return pl.BlockSpec(shape, lambda *_: (0,) * nd, pipeline_mode=pl.Buffered(1))


def _sigmoid(x):
    return 1.0 / (1.0 + jnp.exp(-x))


def _rms(x, g):
    return x * lax.rsqrt(jnp.mean(x * x, axis=-1, keepdims=True) + EPS) * g


def _pack_bf16_pair(a, b):
    def rnd(x):
        u = pltpu.bitcast(x, U32)
        return (u + jnp.uint32(0x7FFF) + ((u >> 16) & jnp.uint32(1))) >> 16
    return pltpu.bitcast((rnd(a) << 16) | rnd(b), I32)


def _unpack_bf16_pair(p):
    p = pltpu.bitcast(p, U32)
    a = pltpu.bitcast(p & jnp.uint32(0xFFFF0000), F32)
    b = pltpu.bitcast(p << 16, F32)
    return a, b


def _inproj_body(xp_ref, xs_ref, g_ref, w_ref, xg_ref, c_ref, sga_ref, sgb_ref, usc, *,
                 tm, n_p_tiles, d_ssm, d_conv, d_model, col):
    i = pl.program_id(0)
    x = jnp.where(i < n_p_tiles, xp_ref[...], xs_ref[...])
    y = _rms(x, g_ref[...]).astype(BF16)

    def proj(lo, n):
        return jnp.dot(y, w_ref[:, lo:lo + n], preferred_element_type=F32)

    for k in range(d_ssm // col):
        u = proj(k * col, col)
        for b in range(col // LANES):
            usc[k * (col // LANES) + b] = u[:, b * LANES:(b + 1) * LANES]
    for b in range(d_ssm // LANES):
        for s in range(CHUNK):
            c0 = (b * CHUNK + s) * LANES
            xg_ref[:, c0:c0 + LANES] = usc[b, pl.ds(s, tm // CHUNK, stride=CHUNK), :].astype(BF16)
    for k in range(d_conv // col):
        cv = proj(d_ssm + k * col, col)
        cg = proj(d_ssm + d_conv + k * col, col)
        c_ref[:, k * col:(k + 1) * col] = (cv * _sigmoid(cg)).astype(BF16)
    base = d_ssm + 2 * d_conv
    for k in range(d_model // col):
        sga_ref[:, k * col:(k + 1) * col] = _sigmoid(proj(base + k * col, col)).astype(BF16)
        sgb_ref[:, k * col:(k + 1) * col] = _sigmoid(
            proj(base + d_model + k * col, col)).astype(BF16)


def _inproj(xp2, xs2, g, w_bf, *, d_ssm, d_conv, tm):
    n_p, d_model = xp2.shape
    n_s = xs2.shape[0]
    n_p_tiles, n_s_tiles = n_p // tm, n_s // tm
    t = n_p + n_s
    col = min(1024, d_ssm, d_conv, d_model)
    body = functools.partial(_inproj_body, tm=tm, n_p_tiles=n_p_tiles, d_ssm=d_ssm, d_conv=d_conv,
                             d_model=d_model, col=col)
    row = lambda i: (i, 0)
    return pl.pallas_call(
        body,
        grid=(n_p_tiles + n_s_tiles,),
        in_specs=[
            pl.BlockSpec((tm, d_model), lambda i: (jnp.minimum(i, n_p_tiles - 1), 0)),
            pl.BlockSpec((tm, d_model), lambda i: (jnp.maximum(i - n_p_tiles, 0), 0)),
            _resident((1, d_model)),
            _resident(w_bf.shape),
        ],
        out_specs=[pl.BlockSpec((tm // CHUNK, CHUNK * d_ssm), row), pl.BlockSpec((tm, d_conv), row),
                   pl.BlockSpec((tm, d_model), row), pl.BlockSpec((tm, d_model), row)],
        out_shape=[jax.ShapeDtypeStruct((t // CHUNK, CHUNK * d_ssm), BF16),
                   jax.ShapeDtypeStruct((t, d_conv), BF16),
                   jax.ShapeDtypeStruct((t, d_model), BF16), jax.ShapeDtypeStruct((t, d_model), BF16)],
        scratch_shapes=[pltpu.VMEM((d_ssm // LANES, tm, LANES), F32)],
        compiler_params=_cparams(1),
        name="inproj",
    )(xp2, xs2, g, w_bf)


def _meta_body(m_ref, g_ref, w_ref, xg_ref, c_ref, *, d_ssm, d_conv):
    y = _rms(m_ref[...], g_ref[...]).astype(BF16)
    u = jnp.dot(y, w_ref[:, 0:d_ssm], preferred_element_type=F32)
    first = lax.broadcasted_iota(I32, (CHUNK, LANES), 0) == 0
    for b in range(d_ssm // LANES):
        for s in range(CHUNK):
            c0 = (b * CHUNK + s) * LANES
            piece = jnp.broadcast_to(u[s:s + 1, b * LANES:(b + 1) * LANES], (CHUNK, LANES))
            xg_ref[:, c0:c0 + LANES] = jnp.where(first, piece, 0.0).astype(BF16)
    cv = jnp.dot(y, w_ref[:, d_ssm:d_ssm + d_conv], preferred_element_type=F32)
    cg = jnp.dot(y, w_ref[:, d_ssm + d_conv:d_ssm + 2 * d_conv], preferred_element_type=F32)
    c_ref[...] = (cv * _sigmoid(cg)).astype(BF16)


def _meta_inproj(meta, g, w_bf, *, d_ssm, d_conv):
    n_meta, d_model = meta.shape
    ncol = d_ssm + 2 * d_conv
    return pl.pallas_call(
        functools.partial(_meta_body, d_ssm=d_ssm, d_conv=d_conv),
        grid=(1,),
        in_specs=[pl.BlockSpec((n_meta, d_model), lambda i: (0, 0)),
                  pl.BlockSpec((1, d_model), lambda i: (0, 0)),
                  pl.BlockSpec((d_model, ncol), lambda i: (0, 0))],
        out_specs=[pl.BlockSpec((CHUNK, CHUNK * d_ssm), lambda i: (0, 0)),
                   pl.BlockSpec((n_meta, d_conv), lambda i: (0, 0))],
        out_shape=[jax.ShapeDtypeStruct((CHUNK, CHUNK * d_ssm), BF16),
                   jax.ShapeDtypeStruct((n_meta, d_conv), BF16)],
        compiler_params=_cparams(1),
        name="meta_inproj",
    )(meta, g, w_bf)


def _cmul(ar, ai, br, bi):
    return ar * br - ai * bi, ar * bi + ai * br


def _discretize(lam_re, lam_im, log_step):
    lr = jnp.minimum(lam_re, LAM_RE_MAX)
    dt = jnp.exp(log_step)
    mag = jnp.exp(lr * dt)
    ar = mag * jnp.cos(lam_im * dt)
    ai = mag * jnp.sin(lam_im * dt)
    den = lr * lr + lam_im * lam_im
    nr = ar - 1.0
    fr = (nr * lr + ai * lam_im) / den
    fi = (ai * lr - nr * lam_im) / den
    return ar, ai, fr, fi


def _cpow(ar, ai, k, nbits, shape):
    pr = jnp.ones(shape, F32)
    pi = jnp.zeros(shape, F32)
    br = jnp.broadcast_to(ar, shape)
    bi = jnp.broadcast_to(ai, shape)
    kk = jnp.broadcast_to(k, shape)
    for b in range(nbits):
        sel = ((kk >> b) & 1) == 1
        nr, ni = _cmul(pr, pi, br, bi)
        pr = jnp.where(sel, nr, pr)
        pi = jnp.where(sel, ni, pi)
        br, bi = _cmul(br, bi, br, bi)
    return pr, pi


def _spread(x, n_ch, shift):
    rows, w = x.shape
    lane = lax.broadcasted_iota(I32, (rows, LANES), 1)
    keep = (lane >= shift) & (lane < shift + n_ch)
    out = []
    for b in range(w // n_ch):
        src = (b * n_ch // LANES) * LANES
        amount = lax.rem(shift + LANES - (b * n_ch) % LANES, LANES)
        out.append(jnp.where(keep, pltpu.roll(x[:, src:src + LANES], amount, 1), 0.0))
    return jnp.concatenate(out, axis=1)


def _tile_lanes(x, n_ch, width):
    span = n_ch
    while span < LANES:
        x = x + pltpu.roll(x, span, 1)
        span *= 2
    return jnp.concatenate([x] * (width // LANES), axis=1)


def _sg_prep_body(lam_re_c, lam_im_c, ls_c, ct_re, ct_im, bt_re, bt_im,
                  lam_re_r, lam_im_r, ls_r, btr_re, btr_im, dpad,
                  m_ref, b_ref, c_ref, aq_ref, *, n_state, n_ch):
    q = CHUNK
    width = q * n_ch
    per_blk = LANES // n_ch
    gl = pl.program_id(0) % per_blk
    shift = gl * n_ch
    kblk = lax.broadcasted_iota(I32, (1, width), 1) // n_ch
    strips = []
    for d in range(2):
        ar, ai, fr, fi = _discretize(lam_re_c[d], lam_im_c[d], ls_c[d])
        kexp = kblk if d == 0 else (q - 1) - kblk
        wr, wi = _cpow(ar, ai, kexp, 4, (n_state, width))
        gcr, gci = _cmul(_tile_lanes(ct_re[d], n_ch, width), _tile_lanes(ct_im[d], n_ch, width),
                         wr, wi)
        gfr, gfi = _cmul(gcr, gci, fr, fi)
        strips.append(jnp.dot(bt_re[d], gfr, precision=HIGHEST, preferred_element_type=F32)
                      - jnp.dot(bt_im[d], gfi, precision=HIGHEST, preferred_element_type=F32))
        g1r, g1i = _cmul(gcr, gci, ar, ai)
        for comp, val in ((0, g1r), (1, -g1i)):
            c_ref[2 * d + comp] = _spread(val, n_ch, shift).astype(BF16)

    zf, zb = strips
    zero = jnp.zeros((n_ch, width), F32)
    z512 = jnp.concatenate([zb, zero], axis=1) + pltpu.roll(
        jnp.concatenate([zf, zero], axis=1), (q - 1) * n_ch, 1)
    row = lax.broadcasted_iota(I32, (n_ch, 2 * width), 0)
    lane = lax.broadcasted_iota(I32, (n_ch, 2 * width), 1)
    z512 = z512 + jnp.where(lane - (q - 1) * n_ch == row, dpad[...], 0.0)
    zwide = _spread(z512, n_ch, shift)
    for s in range(q):
        lo = (q - 1 - s) * LANES
        m_ref[s] = zwide[:, lo:lo + q * LANES].astype(BF16)

    parmask = (lax.broadcasted_iota(I32, (1, 2 * n_state), 1) // n_state) == gl % 2
    pieces = [[None] * 4 for _ in range(q)]
    for d in range(2):
        ar, ai, fr, fi = _discretize(lam_re_r[d], lam_im_r[d], ls_r[d])
        pw = [(jnp.ones_like(ar), jnp.zeros_like(ar))]
        for _ in range(q):
            pw.append(_cmul(pw[-1][0], pw[-1][1], ar, ai))
        for s in range(q):
            e = (q - 1 - s) if d == 0 else s
            cr, ci = _cmul(fr, fi, pw[e][0], pw[e][1])
            br, bi = _cmul(btr_re[d], btr_im[d], cr, ci)
            pieces[s][2 * d] = jnp.where(parmask, br, 0.0)
            pieces[s][2 * d + 1] = jnp.where(parmask, bi, 0.0)
        aq_ref[:, 2 * d * n_state:(2 * d + 1) * n_state] = pw[q][0][:, 0:n_state]
        aq_ref[:, (2 * d + 1) * n_state:(2 * d + 2) * n_state] = pw[q][1][:, 0:n_state]
    n_pair = per_blk // 2
    for s in range(q):
        cols = []
        for seg in range(4):
            for blk in range(n_pair):
                cols.append(jnp.where(gl // 2 == blk, pieces[s][seg], 0.0))
        b_ref[s] = jnp.concatenate(cols, axis=1).astype(BF16)


def _sg_prep(lam_re, lam_im, log_step, b_re, b_im, c_re, c_im, d_skip):
    _, n_g, n_state, n_ch = b_re.shape
    q = CHUNK
    width = q * n_ch
    per_blk = LANES // n_ch
    n_blk = n_g // per_blk
    sw = 4 * per_blk * n_state
    dup = lambda x: jnp.concatenate([x, x], axis=-1)
    lane_pad = lambda x: jnp.pad(x, ((0, 0),) * 3 + ((0, LANES - n_ch),))
    lam_re_c, lam_im_c = lam_re[..., None], lam_im[..., None]
    ls_c = log_step[..., None, None]
    ct_re = lane_pad(jnp.swapaxes(c_re, -1, -2))
    ct_im = lane_pad(jnp.swapaxes(c_im, -1, -2))
    bt_re, bt_im = jnp.swapaxes(b_re, -1, -2), jnp.swapaxes(b_im, -1, -2)
    lam_re_r, lam_im_r = dup(lam_re)[:, :, None, :], dup(lam_im)[:, :, None, :]
    ls_r = jnp.broadcast_to(log_step[..., None, None], (2, n_g, 1, 2 * n_state))
    btr_re, btr_im = dup(bt_re), dup(bt_im)
    dpad = jnp.pad(d_skip.reshape(n_g, 1, n_ch), ((0, 0), (0, 0), ((q - 1) * n_ch, width)))

    def dspec(shape):
        return pl.BlockSpec((2, None) + shape, lambda g: (0, g, 0, 0))

    body = functools.partial(_sg_prep_body, n_state=n_state, n_ch=n_ch)
    m4, b4, c5, aq = pl.pallas_call(
        body,
        grid=(n_g,),
        in_specs=[dspec((n_state, 1)), dspec((n_state, 1)), dspec((1, 1)),
                  dspec((n_state, LANES)), dspec((n_state, LANES)),
                  dspec((n_ch, n_state)), dspec((n_ch, n_state)),
                  dspec((1, 2 * n_state)), dspec((1, 2 * n_state)), dspec((1, 2 * n_state)),
                  dspec((n_ch, 2 * n_state)), dspec((n_ch, 2 * n_state)),
                  pl.BlockSpec((None, 1, 2 * width), lambda g: (g, 0, 0))],
        out_specs=[
            pl.BlockSpec((None, q, n_ch, q * LANES), lambda g: (g // per_blk, 0, g % per_blk, 0)),
            pl.BlockSpec((None, q, n_ch, sw), lambda g: (g // per_blk, 0, g % per_blk, 0)),
            pl.BlockSpec((None, 4, None, n_state, q * LANES),
                         lambda g: (g // per_blk, 0, g % per_blk, 0, 0)),
            pl.BlockSpec((None, 1, 4 * n_state), lambda g: (g, 0, 0))],
        out_shape=[jax.ShapeDtypeStruct((n_blk, q, LANES, q * LANES), BF16),
                   jax.ShapeDtypeStruct((n_blk, q, LANES, sw), BF16),
                   jax.ShapeDtypeStruct((n_blk, 4, per_blk, n_state, q * LANES), BF16),
                   jax.ShapeDtypeStruct((n_g, 1, 4 * n_state), F32)],
        compiler_params=_cparams(1),
        name="ssm_prep",
    )(lam_re_c, lam_im_c, ls_c, ct_re, ct_im, bt_re, bt_im,
      lam_re_r, lam_im_r, ls_r, btr_re, btr_im, dpad)
    intra = m4.reshape(n_blk, q * LANES, q * LANES)
    bst = b4.reshape(n_blk, q * LANES, sw)
    cst = c5.reshape(n_blk, sw, q * LANES)
    a = aq.reshape(n_blk, per_blk // 2, 2, 2, 2, n_state)
    a = a.transpose(3, 4, 0, 1, 2, 5).reshape(2, 2, n_blk, per_blk // 2, LANES)
    coef = []
    for d in range(2):
        re, im = a[d, 0], a[d, 1]
        coef.append((jnp.concatenate([re, re], axis=1), jnp.concatenate([-im, im], axis=1)))
    return intra, bst, cst, coef


def _sg_mm_intra_body(x_ref, m_ref, y_ref):
    y_ref[...] = jnp.dot(x_ref[...], m_ref[...], preferred_element_type=F32).astype(BF16)


def _sg_mm_intra(xg, intra, *, rt):
    nc = xg.shape[0]
    n_blk, kw, _ = intra.shape
    return pl.pallas_call(
        _sg_mm_intra_body,
        grid=(n_blk, nc // rt),
        in_specs=[pl.BlockSpec((rt, kw), lambda b, i: (i, b)),
                  pl.BlockSpec((None, kw, kw), lambda b, i: (b, 0, 0))],
        out_specs=pl.BlockSpec((rt, kw), lambda b, i: (i, b)),
        out_shape=jax.ShapeDtypeStruct((nc, n_blk * kw), BF16),
        compiler_params=_cparams(2),
        name="ssm_intra",
    )(xg, intra)


def _sg_mm_state_body(x_ref, xm_ref, b_ref, sf_ref, sb_ref, init_ref, *, rt, n_sub):
    r = jnp.dot(x_ref[...], b_ref[...], preferred_element_type=F32)
    for k in range(n_sub):
        sf_ref[pl.ds(k, rt, stride=n_sub), :] = r[:, k * LANES:(k + 1) * LANES]
        sb_ref[pl.ds(k, rt, stride=n_sub), :] = r[:, (n_sub + k) * LANES:(n_sub + k + 1) * LANES]

    @pl.when(pl.program_id(1) == 0)
    def _():
        r0 = jnp.dot(xm_ref[...], b_ref[...], preferred_element_type=F32)
        for k in range(n_sub):
            init_ref[k:k + 1, :] = r0[0:1, k * LANES:(k + 1) * LANES]


def _sg_mm_state(xg, xg_meta, bst, *, rt):
    nc = xg.shape[0]
    n_blk, kw, sw = bst.shape
    n_sub = sw // (2 * LANES)
    n_rt = nc // rt
    body = functools.partial(_sg_mm_state_body, rt=rt, n_sub=n_sub)
    dense = jax.ShapeDtypeStruct((n_blk * nc * n_sub, LANES), F32)
    dspec = pl.BlockSpec((rt * n_sub, LANES), lambda b, i: (b * n_rt + i, 0))
    return pl.pallas_call(
        body,
        grid=(n_blk, n_rt),
        in_specs=[pl.BlockSpec((rt, kw), lambda b, i: (i, b)),
                  pl.BlockSpec((CHUNK, kw), lambda b, i: (0, b)),
                  pl.BlockSpec((None, kw, sw), lambda b, i: (b, 0, 0))],
        out_specs=[dspec, dspec, pl.BlockSpec((None, n_sub, LANES), lambda b, i: (b, 0, 0))],
        out_shape=[dense, dense, jax.ShapeDtypeStruct((n_blk, n_sub, LANES), F32)],
        compiler_params=_cparams(2),
        name="ssm_state",
    )(xg, xg_meta, bst)


def _sg_scan_body(sf_ref, sb_ref, a1f_ref, a2f_ref, a1b_ref, a2b_ref, init_ref, xf_ref, xb_ref,
                  stf, stb, *, cb, n_blk, half, n_pblk, blk_pseq, blk_sseq):
    j = pl.program_id(0)
    pos = jnp.where(j < n_pblk, lax.rem(j, blk_pseq), lax.rem(jnp.maximum(j - n_pblk, 0), blk_sseq))

    @pl.when(pos == 0)
    def _():
        stf[...] = init_ref[...]
        stb[...] = jnp.zeros_like(stb)

    def body(i, carry):
        ef, eb = carry
        ib = cb - 1 - i
        nf, nb = [], []
        for g in range(n_blk):
            xf_ref[g, i] = ef[g]
            xb_ref[g, ib] = eb[g]
            nf.append(a1f_ref[g] * ef[g] + a2f_ref[g] * pltpu.roll(ef[g], half, 0) + sf_ref[g, i])
            nb.append(a1b_ref[g] * eb[g] + a2b_ref[g] * pltpu.roll(eb[g], half, 0) + sb_ref[g, ib])
        return tuple(nf), tuple(nb)

    ef, eb = lax.fori_loop(0, cb, body, (tuple(stf[g] for g in range(n_blk)),
                                         tuple(stb[g] for g in range(n_blk))))
    for g in range(n_blk):
        stf[g] = ef[g]
        stb[g] = eb[g]


def _sg_scan(sf, sb, coef, init, *, cb, chunks_pseq, chunks_sseq, n_pchunks):
    n_blk, nc, n_sub, _ = sf.shape
    n_pblk, blk_pseq, blk_sseq = n_pchunks // cb, chunks_pseq // cb, chunks_sseq // cb

    def bwd_block(j):
        in_p = j < n_pblk
        pos = jnp.where(in_p, lax.rem(j, blk_pseq), lax.rem(jnp.maximum(j - n_pblk, 0), blk_sseq))
        ln = jnp.where(in_p, blk_pseq, blk_sseq)
        return j - pos + ln - 1 - pos

    body = functools.partial(_sg_scan_body, cb=cb, n_blk=n_blk, half=n_sub // 2, n_pblk=n_pblk,
                             blk_pseq=blk_pseq, blk_sseq=blk_sseq)
    fwd = pl.BlockSpec((n_blk, cb, n_sub, LANES), lambda j: (0, j, 0, 0))
    bwd = pl.BlockSpec((n_blk, cb, n_sub, LANES), lambda j: (0, bwd_block(j), 0, 0))
    small = pl.BlockSpec((n_blk, n_sub, LANES), lambda j: (0, 0, 0))
    return pl.pallas_call(
        body,
        grid=(nc // cb,),
        in_specs=[fwd, bwd, small, small, small, small, small],
        out_specs=[fwd, bwd],
        out_shape=[jax.ShapeDtypeStruct(sf.shape, F32), jax.ShapeDtypeStruct(sb.shape, F32)],
        scratch_shapes=[pltpu.VMEM((n_blk, n_sub, LANES), F32), pltpu.VMEM((n_blk, n_sub, LANES), F32)],
        compiler_params=_cparams(1),
        name="ssm_scan",
    )(sf, sb, coef[0][0], coef[0][1], coef[1][0], coef[1][1], init)


def _gelu_tanh(x):
    return 0.5 * x * (1.0 + jnp.tanh(0.7978845608028654 * (x + 0.044715 * (x * x * x))))


def _sg_mm_out_body(y_ref, xf_ref, xb_ref, c_ref, z_ref, *, rt, n_sub):
    cols = [xf_ref[pl.ds(k, rt, stride=n_sub), :] for k in range(n_sub)]
    cols += [xb_ref[pl.ds(k, rt, stride=n_sub), :] for k in range(n_sub)]
    state = jnp.concatenate(cols, axis=1).astype(BF16)
    y = y_ref[...].astype(F32) + jnp.dot(state, c_ref[...], preferred_element_type=F32)
    z_ref[...] = _gelu_tanh(y).astype(BF16)


def _sg_mm_out(yi, xf, xb, cst, *, rt):
    nc = yi.shape[0]
    n_blk, sw, kw = cst.shape
    n_sub = sw // (2 * LANES)
    n_rt = nc // rt
    dspec = pl.BlockSpec((rt * n_sub, LANES), lambda b, i: (b * n_rt + i, 0))
    return pl.pallas_call(
        functools.partial(_sg_mm_out_body, rt=rt, n_sub=n_sub),
        grid=(n_blk, n_rt),
        in_specs=[pl.BlockSpec((rt, kw), lambda b, i: (i, b)), dspec, dspec,
                  pl.BlockSpec((None, sw, kw), lambda b, i: (b, 0, 0))],
        out_specs=pl.BlockSpec((rt, kw), lambda b, i: (i, b)),
        out_shape=jax.ShapeDtypeStruct((nc, n_blk * kw), BF16),
        compiler_params=_cparams(2),
        name="ssm_out",
    )(yi, xf, xb, cst)


def _mix_body(xp_ref, xs_ref, z_ref, c_ref, cprev_ref, cnext_ref, cmeta_ref, sga_ref, sgb_ref,
              wglu_ref, wpw_ref, wout_ref, wconv_ref, bdw_ref, lng_ref, lnb_ref, gffn_ref,
              rw_ref, rb_ref,
              h1_ref, v_ref, route_ref, cnt_ref,
              cw_ref, conv_ref, zsc_ref, *,
              tm, n_p_tiles, tiles_per_pseq, tiles_per_sseq, d_model, d_ssm, d_conv,
              n_grp, n_exp, exp_per_grp):
    i = pl.program_id(0)
    in_prompt = i < n_p_tiles
    x = jnp.where(in_prompt, xp_ref[...], xs_ref[...])
    pos_p = lax.rem(i, tiles_per_pseq)
    pos_s = lax.rem(jnp.maximum(i - n_p_tiles, 0), tiles_per_sseq)
    is_start = jnp.where(in_prompt, pos_p == 0, pos_s == 0)
    is_end = jnp.where(in_prompt, pos_p == tiles_per_pseq - 1, pos_s == tiles_per_sseq - 1)

    halo = CHUNK
    cw_ref[0:halo, :] = jnp.where(is_start, cmeta_ref[...], cprev_ref[...]).astype(F32)
    cw_ref[halo:halo + tm, :] = c_ref[...].astype(F32)
    cw_ref[halo + tm:2 * halo + tm, :] = jnp.where(is_end, 0.0, cnext_ref[...].astype(F32))
    sub = 8
    nq = wconv_ref.shape[1] // LANES
    for lc in range(d_conv // LANES):
        ls = slice(lc * LANES, (lc + 1) * LANES)
        stacked = jnp.concatenate(
            [cw_ref[sub * q:sub * q + tm + sub, ls] for q in range(nq)], axis=1).astype(BF16)
        part = jnp.dot(stacked, wconv_ref[lc], preferred_element_type=F32)
        out = part[0:tm, 0:LANES]
        for r in range(1, sub):
            out = out + part[r:r + tm, r * LANES:(r + 1) * LANES]
        conv_ref[:, ls] = out
    cc = conv_ref[...] + bdw_ref[...]
    mu = jnp.mean(cc, axis=-1, keepdims=True)
    var = jnp.mean(jnp.square(cc - mu), axis=-1, keepdims=True)
    cc = (cc - mu) * lax.rsqrt(var + EPS) * lng_ref[...] + lnb_ref[...]
    cc = (cc * _sigmoid(cc)).astype(BF16)
    y_b = jnp.dot(cc, wpw_ref[...], preferred_element_type=F32)

    for b in range(d_ssm // LANES):
        for s in range(CHUNK):
            c0 = (b * CHUNK + s) * LANES
            zsc_ref[b, pl.ds(s, tm // CHUNK, stride=CHUNK), :] = z_ref[:, c0:c0 + LANES].astype(F32)
    z = jnp.concatenate([zsc_ref[b] for b in range(d_ssm // LANES)], axis=1).astype(BF16)
    va = jnp.dot(z, wglu_ref[:, 0:d_model], preferred_element_type=F32)
    ga = jnp.dot(z, wglu_ref[:, d_model:2 * d_model], preferred_element_type=F32)
    y_a = va * _sigmoid(ga)
    merged = (sga_ref[...].astype(F32) * y_a + sgb_ref[...].astype(F32) * y_b).astype(BF16)
    h1 = x + jnp.dot(merged, wout_ref[...], preferred_element_type=F32)
    h1_ref[...] = h1
    v = _rms(h1, gffn_ref[...])
    half = d_model // 2
    v_ref[...] = _pack_bf16_pair(v[:, 0:half], v[:, half:d_model])

    v_hi = v.astype(BF16)
    v_lo = (v - v_hi.astype(F32)).astype(BF16)
    acc = (jnp.dot(v_hi, rw_ref[...], preferred_element_type=F32)
           + jnp.dot(v_lo, rw_ref[...], preferred_element_type=F32))
    logits = acc + pltpu.roll(acc, LANES // 2, 1) + rb_ref[...]
    lane = lax.broadcasted_iota(I32, (tm, LANES), 1).astype(F32)
    big = jnp.float32(1e9)
    neg = jnp.float32(-jnp.inf)
    gmask = lane < n_grp
    lg = jnp.where(gmask, logits, neg)
    gmax = jnp.max(lg, axis=-1, keepdims=True)
    grp = jnp.min(jnp.where(lg == gmax, lane, big), axis=-1, keepdims=True)
    p_grp = 1.0 / jnp.sum(jnp.where(gmask, jnp.exp(logits - gmax), 0.0), axis=-1, keepdims=True)
    lo = n_grp + grp * exp_per_grp
    emask = (lane >= lo) & (lane < lo + exp_per_grp)
    le = jnp.where(emask, logits, neg)
    m1 = jnp.max(le, axis=-1, keepdims=True)
    i1 = jnp.min(jnp.where(le == m1, lane, big), axis=-1, keepdims=True)
    le2 = jnp.where(lane == i1, neg, le)
    m2 = jnp.max(le2, axis=-1, keepdims=True)
    i2 = jnp.min(jnp.where(le2 == m2, lane, big), axis=-1, keepdims=True)
    t = jnp.exp(m2 - m1)
    w1 = 1.0 / (1.0 + t)
    e1 = i1 - n_grp
    e2 = i2 - n_grp
    route_ref[...] = jnp.where(lane == 0, e1, jnp.where(lane == 1, e2, jnp.where(
        lane == 2, p_grp * w1, jnp.where(lane == 3, p_grp * (t * w1), 0.0))))

    @pl.when(i == 0)
    def _():
        cnt_ref[...] = jnp.zeros_like(cnt_ref)

    hot = jnp.where((lane == e1) | (lane == e2), 1.0, 0.0)
    cnt_ref[...] += jnp.sum(hot, axis=0, keepdims=True)


def _mix(xp2, xs2, z, c, c_meta, sga, sgb, wglu, wpw, wout, wconv, bdw, lng, lnb, gffn, rw, rb, *,
         tm, p_seq, s_seq, n_grp, n_exp):
    n_p, d_model = xp2.shape
    n_s = xs2.shape[0]
    t = n_p + n_s
    d_ssm, d_conv = z.shape[1] // CHUNK, c.shape[1]
    n_p_tiles, n_s_tiles = n_p // tm, n_s // tm
    hpt = tm // CHUNK
    n_hblk = t // CHUNK
    body = functools.partial(
        _mix_body, tm=tm, n_p_tiles=n_p_tiles, tiles_per_pseq=p_seq // tm,
        tiles_per_sseq=s_seq // tm, d_model=d_model, d_ssm=d_ssm, d_conv=d_conv,
        n_grp=n_grp, n_exp=n_exp, exp_per_grp=n_exp // n_grp)
    row = lambda i: (i, 0)
    return pl.pallas_call(
        body,
        grid=(n_p_tiles + n_s_tiles,),
        in_specs=[
            pl.BlockSpec((tm, d_model), lambda i: (jnp.minimum(i, n_p_tiles - 1), 0)),
            pl.BlockSpec((tm, d_model), lambda i: (jnp.maximum(i - n_p_tiles, 0), 0)),
            pl.BlockSpec((tm // CHUNK, CHUNK * d_ssm), row),
            pl.BlockSpec((tm, d_conv), row),
            pl.BlockSpec((CHUNK, d_conv), lambda i: (jnp.maximum(i * hpt - 1, 0), 0)),
            pl.BlockSpec((CHUNK, d_conv), lambda i: (jnp.minimum((i + 1) * hpt, n_hblk - 1), 0)),
            _resident(c_meta.shape),
            pl.BlockSpec((tm, d_model), row),
            pl.BlockSpec((tm, d_model), row),
            _resident(wglu.shape), _resident(wpw.shape), _resident(wout.shape),
            _resident(wconv.shape), _resident(bdw.shape), _resident(lng.shape), _resident(lnb.shape),
            _resident(gffn.shape), _resident(rw.shape), _resident(rb.shape),
        ],
        out_specs=[pl.BlockSpec((tm, d_model), row), pl.BlockSpec((tm, d_model // 2), row),
                   pl.BlockSpec((tm, LANES), row), pl.BlockSpec((1, LANES), lambda i: (0, 0))],
        out_shape=[jax.ShapeDtypeStruct((t, d_model), F32),
                   jax.ShapeDtypeStruct((t, d_model // 2), I32),
                   jax.ShapeDtypeStruct((t, LANES), F32),
                   jax.ShapeDtypeStruct((1, LANES), F32)],
        scratch_shapes=[pltpu.VMEM((tm + 2 * CHUNK, d_conv), F32), pltpu.VMEM((tm, d_conv), F32),
                        pltpu.VMEM((d_ssm // LANES, tm, LANES), F32)],
        compiler_params=_cparams(1),
        name="mix",
    )(xp2, xs2, z, c, c, c, c_meta, sga, sgb, wglu, wpw, wout, wconv, bdw, lng, lnb, gffn, rw, rb)


def _positions_body(route_ref, offs_ref, dest_ref, carry_ref, *, te):
    @pl.when(pl.program_id(0) == 0)
    def _():
        carry_ref[...] = jnp.zeros_like(carry_ref)

    lane = lax.broadcasted_iota(I32, (te, LANES), 1).astype(F32)
    r = route_ref[...]
    oh1 = lane == r[:, 0:1]
    oh2 = lane == r[:, 1:2]
    both = jnp.where(oh1 | oh2, 1.0, 0.0)
    tri = jnp.where(lax.broadcasted_iota(I32, (te, te), 0) > lax.broadcasted_iota(I32, (te, te), 1),
                    1.0, 0.0).astype(BF16)
    before = jnp.dot(tri, both.astype(BF16), preferred_element_type=F32)
    base = before + carry_ref[...] + offs_ref[...]
    d1 = jnp.sum(jnp.where(oh1, base, 0.0), axis=-1, keepdims=True)
    d2 = jnp.sum(jnp.where(oh2, base, 0.0), axis=-1, keepdims=True)
    dest_ref[...] = jnp.where(lane == 0, d1, jnp.where(lane == 1, d2, 0.0)).astype(I32)
    carry_ref[...] += jnp.sum(both, axis=0, keepdims=True)


def _positions(route, offs, *, te):
    t = route.shape[0]
    return pl.pallas_call(
        functools.partial(_positions_body, te=te),
        grid=(t // te,),
        in_specs=[pl.BlockSpec((te, LANES), lambda i: (i, 0)),
                  pl.BlockSpec((1, LANES), lambda i: (0, 0))],
        out_specs=pl.BlockSpec((te, LANES), lambda i: (i, 0)),
        out_shape=jax.ShapeDtypeStruct((t, LANES), I32),
        scratch_shapes=[pltpu.VMEM((1, LANES), F32)],
        compiler_params=_cparams(1),
        name="positions",
    )(route, offs)


def _experts_body(te_ref, first_ref, last_ref, nxt_ref, nused_ref,
                  xs_ref, w1_hbm, w3_hbm, w2_hbm, o_ref,
                  w1b, w3b, w2b, st1, st3, st2, sem, *, half, n_conv):
    i = pl.program_id(0)
    used = i < nused_ref[0]
    nxt = nxt_ref[i]

    def copies(e):
        return (pltpu.make_async_copy(w1_hbm.at[e], st1, sem.at[0]),
                pltpu.make_async_copy(w3_hbm.at[e], st3, sem.at[1]),
                pltpu.make_async_copy(w2_hbm.at[e], st2, sem.at[2]))

    def convert():
        r13 = st1.shape[0] // n_conv
        r2 = st2.shape[0] // n_conv

        def body(c, _):
            rows = pl.ds(pl.multiple_of(c * r13, r13), r13)
            w1b[rows, :] = st1[rows, :].astype(BF16)
            w3b[rows, :] = st3[rows, :].astype(BF16)
            rows2 = pl.ds(pl.multiple_of(c * r2, r2), r2)
            w2b[rows2, :] = st2[rows2, :].astype(BF16)
            return 0
        lax.fori_loop(0, n_conv, body, 0)

    @pl.when(i == 0)
    def _():
        for cp in copies(te_ref[0]):
            cp.start()
        for cp in copies(te_ref[0]):
            cp.wait()
        convert()

    @pl.when(used & (first_ref[i] == 1) & (nxt >= 0))
    def _():
        for cp in copies(nxt):
            cp.start(priority=1)

    @pl.when(used)
    def _():
        a, b = _unpack_bf16_pair(xs_ref[...])
        a, b = a.astype(BF16), b.astype(BF16)

        def up(w_ref):
            return (jnp.dot(a, w_ref[0:half, :], preferred_element_type=F32)
                    + jnp.dot(b, w_ref[half:2 * half, :], preferred_element_type=F32))
        h1 = up(w1b)
        act = (h1 * _sigmoid(h1) * up(w3b)).astype(BF16)
        o = jnp.dot(act, w2b[...], preferred_element_type=F32)
        o_ref[...] = _pack_bf16_pair(o[:, 0:half], o[:, half:2 * half])

    @pl.when(jnp.logical_not(used))
    def _():
        o_ref[...] = jnp.zeros_like(o_ref)

    @pl.when(used & (last_ref[i] == 1) & (nxt >= 0))
    def _():
        for cp in copies(nxt):
            cp.wait()
        convert()


def _experts(tile_expert, first, last, nxt, n_used, xs, w1, w3, w2, *, tme):
    rows, half = xs.shape
    n_e, d_model, d_exp = w1.shape
    grid_spec = pltpu.PrefetchScalarGridSpec(
        num_scalar_prefetch=5,
        grid=(rows // tme,),
        in_specs=[pl.BlockSpec((tme, half), lambda i, te, fi, la, nx, nu: (jnp.minimum(i, nu[0] - 1), 0)),
                  pl.BlockSpec(memory_space=pl.ANY), pl.BlockSpec(memory_space=pl.ANY),
                  pl.BlockSpec(memory_space=pl.ANY)],
        out_specs=pl.BlockSpec((tme, half), lambda i, *_: (i, 0)),
        scratch_shapes=[pltpu.VMEM((d_model, d_exp), BF16), pltpu.VMEM((d_model, d_exp), BF16),
                        pltpu.VMEM((d_exp, d_model), BF16),
                        pltpu.VMEM((d_model, d_exp), F32), pltpu.VMEM((d_model, d_exp), F32),
                        pltpu.VMEM((d_exp, d_model), F32),
                        pltpu.SemaphoreType.DMA((3,))],
    )
    return pl.pallas_call(
        functools.partial(_experts_body, half=half, n_conv=8),
        grid_spec=grid_spec,
        out_shape=jax.ShapeDtypeStruct((rows, half), I32),
        compiler_params=_cparams(1),
        name="experts",
    )(tile_expert, first, last, nxt, n_used, xs, w1, w3, w2)


def _sc_gather(table, idx, *, chunk):
    n_rows, width = idx.shape[0], table.shape[1]
    n_workers = SC_CORES * SC_SUBCORES
    per_w = n_rows // n_workers
    n_chunks = per_w // chunk
    assert per_w * n_workers == n_rows and n_chunks * chunk == per_w and n_chunks % 2 == 0
    mesh = plsc.VectorSubcoreMesh(core_axis_name="c", subcore_axis_name="s",
                                  num_cores=SC_CORES, num_subcores=SC_SUBCORES)

    @functools.partial(
        pl.kernel, mesh=mesh,
        out_type=jax.ShapeDtypeStruct((n_rows, width), table.dtype),
        scratch_types=[pltpu.VMEM((per_w,), I32), pltpu.VMEM((2, chunk, width), table.dtype),
                       pltpu.SemaphoreType.DMA((2,))],
    )
    def gather_kernel(table_hbm, idx_hbm, out_hbm, idx_v, rows_v, sem):
        wid = lax.axis_index("s") * SC_CORES + lax.axis_index("c")
        base = pl.multiple_of(wid * per_w, per_w)
        pltpu.sync_copy(idx_hbm.at[pl.ds(base, per_w)], idx_v)

        def gather(j, slot):
            off = pl.multiple_of(j * chunk, chunk)
            return pltpu.make_async_copy(table_hbm.at[idx_v.at[pl.ds(off, chunk)]],
                                         rows_v.at[slot], sem.at[slot])

        gather(0, 0).start()

        @pl.loop(0, n_chunks, step=2)
        def _(j):
            for slot in range(2):
                jj = j + slot
                gather(jj, slot).wait()

                @pl.when(jj + 1 < n_chunks)
                def _():
                    gather(jj + 1, 1 - slot).start()
                off = pl.multiple_of(base + jj * chunk, chunk)
                pltpu.sync_copy(rows_v.at[slot], out_hbm.at[pl.ds(off, chunk)])

    return gather_kernel(table, idx)


def _sc_inverse(dest_flat, n_rows):
    n_assign = dest_flat.shape[0]
    n_tok = n_assign // 2
    n_workers = SC_CORES * SC_SUBCORES
    per_w = n_rows // n_workers
    assert per_w * n_workers == n_rows and per_w % SC_LANES == 0 and n_assign % SC_LANES == 0
    mesh = plsc.VectorSubcoreMesh(core_axis_name="c", subcore_axis_name="s",
                                  num_cores=SC_CORES, num_subcores=SC_SUBCORES)

    @functools.partial(
        pl.kernel, mesh=mesh,
        out_type=jax.ShapeDtypeStruct((n_rows,), I32),
        scratch_types=[pltpu.VMEM((n_assign,), I32), pltpu.VMEM((per_w,), I32)],
        compiler_params=pltpu.CompilerParams(needs_layout_passes=False),
    )
    def inverse_kernel(dest_hbm, src_hbm, dest_v, src_v):
        wid = lax.axis_index("s") * SC_CORES + lax.axis_index("c")
        lo = pl.multiple_of(wid * per_w, per_w)
        pltpu.sync_copy(dest_hbm, dest_v)
        lane = lax.iota(I32, SC_LANES)

        @pl.loop(0, per_w, step=SC_LANES)
        def _(i):
            src_v[pl.ds(pl.multiple_of(i, SC_LANES), SC_LANES)] = lax.rem(lo + i + lane, n_tok)

        @pl.loop(0, n_assign, step=SC_LANES)
        def _(i):
            d = dest_v[pl.ds(pl.multiple_of(i, SC_LANES), SC_LANES)]
            a = i + lane
            tok = jnp.where(a >= n_tok, a - n_tok, a)
            mine = (d >= lo) & (d < lo + per_w)
            plsc.store_scatter(src_v, [d - lo], tok, mask=mine)

        pltpu.sync_copy(src_v, src_hbm.at[pl.ds(lo, per_w)])

    return inverse_kernel(dest_flat)


def _final_body(g1_ref, g2_ref, h1_ref, route_ref, g_ref, y_ref, *, half):
    route = route_ref[...]
    w1, w2 = route[:, 2:3], route[:, 3:4]
    a1, b1 = _unpack_bf16_pair(g1_ref[...])
    a2, b2 = _unpack_bf16_pair(g2_ref[...])
    h1 = h1_ref[...]
    ha = h1[:, 0:half] + (a1 * w1 + a2 * w2)
    hb = h1[:, half:2 * half] + (b1 * w1 + b2 * w2)
    ms = (jnp.sum(ha * ha, axis=-1, keepdims=True)
          + jnp.sum(hb * hb, axis=-1, keepdims=True)) / (2 * half)
    inv = lax.rsqrt(ms + EPS)
    y_ref[:, 0:half] = ha * inv * g_ref[:, 0:half]
    y_ref[:, half:2 * half] = hb * inv * g_ref[:, half:2 * half]


def _final(gathered, h1, route, g, *, tf, row0, n_rows):
    t, d_model = h1.shape
    half = d_model // 2
    tile0, tiles_t = row0 // tf, t // tf
    return pl.pallas_call(
        functools.partial(_final_body, half=half),
        grid=(n_rows // tf,),
        in_specs=[pl.BlockSpec((tf, half), lambda i: (tile0 + i, 0)),
                  pl.BlockSpec((tf, half), lambda i: (tiles_t + tile0 + i, 0)),
                  pl.BlockSpec((tf, d_model), lambda i: (tile0 + i, 0)),
                  pl.BlockSpec((tf, LANES), lambda i: (tile0 + i, 0)),
                  pl.BlockSpec((1, d_model), lambda i: (0, 0))],
        out_specs=pl.BlockSpec((tf, d_model), lambda i: (i, 0)),
        out_shape=jax.ShapeDtypeStruct((n_rows, d_model), F32),
        compiler_params=_cparams(1),
        name="final",
    )(gathered, gathered, h1, route, g)


def kernel(x_prompt, x_sample, meta, norm_mix_g, w_in, ssm_lam_re, ssm_lam_im, ssm_log_step, ssm_b_re, ssm_b_im, ssm_c_re, ssm_c_im, ssm_d, ssm_w_glu, conv_w_dw, conv_b_dw, conv_ln_g, conv_ln_b, conv_w_pw, w_out, norm_ffn_g, router_group_w, router_group_b, router_expert_w, router_expert_b, expert_w1, expert_w3, expert_w2, final_g):
    assert w_in.shape[0] == 1, "single-layer trunk"
    bp, lp, d_model = x_prompt.shape
    bs, ls, _ = x_sample.shape
    n_meta = meta.shape[0]
    d_ssm = ssm_d.shape[-1]
    d_conv = conv_b_dw.shape[-1]
    n_ch = ssm_b_re.shape[-1]
    n_g = ssm_b_re.shape[2]
    n_grp = router_group_w.shape[-1]
    n_exp = router_expert_w.shape[-1]
    assert n_meta == CHUNK and lp % CHUNK == 0 and ls % CHUNK == 0
    n_p, n_s = bp * lp, bs * ls
    t = n_p + n_s
    tm = min(256, lp, ls)
    assert lp % tm == 0 and ls % tm == 0

    xp2 = x_prompt.reshape(n_p, d_model)
    xs2 = x_sample.reshape(n_s, d_model)
    row = lambda a: a.reshape(1, -1)
    w_in_bf = w_in[0].astype(BF16)

    xg, c, sga, sgb = _inproj(xp2, xs2, row(norm_mix_g[0]), w_in_bf, d_ssm=d_ssm, d_conv=d_conv, tm=tm)
    xg_meta, c_meta = _meta_inproj(meta, row(norm_mix_g[0]), w_in_bf, d_ssm=d_ssm, d_conv=d_conv)

    assert n_ch == CHUNK and LANES % n_ch == 0 and n_g % (LANES // n_ch) == 0
    intra, bst, cst, coef = _sg_prep(ssm_lam_re[0], ssm_lam_im[0], ssm_log_step[0], ssm_b_re[0],
                                     ssm_b_im[0], ssm_c_re[0], ssm_c_im[0], ssm_d[0])
    nc = t // CHUNK
    n_blk = d_ssm // LANES
    rt = min(512, nc)
    yi = _sg_mm_intra(xg, intra, rt=rt)
    sf, sb, init = _sg_mm_state(xg, xg_meta, bst, rt=rt)
    n_sub = sf.shape[0] // (n_blk * nc)
    cb = min(64, lp // CHUNK, ls // CHUNK)
    xf, xb = _sg_scan(sf.reshape(n_blk, nc, n_sub, LANES), sb.reshape(n_blk, nc, n_sub, LANES),
                      coef, init, cb=cb, chunks_pseq=lp // CHUNK, chunks_sseq=ls // CHUNK,
                      n_pchunks=n_p // CHUNK)
    z = _sg_mm_out(yi, xf.reshape(sf.shape), xb.reshape(sb.shape), cst, rt=rt)

    assert n_grp + n_exp <= LANES // 2
    rw32 = jnp.pad(jnp.concatenate([router_group_w[0], router_expert_w[0]], axis=1),
                   ((0, 0), (0, LANES // 2 - n_grp - n_exp)))
    rw_hi = rw32.astype(BF16)
    rw = jnp.concatenate([rw_hi, (rw32 - rw_hi.astype(F32)).astype(BF16)], axis=1)
    rb = jnp.zeros((1, LANES), F32).at[0, 0:n_grp].set(router_group_b[0]).at[
        0, n_grp:n_grp + n_exp].set(router_expert_b[0])
    conv_w = conv_w_dw.shape[1]
    off = CHUNK - conv_w // 2
    assert 0 <= off and conv_w + off <= 2 * CHUNK
    nq = -(-(conv_w + off) // 8)
    taps = jnp.pad(conv_w_dw[0], ((off, 8 * nq - conv_w - off), (0, 0)))
    taps = taps.reshape(nq, 8, d_conv // LANES, LANES).transpose(2, 0, 1, 3)
    wconv = (taps[:, :, None, :, :] * jnp.eye(LANES, dtype=F32)[None, None, :, None, :]).reshape(
        d_conv // LANES, nq * LANES, 8 * LANES).astype(BF16)
    h1, v, route, cnt = _mix(
        xp2, xs2, z, c, c_meta, sga, sgb, ssm_w_glu[0].astype(BF16), conv_w_pw[0].astype(BF16),
        w_out[0].astype(BF16), wconv, row(conv_b_dw[0]), row(conv_ln_g[0]),
        row(conv_ln_b[0]), row(norm_ffn_g[0]), rw, rb,
        tm=tm, p_seq=lp, s_seq=ls, n_grp=n_grp, n_exp=n_exp)

    tme = 256
    counts = cnt[0, 0:n_exp].astype(I32)
    tiles_e = (counts + tme - 1) // tme
    tile_end = jnp.cumsum(tiles_e)
    n_used = tile_end[-1]
    offs = jnp.zeros((1, LANES), F32).at[0, 0:n_exp].set(((tile_end - tiles_e) * tme).astype(F32))
    n_tiles = (2 * t) // tme + n_exp
    ids = jnp.arange(n_tiles, dtype=I32)

    def expert_of(tile):
        return jnp.minimum(jnp.sum((tile[:, None] >= tile_end[None, :]).astype(I32), axis=1), n_exp - 1)
    te_map = expert_of(jnp.minimum(ids, n_used - 1))
    onehot = te_map[:, None] == jnp.arange(n_exp, dtype=I32)[None, :]
    run_end = jnp.sum(jnp.where(onehot, tile_end[None, :], 0), axis=1)
    run_start = run_end - jnp.sum(jnp.where(onehot, tiles_e[None, :], 0), axis=1)
    valid = ids < n_used
    first = (valid & (ids == run_start)).astype(I32)
    last = (valid & (ids == run_end - 1)).astype(I32)
    nxt = jnp.where(valid & (run_end < n_used), expert_of(run_end), -1)

    dest = _positions(route, offs, te=min(512, t))
    idx = jnp.concatenate([dest[:, 0], dest[:, 1]])
    src = _sc_inverse(idx, n_tiles * tme)
    xs = _sc_gather(v, src, chunk=32)
    o = _experts(te_map, first, last, nxt, n_used.reshape(1), xs, expert_w1[0], expert_w3[0],
                 expert_w2[0], tme=tme)

    tf = min(256, n_p, n_s)
    gathered = _sc_gather(o, idx, chunk=32)
    fg = row(final_g)
    y_p = _final(gathered, h1, route, fg, tf=tf, row0=0, n_rows=n_p)
    y_s = _final(gathered, h1, route, fg, tf=tf, row0=n_p, n_rows=n_s)
    return (y_p.reshape(bp, lp, d_model), y_s.reshape(bs, ls, d_model))
```

```python
import functools

import jax
import jax.numpy as jnp
from jax import lax
from jax.experimental import pallas as pl
from jax.experimental.pallas import tpu as pltpu
from jax.experimental.pallas import tpu_sc as plsc

F32 = jnp.float32
BF16 = jnp.bfloat16
U32 = jnp.uint32
I32 = jnp.int32

EPS = 1e-6
LAM_RE_MAX = -1e-4
CHUNK = 16
LANES = 128
VMEM_LIMIT = 56 << 20
SC_CORES = 2
SC_SUBCORES = 16
SC_LANES = 16
HIGHEST = lax.Precision.HIGHEST


def _cparams(n_axes):
    return pltpu.CompilerParams(dimension_semantics=("arbitrary",) * n_axes,
                                vmem_limit_bytes=VMEM_LIMIT)


def _resident(shape):
    nd = len(shape)
    return pl.BlockSpec(shape, lambda *_: (0,) * nd, pipeline_mode=pl.Buffered(1))


def _sigmoid(x):
    return 1.0 / (1.0 + jnp.exp(-x))


def _rms(x, g):
    return x * lax.rsqrt(jnp.mean(x * x, axis=-1, keepdims=True) + EPS) * g


def _pack_bf16_pair(a, b):
    def rnd(x):
        u = pltpu.bitcast(x, U32)
        return (u + jnp.uint32(0x7FFF) + ((u >> 16) & jnp.uint32(1))) >> 16
    return pltpu.bitcast((rnd(a) << 16) | rnd(b), I32)


def _unpack_bf16_pair(p):
    p = pltpu.bitcast(p, U32)
    a = pltpu.bitcast(p & jnp.uint32(0xFFFF0000), F32)
    b = pltpu.bitcast(p << 16, F32)
    return a, b


def _inproj_body(xp_ref, xs_ref, g_ref, w_ref, xg_ref, c_ref, sga_ref, sgb_ref, usc, *,
                 tm, n_p_tiles, d_ssm, d_conv, d_model, col):
    i = pl.program_id(0)
    x = jnp.where(i < n_p_tiles, xp_ref[...], xs_ref[...])
    y = _rms(x, g_ref[...]).astype(BF16)

    def proj(lo, n):
        return jnp.dot(y, w_ref[:, lo:lo + n], preferred_element_type=F32)

    for k in range(d_ssm // col):
        u = proj(k * col, col)
        for b in range(col // LANES):
            usc[k * (col // LANES) + b] = u[:, b * LANES:(b + 1) * LANES]
    for b in range(d_ssm // LANES):
        for s in range(CHUNK):
            c0 = (b * CHUNK + s) * LANES
            xg_ref[:, c0:c0 + LANES] = usc[b, pl.ds(s, tm // CHUNK, stride=CHUNK), :].astype(BF16)
    for k in range(d_conv // col):
        cv = proj(d_ssm + k * col, col)
        cg = proj(d_ssm + d_conv + k * col, col)
        c_ref[:, k * col:(k + 1) * col] = (cv * _sigmoid(cg)).astype(BF16)
    base = d_ssm + 2 * d_conv
    for k in range(d_model // col):
        sga_ref[:, k * col:(k + 1) * col] = _sigmoid(proj(base + k * col, col)).astype(BF16)
        sgb_ref[:, k * col:(k + 1) * col] = _sigmoid(
            proj(base + d_model + k * col, col)).astype(BF16)


def _inproj(xp2, xs2, g, w_bf, *, d_ssm, d_conv, tm):
    n_p, d_model = xp2.shape
    n_s = xs2.shape[0]
    n_p_tiles, n_s_tiles = n_p // tm, n_s // tm
    t = n_p + n_s
    col = min(1024, d_ssm, d_conv, d_model)
    body = functools.partial(_inproj_body, tm=tm, n_p_tiles=n_p_tiles, d_ssm=d_ssm, d_conv=d_conv,
                             d_model=d_model, col=col)
    row = lambda i: (i, 0)
    return pl.pallas_call(
        body,
        grid=(n_p_tiles + n_s_tiles,),
        in_specs=[
            pl.BlockSpec((tm, d_model), lambda i: (jnp.minimum(i, n_p_tiles - 1), 0)),
            pl.BlockSpec((tm, d_model), lambda i: (jnp.maximum(i - n_p_tiles, 0), 0)),
            _resident((1, d_model)),
            _resident(w_bf.shape),
        ],
        out_specs=[pl.BlockSpec((tm // CHUNK, CHUNK * d_ssm), row), pl.BlockSpec((tm, d_conv), row),
                   pl.BlockSpec((tm, d_model), row), pl.BlockSpec((tm, d_model), row)],
        out_shape=[jax.ShapeDtypeStruct((t // CHUNK, CHUNK * d_ssm), BF16),
                   jax.ShapeDtypeStruct((t, d_conv), BF16),
                   jax.ShapeDtypeStruct((t, d_model), BF16), jax.ShapeDtypeStruct((t, d_model), BF16)],
        scratch_shapes=[pltpu.VMEM((d_ssm // LANES, tm, LANES), F32)],
        compiler_params=_cparams(1),
        name="inproj",
    )(xp2, xs2, g, w_bf)


def _meta_body(m_ref, g_ref, w_ref, xg_ref, c_ref, *, d_ssm, d_conv):
    y = _rms(m_ref[...], g_ref[...]).astype(BF16)
    u = jnp.dot(y, w_ref[:, 0:d_ssm], preferred_element_type=F32)
    first = lax.broadcasted_iota(I32, (CHUNK, LANES), 0) == 0
    for b in range(d_ssm // LANES):
        for s in range(CHUNK):
            c0 = (b * CHUNK + s) * LANES
            piece = jnp.broadcast_to(u[s:s + 1, b * LANES:(b + 1) * LANES], (CHUNK, LANES))
            xg_ref[:, c0:c0 + LANES] = jnp.where(first, piece, 0.0).astype(BF16)
    cv = jnp.dot(y, w_ref[:, d_ssm:d_ssm + d_conv], preferred_element_type=F32)
    cg = jnp.dot(y, w_ref[:, d_ssm + d_conv:d_ssm + 2 * d_conv], preferred_element_type=F32)
    c_ref[...] = (cv * _sigmoid(cg)).astype(BF16)


def _meta_inproj(meta, g, w_bf, *, d_ssm, d_conv):
    n_meta, d_model = meta.shape
    ncol = d_ssm + 2 * d_conv
    return pl.pallas_call(
        functools.partial(_meta_body, d_ssm=d_ssm, d_conv=d_conv),
        grid=(1,),
        in_specs=[pl.BlockSpec((n_meta, d_model), lambda i: (0, 0)),
                  pl.BlockSpec((1, d_model), lambda i: (0, 0)),
                  pl.BlockSpec((d_model, ncol), lambda i: (0, 0))],
        out_specs=[pl.BlockSpec((CHUNK, CHUNK * d_ssm), lambda i: (0, 0)),
                   pl.BlockSpec((n_meta, d_conv), lambda i: (0, 0))],
        out_shape=[jax.ShapeDtypeStruct((CHUNK, CHUNK * d_ssm), BF16),
                   jax.ShapeDtypeStruct((n_meta, d_conv), BF16)],
        compiler_params=_cparams(1),
        name="meta_inproj",
    )(meta, g, w_bf)


def _cmul(ar, ai, br, bi):
    return ar * br - ai * bi, ar * bi + ai * br


def _discretize(lam_re, lam_im, log_step):
    lr = jnp.minimum(lam_re, LAM_RE_MAX)
    dt = jnp.exp(log_step)
    mag = jnp.exp(lr * dt)
    ar = mag * jnp.cos(lam_im * dt)
    ai = mag * jnp.sin(lam_im * dt)
    den = lr * lr + lam_im * lam_im
    nr = ar - 1.0
    fr = (nr * lr + ai * lam_im) / den
    fi = (ai * lr - nr * lam_im) / den
    return ar, ai, fr, fi


def _cpow(ar, ai, k, nbits, shape):
    pr = jnp.ones(shape, F32)
    pi = jnp.zeros(shape, F32)
    br = jnp.broadcast_to(ar, shape)
    bi = jnp.broadcast_to(ai, shape)
    kk = jnp.broadcast_to(k, shape)
    for b in range(nbits):
        sel = ((kk >> b) & 1) == 1
        nr, ni = _cmul(pr, pi, br, bi)
        pr = jnp.where(sel, nr, pr)
        pi = jnp.where(sel, ni, pi)
        br, bi = _cmul(br, bi, br, bi)
    return pr, pi


def _spread(x, n_ch, shift):
    rows, w = x.shape
    lane = lax.broadcasted_iota(I32, (rows, LANES), 1)
    keep = (lane >= shift) & (lane < shift + n_ch)
    out = []
    for b in range(w // n_ch):
        src = (b * n_ch // LANES) * LANES
        amount = lax.rem(shift + LANES - (b * n_ch) % LANES, LANES)
        out.append(jnp.where(keep, pltpu.roll(x[:, src:src + LANES], amount, 1), 0.0))
    return jnp.concatenate(out, axis=1)


def _tile_lanes(x, n_ch, width):
    span = n_ch
    while span < LANES:
        x = x + pltpu.roll(x, span, 1)
        span *= 2
    return jnp.concatenate([x] * (width // LANES), axis=1)


def _disc_body(lam_re_ref, lam_im_ref, ls_ref, ar_ref, ai_ref, fr_ref, fi_ref):
    ar, ai, fr, fi = _discretize(lam_re_ref[...], lam_im_ref[...], ls_ref[...])
    ar_ref[...] = ar
    ai_ref[...] = ai
    fr_ref[...] = fr
    fi_ref[...] = fi


def _ssm_discretize(lam_re, lam_im, log_step):
    shape = lam_re.shape
    rows = shape[0] * shape[1]
    flat = lambda x: x.reshape(rows, shape[2])
    ls = jnp.broadcast_to(log_step[..., None], shape)
    full = pl.BlockSpec((rows, shape[2]), lambda i: (0, 0))
    out = pl.pallas_call(
        _disc_body,
        grid=(1,),
        in_specs=[full, full, full],
        out_specs=[full] * 4,
        out_shape=[jax.ShapeDtypeStruct((rows, shape[2]), F32)] * 4,
        compiler_params=_cparams(1),
        name="ssm_disc",
    )(flat(lam_re), flat(lam_im), flat(ls))
    return [x.reshape(shape) for x in out]


def _sg_prep_body(ar_c, ai_c, fr_c, fi_c, ct_re, ct_im, bt_re, bt_im,
                  ar_r, ai_r, fr_r, fi_r, btr_re, btr_im, dpad,
                  m_ref, b_ref, c_ref, aq_ref, *, n_state, n_ch, per_step):
    per_blk = LANES // n_ch
    first = (pl.program_id(0) * per_step) % per_blk
    for j in range(per_step):
        _sg_prep_group(j, first + j, ar_c, ai_c, fr_c, fi_c, ct_re, ct_im, bt_re, bt_im,
                       ar_r, ai_r, fr_r, fi_r, btr_re, btr_im, dpad,
                       m_ref, b_ref, c_ref, aq_ref, n_state=n_state, n_ch=n_ch)


def _sg_prep_group(j, gl, ar_c, ai_c, fr_c, fi_c, ct_re, ct_im, bt_re, bt_im,
                   ar_r, ai_r, fr_r, fi_r, btr_re, btr_im, dpad,
                   m_ref, b_ref, c_ref, aq_ref, *, n_state, n_ch):
    q = CHUNK
    width = q * n_ch
    per_blk = LANES // n_ch
    rows = slice(j * n_ch, (j + 1) * n_ch)
    shift = gl * n_ch
    kblk = lax.broadcasted_iota(I32, (1, width), 1) // n_ch
    strips = []
    for d in range(2):
        ar, ai, fr, fi = ar_c[d, j], ai_c[d, j], fr_c[d, j], fi_c[d, j]
        kexp = kblk if d == 0 else (q - 1) - kblk
        wr, wi = _cpow(ar, ai, kexp, 4, (n_state, width))
        gcr, gci = _cmul(_tile_lanes(ct_re[d, j], n_ch, width),
                         _tile_lanes(ct_im[d, j], n_ch, width), wr, wi)
        gfr, gfi = _cmul(gcr, gci, fr, fi)
        strips.append(jnp.dot(bt_re[d, j], gfr, precision=HIGHEST, preferred_element_type=F32)
                      - jnp.dot(bt_im[d, j], gfi, precision=HIGHEST, preferred_element_type=F32))
        g1r, g1i = _cmul(gcr, gci, ar, ai)
        for comp, val in ((0, g1r), (1, -g1i)):
            c_ref[2 * d + comp, j] = _spread(val, n_ch, shift).astype(BF16)

    zf, zb = strips
    zero = jnp.zeros((n_ch, width), F32)
    z512 = jnp.concatenate([zb, zero], axis=1) + pltpu.roll(
        jnp.concatenate([zf, zero], axis=1), (q - 1) * n_ch, 1)
    row = lax.broadcasted_iota(I32, (n_ch, 2 * width), 0)
    lane = lax.broadcasted_iota(I32, (n_ch, 2 * width), 1)
    z512 = z512 + jnp.where(lane - (q - 1) * n_ch == row, dpad[j], 0.0)
    zwide = _spread(z512, n_ch, shift)
    for s in range(q):
        lo = (q - 1 - s) * LANES
        m_ref[s, rows, :] = zwide[:, lo:lo + q * LANES].astype(BF16)

    parmask = (lax.broadcasted_iota(I32, (1, 2 * n_state), 1) // n_state) == gl % 2
    pieces = [[None] * 4 for _ in range(q)]
    for d in range(2):
        ar, ai, fr, fi = ar_r[d, j], ai_r[d, j], fr_r[d, j], fi_r[d, j]
        pw = [(jnp.ones_like(ar), jnp.zeros_like(ar))]
        for _ in range(q):
            pw.append(_cmul(pw[-1][0], pw[-1][1], ar, ai))
        for s in range(q):
            e = (q - 1 - s) if d == 0 else s
            cr, ci = _cmul(fr, fi, pw[e][0], pw[e][1])
            br, bi = _cmul(btr_re[d, j], btr_im[d, j], cr, ci)
            pieces[s][2 * d] = jnp.where(parmask, br, 0.0)
            pieces[s][2 * d + 1] = jnp.where(parmask, bi, 0.0)
        aq_ref[j, :, 2 * d * n_state:(2 * d + 1) * n_state] = pw[q][0][:, 0:n_state]
        aq_ref[j, :, (2 * d + 1) * n_state:(2 * d + 2) * n_state] = pw[q][1][:, 0:n_state]
    n_pair = per_blk // 2
    for s in range(q):
        cols = []
        for seg in range(4):
            for blk in range(n_pair):
                cols.append(jnp.where(gl // 2 == blk, pieces[s][seg], 0.0))
        b_ref[s, rows, :] = jnp.concatenate(cols, axis=1).astype(BF16)


def _sg_prep(lam_re, lam_im, log_step, b_re, b_im, c_re, c_im, d_skip):
    _, n_g, n_state, n_ch = b_re.shape
    q = CHUNK
    width = q * n_ch
    per_blk = LANES // n_ch
    n_blk = n_g // per_blk
    sw = 4 * per_blk * n_state
    dup = lambda x: jnp.concatenate([x, x], axis=-1)
    lane_pad = lambda x: jnp.pad(x, ((0, 0),) * 3 + ((0, LANES - n_ch),))
    disc = _ssm_discretize(lam_re, lam_im, log_step)
    disc_c = [x[..., None] for x in disc]
    disc_r = [dup(x)[:, :, None, :] for x in disc]
    ct_re = lane_pad(jnp.swapaxes(c_re, -1, -2))
    ct_im = lane_pad(jnp.swapaxes(c_im, -1, -2))
    bt_re, bt_im = jnp.swapaxes(b_re, -1, -2), jnp.swapaxes(b_im, -1, -2)
    btr_re, btr_im = dup(bt_re), dup(bt_im)
    dpad = jnp.pad(d_skip.reshape(n_g, 1, n_ch), ((0, 0), (0, 0), ((q - 1) * n_ch, width)))

    per_step = 2
    spb = per_blk // per_step

    def dspec(shape):
        return pl.BlockSpec((2, per_step) + shape, lambda g: (0, g, 0, 0))

    body = functools.partial(_sg_prep_body, n_state=n_state, n_ch=n_ch, per_step=per_step)
    m4, b4, c5, aq = pl.pallas_call(
        body,
        grid=(n_g // per_step,),
        in_specs=[dspec((n_state, 1))] * 4 + [
                  dspec((n_state, LANES)), dspec((n_state, LANES)),
                  dspec((n_ch, n_state)), dspec((n_ch, n_state))] + [dspec((1, 2 * n_state))] * 4 + [
                  dspec((n_ch, 2 * n_state)), dspec((n_ch, 2 * n_state)),
                  pl.BlockSpec((per_step, 1, 2 * width), lambda g: (g, 0, 0))],
        out_specs=[
            pl.BlockSpec((None, q, per_step * n_ch, q * LANES), lambda g: (g // spb, 0, g % spb, 0)),
            pl.BlockSpec((None, q, per_step * n_ch, sw), lambda g: (g // spb, 0, g % spb, 0)),
            pl.BlockSpec((None, 4, per_step, n_state, q * LANES),
                         lambda g: (g // spb, 0, g % spb, 0, 0)),
            pl.BlockSpec((per_step, 1, 4 * n_state), lambda g: (g, 0, 0))],
        out_shape=[jax.ShapeDtypeStruct((n_blk, q, LANES, q * LANES), BF16),
                   jax.ShapeDtypeStruct((n_blk, q, LANES, sw), BF16),
                   jax.ShapeDtypeStruct((n_blk, 4, per_blk, n_state, q * LANES), BF16),
                   jax.ShapeDtypeStruct((n_g, 1, 4 * n_state), F32)],
        compiler_params=_cparams(1),
        name="ssm_prep",
    )(*disc_c, ct_re, ct_im, bt_re, bt_im, *disc_r, btr_re, btr_im, dpad)
    intra = m4.reshape(n_blk, q * LANES, q * LANES)
    bst = b4.reshape(n_blk, q * LANES, sw)
    cst = c5.reshape(n_blk, sw, q * LANES)
    a = aq.reshape(n_blk, per_blk // 2, 2, 2, 2, n_state)
    a = a.transpose(3, 4, 0, 1, 2, 5).reshape(2, 2, n_blk, per_blk // 2, LANES)
    coef = []
    for d in range(2):
        re, im = a[d, 0], a[d, 1]
        coef.append((jnp.concatenate([re, re], axis=1), jnp.concatenate([-im, im], axis=1)))
    return intra, bst, cst, coef


def _sg_mm_intra_body(x_ref, m_ref, y_ref):
    y_ref[...] = jnp.dot(x_ref[...], m_ref[...], preferred_element_type=F32).astype(BF16)


def _sg_mm_intra(xg, intra, *, rt):
    nc = xg.shape[0]
    n_blk, kw, _ = intra.shape
    return pl.pallas_call(
        _sg_mm_intra_body,
        grid=(n_blk, nc // rt),
        in_specs=[pl.BlockSpec((rt, kw), lambda b, i: (i, b)),
                  pl.BlockSpec((None, kw, kw), lambda b, i: (b, 0, 0))],
        out_specs=pl.BlockSpec((rt, kw), lambda b, i: (i, b)),
        out_shape=jax.ShapeDtypeStruct((nc, n_blk * kw), BF16),
        compiler_params=_cparams(2),
        name="ssm_intra",
    )(xg, intra)


def _sg_mm_state_body(x_ref, xm_ref, b_ref, sf_ref, sb_ref, init_ref, *, rt, n_sub):
    r = jnp.dot(x_ref[...], b_ref[...], preferred_element_type=F32)
    for k in range(n_sub):
        sf_ref[pl.ds(k, rt, stride=n_sub), :] = r[:, k * LANES:(k + 1) * LANES]
        sb_ref[pl.ds(k, rt, stride=n_sub), :] = r[:, (n_sub + k) * LANES:(n_sub + k + 1) * LANES]

    @pl.when(pl.program_id(1) == 0)
    def _():
        r0 = jnp.dot(xm_ref[...], b_ref[...], preferred_element_type=F32)
        for k in range(n_sub):
            init_ref[k:k + 1, :] = r0[0:1, k * LANES:(k + 1) * LANES]


def _sg_mm_state(xg, xg_meta, bst, *, rt):
    nc = xg.shape[0]
    n_blk, kw, sw = bst.shape
    n_sub = sw // (2 * LANES)
    n_rt = nc // rt
    body = functools.partial(_sg_mm_state_body, rt=rt, n_sub=n_sub)
    dense = jax.ShapeDtypeStruct((n_blk * nc * n_sub, LANES), F32)
    dspec = pl.BlockSpec((rt * n_sub, LANES), lambda b, i: (b * n_rt + i, 0))
    return pl.pallas_call(
        body,
        grid=(n_blk, n_rt),
        in_specs=[pl.BlockSpec((rt, kw), lambda b, i: (i, b)),
                  pl.BlockSpec((CHUNK, kw), lambda b, i: (0, b)),
                  pl.BlockSpec((None, kw, sw), lambda b, i: (b, 0, 0))],
        out_specs=[dspec, dspec, pl.BlockSpec((None, n_sub, LANES), lambda b, i: (b, 0, 0))],
        out_shape=[dense, dense, jax.ShapeDtypeStruct((n_blk, n_sub, LANES), F32)],
        compiler_params=_cparams(2),
        name="ssm_state",
    )(xg, xg_meta, bst)


def _sg_scan_body(sf_ref, sb_ref, a1f_ref, a2f_ref, a1b_ref, a2b_ref, init_ref, xf_ref, xb_ref,
                  stf, stb, *, cb, n_blk, half, n_pblk, blk_pseq, blk_sseq):
    j = pl.program_id(0)
    pos = jnp.where(j < n_pblk, lax.rem(j, blk_pseq), lax.rem(jnp.maximum(j - n_pblk, 0), blk_sseq))

    @pl.when(pos == 0)
    def _():
        stf[...] = init_ref[...]
        stb[...] = jnp.zeros_like(stb)

    def body(i, carry):
        ef, eb = carry
        ib = cb - 1 - i
        nf, nb = [], []
        for g in range(n_blk):
            xf_ref[g, i] = ef[g]
            xb_ref[g, ib] = eb[g]
            nf.append(a1f_ref[g] * ef[g] + a2f_ref[g] * pltpu.roll(ef[g], half, 0) + sf_ref[g, i])
            nb.append(a1b_ref[g] * eb[g] + a2b_ref[g] * pltpu.roll(eb[g], half, 0) + sb_ref[g, ib])
        return tuple(nf), tuple(nb)

    ef, eb = lax.fori_loop(0, cb, body, (tuple(stf[g] for g in range(n_blk)),
                                         tuple(stb[g] for g in range(n_blk))))
    for g in range(n_blk):
        stf[g] = ef[g]
        stb[g] = eb[g]


def _sg_scan(sf, sb, coef, init, *, cb, chunks_pseq, chunks_sseq, n_pchunks):
    n_blk, nc, n_sub, _ = sf.shape
    n_pblk, blk_pseq, blk_sseq = n_pchunks // cb, chunks_pseq // cb, chunks_sseq // cb

    def bwd_block(j):
        in_p = j < n_pblk
        pos = jnp.where(in_p, lax.rem(j, blk_pseq), lax.rem(jnp.maximum(j - n_pblk, 0), blk_sseq))
        ln = jnp.where(in_p, blk_pseq, blk_sseq)
        return j - pos + ln - 1 - pos

    body = functools.partial(_sg_scan_body, cb=cb, n_blk=n_blk, half=n_sub // 2, n_pblk=n_pblk,
                             blk_pseq=blk_pseq, blk_sseq=blk_sseq)
    fwd = pl.BlockSpec((n_blk, cb, n_sub, LANES), lambda j: (0, j, 0, 0))
    bwd = pl.BlockSpec((n_blk, cb, n_sub, LANES), lambda j: (0, bwd_block(j), 0, 0))
    small = pl.BlockSpec((n_blk, n_sub, LANES), lambda j: (0, 0, 0))
    return pl.pallas_call(
        body,
        grid=(nc // cb,),
        in_specs=[fwd, bwd, small, small, small, small, small],
        out_specs=[fwd, bwd],
        out_shape=[jax.ShapeDtypeStruct(sf.shape, F32), jax.ShapeDtypeStruct(sb.shape, F32)],
        scratch_shapes=[pltpu.VMEM((n_blk, n_sub, LANES), F32), pltpu.VMEM((n_blk, n_sub, LANES), F32)],
        compiler_params=_cparams(1),
        name="ssm_scan",
    )(sf, sb, coef[0][0], coef[0][1], coef[1][0], coef[1][1], init)


def _gelu_tanh(x):
    return 0.5 * x * (1.0 + jnp.tanh(0.7978845608028654 * (x + 0.044715 * (x * x * x))))


def _sg_mm_out_body(y_ref, xf_ref, xb_ref, c_ref, z_ref, *, rt, n_sub):
    cols = [xf_ref[pl.ds(k, rt, stride=n_sub), :] for k in range(n_sub)]
    cols += [xb_ref[pl.ds(k, rt, stride=n_sub), :] for k in range(n_sub)]
    state = jnp.concatenate(cols, axis=1).astype(BF16)
    y = y_ref[...].astype(F32) + jnp.dot(state, c_ref[...], preferred_element_type=F32)
    z_ref[...] = _gelu_tanh(y).astype(BF16)


def _sg_mm_out(yi, xf, xb, cst, *, rt):
    nc = yi.shape[0]
    n_blk, sw, kw = cst.shape
    n_sub = sw // (2 * LANES)
    n_rt = nc // rt
    dspec = pl.BlockSpec((rt * n_sub, LANES), lambda b, i: (b * n_rt + i, 0))
    return pl.pallas_call(
        functools.partial(_sg_mm_out_body, rt=rt, n_sub=n_sub),
        grid=(n_blk, n_rt),
        in_specs=[pl.BlockSpec((rt, kw), lambda b, i: (i, b)), dspec, dspec,
                  pl.BlockSpec((None, sw, kw), lambda b, i: (b, 0, 0))],
        out_specs=pl.BlockSpec((rt, kw), lambda b, i: (i, b)),
        out_shape=jax.ShapeDtypeStruct((nc, n_blk * kw), BF16),
        compiler_params=_cparams(2),
        name="ssm_out",
    )(yi, xf, xb, cst)


def _mix_body(xp_ref, xs_ref, z_ref, c_ref, cprev_ref, cnext_ref, cmeta_ref, sga_ref, sgb_ref,
              wglu_ref, wpw_ref, wout_ref, wconv_ref, bdw_ref, lng_ref, lnb_ref, gffn_ref,
              rw_ref, rb_ref,
              h1_ref, v_ref, route_ref, cnt_ref,
              cw_ref, conv_ref, zsc_ref, *,
              tm, n_p_tiles, tiles_per_pseq, tiles_per_sseq, d_model, d_ssm, d_conv,
              n_grp, n_exp, exp_per_grp):
    i = pl.program_id(0)
    in_prompt = i < n_p_tiles
    x = jnp.where(in_prompt, xp_ref[...], xs_ref[...])
    pos_p = lax.rem(i, tiles_per_pseq)
    pos_s = lax.rem(jnp.maximum(i - n_p_tiles, 0), tiles_per_sseq)
    is_start = jnp.where(in_prompt, pos_p == 0, pos_s == 0)
    is_end = jnp.where(in_prompt, pos_p == tiles_per_pseq - 1, pos_s == tiles_per_sseq - 1)

    halo = CHUNK
    cw_ref[0:halo, :] = jnp.where(is_start, cmeta_ref[...], cprev_ref[...]).astype(F32)
    cw_ref[halo:halo + tm, :] = c_ref[...].astype(F32)
    cw_ref[halo + tm:2 * halo + tm, :] = jnp.where(is_end, 0.0, cnext_ref[...].astype(F32))
    sub = 8
    nq = wconv_ref.shape[1] // LANES
    for lc in range(d_conv // LANES):
        ls = slice(lc * LANES, (lc + 1) * LANES)
        stacked = jnp.concatenate(
            [cw_ref[sub * q:sub * q + tm + sub, ls] for q in range(nq)], axis=1).astype(BF16)
        part = jnp.dot(stacked, wconv_ref[lc], preferred_element_type=F32)
        out = part[0:tm, 0:LANES]
        for r in range(1, sub):
            out = out + part[r:r + tm, r * LANES:(r + 1) * LANES]
        conv_ref[:, ls] = out
    cc = conv_ref[...] + bdw_ref[...]
    mu = jnp.mean(cc, axis=-1, keepdims=True)
    var = jnp.mean(jnp.square(cc - mu), axis=-1, keepdims=True)
    cc = (cc - mu) * lax.rsqrt(var + EPS) * lng_ref[...] + lnb_ref[...]
    cc = (cc * _sigmoid(cc)).astype(BF16)
    y_b = jnp.dot(cc, wpw_ref[...], preferred_element_type=F32)

    for b in range(d_ssm // LANES):
        for s in range(CHUNK):
            c0 = (b * CHUNK + s) * LANES
            zsc_ref[b, pl.ds(s, tm // CHUNK, stride=CHUNK), :] = z_ref[:, c0:c0 + LANES].astype(F32)
    z = jnp.concatenate([zsc_ref[b] for b in range(d_ssm // LANES)], axis=1).astype(BF16)
    va = jnp.dot(z, wglu_ref[:, 0:d_model], preferred_element_type=F32)
    ga = jnp.dot(z, wglu_ref[:, d_model:2 * d_model], preferred_element_type=F32)
    y_a = va * _sigmoid(ga)
    merged = (sga_ref[...].astype(F32) * y_a + sgb_ref[...].astype(F32) * y_b).astype(BF16)
    h1 = x + jnp.dot(merged, wout_ref[...], preferred_element_type=F32)
    h1_ref[...] = h1
    v = _rms(h1, gffn_ref[...])
    half = d_model // 2
    v_ref[...] = _pack_bf16_pair(v[:, 0:half], v[:, half:d_model])

    v_hi = v.astype(BF16)
    v_lo = (v - v_hi.astype(F32)).astype(BF16)
    acc = (jnp.dot(v_hi, rw_ref[...], preferred_element_type=F32)
           + jnp.dot(v_lo, rw_ref[...], preferred_element_type=F32))
    logits = acc + pltpu.roll(acc, LANES // 2, 1) + rb_ref[...]
    lane = lax.broadcasted_iota(I32, (tm, LANES), 1).astype(F32)
    big = jnp.float32(1e9)
    neg = jnp.float32(-jnp.inf)
    gmask = lane < n_grp
    lg = jnp.where(gmask, logits, neg)
    gmax = jnp.max(lg, axis=-1, keepdims=True)
    grp = jnp.min(jnp.where(lg == gmax, lane, big), axis=-1, keepdims=True)
    p_grp = 1.0 / jnp.sum(jnp.where(gmask, jnp.exp(logits - gmax), 0.0), axis=-1, keepdims=True)
    lo = n_grp + grp * exp_per_grp
    emask = (lane >= lo) & (lane < lo + exp_per_grp)
    le = jnp.where(emask, logits, neg)
    m1 = jnp.max(le, axis=-1, keepdims=True)
    i1 = jnp.min(jnp.where(le == m1, lane, big), axis=-1, keepdims=True)
    le2 = jnp.where(lane == i1, neg, le)
    m2 = jnp.max(le2, axis=-1, keepdims=True)
    i2 = jnp.min(jnp.where(le2 == m2, lane, big), axis=-1, keepdims=True)
    t = jnp.exp(m2 - m1)
    w1 = 1.0 / (1.0 + t)
    e1 = i1 - n_grp
    e2 = i2 - n_grp
    route_ref[...] = jnp.where(lane == 0, e1, jnp.where(lane == 1, e2, jnp.where(
        lane == 2, p_grp * w1, jnp.where(lane == 3, p_grp * (t * w1), 0.0))))

    @pl.when(i == 0)
    def _():
        cnt_ref[...] = jnp.zeros_like(cnt_ref)

    hot = jnp.where((lane == e1) | (lane == e2), 1.0, 0.0)
    cnt_ref[...] += jnp.sum(hot, axis=0, keepdims=True)


def _conv_weights_body(taps_ref, w_ref, *, n_taps):
    eye = (lax.broadcasted_iota(I32, (LANES, LANES), 0)
           == lax.broadcasted_iota(I32, (LANES, LANES), 1))
    for p in range(n_taps):
        q, r = divmod(p, 8)
        diag = jnp.where(eye, taps_ref[p:p + 1, :], 0.0)
        w_ref[q * LANES:(q + 1) * LANES, r * LANES:(r + 1) * LANES] = diag.astype(BF16)


def _conv_weights(taps):
    n_taps, d_conv = taps.shape
    nq = n_taps // 8
    return pl.pallas_call(
        functools.partial(_conv_weights_body, n_taps=n_taps),
        grid=(d_conv // LANES,),
        in_specs=[pl.BlockSpec((n_taps, LANES), lambda b: (0, b))],
        out_specs=pl.BlockSpec((None, nq * LANES, 8 * LANES), lambda b: (b, 0, 0)),
        out_shape=jax.ShapeDtypeStruct((d_conv // LANES, nq * LANES, 8 * LANES), BF16),
        compiler_params=_cparams(1),
        name="conv_weights",
    )(taps)


def _mix(xp2, xs2, z, c, c_meta, sga, sgb, wglu, wpw, wout, wconv, bdw, lng, lnb, gffn, rw, rb, *,
         tm, p_seq, s_seq, n_grp, n_exp):
    n_p, d_model = xp2.shape
    n_s = xs2.shape[0]
    t = n_p + n_s
    d_ssm, d_conv = z.shape[1] // CHUNK, c.shape[1]
    n_p_tiles, n_s_tiles = n_p // tm, n_s // tm
    hpt = tm // CHUNK
    n_hblk = t // CHUNK
    body = functools.partial(
        _mix_body, tm=tm, n_p_tiles=n_p_tiles, tiles_per_pseq=p_seq // tm,
        tiles_per_sseq=s_seq // tm, d_model=d_model, d_ssm=d_ssm, d_conv=d_conv,
        n_grp=n_grp, n_exp=n_exp, exp_per_grp=n_exp // n_grp)
    row = lambda i: (i, 0)
    return pl.pallas_call(
        body,
        grid=(n_p_tiles + n_s_tiles,),
        in_specs=[
            pl.BlockSpec((tm, d_model), lambda i: (jnp.minimum(i, n_p_tiles - 1), 0)),
            pl.BlockSpec((tm, d_model), lambda i: (jnp.maximum(i - n_p_tiles, 0), 0)),
            pl.BlockSpec((tm // CHUNK, CHUNK * d_ssm), row),
            pl.BlockSpec((tm, d_conv), row),
            pl.BlockSpec((CHUNK, d_conv), lambda i: (jnp.maximum(i * hpt - 1, 0), 0)),
            pl.BlockSpec((CHUNK, d_conv), lambda i: (jnp.minimum((i + 1) * hpt, n_hblk - 1), 0)),
            _resident(c_meta.shape),
            pl.BlockSpec((tm, d_model), row),
            pl.BlockSpec((tm, d_model), row),
            _resident(wglu.shape), _resident(wpw.shape), _resident(wout.shape),
            _resident(wconv.shape), _resident(bdw.shape), _resident(lng.shape), _resident(lnb.shape),
            _resident(gffn.shape), _resident(rw.shape), _resident(rb.shape),
        ],
        out_specs=[pl.BlockSpec((tm, d_model), row), pl.BlockSpec((tm, d_model // 2), row),
                   pl.BlockSpec((tm, LANES), row), pl.BlockSpec((1, LANES), lambda i: (0, 0))],
        out_shape=[jax.ShapeDtypeStruct((t, d_model), F32),
                   jax.ShapeDtypeStruct((t, d_model // 2), I32),
                   jax.ShapeDtypeStruct((t, LANES), F32),
                   jax.ShapeDtypeStruct((1, LANES), F32)],
        scratch_shapes=[pltpu.VMEM((tm + 2 * CHUNK, d_conv), F32), pltpu.VMEM((tm, d_conv), F32),
                        pltpu.VMEM((d_ssm // LANES, tm, LANES), F32)],
        compiler_params=_cparams(1),
        name="mix",
    )(xp2, xs2, z, c, c, c, c_meta, sga, sgb, wglu, wpw, wout, wconv, bdw, lng, lnb, gffn, rw, rb)


def _positions_body(route_ref, offs_ref, dest_ref, carry_ref, *, te):
    @pl.when(pl.program_id(0) == 0)
    def _():
        carry_ref[...] = jnp.zeros_like(carry_ref)

    lane = lax.broadcasted_iota(I32, (te, LANES), 1).astype(F32)
    r = route_ref[...]
    oh1 = lane == r[:, 0:1]
    oh2 = lane == r[:, 1:2]
    both = jnp.where(oh1 | oh2, 1.0, 0.0)
    tri = jnp.where(lax.broadcasted_iota(I32, (te, te), 0) > lax.broadcasted_iota(I32, (te, te), 1),
                    1.0, 0.0).astype(BF16)
    before = jnp.dot(tri, both.astype(BF16), preferred_element_type=F32)
    base = before + carry_ref[...] + offs_ref[...]
    d1 = jnp.sum(jnp.where(oh1, base, 0.0), axis=-1, keepdims=True)
    d2 = jnp.sum(jnp.where(oh2, base, 0.0), axis=-1, keepdims=True)
    dest_ref[...] = jnp.where(lane == 0, d1, jnp.where(lane == 1, d2, 0.0)).astype(I32)
    carry_ref[...] += jnp.sum(both, axis=0, keepdims=True)


def _positions(route, offs, *, te):
    t = route.shape[0]
    return pl.pallas_call(
        functools.partial(_positions_body, te=te),
        grid=(t // te,),
        in_specs=[pl.BlockSpec((te, LANES), lambda i: (i, 0)),
                  pl.BlockSpec((1, LANES), lambda i: (0, 0))],
        out_specs=pl.BlockSpec((te, LANES), lambda i: (i, 0)),
        out_shape=jax.ShapeDtypeStruct((t, LANES), I32),
        scratch_shapes=[pltpu.VMEM((1, LANES), F32)],
        compiler_params=_cparams(1),
        name="positions",
    )(route, offs)


def _experts_body(te_ref, first_ref, last_ref, nxt_ref, nused_ref,
                  xs_ref, w1_hbm, w3_hbm, w2_hbm, o_ref,
                  w1b, w3b, w2b, st1, st3, st2, sem, *, half, n_conv):
    i = pl.program_id(0)
    used = i < nused_ref[0]
    nxt = nxt_ref[i]

    def copies(e):
        return (pltpu.make_async_copy(w1_hbm.at[e], st1, sem.at[0]),
                pltpu.make_async_copy(w3_hbm.at[e], st3, sem.at[1]),
                pltpu.make_async_copy(w2_hbm.at[e], st2, sem.at[2]))

    def convert():
        r13 = st1.shape[0] // n_conv
        r2 = st2.shape[0] // n_conv

        def body(c, _):
            rows = pl.ds(pl.multiple_of(c * r13, r13), r13)
            w1b[rows, :] = st1[rows, :].astype(BF16)
            w3b[rows, :] = st3[rows, :].astype(BF16)
            rows2 = pl.ds(pl.multiple_of(c * r2, r2), r2)
            w2b[rows2, :] = st2[rows2, :].astype(BF16)
            return 0
        lax.fori_loop(0, n_conv, body, 0)

    @pl.when(i == 0)
    def _():
        for cp in copies(te_ref[0]):
            cp.start()
        for cp in copies(te_ref[0]):
            cp.wait()
        convert()

    @pl.when(used & (first_ref[i] == 1) & (nxt >= 0))
    def _():
        for cp in copies(nxt):
            cp.start(priority=1)

    @pl.when(used)
    def _():
        a, b = _unpack_bf16_pair(xs_ref[...])
        a, b = a.astype(BF16), b.astype(BF16)

        def up(w_ref):
            return (jnp.dot(a, w_ref[0:half, :], preferred_element_type=F32)
                    + jnp.dot(b, w_ref[half:2 * half, :], preferred_element_type=F32))
        h1 = up(w1b)
        act = (h1 * _sigmoid(h1) * up(w3b)).astype(BF16)
        o = jnp.dot(act, w2b[...], preferred_element_type=F32)
        o_ref[...] = _pack_bf16_pair(o[:, 0:half], o[:, half:2 * half])

    @pl.when(jnp.logical_not(used))
    def _():
        o_ref[...] = jnp.zeros_like(o_ref)

    @pl.when(used & (last_ref[i] == 1) & (nxt >= 0))
    def _():
        for cp in copies(nxt):
            cp.wait()
        convert()


def _experts(tile_expert, first, last, nxt, n_used, xs, w1, w3, w2, *, tme):
    rows, half = xs.shape
    n_e, d_model, d_exp = w1.shape
    grid_spec = pltpu.PrefetchScalarGridSpec(
        num_scalar_prefetch=5,
        grid=(rows // tme,),
        in_specs=[pl.BlockSpec((tme, half), lambda i, te, fi, la, nx, nu: (jnp.minimum(i, nu[0] - 1), 0)),
                  pl.BlockSpec(memory_space=pl.ANY), pl.BlockSpec(memory_space=pl.ANY),
                  pl.BlockSpec(memory_space=pl.ANY)],
        out_specs=pl.BlockSpec((tme, half), lambda i, *_: (i, 0)),
        scratch_shapes=[pltpu.VMEM((d_model, d_exp), BF16), pltpu.VMEM((d_model, d_exp), BF16),
                        pltpu.VMEM((d_exp, d_model), BF16),
                        pltpu.VMEM((d_model, d_exp), F32), pltpu.VMEM((d_model, d_exp), F32),
                        pltpu.VMEM((d_exp, d_model), F32),
                        pltpu.SemaphoreType.DMA((3,))],
    )
    return pl.pallas_call(
        functools.partial(_experts_body, half=half, n_conv=8),
        grid_spec=grid_spec,
        out_shape=jax.ShapeDtypeStruct((rows, half), I32),
        compiler_params=_cparams(1),
        name="experts",
    )(tile_expert, first, last, nxt, n_used, xs, w1, w3, w2)


def _sc_gather(table, idx, *, chunk):
    n_rows, width = idx.shape[0], table.shape[1]
    n_workers = SC_CORES * SC_SUBCORES
    per_w = n_rows // n_workers
    n_chunks = per_w // chunk
    assert per_w * n_workers == n_rows and n_chunks * chunk == per_w and n_chunks % 2 == 0
    mesh = plsc.VectorSubcoreMesh(core_axis_name="c", subcore_axis_name="s",
                                  num_cores=SC_CORES, num_subcores=SC_SUBCORES)

    @functools.partial(
        pl.kernel, mesh=mesh,
        out_type=jax.ShapeDtypeStruct((n_rows, width), table.dtype),
        scratch_types=[pltpu.VMEM((per_w,), I32), pltpu.VMEM((2, chunk, width), table.dtype),
                       pltpu.SemaphoreType.DMA((2,))],
    )
    def gather_kernel(table_hbm, idx_hbm, out_hbm, idx_v, rows_v, sem):
        wid = lax.axis_index("s") * SC_CORES + lax.axis_index("c")
        base = pl.multiple_of(wid * per_w, per_w)
        pltpu.sync_copy(idx_hbm.at[pl.ds(base, per_w)], idx_v)

        def gather(j, slot):
            off = pl.multiple_of(j * chunk, chunk)
            return pltpu.make_async_copy(table_hbm.at[idx_v.at[pl.ds(off, chunk)]],
                                         rows_v.at[slot], sem.at[slot])

        gather(0, 0).start()

        @pl.loop(0, n_chunks, step=2)
        def _(j):
            for slot in range(2):
                jj = j + slot
                gather(jj, slot).wait()

                @pl.when(jj + 1 < n_chunks)
                def _():
                    gather(jj + 1, 1 - slot).start()
                off = pl.multiple_of(base + jj * chunk, chunk)
                pltpu.sync_copy(rows_v.at[slot], out_hbm.at[pl.ds(off, chunk)])

    return gather_kernel(table, idx)


def _sc_inverse(dest_flat, n_rows):
    n_assign = dest_flat.shape[0]
    n_tok = n_assign // 2
    n_workers = SC_CORES * SC_SUBCORES
    per_w = n_rows // n_workers
    assert per_w * n_workers == n_rows and per_w % SC_LANES == 0 and n_assign % SC_LANES == 0
    mesh = plsc.VectorSubcoreMesh(core_axis_name="c", subcore_axis_name="s",
                                  num_cores=SC_CORES, num_subcores=SC_SUBCORES)

    @functools.partial(
        pl.kernel, mesh=mesh,
        out_type=jax.ShapeDtypeStruct((n_rows,), I32),
        scratch_types=[pltpu.VMEM((n_assign,), I32), pltpu.VMEM((per_w,), I32)],
        compiler_params=pltpu.CompilerParams(needs_layout_passes=False),
    )
    def inverse_kernel(dest_hbm, src_hbm, dest_v, src_v):
        wid = lax.axis_index("s") * SC_CORES + lax.axis_index("c")
        lo = pl.multiple_of(wid * per_w, per_w)
        pltpu.sync_copy(dest_hbm, dest_v)
        lane = lax.iota(I32, SC_LANES)

        @pl.loop(0, per_w, step=SC_LANES)
        def _(i):
            src_v[pl.ds(pl.multiple_of(i, SC_LANES), SC_LANES)] = lax.rem(lo + i + lane, n_tok)

        @pl.loop(0, n_assign, step=SC_LANES)
        def _(i):
            d = dest_v[pl.ds(pl.multiple_of(i, SC_LANES), SC_LANES)]
            a = i + lane
            tok = jnp.where(a >= n_tok, a - n_tok, a)
            mine = (d >= lo) & (d < lo + per_w)
            plsc.store_scatter(src_v, [d - lo], tok, mask=mine)

        pltpu.sync_copy(src_v, src_hbm.at[pl.ds(lo, per_w)])

    return inverse_kernel(dest_flat)


def _final_body(g1_ref, g2_ref, h1_ref, route_ref, g_ref, y_ref, *, half):
    route = route_ref[...]
    w1, w2 = route[:, 2:3], route[:, 3:4]
    a1, b1 = _unpack_bf16_pair(g1_ref[...])
    a2, b2 = _unpack_bf16_pair(g2_ref[...])
    h1 = h1_ref[...]
    ha = h1[:, 0:half] + (a1 * w1 + a2 * w2)
    hb = h1[:, half:2 * half] + (b1 * w1 + b2 * w2)
    ms = (jnp.sum(ha * ha, axis=-1, keepdims=True)
          + jnp.sum(hb * hb, axis=-1, keepdims=True)) / (2 * half)
    inv = lax.rsqrt(ms + EPS)
    y_ref[:, 0:half] = ha * inv * g_ref[:, 0:half]
    y_ref[:, half:2 * half] = hb * inv * g_ref[:, half:2 * half]


def _final(gathered, h1, route, g, *, tf, row0, n_rows):
    t, d_model = h1.shape
    half = d_model // 2
    tile0, tiles_t = row0 // tf, t // tf
    return pl.pallas_call(
        functools.partial(_final_body, half=half),
        grid=(n_rows // tf,),
        in_specs=[pl.BlockSpec((tf, half), lambda i: (tile0 + i, 0)),
                  pl.BlockSpec((tf, half), lambda i: (tiles_t + tile0 + i, 0)),
                  pl.BlockSpec((tf, d_model), lambda i: (tile0 + i, 0)),
                  pl.BlockSpec((tf, LANES), lambda i: (tile0 + i, 0)),
                  pl.BlockSpec((1, d_model), lambda i: (0, 0))],
        out_specs=pl.BlockSpec((tf, d_model), lambda i: (i, 0)),
        out_shape=jax.ShapeDtypeStruct((n_rows, d_model), F32),
        compiler_params=_cparams(1),
        name="final",
    )(gathered, gathered, h1, route, g)


def kernel(x_prompt, x_sample, meta, norm_mix_g, w_in, ssm_lam_re, ssm_lam_im, ssm_log_step, ssm_b_re, ssm_b_im, ssm_c_re, ssm_c_im, ssm_d, ssm_w_glu, conv_w_dw, conv_b_dw, conv_ln_g, conv_ln_b, conv_w_pw, w_out, norm_ffn_g, router_group_w, router_group_b, router_expert_w, router_expert_b, expert_w1, expert_w3, expert_w2, final_g):
    assert w_in.shape[0] == 1, "single-layer trunk"
    bp, lp, d_model = x_prompt.shape
    bs, ls, _ = x_sample.shape
    n_meta = meta.shape[0]
    d_ssm = ssm_d.shape[-1]
    d_conv = conv_b_dw.shape[-1]
    n_ch = ssm_b_re.shape[-1]
    n_g = ssm_b_re.shape[2]
    n_grp = router_group_w.shape[-1]
    n_exp = router_expert_w.shape[-1]
    assert n_meta == CHUNK and lp % CHUNK == 0 and ls % CHUNK == 0
    n_p, n_s = bp * lp, bs * ls
    t = n_p + n_s
    tm = min(256, lp, ls)
    assert lp % tm == 0 and ls % tm == 0

    xp2 = x_prompt.reshape(n_p, d_model)
    xs2 = x_sample.reshape(n_s, d_model)
    row = lambda a: a.reshape(1, -1)
    w_in_bf = w_in[0].astype(BF16)

    xg, c, sga, sgb = _inproj(xp2, xs2, row(norm_mix_g[0]), w_in_bf, d_ssm=d_ssm, d_conv=d_conv, tm=tm)
    xg_meta, c_meta = _meta_inproj(meta, row(norm_mix_g[0]), w_in_bf, d_ssm=d_ssm, d_conv=d_conv)

    assert n_ch == CHUNK and LANES % n_ch == 0 and n_g % (LANES // n_ch) == 0
    intra, bst, cst, coef = _sg_prep(ssm_lam_re[0], ssm_lam_im[0], ssm_log_step[0], ssm_b_re[0],
                                     ssm_b_im[0], ssm_c_re[0], ssm_c_im[0], ssm_d[0])
    nc = t // CHUNK
    n_blk = d_ssm // LANES
    rt = min(512, nc)
    yi = _sg_mm_intra(xg, intra, rt=rt)
    sf, sb, init = _sg_mm_state(xg, xg_meta, bst, rt=rt)
    n_sub = sf.shape[0] // (n_blk * nc)
    cb = min(64, lp // CHUNK, ls // CHUNK)
    xf, xb = _sg_scan(sf.reshape(n_blk, nc, n_sub, LANES), sb.reshape(n_blk, nc, n_sub, LANES),
                      coef, init, cb=cb, chunks_pseq=lp // CHUNK, chunks_sseq=ls // CHUNK,
                      n_pchunks=n_p // CHUNK)
    z = _sg_mm_out(yi, xf.reshape(sf.shape), xb.reshape(sb.shape), cst, rt=rt)

    assert n_grp + n_exp <= LANES // 2
    rw32 = jnp.pad(jnp.concatenate([router_group_w[0], router_expert_w[0]], axis=1),
                   ((0, 0), (0, LANES // 2 - n_grp - n_exp)))
    rw_hi = rw32.astype(BF16)
    rw = jnp.concatenate([rw_hi, (rw32 - rw_hi.astype(F32)).astype(BF16)], axis=1)
    rb = jnp.zeros((1, LANES), F32).at[0, 0:n_grp].set(router_group_b[0]).at[
        0, n_grp:n_grp + n_exp].set(router_expert_b[0])
    conv_w = conv_w_dw.shape[1]
    off = CHUNK - conv_w // 2
    assert 0 <= off and conv_w + off <= 2 * CHUNK
    taps = jnp.pad(conv_w_dw[0], ((off, -(conv_w + off) % 8), (0, 0)))
    h1, v, route, cnt = _mix(
        xp2, xs2, z, c, c_meta, sga, sgb, ssm_w_glu[0].astype(BF16), conv_w_pw[0].astype(BF16),
        w_out[0].astype(BF16), _conv_weights(taps), row(conv_b_dw[0]), row(conv_ln_g[0]),
        row(conv_ln_b[0]), row(norm_ffn_g[0]), rw, rb,
        tm=tm, p_seq=lp, s_seq=ls, n_grp=n_grp, n_exp=n_exp)

    tme = 256
    counts = cnt[0, 0:n_exp].astype(I32)
    tiles_e = (counts + tme - 1) // tme
    tile_end = jnp.cumsum(tiles_e)
    n_used = tile_end[-1]
    offs = jnp.zeros((1, LANES), F32).at[0, 0:n_exp].set(((tile_end - tiles_e) * tme).astype(F32))
    n_tiles = (2 * t) // tme + n_exp
    ids = jnp.arange(n_tiles, dtype=I32)

    def expert_of(tile):
        return jnp.minimum(jnp.sum((tile[:, None] >= tile_end[None, :]).astype(I32), axis=1), n_exp - 1)
    te_map = expert_of(jnp.minimum(ids, n_used - 1))
    onehot = te_map[:, None] == jnp.arange(n_exp, dtype=I32)[None, :]
    run_end = jnp.sum(jnp.where(onehot, tile_end[None, :], 0), axis=1)
    run_start = run_end - jnp.sum(jnp.where(onehot, tiles_e[None, :], 0), axis=1)
    valid = ids < n_used
    first = (valid & (ids == run_start)).astype(I32)
    last = (valid & (ids == run_end - 1)).astype(I32)
    nxt = jnp.where(valid & (run_end < n_used), expert_of(run_end), -1)

    dest = _positions(route, offs, te=min(512, t))
    idx = jnp.concatenate([dest[:, 0], dest[:, 1]])
    src = _sc_inverse(idx, n_tiles * tme)
    xs = _sc_gather(v, src, chunk=32)
    o = _experts(te_map, first, last, nxt, n_used.reshape(1), xs, expert_w1[0], expert_w3[0],
                 expert_w2[0], tme=tme)

    tf = min(256, n_p, n_s)
    gathered = _sc_gather(o, idx, chunk=32)
    fg = row(final_g)
    y_p = _final(gathered, h1, route, fg, tf=tf, row0=0, n_rows=n_p)
    y_s = _final(gathered, h1, route, fg, tf=tf, row0=n_p, n_rows=n_s)
    return (y_p.reshape(bp, lp, d_model), y_s.reshape(bs, ls, d_model))
```

```python
import functools

import jax
import jax.numpy as jnp
from jax import lax
from jax.experimental import pallas as pl
from jax.experimental.pallas import tpu as pltpu
from jax.experimental.pallas import tpu_sc as plsc

F32 = jnp.float32
BF16 = jnp.bfloat16
U32 = jnp.uint32
I32 = jnp.int32

EPS = 1e-6
LAM_RE_MAX = -1e-4
CHUNK = 16
LANES = 128
VMEM_LIMIT = 56 << 20
SC_CORES = 2
SC_SUBCORES = 16
SC_LANES = 16
HIGHEST = lax.Precision.HIGHEST


def _cparams(n_axes):
    return pltpu.CompilerParams(dimension_semantics=("arbitrary",) * n_axes,
                                vmem_limit_bytes=VMEM_LIMIT)


def _resident(shape):
    nd = len(shape)
    return pl.BlockSpec(shape, lambda *_: (0,) * nd, pipeline_mode=pl.Buffered(1))


def _sigmoid(x):
    return 1.0 / (1.0 + jnp.exp(-x))


def _rms(x, g):
    return x * lax.rsqrt(jnp.mean(x * x, axis=-1, keepdims=True) + EPS) * g


def _pack_bf16_pair(a, b):
    def rnd(x):
        u = pltpu.bitcast(x, U32)
        return (u + jnp.uint32(0x7FFF) + ((u >> 16) & jnp.uint32(1))) >> 16
    return pltpu.bitcast((rnd(a) << 16) | rnd(b), I32)


def _unpack_bf16_pair(p):
    p = pltpu.bitcast(p, U32)
    a = pltpu.bitcast(p & jnp.uint32(0xFFFF0000), F32)
    b = pltpu.bitcast(p << 16, F32)
    return a, b


def _inproj_body(xp_ref, xs_ref, g_ref, w_ref, xg_ref, c_ref, sga_ref, sgb_ref, usc, *,
                 tm, n_p_tiles, d_ssm, d_conv, d_model, col):
    i = pl.program_id(0)
    x = jnp.where(i < n_p_tiles, xp_ref[...], xs_ref[...])
    y = _rms(x, g_ref[...]).astype(BF16)

    def proj(lo, n):
        return jnp.dot(y, w_ref[:, lo:lo + n], preferred_element_type=F32)

    for k in range(d_ssm // col):
        u = proj(k * col, col)
        for b in range(col // LANES):
            usc[k * (col // LANES) + b] = u[:, b * LANES:(b + 1) * LANES]
    for b in range(d_ssm // LANES):
        for s in range(CHUNK):
            c0 = (b * CHUNK + s) * LANES
            xg_ref[:, c0:c0 + LANES] = usc[b, pl.ds(s, tm // CHUNK, stride=CHUNK), :].astype(BF16)
    for k in range(d_conv // col):
        cv = proj(d_ssm + k * col, col)
        cg = proj(d_ssm + d_conv + k * col, col)
        c_ref[:, k * col:(k + 1) * col] = (cv * _sigmoid(cg)).astype(BF16)
    base = d_ssm + 2 * d_conv
    for k in range(d_model // col):
        sga_ref[:, k * col:(k + 1) * col] = _sigmoid(proj(base + k * col, col)).astype(BF16)
        sgb_ref[:, k * col:(k + 1) * col] = _sigmoid(
            proj(base + d_model + k * col, col)).astype(BF16)


def _inproj(xp2, xs2, g, w_bf, *, d_ssm, d_conv, tm):
    n_p, d_model = xp2.shape
    n_s = xs2.shape[0]
    n_p_tiles, n_s_tiles = n_p // tm, n_s // tm
    t = n_p + n_s
    col = min(1024, d_ssm, d_conv, d_model)
    body = functools.partial(_inproj_body, tm=tm, n_p_tiles=n_p_tiles, d_ssm=d_ssm, d_conv=d_conv,
                             d_model=d_model, col=col)
    row = lambda i: (i, 0)
    return pl.pallas_call(
        body,
        grid=(n_p_tiles + n_s_tiles,),
        in_specs=[
            pl.BlockSpec((tm, d_model), lambda i: (jnp.minimum(i, n_p_tiles - 1), 0)),
            pl.BlockSpec((tm, d_model), lambda i: (jnp.maximum(i - n_p_tiles, 0), 0)),
            _resident((1, d_model)),
            _resident(w_bf.shape),
        ],
        out_specs=[pl.BlockSpec((tm // CHUNK, CHUNK * d_ssm), row), pl.BlockSpec((tm, d_conv), row),
                   pl.BlockSpec((tm, d_model), row), pl.BlockSpec((tm, d_model), row)],
        out_shape=[jax.ShapeDtypeStruct((t // CHUNK, CHUNK * d_ssm), BF16),
                   jax.ShapeDtypeStruct((t, d_conv), BF16),
                   jax.ShapeDtypeStruct((t, d_model), BF16), jax.ShapeDtypeStruct((t, d_model), BF16)],
        scratch_shapes=[pltpu.VMEM((d_ssm // LANES, tm, LANES), F32)],
        compiler_params=_cparams(1),
        name="inproj",
    )(xp2, xs2, g, w_bf)


def _meta_body(m_ref, g_ref, w_ref, xg_ref, c_ref, *, d_ssm, d_conv):
    y = _rms(m_ref[...], g_ref[...]).astype(BF16)
    u = jnp.dot(y, w_ref[:, 0:d_ssm], preferred_element_type=F32)
    first = lax.broadcasted_iota(I32, (CHUNK, LANES), 0) == 0
    for b in range(d_ssm // LANES):
        for s in range(CHUNK):
            c0 = (b * CHUNK + s) * LANES
            piece = jnp.broadcast_to(u[s:s + 1, b * LANES:(b + 1) * LANES], (CHUNK, LANES))
            xg_ref[:, c0:c0 + LANES] = jnp.where(first, piece, 0.0).astype(BF16)
    cv = jnp.dot(y, w_ref[:, d_ssm:d_ssm + d_conv], preferred_element_type=F32)
    cg = jnp.dot(y, w_ref[:, d_ssm + d_conv:d_ssm + 2 * d_conv], preferred_element_type=F32)
    c_ref[...] = (cv * _sigmoid(cg)).astype(BF16)


def _meta_inproj(meta, g, w_bf, *, d_ssm, d_conv):
    n_meta, d_model = meta.shape
    ncol = d_ssm + 2 * d_conv
    return pl.pallas_call(
        functools.partial(_meta_body, d_ssm=d_ssm, d_conv=d_conv),
        grid=(1,),
        in_specs=[pl.BlockSpec((n_meta, d_model), lambda i: (0, 0)),
                  pl.BlockSpec((1, d_model), lambda i: (0, 0)),
                  pl.BlockSpec((d_model, ncol), lambda i: (0, 0))],
        out_specs=[pl.BlockSpec((CHUNK, CHUNK * d_ssm), lambda i: (0, 0)),
                   pl.BlockSpec((n_meta, d_conv), lambda i: (0, 0))],
        out_shape=[jax.ShapeDtypeStruct((CHUNK, CHUNK * d_ssm), BF16),
                   jax.ShapeDtypeStruct((n_meta, d_conv), BF16)],
        compiler_params=_cparams(1),
        name="meta_inproj",
    )(meta, g, w_bf)


def _cmul(ar, ai, br, bi):
    return ar * br - ai * bi, ar * bi + ai * br


def _discretize(lam_re, lam_im, log_step):
    lr = jnp.minimum(lam_re, LAM_RE_MAX)
    dt = jnp.exp(log_step)
    mag = jnp.exp(lr * dt)
    ar = mag * jnp.cos(lam_im * dt)
    ai = mag * jnp.sin(lam_im * dt)
    den = lr * lr + lam_im * lam_im
    nr = ar - 1.0
    fr = (nr * lr + ai * lam_im) / den
    fi = (ai * lr - nr * lam_im) / den
    return ar, ai, fr, fi


def _cpow(ar, ai, k, nbits, shape):
    pr = jnp.ones(shape, F32)
    pi = jnp.zeros(shape, F32)
    br = jnp.broadcast_to(ar, shape)
    bi = jnp.broadcast_to(ai, shape)
    kk = jnp.broadcast_to(k, shape)
    for b in range(nbits):
        sel = ((kk >> b) & 1) == 1
        nr, ni = _cmul(pr, pi, br, bi)
        pr = jnp.where(sel, nr, pr)
        pi = jnp.where(sel, ni, pi)
        br, bi = _cmul(br, bi, br, bi)
    return pr, pi


def _spread(x, n_ch, shift):
    rows, w = x.shape
    lane = lax.broadcasted_iota(I32, (rows, LANES), 1)
    keep = (lane >= shift) & (lane < shift + n_ch)
    out = []
    for b in range(w // n_ch):
        src = (b * n_ch // LANES) * LANES
        amount = lax.rem(shift + LANES - (b * n_ch) % LANES, LANES)
        out.append(jnp.where(keep, pltpu.roll(x[:, src:src + LANES], amount, 1), 0.0))
    return jnp.concatenate(out, axis=1)


def _tile_lanes(x, n_ch, width):
    span = n_ch
    while span < LANES:
        x = x + pltpu.roll(x, span, 1)
        span *= 2
    return jnp.concatenate([x] * (width // LANES), axis=1)


def _disc_body(lam_re_ref, lam_im_ref, ls_ref, ar_ref, ai_ref, fr_ref, fi_ref):
    ar, ai, fr, fi = _discretize(lam_re_ref[...], lam_im_ref[...], ls_ref[...])
    ar_ref[...] = ar
    ai_ref[...] = ai
    fr_ref[...] = fr
    fi_ref[...] = fi


def _ssm_discretize(lam_re, lam_im, log_step):
    shape = lam_re.shape
    rows = shape[0] * shape[1]
    flat = lambda x: x.reshape(rows, shape[2])
    ls = jnp.broadcast_to(log_step[..., None], shape)
    full = pl.BlockSpec((rows, shape[2]), lambda i: (0, 0))
    out = pl.pallas_call(
        _disc_body,
        grid=(1,),
        in_specs=[full, full, full],
        out_specs=[full] * 4,
        out_shape=[jax.ShapeDtypeStruct((rows, shape[2]), F32)] * 4,
        compiler_params=_cparams(1),
        name="ssm_disc",
    )(flat(lam_re), flat(lam_im), flat(ls))
    return [x.reshape(shape) for x in out]


def _sg_prep_body(ar_c, ai_c, fr_c, fi_c, ct_re, ct_im, bt_re, bt_im,
                  ar_r, ai_r, fr_r, fi_r, btr_re, btr_im, dpad,
                  m_ref, b_ref, c_ref, aq_ref, *, n_state, n_ch, per_step):
    per_blk = LANES // n_ch
    first = (pl.program_id(0) * per_step) % per_blk
    for j in range(per_step):
        _sg_prep_group(j, first + j, ar_c, ai_c, fr_c, fi_c, ct_re, ct_im, bt_re, bt_im,
                       ar_r, ai_r, fr_r, fi_r, btr_re, btr_im, dpad,
                       m_ref, b_ref, c_ref, aq_ref, n_state=n_state, n_ch=n_ch)


def _sg_prep_group(j, gl, ar_c, ai_c, fr_c, fi_c, ct_re, ct_im, bt_re, bt_im,
                   ar_r, ai_r, fr_r, fi_r, btr_re, btr_im, dpad,
                   m_ref, b_ref, c_ref, aq_ref, *, n_state, n_ch):
    q = CHUNK
    width = q * n_ch
    per_blk = LANES // n_ch
    rows = slice(j * n_ch, (j + 1) * n_ch)
    shift = gl * n_ch
    kblk = lax.broadcasted_iota(I32, (1, width), 1) // n_ch
    strips = []
    for d in range(2):
        ar, ai, fr, fi = ar_c[d, j], ai_c[d, j], fr_c[d, j], fi_c[d, j]
        kexp = kblk if d == 0 else (q - 1) - kblk
        wr, wi = _cpow(ar, ai, kexp, 4, (n_state, width))
        gcr, gci = _cmul(_tile_lanes(ct_re[d, j], n_ch, width),
                         _tile_lanes(ct_im[d, j], n_ch, width), wr, wi)
        gfr, gfi = _cmul(gcr, gci, fr, fi)
        strips.append(jnp.dot(bt_re[d, j], gfr, precision=HIGHEST, preferred_element_type=F32)
                      - jnp.dot(bt_im[d, j], gfi, precision=HIGHEST, preferred_element_type=F32))
        g1r, g1i = _cmul(gcr, gci, ar, ai)
        for comp, val in ((0, g1r), (1, -g1i)):
            c_ref[2 * d + comp, j] = _spread(val, n_ch, shift).astype(BF16)

    zf, zb = strips
    zero = jnp.zeros((n_ch, width), F32)
    z512 = jnp.concatenate([zb, zero], axis=1) + pltpu.roll(
        jnp.concatenate([zf, zero], axis=1), (q - 1) * n_ch, 1)
    row = lax.broadcasted_iota(I32, (n_ch, 2 * width), 0)
    lane = lax.broadcasted_iota(I32, (n_ch, 2 * width), 1)
    z512 = z512 + jnp.where(lane - (q - 1) * n_ch == row, dpad[j], 0.0)
    zwide = _spread(z512, n_ch, shift)
    for s in range(q):
        lo = (q - 1 - s) * LANES
        m_ref[s, rows, :] = zwide[:, lo:lo + q * LANES].astype(BF16)

    parmask = (lax.broadcasted_iota(I32, (1, 2 * n_state), 1) // n_state) == gl % 2
    pieces = [[None] * 4 for _ in range(q)]
    for d in range(2):
        ar, ai, fr, fi = ar_r[d, j], ai_r[d, j], fr_r[d, j], fi_r[d, j]
        pw = [(jnp.ones_like(ar), jnp.zeros_like(ar))]
        for _ in range(q):
            pw.append(_cmul(pw[-1][0], pw[-1][1], ar, ai))
        for s in range(q):
            e = (q - 1 - s) if d == 0 else s
            cr, ci = _cmul(fr, fi, pw[e][0], pw[e][1])
            br, bi = _cmul(btr_re[d, j], btr_im[d, j], cr, ci)
            pieces[s][2 * d] = jnp.where(parmask, br, 0.0)
            pieces[s][2 * d + 1] = jnp.where(parmask, bi, 0.0)
        aq_ref[j, :, 2 * d * n_state:(2 * d + 1) * n_state] = pw[q][0][:, 0:n_state]
        aq_ref[j, :, (2 * d + 1) * n_state:(2 * d + 2) * n_state] = pw[q][1][:, 0:n_state]
    n_pair = per_blk // 2
    for s in range(q):
        cols = []
        for seg in range(4):
            for blk in range(n_pair):
                cols.append(jnp.where(gl // 2 == blk, pieces[s][seg], 0.0))
        b_ref[s, rows, :] = jnp.concatenate(cols, axis=1).astype(BF16)


def _sg_prep(lam_re, lam_im, log_step, b_re, b_im, c_re, c_im, d_skip):
    _, n_g, n_state, n_ch = b_re.shape
    q = CHUNK
    width = q * n_ch
    per_blk = LANES // n_ch
    n_blk = n_g // per_blk
    sw = 4 * per_blk * n_state
    dup = lambda x: jnp.concatenate([x, x], axis=-1)
    lane_pad = lambda x: jnp.pad(x, ((0, 0),) * 3 + ((0, LANES - n_ch),))
    disc = _ssm_discretize(lam_re, lam_im, log_step)
    disc_c = [x[..., None] for x in disc]
    disc_r = [dup(x)[:, :, None, :] for x in disc]
    ct_re = lane_pad(jnp.swapaxes(c_re, -1, -2))
    ct_im = lane_pad(jnp.swapaxes(c_im, -1, -2))
    bt_re, bt_im = jnp.swapaxes(b_re, -1, -2), jnp.swapaxes(b_im, -1, -2)
    btr_re, btr_im = dup(bt_re), dup(bt_im)
    dpad = jnp.pad(d_skip.reshape(n_g, 1, n_ch), ((0, 0), (0, 0), ((q - 1) * n_ch, width)))

    per_step = 2
    spb = per_blk // per_step

    def dspec(shape):
        return pl.BlockSpec((2, per_step) + shape, lambda g: (0, g, 0, 0))

    body = functools.partial(_sg_prep_body, n_state=n_state, n_ch=n_ch, per_step=per_step)
    m4, b4, c5, aq = pl.pallas_call(
        body,
        grid=(n_g // per_step,),
        in_specs=[dspec((n_state, 1))] * 4 + [
                  dspec((n_state, LANES)), dspec((n_state, LANES)),
                  dspec((n_ch, n_state)), dspec((n_ch, n_state))] + [dspec((1, 2 * n_state))] * 4 + [
                  dspec((n_ch, 2 * n_state)), dspec((n_ch, 2 * n_state)),
                  pl.BlockSpec((per_step, 1, 2 * width), lambda g: (g, 0, 0))],
        out_specs=[
            pl.BlockSpec((None, q, per_step * n_ch, q * LANES), lambda g: (g // spb, 0, g % spb, 0)),
            pl.BlockSpec((None, q, per_step * n_ch, sw), lambda g: (g // spb, 0, g % spb, 0)),
            pl.BlockSpec((None, 4, per_step, n_state, q * LANES),
                         lambda g: (g // spb, 0, g % spb, 0, 0)),
            pl.BlockSpec((per_step, 1, 4 * n_state), lambda g: (g, 0, 0))],
        out_shape=[jax.ShapeDtypeStruct((n_blk, q, LANES, q * LANES), BF16),
                   jax.ShapeDtypeStruct((n_blk, q, LANES, sw), BF16),
                   jax.ShapeDtypeStruct((n_blk, 4, per_blk, n_state, q * LANES), BF16),
                   jax.ShapeDtypeStruct((n_g, 1, 4 * n_state), F32)],
        compiler_params=_cparams(1),
        name="ssm_prep",
    )(*disc_c, ct_re, ct_im, bt_re, bt_im, *disc_r, btr_re, btr_im, dpad)
    intra = m4.reshape(n_blk, q * LANES, q * LANES)
    bst = b4.reshape(n_blk, q * LANES, sw)
    cst = c5.reshape(n_blk, sw, q * LANES)
    a = aq.reshape(n_blk, per_blk // 2, 2, 2, 2, n_state)
    a = a.transpose(3, 4, 0, 1, 2, 5).reshape(2, 2, n_blk, per_blk // 2, LANES)
    coef = []
    for d in range(2):
        re, im = a[d, 0], a[d, 1]
        coef.append((jnp.concatenate([re, re], axis=1), jnp.concatenate([-im, im], axis=1)))
    return intra, bst, cst, coef


def _sg_mm_state_body(x_ref, xm_ref, b_ref, sf_ref, sb_ref, init_ref, *, rt, n_sub):
    r = jnp.dot(x_ref[...], b_ref[...], preferred_element_type=F32)
    for k in range(n_sub):
        sf_ref[pl.ds(k, rt, stride=n_sub), :] = r[:, k * LANES:(k + 1) * LANES]
        sb_ref[pl.ds(k, rt, stride=n_sub), :] = r[:, (n_sub + k) * LANES:(n_sub + k + 1) * LANES]

    @pl.when(pl.program_id(1) == 0)
    def _():
        r0 = jnp.dot(xm_ref[...], b_ref[...], preferred_element_type=F32)
        for k in range(n_sub):
            init_ref[k:k + 1, :] = r0[0:1, k * LANES:(k + 1) * LANES]


def _sg_mm_state(xg, xg_meta, bst, *, rt):
    nc = xg.shape[0]
    n_blk, kw, sw = bst.shape
    n_sub = sw // (2 * LANES)
    n_rt = nc // rt
    body = functools.partial(_sg_mm_state_body, rt=rt, n_sub=n_sub)
    dense = jax.ShapeDtypeStruct((n_blk * nc * n_sub, LANES), F32)
    dspec = pl.BlockSpec((rt * n_sub, LANES), lambda b, i: (b * n_rt + i, 0))
    return pl.pallas_call(
        body,
        grid=(n_blk, n_rt),
        in_specs=[pl.BlockSpec((rt, kw), lambda b, i: (i, b)),
                  pl.BlockSpec((CHUNK, kw), lambda b, i: (0, b)),
                  pl.BlockSpec((None, kw, sw), lambda b, i: (b, 0, 0))],
        out_specs=[dspec, dspec, pl.BlockSpec((None, n_sub, LANES), lambda b, i: (b, 0, 0))],
        out_shape=[dense, dense, jax.ShapeDtypeStruct((n_blk, n_sub, LANES), F32)],
        compiler_params=_cparams(2),
        name="ssm_state",
    )(xg, xg_meta, bst)


def _sg_scan_body(sf_ref, sb_ref, a1f_ref, a2f_ref, a1b_ref, a2b_ref, init_ref, xf_ref, xb_ref,
                  stf, stb, *, cb, n_blk, half, n_pblk, blk_pseq, blk_sseq):
    j = pl.program_id(0)
    pos = jnp.where(j < n_pblk, lax.rem(j, blk_pseq), lax.rem(jnp.maximum(j - n_pblk, 0), blk_sseq))

    @pl.when(pos == 0)
    def _():
        stf[...] = init_ref[...]
        stb[...] = jnp.zeros_like(stb)

    def body(i, carry):
        ef, eb = carry
        ib = cb - 1 - i
        nf, nb = [], []
        for g in range(n_blk):
            xf_ref[g, i] = ef[g]
            xb_ref[g, ib] = eb[g]
            nf.append(a1f_ref[g] * ef[g] + a2f_ref[g] * pltpu.roll(ef[g], half, 0) + sf_ref[g, i])
            nb.append(a1b_ref[g] * eb[g] + a2b_ref[g] * pltpu.roll(eb[g], half, 0) + sb_ref[g, ib])
        return tuple(nf), tuple(nb)

    ef, eb = lax.fori_loop(0, cb, body, (tuple(stf[g] for g in range(n_blk)),
                                         tuple(stb[g] for g in range(n_blk))))
    for g in range(n_blk):
        stf[g] = ef[g]
        stb[g] = eb[g]


def _sg_scan(sf, sb, coef, init, *, cb, chunks_pseq, chunks_sseq, n_pchunks):
    n_blk, nc, n_sub, _ = sf.shape
    n_pblk, blk_pseq, blk_sseq = n_pchunks // cb, chunks_pseq // cb, chunks_sseq // cb

    def bwd_block(j):
        in_p = j < n_pblk
        pos = jnp.where(in_p, lax.rem(j, blk_pseq), lax.rem(jnp.maximum(j - n_pblk, 0), blk_sseq))
        ln = jnp.where(in_p, blk_pseq, blk_sseq)
        return j - pos + ln - 1 - pos

    body = functools.partial(_sg_scan_body, cb=cb, n_blk=n_blk, half=n_sub // 2, n_pblk=n_pblk,
                             blk_pseq=blk_pseq, blk_sseq=blk_sseq)
    fwd = pl.BlockSpec((n_blk, cb, n_sub, LANES), lambda j: (0, j, 0, 0))
    bwd = pl.BlockSpec((n_blk, cb, n_sub, LANES), lambda j: (0, bwd_block(j), 0, 0))
    small = pl.BlockSpec((n_blk, n_sub, LANES), lambda j: (0, 0, 0))
    return pl.pallas_call(
        body,
        grid=(nc // cb,),
        in_specs=[fwd, bwd, small, small, small, small, small],
        out_specs=[fwd, bwd],
        out_shape=[jax.ShapeDtypeStruct(sf.shape, F32), jax.ShapeDtypeStruct(sb.shape, F32)],
        scratch_shapes=[pltpu.VMEM((n_blk, n_sub, LANES), F32), pltpu.VMEM((n_blk, n_sub, LANES), F32)],
        compiler_params=_cparams(1),
        name="ssm_scan",
    )(sf, sb, coef[0][0], coef[0][1], coef[1][0], coef[1][1], init)


def _gelu_tanh(x):
    return 0.5 * x * (1.0 + jnp.tanh(0.7978845608028654 * (x + 0.044715 * (x * x * x))))


def _sg_mm_out_body(x_ref, m_ref, xf_ref, xb_ref, c_ref, z_ref, *, rt, n_sub):
    cols = [xf_ref[pl.ds(k, rt, stride=n_sub), :] for k in range(n_sub)]
    cols += [xb_ref[pl.ds(k, rt, stride=n_sub), :] for k in range(n_sub)]
    state = jnp.concatenate(cols, axis=1).astype(BF16)
    y = (jnp.dot(x_ref[...], m_ref[...], preferred_element_type=F32)
         + jnp.dot(state, c_ref[...], preferred_element_type=F32))
    z_ref[...] = _gelu_tanh(y).astype(BF16)


def _sg_mm_out(xg, intra, xf, xb, cst, *, rt):
    nc = xg.shape[0]
    n_blk, sw, kw = cst.shape
    n_sub = sw // (2 * LANES)
    n_rt = nc // rt
    dspec = pl.BlockSpec((rt * n_sub, LANES), lambda b, i: (b * n_rt + i, 0))
    return pl.pallas_call(
        functools.partial(_sg_mm_out_body, rt=rt, n_sub=n_sub),
        grid=(n_blk, n_rt),
        in_specs=[pl.BlockSpec((rt, kw), lambda b, i: (i, b)),
                  pl.BlockSpec((None, kw, kw), lambda b, i: (b, 0, 0)), dspec, dspec,
                  pl.BlockSpec((None, sw, kw), lambda b, i: (b, 0, 0))],
        out_specs=pl.BlockSpec((rt, kw), lambda b, i: (i, b)),
        out_shape=jax.ShapeDtypeStruct((nc, n_blk * kw), BF16),
        compiler_params=_cparams(2),
        name="ssm_out",
    )(xg, intra, xf, xb, cst)


def _mix_body(xp_ref, xs_ref, z_ref, c_ref, cprev_ref, cnext_ref, cmeta_ref, sga_ref, sgb_ref,
              wglu_ref, wpw_ref, wout_ref, wconv_ref, bdw_ref, lng_ref, lnb_ref, gffn_ref,
              rw_ref, rb_ref,
              h1_ref, v_ref, route_ref, cnt_ref,
              cw_ref, conv_ref, zsc_ref, *,
              tm, n_p_tiles, tiles_per_pseq, tiles_per_sseq, d_model, d_ssm, d_conv,
              n_grp, n_exp, exp_per_grp):
    i = pl.program_id(0)
    in_prompt = i < n_p_tiles
    x = jnp.where(in_prompt, xp_ref[...], xs_ref[...])
    pos_p = lax.rem(i, tiles_per_pseq)
    pos_s = lax.rem(jnp.maximum(i - n_p_tiles, 0), tiles_per_sseq)
    is_start = jnp.where(in_prompt, pos_p == 0, pos_s == 0)
    is_end = jnp.where(in_prompt, pos_p == tiles_per_pseq - 1, pos_s == tiles_per_sseq - 1)

    halo = CHUNK
    cw_ref[0:halo, :] = jnp.where(is_start, cmeta_ref[...], cprev_ref[...]).astype(F32)
    cw_ref[halo:halo + tm, :] = c_ref[...].astype(F32)
    cw_ref[halo + tm:2 * halo + tm, :] = jnp.where(is_end, 0.0, cnext_ref[...].astype(F32))
    sub = 8
    nq = wconv_ref.shape[1] // LANES
    for lc in range(d_conv // LANES):
        ls = slice(lc * LANES, (lc + 1) * LANES)
        stacked = jnp.concatenate(
            [cw_ref[sub * q:sub * q + tm + sub, ls] for q in range(nq)], axis=1).astype(BF16)
        part = jnp.dot(stacked, wconv_ref[lc], preferred_element_type=F32)
        out = part[0:tm, 0:LANES]
        for r in range(1, sub):
            out = out + part[r:r + tm, r * LANES:(r + 1) * LANES]
        conv_ref[:, ls] = out
    cc = conv_ref[...] + bdw_ref[...]
    mu = jnp.mean(cc, axis=-1, keepdims=True)
    var = jnp.mean(jnp.square(cc - mu), axis=-1, keepdims=True)
    cc = (cc - mu) * lax.rsqrt(var + EPS) * lng_ref[...] + lnb_ref[...]
    cc = (cc * _sigmoid(cc)).astype(BF16)
    y_b = jnp.dot(cc, wpw_ref[...], preferred_element_type=F32)

    for b in range(d_ssm // LANES):
        for s in range(CHUNK):
            c0 = (b * CHUNK + s) * LANES
            zsc_ref[b, pl.ds(s, tm // CHUNK, stride=CHUNK), :] = z_ref[:, c0:c0 + LANES].astype(F32)
    z = jnp.concatenate([zsc_ref[b] for b in range(d_ssm // LANES)], axis=1).astype(BF16)
    va = jnp.dot(z, wglu_ref[:, 0:d_model], preferred_element_type=F32)
    ga = jnp.dot(z, wglu_ref[:, d_model:2 * d_model], preferred_element_type=F32)
    y_a = va * _sigmoid(ga)
    merged = (sga_ref[...].astype(F32) * y_a + sgb_ref[...].astype(F32) * y_b).astype(BF16)
    h1 = x + jnp.dot(merged, wout_ref[...], preferred_element_type=F32)
    h1_ref[...] = h1
    v = _rms(h1, gffn_ref[...])
    half = d_model // 2
    v_ref[...] = _pack_bf16_pair(v[:, 0:half], v[:, half:d_model])

    v_hi = v.astype(BF16)
    v_lo = (v - v_hi.astype(F32)).astype(BF16)
    acc = (jnp.dot(v_hi, rw_ref[...], preferred_element_type=F32)
           + jnp.dot(v_lo, rw_ref[...], preferred_element_type=F32))
    logits = acc + pltpu.roll(acc, LANES // 2, 1) + rb_ref[...]
    lane = lax.broadcasted_iota(I32, (tm, LANES), 1).astype(F32)
    big = jnp.float32(1e9)
    neg = jnp.float32(-jnp.inf)
    gmask = lane < n_grp
    lg = jnp.where(gmask, logits, neg)
    gmax = jnp.max(lg, axis=-1, keepdims=True)
    grp = jnp.min(jnp.where(lg == gmax, lane, big), axis=-1, keepdims=True)
    p_grp = 1.0 / jnp.sum(jnp.where(gmask, jnp.exp(logits - gmax), 0.0), axis=-1, keepdims=True)
    lo = n_grp + grp * exp_per_grp
    emask = (lane >= lo) & (lane < lo + exp_per_grp)
    le = jnp.where(emask, logits, neg)
    m1 = jnp.max(le, axis=-1, keepdims=True)
    i1 = jnp.min(jnp.where(le == m1, lane, big), axis=-1, keepdims=True)
    le2 = jnp.where(lane == i1, neg, le)
    m2 = jnp.max(le2, axis=-1, keepdims=True)
    i2 = jnp.min(jnp.where(le2 == m2, lane, big), axis=-1, keepdims=True)
    t = jnp.exp(m2 - m1)
    w1 = 1.0 / (1.0 + t)
    e1 = i1 - n_grp
    e2 = i2 - n_grp
    route_ref[...] = jnp.where(lane == 0, e1, jnp.where(lane == 1, e2, jnp.where(
        lane == 2, p_grp * w1, jnp.where(lane == 3, p_grp * (t * w1), 0.0))))

    @pl.when(i == 0)
    def _():
        cnt_ref[...] = jnp.zeros_like(cnt_ref)

    hot = jnp.where((lane == e1) | (lane == e2), 1.0, 0.0)
    cnt_ref[...] += jnp.sum(hot, axis=0, keepdims=True)


def _conv_weights_body(taps_ref, w_ref, *, n_taps):
    eye = (lax.broadcasted_iota(I32, (LANES, LANES), 0)
           == lax.broadcasted_iota(I32, (LANES, LANES), 1))
    for p in range(n_taps):
        q, r = divmod(p, 8)
        diag = jnp.where(eye, taps_ref[p:p + 1, :], 0.0)
        w_ref[q * LANES:(q + 1) * LANES, r * LANES:(r + 1) * LANES] = diag.astype(BF16)


def _conv_weights(taps):
    n_taps, d_conv = taps.shape
    nq = n_taps // 8
    return pl.pallas_call(
        functools.partial(_conv_weights_body, n_taps=n_taps),
        grid=(d_conv // LANES,),
        in_specs=[pl.BlockSpec((n_taps, LANES), lambda b: (0, b))],
        out_specs=pl.BlockSpec((None, nq * LANES, 8 * LANES), lambda b: (b, 0, 0)),
        out_shape=jax.ShapeDtypeStruct((d_conv // LANES, nq * LANES, 8 * LANES), BF16),
        compiler_params=_cparams(1),
        name="conv_weights",
    )(taps)


def _mix(xp2, xs2, z, c, c_meta, sga, sgb, wglu, wpw, wout, wconv, bdw, lng, lnb, gffn, rw, rb, *,
         tm, p_seq, s_seq, n_grp, n_exp):
    n_p, d_model = xp2.shape
    n_s = xs2.shape[0]
    t = n_p + n_s
    d_ssm, d_conv = z.shape[1] // CHUNK, c.shape[1]
    n_p_tiles, n_s_tiles = n_p // tm, n_s // tm
    hpt = tm // CHUNK
    n_hblk = t // CHUNK
    body = functools.partial(
        _mix_body, tm=tm, n_p_tiles=n_p_tiles, tiles_per_pseq=p_seq // tm,
        tiles_per_sseq=s_seq // tm, d_model=d_model, d_ssm=d_ssm, d_conv=d_conv,
        n_grp=n_grp, n_exp=n_exp, exp_per_grp=n_exp // n_grp)
    row = lambda i: (i, 0)
    return pl.pallas_call(
        body,
        grid=(n_p_tiles + n_s_tiles,),
        in_specs=[
            pl.BlockSpec((tm, d_model), lambda i: (jnp.minimum(i, n_p_tiles - 1), 0)),
            pl.BlockSpec((tm, d_model), lambda i: (jnp.maximum(i - n_p_tiles, 0), 0)),
            pl.BlockSpec((tm // CHUNK, CHUNK * d_ssm), row),
            pl.BlockSpec((tm, d_conv), row),
            pl.BlockSpec((CHUNK, d_conv), lambda i: (jnp.maximum(i * hpt - 1, 0), 0)),
            pl.BlockSpec((CHUNK, d_conv), lambda i: (jnp.minimum((i + 1) * hpt, n_hblk - 1), 0)),
            _resident(c_meta.shape),
            pl.BlockSpec((tm, d_model), row),
            pl.BlockSpec((tm, d_model), row),
            _resident(wglu.shape), _resident(wpw.shape), _resident(wout.shape),
            _resident(wconv.shape), _resident(bdw.shape), _resident(lng.shape), _resident(lnb.shape),
            _resident(gffn.shape), _resident(rw.shape), _resident(rb.shape),
        ],
        out_specs=[pl.BlockSpec((tm, d_model), row), pl.BlockSpec((tm, d_model // 2), row),
                   pl.BlockSpec((tm, LANES), row), pl.BlockSpec((1, LANES), lambda i: (0, 0))],
        out_shape=[jax.ShapeDtypeStruct((t, d_model), F32),
                   jax.ShapeDtypeStruct((t, d_model // 2), I32),
                   jax.ShapeDtypeStruct((t, LANES), F32),
                   jax.ShapeDtypeStruct((1, LANES), F32)],
        scratch_shapes=[pltpu.VMEM((tm + 2 * CHUNK, d_conv), F32), pltpu.VMEM((tm, d_conv), F32),
                        pltpu.VMEM((d_ssm // LANES, tm, LANES), F32)],
        compiler_params=_cparams(1),
        name="mix",
    )(xp2, xs2, z, c, c, c, c_meta, sga, sgb, wglu, wpw, wout, wconv, bdw, lng, lnb, gffn, rw, rb)


def _positions_body(route_ref, offs_ref, dest_ref, carry_ref, *, te):
    @pl.when(pl.program_id(0) == 0)
    def _():
        carry_ref[...] = jnp.zeros_like(carry_ref)

    lane = lax.broadcasted_iota(I32, (te, LANES), 1).astype(F32)
    r = route_ref[...]
    oh1 = lane == r[:, 0:1]
    oh2 = lane == r[:, 1:2]
    both = jnp.where(oh1 | oh2, 1.0, 0.0)
    tri = jnp.where(lax.broadcasted_iota(I32, (te, te), 0) > lax.broadcasted_iota(I32, (te, te), 1),
                    1.0, 0.0).astype(BF16)
    before = jnp.dot(tri, both.astype(BF16), preferred_element_type=F32)
    base = before + carry_ref[...] + offs_ref[...]
    d1 = jnp.sum(jnp.where(oh1, base, 0.0), axis=-1, keepdims=True)
    d2 = jnp.sum(jnp.where(oh2, base, 0.0), axis=-1, keepdims=True)
    dest_ref[...] = jnp.where(lane == 0, d1, jnp.where(lane == 1, d2, 0.0)).astype(I32)
    carry_ref[...] += jnp.sum(both, axis=0, keepdims=True)


def _positions(route, offs, *, te):
    t = route.shape[0]
    return pl.pallas_call(
        functools.partial(_positions_body, te=te),
        grid=(t // te,),
        in_specs=[pl.BlockSpec((te, LANES), lambda i: (i, 0)),
                  pl.BlockSpec((1, LANES), lambda i: (0, 0))],
        out_specs=pl.BlockSpec((te, LANES), lambda i: (i, 0)),
        out_shape=jax.ShapeDtypeStruct((t, LANES), I32),
        scratch_shapes=[pltpu.VMEM((1, LANES), F32)],
        compiler_params=_cparams(1),
        name="positions",
    )(route, offs)


def _experts_body(te_ref, first_ref, last_ref, nxt_ref, nused_ref,
                  xs_ref, w1_hbm, w3_hbm, w2_hbm, o_ref,
                  w1b, w3b, w2b, st1, st3, st2, sem, *, half, n_conv):
    i = pl.program_id(0)
    used = i < nused_ref[0]
    nxt = nxt_ref[i]

    def copies(e):
        return (pltpu.make_async_copy(w1_hbm.at[e], st1, sem.at[0]),
                pltpu.make_async_copy(w3_hbm.at[e], st3, sem.at[1]),
                pltpu.make_async_copy(w2_hbm.at[e], st2, sem.at[2]))

    def convert():
        r13 = st1.shape[0] // n_conv
        r2 = st2.shape[0] // n_conv

        def body(c, _):
            rows = pl.ds(pl.multiple_of(c * r13, r13), r13)
            w1b[rows, :] = st1[rows, :].astype(BF16)
            w3b[rows, :] = st3[rows, :].astype(BF16)
            rows2 = pl.ds(pl.multiple_of(c * r2, r2), r2)
            w2b[rows2, :] = st2[rows2, :].astype(BF16)
            return 0
        lax.fori_loop(0, n_conv, body, 0)

    @pl.when(i == 0)
    def _():
        for cp in copies(te_ref[0]):
            cp.start()
        for cp in copies(te_ref[0]):
            cp.wait()
        convert()

    @pl.when(used & (first_ref[i] == 1) & (nxt >= 0))
    def _():
        for cp in copies(nxt):
            cp.start(priority=1)

    @pl.when(used)
    def _():
        a, b = _unpack_bf16_pair(xs_ref[...])
        a, b = a.astype(BF16), b.astype(BF16)

        def up(w_ref):
            return (jnp.dot(a, w_ref[0:half, :], preferred_element_type=F32)
                    + jnp.dot(b, w_ref[half:2 * half, :], preferred_element_type=F32))
        h1 = up(w1b)
        act = (h1 * _sigmoid(h1) * up(w3b)).astype(BF16)
        o = jnp.dot(act, w2b[...], preferred_element_type=F32)
        o_ref[...] = _pack_bf16_pair(o[:, 0:half], o[:, half:2 * half])

    @pl.when(jnp.logical_not(used))
    def _():
        o_ref[...] = jnp.zeros_like(o_ref)

    @pl.when(used & (last_ref[i] == 1) & (nxt >= 0))
    def _():
        for cp in copies(nxt):
            cp.wait()
        convert()


def _experts(tile_expert, first, last, nxt, n_used, xs, w1, w3, w2, *, tme):
    rows, half = xs.shape
    n_e, d_model, d_exp = w1.shape
    grid_spec = pltpu.PrefetchScalarGridSpec(
        num_scalar_prefetch=5,
        grid=(rows // tme,),
        in_specs=[pl.BlockSpec((tme, half), lambda i, te, fi, la, nx, nu: (jnp.minimum(i, nu[0] - 1), 0)),
                  pl.BlockSpec(memory_space=pl.ANY), pl.BlockSpec(memory_space=pl.ANY),
                  pl.BlockSpec(memory_space=pl.ANY)],
        out_specs=pl.BlockSpec((tme, half), lambda i, *_: (i, 0)),
        scratch_shapes=[pltpu.VMEM((d_model, d_exp), BF16), pltpu.VMEM((d_model, d_exp), BF16),
                        pltpu.VMEM((d_exp, d_model), BF16),
                        pltpu.VMEM((d_model, d_exp), F32), pltpu.VMEM((d_model, d_exp), F32),
                        pltpu.VMEM((d_exp, d_model), F32),
                        pltpu.SemaphoreType.DMA((3,))],
    )
    return pl.pallas_call(
        functools.partial(_experts_body, half=half, n_conv=8),
        grid_spec=grid_spec,
        out_shape=jax.ShapeDtypeStruct((rows, half), I32),
        compiler_params=_cparams(1),
        name="experts",
    )(tile_expert, first, last, nxt, n_used, xs, w1, w3, w2)


def _sc_gather(table, idx, *, chunk):
    n_rows, width = idx.shape[0], table.shape[1]
    n_workers = SC_CORES * SC_SUBCORES
    per_w = n_rows // n_workers
    n_chunks = per_w // chunk
    assert per_w * n_workers == n_rows and n_chunks * chunk == per_w and n_chunks % 2 == 0
    mesh = plsc.VectorSubcoreMesh(core_axis_name="c", subcore_axis_name="s",
                                  num_cores=SC_CORES, num_subcores=SC_SUBCORES)

    @functools.partial(
        pl.kernel, mesh=mesh,
        out_type=jax.ShapeDtypeStruct((n_rows, width), table.dtype),
        scratch_types=[pltpu.VMEM((per_w,), I32), pltpu.VMEM((2, chunk, width), table.dtype),
                       pltpu.SemaphoreType.DMA((2,))],
    )
    def gather_kernel(table_hbm, idx_hbm, out_hbm, idx_v, rows_v, sem):
        wid = lax.axis_index("s") * SC_CORES + lax.axis_index("c")
        base = pl.multiple_of(wid * per_w, per_w)
        pltpu.sync_copy(idx_hbm.at[pl.ds(base, per_w)], idx_v)

        def gather(j, slot):
            off = pl.multiple_of(j * chunk, chunk)
            return pltpu.make_async_copy(table_hbm.at[idx_v.at[pl.ds(off, chunk)]],
                                         rows_v.at[slot], sem.at[slot])

        gather(0, 0).start()

        @pl.loop(0, n_chunks, step=2)
        def _(j):
            for slot in range(2):
                jj = j + slot
                gather(jj, slot).wait()

                @pl.when(jj + 1 < n_chunks)
                def _():
                    gather(jj + 1, 1 - slot).start()
                off = pl.multiple_of(base + jj * chunk, chunk)
                pltpu.sync_copy(rows_v.at[slot], out_hbm.at[pl.ds(off, chunk)])

    return gather_kernel(table, idx)


def _sc_inverse(dest_flat, n_rows):
    n_assign = dest_flat.shape[0]
    n_tok = n_assign // 2
    n_workers = SC_CORES * SC_SUBCORES
    per_w = n_rows // n_workers
    assert per_w * n_workers == n_rows and per_w % SC_LANES == 0 and n_assign % SC_LANES == 0
    mesh = plsc.VectorSubcoreMesh(core_axis_name="c", subcore_axis_name="s",
                                  num_cores=SC_CORES, num_subcores=SC_SUBCORES)

    @functools.partial(
        pl.kernel, mesh=mesh,
        out_type=jax.ShapeDtypeStruct((n_rows,), I32),
        scratch_types=[pltpu.VMEM((n_assign,), I32), pltpu.VMEM((per_w,), I32)],
        compiler_params=pltpu.CompilerParams(needs_layout_passes=False),
    )
    def inverse_kernel(dest_hbm, src_hbm, dest_v, src_v):
        wid = lax.axis_index("s") * SC_CORES + lax.axis_index("c")
        lo = pl.multiple_of(wid * per_w, per_w)
        pltpu.sync_copy(dest_hbm, dest_v)
        lane = lax.iota(I32, SC_LANES)

        @pl.loop(0, per_w, step=SC_LANES)
        def _(i):
            src_v[pl.ds(pl.multiple_of(i, SC_LANES), SC_LANES)] = lax.rem(lo + i + lane, n_tok)

        @pl.loop(0, n_assign, step=SC_LANES)
        def _(i):
            d = dest_v[pl.ds(pl.multiple_of(i, SC_LANES), SC_LANES)]
            a = i + lane
            tok = jnp.where(a >= n_tok, a - n_tok, a)
            mine = (d >= lo) & (d < lo + per_w)
            plsc.store_scatter(src_v, [d - lo], tok, mask=mine)

        pltpu.sync_copy(src_v, src_hbm.at[pl.ds(lo, per_w)])

    return inverse_kernel(dest_flat)


def _final_body(g1_ref, g2_ref, h1_ref, route_ref, g_ref, y_ref, *, half):
    route = route_ref[...]
    w1, w2 = route[:, 2:3], route[:, 3:4]
    a1, b1 = _unpack_bf16_pair(g1_ref[...])
    a2, b2 = _unpack_bf16_pair(g2_ref[...])
    h1 = h1_ref[...]
    ha = h1[:, 0:half] + (a1 * w1 + a2 * w2)
    hb = h1[:, half:2 * half] + (b1 * w1 + b2 * w2)
    ms = (jnp.sum(ha * ha, axis=-1, keepdims=True)
          + jnp.sum(hb * hb, axis=-1, keepdims=True)) / (2 * half)
    inv = lax.rsqrt(ms + EPS)
    y_ref[:, 0:half] = ha * inv * g_ref[:, 0:half]
    y_ref[:, half:2 * half] = hb * inv * g_ref[:, half:2 * half]


def _final(gathered, h1, route, g, *, tf, row0, n_rows):
    t, d_model = h1.shape
    half = d_model // 2
    tile0, tiles_t = row0 // tf, t // tf
    return pl.pallas_call(
        functools.partial(_final_body, half=half),
        grid=(n_rows // tf,),
        in_specs=[pl.BlockSpec((tf, half), lambda i: (tile0 + i, 0)),
                  pl.BlockSpec((tf, half), lambda i: (tiles_t + tile0 + i, 0)),
                  pl.BlockSpec((tf, d_model), lambda i: (tile0 + i, 0)),
                  pl.BlockSpec((tf, LANES), lambda i: (tile0 + i, 0)),
                  pl.BlockSpec((1, d_model), lambda i: (0, 0))],
        out_specs=pl.BlockSpec((tf, d_model), lambda i: (i, 0)),
        out_shape=jax.ShapeDtypeStruct((n_rows, d_model), F32),
        compiler_params=_cparams(1),
        name="final",
    )(gathered, gathered, h1, route, g)


def kernel(x_prompt, x_sample, meta, norm_mix_g, w_in, ssm_lam_re, ssm_lam_im, ssm_log_step, ssm_b_re, ssm_b_im, ssm_c_re, ssm_c_im, ssm_d, ssm_w_glu, conv_w_dw, conv_b_dw, conv_ln_g, conv_ln_b, conv_w_pw, w_out, norm_ffn_g, router_group_w, router_group_b, router_expert_w, router_expert_b, expert_w1, expert_w3, expert_w2, final_g):
    assert w_in.shape[0] == 1, "single-layer trunk"
    bp, lp, d_model = x_prompt.shape
    bs, ls, _ = x_sample.shape
    n_meta = meta.shape[0]
    d_ssm = ssm_d.shape[-1]
    d_conv = conv_b_dw.shape[-1]
    n_ch = ssm_b_re.shape[-1]
    n_g = ssm_b_re.shape[2]
    n_grp = router_group_w.shape[-1]
    n_exp = router_expert_w.shape[-1]
    assert n_meta == CHUNK and lp % CHUNK == 0 and ls % CHUNK == 0
    n_p, n_s = bp * lp, bs * ls
    t = n_p + n_s
    tm = min(256, lp, ls)
    assert lp % tm == 0 and ls % tm == 0

    xp2 = x_prompt.reshape(n_p, d_model)
    xs2 = x_sample.reshape(n_s, d_model)
    row = lambda a: a.reshape(1, -1)
    w_in_bf = w_in[0].astype(BF16)

    xg, c, sga, sgb = _inproj(xp2, xs2, row(norm_mix_g[0]), w_in_bf, d_ssm=d_ssm, d_conv=d_conv, tm=tm)
    xg_meta, c_meta = _meta_inproj(meta, row(norm_mix_g[0]), w_in_bf, d_ssm=d_ssm, d_conv=d_conv)

    assert n_ch == CHUNK and LANES % n_ch == 0 and n_g % (LANES // n_ch) == 0
    intra, bst, cst, coef = _sg_prep(ssm_lam_re[0], ssm_lam_im[0], ssm_log_step[0], ssm_b_re[0],
                                     ssm_b_im[0], ssm_c_re[0], ssm_c_im[0], ssm_d[0])
    nc = t // CHUNK
    n_blk = d_ssm // LANES
    rt = min(512, nc)
    sf, sb, init = _sg_mm_state(xg, xg_meta, bst, rt=rt)
    n_sub = sf.shape[0] // (n_blk * nc)
    cb = min(64, lp // CHUNK, ls // CHUNK)
    xf, xb = _sg_scan(sf.reshape(n_blk, nc, n_sub, LANES), sb.reshape(n_blk, nc, n_sub, LANES),
                      coef, init, cb=cb, chunks_pseq=lp // CHUNK, chunks_sseq=ls // CHUNK,
                      n_pchunks=n_p // CHUNK)
    z = _sg_mm_out(xg, intra, xf.reshape(sf.shape), xb.reshape(sb.shape), cst, rt=rt)

    assert n_grp + n_exp <= LANES // 2
    rw32 = jnp.pad(jnp.concatenate([router_group_w[0], router_expert_w[0]], axis=1),
                   ((0, 0), (0, LANES // 2 - n_grp - n_exp)))
    rw_hi = rw32.astype(BF16)
    rw = jnp.concatenate([rw_hi, (rw32 - rw_hi.astype(F32)).astype(BF16)], axis=1)
    rb = jnp.zeros((1, LANES), F32).at[0, 0:n_grp].set(router_group_b[0]).at[
        0, n_grp:n_grp + n_exp].set(router_expert_b[0])
    conv_w = conv_w_dw.shape[1]
    off = CHUNK - conv_w // 2
    assert 0 <= off and conv_w + off <= 2 * CHUNK
    taps = jnp.pad(conv_w_dw[0], ((off, -(conv_w + off) % 8), (0, 0)))
    h1, v, route, cnt = _mix(
        xp2, xs2, z, c, c_meta, sga, sgb, ssm_w_glu[0].astype(BF16), conv_w_pw[0].astype(BF16),
        w_out[0].astype(BF16), _conv_weights(taps), row(conv_b_dw[0]), row(conv_ln_g[0]),
        row(conv_ln_b[0]), row(norm_ffn_g[0]), rw, rb,
        tm=tm, p_seq=lp, s_seq=ls, n_grp=n_grp, n_exp=n_exp)

    tme = 256
    counts = cnt[0, 0:n_exp].astype(I32)
    tiles_e = (counts + tme - 1) // tme
    tile_end = jnp.cumsum(tiles_e)
    n_used = tile_end[-1]
    offs = jnp.zeros((1, LANES), F32).at[0, 0:n_exp].set(((tile_end - tiles_e) * tme).astype(F32))
    n_tiles = (2 * t) // tme + n_exp
    ids = jnp.arange(n_tiles, dtype=I32)

    def expert_of(tile):
        return jnp.minimum(jnp.sum((tile[:, None] >= tile_end[None, :]).astype(I32), axis=1), n_exp - 1)
    te_map = expert_of(jnp.minimum(ids, n_used - 1))
    onehot = te_map[:, None] == jnp.arange(n_exp, dtype=I32)[None, :]
    run_end = jnp.sum(jnp.where(onehot, tile_end[None, :], 0), axis=1)
    run_start = run_end - jnp.sum(jnp.where(onehot, tiles_e[None, :], 0), axis=1)
    valid = ids < n_used
    first = (valid & (ids == run_start)).astype(I32)
    last = (valid & (ids == run_end - 1)).astype(I32)
    nxt = jnp.where(valid & (run_end < n_used), expert_of(run_end), -1)

    dest = _positions(route, offs, te=min(512, t))
    idx = jnp.concatenate([dest[:, 0], dest[:, 1]])
    src = _sc_inverse(idx, n_tiles * tme)
    xs = _sc_gather(v, src, chunk=32)
    o = _experts(te_map, first, last, nxt, n_used.reshape(1), xs, expert_w1[0], expert_w3[0],
                 expert_w2[0], tme=tme)

    tf = min(256, n_p, n_s)
    gathered = _sc_gather(o, idx, chunk=32)
    fg = row(final_g)
    y_p = _final(gathered, h1, route, fg, tf=tf, row0=0, n_rows=n_p)
    y_s = _final(gathered, h1, route, fg, tf=tf, row0=n_p, n_rows=n_s)
    return (y_p.reshape(bp, lp, d_model), y_s.reshape(bs, ls, d_model))
```

```python
import functools

import jax
import jax.numpy as jnp
from jax import lax
from jax.experimental import pallas as pl
from jax.experimental.pallas import tpu as pltpu
from jax.experimental.pallas import tpu_sc as plsc

F32 = jnp.float32
BF16 = jnp.bfloat16
U32 = jnp.uint32
I32 = jnp.int32

EPS = 1e-6
LAM_RE_MAX = -1e-4
CHUNK = 16
LANES = 128
VMEM_LIMIT = 56 << 20
SC_CORES = 2
SC_SUBCORES = 16
SC_LANES = 16
HIGHEST = lax.Precision.HIGHEST


def _cparams(n_axes):
    return pltpu.CompilerParams(dimension_semantics=("arbitrary",) * n_axes,
                                vmem_limit_bytes=VMEM_LIMIT)


def _resident(shape):
    nd = len(shape)
    return pl.BlockSpec(shape, lambda *_: (0,) * nd, pipeline_mode=pl.Buffered(1))


def _sigmoid(x):
    return 1.0 / (1.0 + jnp.exp(-x))


def _rms(x, g):
    return x * lax.rsqrt(jnp.mean(x * x, axis=-1, keepdims=True) + EPS) * g


def _pack_bf16_pair(a, b):
    def rnd(x):
        u = pltpu.bitcast(x, U32)
        return (u + jnp.uint32(0x7FFF) + ((u >> 16) & jnp.uint32(1))) >> 16
    return pltpu.bitcast((rnd(a) << 16) | rnd(b), I32)


def _unpack_bf16_pair(p):
    p = pltpu.bitcast(p, U32)
    a = pltpu.bitcast(p & jnp.uint32(0xFFFF0000), F32)
    b = pltpu.bitcast(p << 16, F32)
    return a, b


def _inproj_body(xp_ref, xs_ref, g_ref, w_ref, xg_ref, c_ref, sga_ref, sgb_ref, usc, *,
                 tm, n_p_tiles, d_ssm, d_conv, d_model, col):
    i = pl.program_id(0)
    x = jnp.where(i < n_p_tiles, xp_ref[...], xs_ref[...])
    y = _rms(x, g_ref[...]).astype(BF16)

    def proj(lo, n):
        return jnp.dot(y, w_ref[:, lo:lo + n], preferred_element_type=F32)

    for k in range(d_ssm // col):
        u = proj(k * col, col)
        for b in range(col // LANES):
            usc[k * (col // LANES) + b] = u[:, b * LANES:(b + 1) * LANES]
    for b in range(d_ssm // LANES):
        for s in range(CHUNK):
            c0 = (b * CHUNK + s) * LANES
            xg_ref[:, c0:c0 + LANES] = usc[b, pl.ds(s, tm // CHUNK, stride=CHUNK), :].astype(BF16)
    for k in range(d_conv // col):
        cv = proj(d_ssm + k * col, col)
        cg = proj(d_ssm + d_conv + k * col, col)
        c_ref[:, k * col:(k + 1) * col] = (cv * _sigmoid(cg)).astype(BF16)
    base = d_ssm + 2 * d_conv
    for k in range(d_model // col):
        sga_ref[:, k * col:(k + 1) * col] = _sigmoid(proj(base + k * col, col)).astype(BF16)
        sgb_ref[:, k * col:(k + 1) * col] = _sigmoid(
            proj(base + d_model + k * col, col)).astype(BF16)


def _inproj(xp2, xs2, g, w_bf, *, d_ssm, d_conv, tm):
    n_p, d_model = xp2.shape
    n_s = xs2.shape[0]
    n_p_tiles, n_s_tiles = n_p // tm, n_s // tm
    t = n_p + n_s
    col = min(1024, d_ssm, d_conv, d_model)
    body = functools.partial(_inproj_body, tm=tm, n_p_tiles=n_p_tiles, d_ssm=d_ssm, d_conv=d_conv,
                             d_model=d_model, col=col)
    row = lambda i: (i, 0)
    return pl.pallas_call(
        body,
        grid=(n_p_tiles + n_s_tiles,),
        in_specs=[
            pl.BlockSpec((tm, d_model), lambda i: (jnp.minimum(i, n_p_tiles - 1), 0)),
            pl.BlockSpec((tm, d_model), lambda i: (jnp.maximum(i - n_p_tiles, 0), 0)),
            _resident((1, d_model)),
            _resident(w_bf.shape),
        ],
        out_specs=[pl.BlockSpec((tm // CHUNK, CHUNK * d_ssm), row), pl.BlockSpec((tm, d_conv), row),
                   pl.BlockSpec((tm, d_model), row), pl.BlockSpec((tm, d_model), row)],
        out_shape=[jax.ShapeDtypeStruct((t // CHUNK, CHUNK * d_ssm), BF16),
                   jax.ShapeDtypeStruct((t, d_conv), BF16),
                   jax.ShapeDtypeStruct((t, d_model), BF16), jax.ShapeDtypeStruct((t, d_model), BF16)],
        scratch_shapes=[pltpu.VMEM((d_ssm // LANES, tm, LANES), F32)],
        compiler_params=_cparams(1),
        name="inproj",
    )(xp2, xs2, g, w_bf)


def _meta_body(m_ref, g_ref, w_ref, xg_ref, c_ref, *, d_ssm, d_conv):
    y = _rms(m_ref[...], g_ref[...]).astype(BF16)
    u = jnp.dot(y, w_ref[:, 0:d_ssm], preferred_element_type=F32)
    first = lax.broadcasted_iota(I32, (CHUNK, LANES), 0) == 0
    for b in range(d_ssm // LANES):
        for s in range(CHUNK):
            c0 = (b * CHUNK + s) * LANES
            piece = jnp.broadcast_to(u[s:s + 1, b * LANES:(b + 1) * LANES], (CHUNK, LANES))
            xg_ref[:, c0:c0 + LANES] = jnp.where(first, piece, 0.0).astype(BF16)
    cv = jnp.dot(y, w_ref[:, d_ssm:d_ssm + d_conv], preferred_element_type=F32)
    cg = jnp.dot(y, w_ref[:, d_ssm + d_conv:d_ssm + 2 * d_conv], preferred_element_type=F32)
    c_ref[...] = (cv * _sigmoid(cg)).astype(BF16)


def _meta_inproj(meta, g, w_bf, *, d_ssm, d_conv):
    n_meta, d_model = meta.shape
    ncol = d_ssm + 2 * d_conv
    return pl.pallas_call(
        functools.partial(_meta_body, d_ssm=d_ssm, d_conv=d_conv),
        grid=(1,),
        in_specs=[pl.BlockSpec((n_meta, d_model), lambda i: (0, 0)),
                  pl.BlockSpec((1, d_model), lambda i: (0, 0)),
                  pl.BlockSpec((d_model, ncol), lambda i: (0, 0))],
        out_specs=[pl.BlockSpec((CHUNK, CHUNK * d_ssm), lambda i: (0, 0)),
                   pl.BlockSpec((n_meta, d_conv), lambda i: (0, 0))],
        out_shape=[jax.ShapeDtypeStruct((CHUNK, CHUNK * d_ssm), BF16),
                   jax.ShapeDtypeStruct((n_meta, d_conv), BF16)],
        compiler_params=_cparams(1),
        name="meta_inproj",
    )(meta, g, w_bf)


def _cmul(ar, ai, br, bi):
    return ar * br - ai * bi, ar * bi + ai * br


def _discretize(lam_re, lam_im, log_step):
    lr = jnp.minimum(lam_re, LAM_RE_MAX)
    dt = jnp.exp(log_step)
    mag = jnp.exp(lr * dt)
    ar = mag * jnp.cos(lam_im * dt)
    ai = mag * jnp.sin(lam_im * dt)
    den = lr * lr + lam_im * lam_im
    nr = ar - 1.0
    fr = (nr * lr + ai * lam_im) / den
    fi = (ai * lr - nr * lam_im) / den
    return ar, ai, fr, fi


def _cpow(ar, ai, k, nbits, shape):
    pr = jnp.ones(shape, F32)
    pi = jnp.zeros(shape, F32)
    br = jnp.broadcast_to(ar, shape)
    bi = jnp.broadcast_to(ai, shape)
    kk = jnp.broadcast_to(k, shape)
    for b in range(nbits):
        sel = ((kk >> b) & 1) == 1
        nr, ni = _cmul(pr, pi, br, bi)
        pr = jnp.where(sel, nr, pr)
        pi = jnp.where(sel, ni, pi)
        br, bi = _cmul(br, bi, br, bi)
    return pr, pi


def _spread(x, n_ch, shift):
    rows, w = x.shape
    lane = lax.broadcasted_iota(I32, (rows, LANES), 1)
    keep = (lane >= shift) & (lane < shift + n_ch)
    out = []
    for b in range(w // n_ch):
        src = (b * n_ch // LANES) * LANES
        amount = lax.rem(shift + LANES - (b * n_ch) % LANES, LANES)
        out.append(jnp.where(keep, pltpu.roll(x[:, src:src + LANES], amount, 1), 0.0))
    return jnp.concatenate(out, axis=1)


def _tile_lanes(x, n_ch, width):
    span = n_ch
    while span < LANES:
        x = x + pltpu.roll(x, span, 1)
        span *= 2
    return jnp.concatenate([x] * (width // LANES), axis=1)


def _disc_body(lam_re_ref, lam_im_ref, ls_ref, ar_ref, ai_ref, fr_ref, fi_ref):
    ar, ai, fr, fi = _discretize(lam_re_ref[...], lam_im_ref[...], ls_ref[...])
    ar_ref[...] = ar
    ai_ref[...] = ai
    fr_ref[...] = fr
    fi_ref[...] = fi


def _ssm_discretize(lam_re, lam_im, log_step):
    shape = lam_re.shape
    rows = shape[0] * shape[1]
    flat = lambda x: x.reshape(rows, shape[2])
    ls = jnp.broadcast_to(log_step[..., None], shape)
    full = pl.BlockSpec((rows, shape[2]), lambda i: (0, 0))
    out = pl.pallas_call(
        _disc_body,
        grid=(1,),
        in_specs=[full, full, full],
        out_specs=[full] * 4,
        out_shape=[jax.ShapeDtypeStruct((rows, shape[2]), F32)] * 4,
        compiler_params=_cparams(1),
        name="ssm_disc",
    )(flat(lam_re), flat(lam_im), flat(ls))
    return [x.reshape(shape) for x in out]


def _sg_prep_body(ar_c, ai_c, fr_c, fi_c, ct_re, ct_im, bt_re, bt_im,
                  ar_r, ai_r, fr_r, fi_r, btr_re, btr_im, dpad,
                  m_ref, b_ref, c_ref, aq_ref, *, n_state, n_ch, per_step):
    per_blk = LANES // n_ch
    first = (pl.program_id(0) * per_step) % per_blk
    for j in range(per_step):
        _sg_prep_group(j, first + j, ar_c, ai_c, fr_c, fi_c, ct_re, ct_im, bt_re, bt_im,
                       ar_r, ai_r, fr_r, fi_r, btr_re, btr_im, dpad,
                       m_ref, b_ref, c_ref, aq_ref, n_state=n_state, n_ch=n_ch)


def _sg_prep_group(j, gl, ar_c, ai_c, fr_c, fi_c, ct_re, ct_im, bt_re, bt_im,
                   ar_r, ai_r, fr_r, fi_r, btr_re, btr_im, dpad,
                   m_ref, b_ref, c_ref, aq_ref, *, n_state, n_ch):
    q = CHUNK
    width = q * n_ch
    per_blk = LANES // n_ch
    rows = slice(j * n_ch, (j + 1) * n_ch)
    shift = gl * n_ch
    kblk = lax.broadcasted_iota(I32, (1, width), 1) // n_ch
    strips = []
    for d in range(2):
        ar, ai, fr, fi = ar_c[d, j], ai_c[d, j], fr_c[d, j], fi_c[d, j]
        kexp = kblk if d == 0 else (q - 1) - kblk
        wr, wi = _cpow(ar, ai, kexp, 4, (n_state, width))
        gcr, gci = _cmul(_tile_lanes(ct_re[d, j], n_ch, width),
                         _tile_lanes(ct_im[d, j], n_ch, width), wr, wi)
        gfr, gfi = _cmul(gcr, gci, fr, fi)
        strips.append(jnp.dot(bt_re[d, j], gfr, precision=HIGHEST, preferred_element_type=F32)
                      - jnp.dot(bt_im[d, j], gfi, precision=HIGHEST, preferred_element_type=F32))
        g1r, g1i = _cmul(gcr, gci, ar, ai)
        for comp, val in ((0, g1r), (1, -g1i)):
            c_ref[2 * d + comp, j] = _spread(val, n_ch, shift).astype(BF16)

    zf, zb = strips
    zero = jnp.zeros((n_ch, width), F32)
    z512 = jnp.concatenate([zb, zero], axis=1) + pltpu.roll(
        jnp.concatenate([zf, zero], axis=1), (q - 1) * n_ch, 1)
    row = lax.broadcasted_iota(I32, (n_ch, 2 * width), 0)
    lane = lax.broadcasted_iota(I32, (n_ch, 2 * width), 1)
    z512 = z512 + jnp.where(lane - (q - 1) * n_ch == row, dpad[j], 0.0)
    zwide = _spread(z512, n_ch, shift)
    for s in range(q):
        lo = (q - 1 - s) * LANES
        m_ref[s, rows, :] = zwide[:, lo:lo + q * LANES].astype(BF16)

    parmask = (lax.broadcasted_iota(I32, (1, 2 * n_state), 1) // n_state) == gl % 2
    pieces = [[None] * 4 for _ in range(q)]
    for d in range(2):
        ar, ai, fr, fi = ar_r[d, j], ai_r[d, j], fr_r[d, j], fi_r[d, j]
        pw = [(jnp.ones_like(ar), jnp.zeros_like(ar))]
        for _ in range(q):
            pw.append(_cmul(pw[-1][0], pw[-1][1], ar, ai))
        for s in range(q):
            e = (q - 1 - s) if d == 0 else s
            cr, ci = _cmul(fr, fi, pw[e][0], pw[e][1])
            br, bi = _cmul(btr_re[d, j], btr_im[d, j], cr, ci)
            pieces[s][2 * d] = jnp.where(parmask, br, 0.0)
            pieces[s][2 * d + 1] = jnp.where(parmask, bi, 0.0)
        aq_ref[j, :, 2 * d * n_state:(2 * d + 1) * n_state] = pw[q][0][:, 0:n_state]
        aq_ref[j, :, (2 * d + 1) * n_state:(2 * d + 2) * n_state] = pw[q][1][:, 0:n_state]
    n_pair = per_blk // 2
    for s in range(q):
        cols = []
        for seg in range(4):
            for blk in range(n_pair):
                cols.append(jnp.where(gl // 2 == blk, pieces[s][seg], 0.0))
        b_ref[s, rows, :] = jnp.concatenate(cols, axis=1).astype(BF16)


def _sg_prep(lam_re, lam_im, log_step, b_re, b_im, c_re, c_im, d_skip):
    _, n_g, n_state, n_ch = b_re.shape
    q = CHUNK
    width = q * n_ch
    per_blk = LANES // n_ch
    n_blk = n_g // per_blk
    sw = 4 * per_blk * n_state
    dup = lambda x: jnp.concatenate([x, x], axis=-1)
    lane_pad = lambda x: jnp.pad(x, ((0, 0),) * 3 + ((0, LANES - n_ch),))
    disc = _ssm_discretize(lam_re, lam_im, log_step)
    disc_c = [x[..., None] for x in disc]
    disc_r = [dup(x)[:, :, None, :] for x in disc]
    ct_re = lane_pad(jnp.swapaxes(c_re, -1, -2))
    ct_im = lane_pad(jnp.swapaxes(c_im, -1, -2))
    bt_re, bt_im = jnp.swapaxes(b_re, -1, -2), jnp.swapaxes(b_im, -1, -2)
    btr_re, btr_im = dup(bt_re), dup(bt_im)
    dpad = jnp.pad(d_skip.reshape(n_g, 1, n_ch), ((0, 0), (0, 0), ((q - 1) * n_ch, width)))

    per_step = 2
    spb = per_blk // per_step

    def dspec(shape):
        return pl.BlockSpec((2, per_step) + shape, lambda g: (0, g, 0, 0))

    body = functools.partial(_sg_prep_body, n_state=n_state, n_ch=n_ch, per_step=per_step)
    m4, b4, c5, aq = pl.pallas_call(
        body,
        grid=(n_g // per_step,),
        in_specs=[dspec((n_state, 1))] * 4 + [
                  dspec((n_state, LANES)), dspec((n_state, LANES)),
                  dspec((n_ch, n_state)), dspec((n_ch, n_state))] + [dspec((1, 2 * n_state))] * 4 + [
                  dspec((n_ch, 2 * n_state)), dspec((n_ch, 2 * n_state)),
                  pl.BlockSpec((per_step, 1, 2 * width), lambda g: (g, 0, 0))],
        out_specs=[
            pl.BlockSpec((None, q, per_step * n_ch, q * LANES), lambda g: (g // spb, 0, g % spb, 0)),
            pl.BlockSpec((None, q, per_step * n_ch, sw), lambda g: (g // spb, 0, g % spb, 0)),
            pl.BlockSpec((None, 4, per_step, n_state, q * LANES),
                         lambda g: (g // spb, 0, g % spb, 0, 0)),
            pl.BlockSpec((per_step, 1, 4 * n_state), lambda g: (g, 0, 0))],
        out_shape=[jax.ShapeDtypeStruct((n_blk, q, LANES, q * LANES), BF16),
                   jax.ShapeDtypeStruct((n_blk, q, LANES, sw), BF16),
                   jax.ShapeDtypeStruct((n_blk, 4, per_blk, n_state, q * LANES), BF16),
                   jax.ShapeDtypeStruct((n_g, 1, 4 * n_state), F32)],
        compiler_params=_cparams(1),
        name="ssm_prep",
    )(*disc_c, ct_re, ct_im, bt_re, bt_im, *disc_r, btr_re, btr_im, dpad)
    intra = m4.reshape(n_blk, q * LANES, q * LANES)
    bst = b4.reshape(n_blk, q * LANES, sw)
    cst = c5.reshape(n_blk, sw, q * LANES)
    a = aq.reshape(n_blk, per_blk // 2, 2, 2, 2, n_state)
    a = a.transpose(3, 4, 0, 1, 2, 5).reshape(2, 2, n_blk, per_blk // 2, LANES)
    coef = []
    for d in range(2):
        re, im = a[d, 0], a[d, 1]
        coef.append((jnp.concatenate([re, re], axis=1), jnp.concatenate([-im, im], axis=1)))
    return intra, bst, cst, coef


def _sg_mm_state_body(x_ref, xm_ref, b_ref, sf_ref, sb_ref, init_ref, *, rt, n_sub):
    r = jnp.dot(x_ref[...], b_ref[...], preferred_element_type=F32)
    for k in range(n_sub):
        sf_ref[pl.ds(k, rt, stride=n_sub), :] = r[:, k * LANES:(k + 1) * LANES]
        sb_ref[pl.ds(k, rt, stride=n_sub), :] = r[:, (n_sub + k) * LANES:(n_sub + k + 1) * LANES]

    @pl.when(pl.program_id(1) == 0)
    def _():
        r0 = jnp.dot(xm_ref[...], b_ref[...], preferred_element_type=F32)
        for k in range(n_sub):
            init_ref[k:k + 1, :] = r0[0:1, k * LANES:(k + 1) * LANES]


def _sg_mm_state(xg, xg_meta, bst, *, rt):
    nc = xg.shape[0]
    n_blk, kw, sw = bst.shape
    n_sub = sw // (2 * LANES)
    n_rt = nc // rt
    body = functools.partial(_sg_mm_state_body, rt=rt, n_sub=n_sub)
    dense = jax.ShapeDtypeStruct((n_blk * nc * n_sub, LANES), F32)
    dspec = pl.BlockSpec((rt * n_sub, LANES), lambda b, i: (b * n_rt + i, 0))
    return pl.pallas_call(
        body,
        grid=(n_blk, n_rt),
        in_specs=[pl.BlockSpec((rt, kw), lambda b, i: (i, b)),
                  pl.BlockSpec((CHUNK, kw), lambda b, i: (0, b)),
                  pl.BlockSpec((None, kw, sw), lambda b, i: (b, 0, 0))],
        out_specs=[dspec, dspec, pl.BlockSpec((None, n_sub, LANES), lambda b, i: (b, 0, 0))],
        out_shape=[dense, dense, jax.ShapeDtypeStruct((n_blk, n_sub, LANES), F32)],
        compiler_params=_cparams(2),
        name="ssm_state",
    )(xg, xg_meta, bst)


def _sg_scan_body(sf_ref, sb_ref, a1f_ref, a2f_ref, a1b_ref, a2b_ref, init_ref, xf_ref, xb_ref,
                  stf, stb, *, cb, n_blk, half, n_pblk, blk_pseq, blk_sseq):
    j = pl.program_id(0)
    pos = jnp.where(j < n_pblk, lax.rem(j, blk_pseq), lax.rem(jnp.maximum(j - n_pblk, 0), blk_sseq))

    @pl.when(pos == 0)
    def _():
        stf[...] = init_ref[...]
        stb[...] = jnp.zeros_like(stb)

    def body(i, carry):
        ef, eb = carry
        ib = cb - 1 - i
        nf, nb = [], []
        for g in range(n_blk):
            xf_ref[g, i] = ef[g]
            xb_ref[g, ib] = eb[g]
            nf.append(a1f_ref[g] * ef[g] + a2f_ref[g] * pltpu.roll(ef[g], half, 0) + sf_ref[g, i])
            nb.append(a1b_ref[g] * eb[g] + a2b_ref[g] * pltpu.roll(eb[g], half, 0) + sb_ref[g, ib])
        return tuple(nf), tuple(nb)

    ef, eb = lax.fori_loop(0, cb, body, (tuple(stf[g] for g in range(n_blk)),
                                         tuple(stb[g] for g in range(n_blk))))
    for g in range(n_blk):
        stf[g] = ef[g]
        stb[g] = eb[g]


def _sg_scan(sf, sb, coef, init, *, cb, chunks_pseq, chunks_sseq, n_pchunks):
    n_blk, nc, n_sub, _ = sf.shape
    n_pblk, blk_pseq, blk_sseq = n_pchunks // cb, chunks_pseq // cb, chunks_sseq // cb

    def bwd_block(j):
        in_p = j < n_pblk
        pos = jnp.where(in_p, lax.rem(j, blk_pseq), lax.rem(jnp.maximum(j - n_pblk, 0), blk_sseq))
        ln = jnp.where(in_p, blk_pseq, blk_sseq)
        return j - pos + ln - 1 - pos

    body = functools.partial(_sg_scan_body, cb=cb, n_blk=n_blk, half=n_sub // 2, n_pblk=n_pblk,
                             blk_pseq=blk_pseq, blk_sseq=blk_sseq)
    fwd = pl.BlockSpec((n_blk, cb, n_sub, LANES), lambda j: (0, j, 0, 0))
    bwd = pl.BlockSpec((n_blk, cb, n_sub, LANES), lambda j: (0, bwd_block(j), 0, 0))
    small = pl.BlockSpec((n_blk, n_sub, LANES), lambda j: (0, 0, 0))
    return pl.pallas_call(
        body,
        grid=(nc // cb,),
        in_specs=[fwd, bwd, small, small, small, small, small],
        out_specs=[fwd, bwd],
        out_shape=[jax.ShapeDtypeStruct(sf.shape, F32), jax.ShapeDtypeStruct(sb.shape, F32)],
        scratch_shapes=[pltpu.VMEM((n_blk, n_sub, LANES), F32), pltpu.VMEM((n_blk, n_sub, LANES), F32)],
        compiler_params=_cparams(1),
        name="ssm_scan",
    )(sf, sb, coef[0][0], coef[0][1], coef[1][0], coef[1][1], init)


def _gelu_tanh(x):
    return 0.5 * x * (1.0 + jnp.tanh(0.7978845608028654 * (x + 0.044715 * (x * x * x))))


def _sg_mm_out_body(x_ref, m_ref, xf_ref, xb_ref, c_ref, z_ref, *, rt, n_sub):
    cols = [xf_ref[pl.ds(k, rt, stride=n_sub), :] for k in range(n_sub)]
    cols += [xb_ref[pl.ds(k, rt, stride=n_sub), :] for k in range(n_sub)]
    state = jnp.concatenate(cols, axis=1).astype(BF16)
    y = (jnp.dot(x_ref[...], m_ref[...], preferred_element_type=F32)
         + jnp.dot(state, c_ref[...], preferred_element_type=F32))
    z_ref[...] = _gelu_tanh(y).astype(BF16)


def _sg_mm_out(xg, intra, xf, xb, cst, *, rt):
    nc = xg.shape[0]
    n_blk, sw, kw = cst.shape
    n_sub = sw // (2 * LANES)
    n_rt = nc // rt
    dspec = pl.BlockSpec((rt * n_sub, LANES), lambda b, i: (b * n_rt + i, 0))
    return pl.pallas_call(
        functools.partial(_sg_mm_out_body, rt=rt, n_sub=n_sub),
        grid=(n_blk, n_rt),
        in_specs=[pl.BlockSpec((rt, kw), lambda b, i: (i, b)),
                  pl.BlockSpec((None, kw, kw), lambda b, i: (b, 0, 0)), dspec, dspec,
                  pl.BlockSpec((None, sw, kw), lambda b, i: (b, 0, 0))],
        out_specs=pl.BlockSpec((rt, kw), lambda b, i: (i, b)),
        out_shape=jax.ShapeDtypeStruct((nc, n_blk * kw), BF16),
        compiler_params=_cparams(2),
        name="ssm_out",
    )(xg, intra, xf, xb, cst)


def _mix_body(xp_ref, xs_ref, z_ref, c_ref, cprev_ref, cnext_ref, cmeta_ref, sga_ref, sgb_ref,
              wglu_ref, wpw_ref, wout_ref, wconv_ref, bdw_ref, lng_ref, lnb_ref, gffn_ref,
              rw_ref, rb_ref,
              h1_ref, v_ref, route_ref, cnt_ref,
              cw_ref, conv_ref, zsc_ref, *,
              tm, n_p_tiles, tiles_per_pseq, tiles_per_sseq, d_model, d_ssm, d_conv,
              n_grp, n_exp, exp_per_grp):
    i = pl.program_id(0)
    in_prompt = i < n_p_tiles
    x = jnp.where(in_prompt, xp_ref[...], xs_ref[...])
    pos_p = lax.rem(i, tiles_per_pseq)
    pos_s = lax.rem(jnp.maximum(i - n_p_tiles, 0), tiles_per_sseq)
    is_start = jnp.where(in_prompt, pos_p == 0, pos_s == 0)
    is_end = jnp.where(in_prompt, pos_p == tiles_per_pseq - 1, pos_s == tiles_per_sseq - 1)

    halo = CHUNK
    cw_ref[0:halo, :] = jnp.where(is_start, cmeta_ref[...], cprev_ref[...]).astype(F32)
    cw_ref[halo:halo + tm, :] = c_ref[...].astype(F32)
    cw_ref[halo + tm:2 * halo + tm, :] = jnp.where(is_end, 0.0, cnext_ref[...].astype(F32))
    sub = 8
    nq = wconv_ref.shape[1] // LANES
    for lc in range(d_conv // LANES):
        ls = slice(lc * LANES, (lc + 1) * LANES)
        stacked = jnp.concatenate(
            [cw_ref[sub * q:sub * q + tm + sub, ls] for q in range(nq)], axis=1).astype(BF16)
        part = jnp.dot(stacked, wconv_ref[lc], preferred_element_type=F32)
        out = part[0:tm, 0:LANES]
        for r in range(1, sub):
            out = out + part[r:r + tm, r * LANES:(r + 1) * LANES]
        conv_ref[:, ls] = out
    cc = conv_ref[...] + bdw_ref[...]
    mu = jnp.mean(cc, axis=-1, keepdims=True)
    var = jnp.mean(jnp.square(cc - mu), axis=-1, keepdims=True)
    cc = (cc - mu) * lax.rsqrt(var + EPS) * lng_ref[...] + lnb_ref[...]
    cc = (cc * _sigmoid(cc)).astype(BF16)
    y_b = jnp.dot(cc, wpw_ref[...], preferred_element_type=F32)

    for b in range(d_ssm // LANES):
        for s in range(CHUNK):
            c0 = (b * CHUNK + s) * LANES
            zsc_ref[b, pl.ds(s, tm // CHUNK, stride=CHUNK), :] = z_ref[:, c0:c0 + LANES].astype(F32)
    z = jnp.concatenate([zsc_ref[b] for b in range(d_ssm // LANES)], axis=1).astype(BF16)
    va = jnp.dot(z, wglu_ref[:, 0:d_model], preferred_element_type=F32)
    ga = jnp.dot(z, wglu_ref[:, d_model:2 * d_model], preferred_element_type=F32)
    y_a = va * _sigmoid(ga)
    merged = (sga_ref[...].astype(F32) * y_a + sgb_ref[...].astype(F32) * y_b).astype(BF16)
    h1 = x + jnp.dot(merged, wout_ref[...], preferred_element_type=F32)
    h1_ref[...] = h1
    v = _rms(h1, gffn_ref[...])
    half = d_model // 2
    v_ref[...] = _pack_bf16_pair(v[:, 0:half], v[:, half:d_model])

    v_hi = v.astype(BF16)
    v_lo = (v - v_hi.astype(F32)).astype(BF16)
    acc = (jnp.dot(v_hi, rw_ref[...], preferred_element_type=F32)
           + jnp.dot(v_lo, rw_ref[...], preferred_element_type=F32))
    logits = acc + pltpu.roll(acc, LANES // 2, 1) + rb_ref[...]
    lane = lax.broadcasted_iota(I32, (tm, LANES), 1).astype(F32)
    big = jnp.float32(1e9)
    neg = jnp.float32(-jnp.inf)
    gmask = lane < n_grp
    lg = jnp.where(gmask, logits, neg)
    gmax = jnp.max(lg, axis=-1, keepdims=True)
    grp = jnp.min(jnp.where(lg == gmax, lane, big), axis=-1, keepdims=True)
    p_grp = 1.0 / jnp.sum(jnp.where(gmask, jnp.exp(logits - gmax), 0.0), axis=-1, keepdims=True)
    lo = n_grp + grp * exp_per_grp
    emask = (lane >= lo) & (lane < lo + exp_per_grp)
    le = jnp.where(emask, logits, neg)
    m1 = jnp.max(le, axis=-1, keepdims=True)
    i1 = jnp.min(jnp.where(le == m1, lane, big), axis=-1, keepdims=True)
    le2 = jnp.where(lane == i1, neg, le)
    m2 = jnp.max(le2, axis=-1, keepdims=True)
    i2 = jnp.min(jnp.where(le2 == m2, lane, big), axis=-1, keepdims=True)
    t = jnp.exp(m2 - m1)
    w1 = 1.0 / (1.0 + t)
    e1 = i1 - n_grp
    e2 = i2 - n_grp
    route_ref[...] = jnp.where(lane == 0, e1, jnp.where(lane == 1, e2, jnp.where(
        lane == 2, p_grp * w1, jnp.where(lane == 3, p_grp * (t * w1), 0.0))))

    @pl.when(i == 0)
    def _():
        cnt_ref[...] = jnp.zeros_like(cnt_ref)

    hot = jnp.where((lane == e1) | (lane == e2), 1.0, 0.0)
    cnt_ref[...] += jnp.sum(hot, axis=0, keepdims=True)


def _conv_weights_body(taps_ref, w_ref, *, n_taps):
    eye = (lax.broadcasted_iota(I32, (LANES, LANES), 0)
           == lax.broadcasted_iota(I32, (LANES, LANES), 1))
    for p in range(n_taps):
        q, r = divmod(p, 8)
        diag = jnp.where(eye, taps_ref[p:p + 1, :], 0.0)
        w_ref[q * LANES:(q + 1) * LANES, r * LANES:(r + 1) * LANES] = diag.astype(BF16)


def _conv_weights(taps):
    n_taps, d_conv = taps.shape
    nq = n_taps // 8
    return pl.pallas_call(
        functools.partial(_conv_weights_body, n_taps=n_taps),
        grid=(d_conv // LANES,),
        in_specs=[pl.BlockSpec((n_taps, LANES), lambda b: (0, b))],
        out_specs=pl.BlockSpec((None, nq * LANES, 8 * LANES), lambda b: (b, 0, 0)),
        out_shape=jax.ShapeDtypeStruct((d_conv // LANES, nq * LANES, 8 * LANES), BF16),
        compiler_params=_cparams(1),
        name="conv_weights",
    )(taps)


def _mix(xp2, xs2, z, c, c_meta, sga, sgb, wglu, wpw, wout, wconv, bdw, lng, lnb, gffn, rw, rb, *,
         tm, p_seq, s_seq, n_grp, n_exp):
    n_p, d_model = xp2.shape
    n_s = xs2.shape[0]
    t = n_p + n_s
    d_ssm, d_conv = z.shape[1] // CHUNK, c.shape[1]
    n_p_tiles, n_s_tiles = n_p // tm, n_s // tm
    hpt = tm // CHUNK
    n_hblk = t // CHUNK
    body = functools.partial(
        _mix_body, tm=tm, n_p_tiles=n_p_tiles, tiles_per_pseq=p_seq // tm,
        tiles_per_sseq=s_seq // tm, d_model=d_model, d_ssm=d_ssm, d_conv=d_conv,
        n_grp=n_grp, n_exp=n_exp, exp_per_grp=n_exp // n_grp)
    row = lambda i: (i, 0)
    return pl.pallas_call(
        body,
        grid=(n_p_tiles + n_s_tiles,),
        in_specs=[
            pl.BlockSpec((tm, d_model), lambda i: (jnp.minimum(i, n_p_tiles - 1), 0)),
            pl.BlockSpec((tm, d_model), lambda i: (jnp.maximum(i - n_p_tiles, 0), 0)),
            pl.BlockSpec((tm // CHUNK, CHUNK * d_ssm), row),
            pl.BlockSpec((tm, d_conv), row),
            pl.BlockSpec((CHUNK, d_conv), lambda i: (jnp.maximum(i * hpt - 1, 0), 0)),
            pl.BlockSpec((CHUNK, d_conv), lambda i: (jnp.minimum((i + 1) * hpt, n_hblk - 1), 0)),
            _resident(c_meta.shape),
            pl.BlockSpec((tm, d_model), row),
            pl.BlockSpec((tm, d_model), row),
            _resident(wglu.shape), _resident(wpw.shape), _resident(wout.shape),
            _resident(wconv.shape), _resident(bdw.shape), _resident(lng.shape), _resident(lnb.shape),
            _resident(gffn.shape), _resident(rw.shape), _resident(rb.shape),
        ],
        out_specs=[pl.BlockSpec((tm, d_model), row), pl.BlockSpec((tm, d_model // 2), row),
                   pl.BlockSpec((tm, LANES), row), pl.BlockSpec((1, LANES), lambda i: (0, 0))],
        out_shape=[jax.ShapeDtypeStruct((t, d_model), F32),
                   jax.ShapeDtypeStruct((t, d_model // 2), I32),
                   jax.ShapeDtypeStruct((t, LANES), F32),
                   jax.ShapeDtypeStruct((1, LANES), F32)],
        scratch_shapes=[pltpu.VMEM((tm + 2 * CHUNK, d_conv), F32), pltpu.VMEM((tm, d_conv), F32),
                        pltpu.VMEM((d_ssm // LANES, tm, LANES), F32)],
        compiler_params=_cparams(1),
        name="mix",
    )(xp2, xs2, z, c, c, c, c_meta, sga, sgb, wglu, wpw, wout, wconv, bdw, lng, lnb, gffn, rw, rb)


def _positions_body(route_ref, offs_ref, dest_ref, carry_ref, *, te):
    @pl.when(pl.program_id(0) == 0)
    def _():
        carry_ref[...] = jnp.zeros_like(carry_ref)

    lane = lax.broadcasted_iota(I32, (te, LANES), 1).astype(F32)
    r = route_ref[...]
    oh1 = lane == r[:, 0:1]
    oh2 = lane == r[:, 1:2]
    both = jnp.where(oh1 | oh2, 1.0, 0.0)
    tri = jnp.where(lax.broadcasted_iota(I32, (te, te), 0) > lax.broadcasted_iota(I32, (te, te), 1),
                    1.0, 0.0).astype(BF16)
    before = jnp.dot(tri, both.astype(BF16), preferred_element_type=F32)
    base = before + carry_ref[...] + offs_ref[...]
    d1 = jnp.sum(jnp.where(oh1, base, 0.0), axis=-1, keepdims=True)
    d2 = jnp.sum(jnp.where(oh2, base, 0.0), axis=-1, keepdims=True)
    dest_ref[...] = jnp.where(lane == 0, d1, jnp.where(lane == 1, d2, 0.0)).astype(I32)
    carry_ref[...] += jnp.sum(both, axis=0, keepdims=True)


def _positions(route, offs, *, te):
    t = route.shape[0]
    return pl.pallas_call(
        functools.partial(_positions_body, te=te),
        grid=(t // te,),
        in_specs=[pl.BlockSpec((te, LANES), lambda i: (i, 0)),
                  pl.BlockSpec((1, LANES), lambda i: (0, 0))],
        out_specs=pl.BlockSpec((te, LANES), lambda i: (i, 0)),
        out_shape=jax.ShapeDtypeStruct((t, LANES), I32),
        scratch_shapes=[pltpu.VMEM((1, LANES), F32)],
        compiler_params=_cparams(1),
        name="positions",
    )(route, offs)


def _experts_body(te_ref, first_ref, last_ref, nxt_ref, short_ref, nused_ref,
                  xs_ref, w1_hbm, w3_hbm, w2_hbm, o_ref,
                  w1b, w3b, w2b, st1, st3, st2, sem, *, half, n_conv):
    i = pl.program_id(0)
    used = i < nused_ref[0]
    nxt = nxt_ref[i]

    def copies(e):
        return (pltpu.make_async_copy(w1_hbm.at[e], st1, sem.at[0]),
                pltpu.make_async_copy(w3_hbm.at[e], st3, sem.at[1]),
                pltpu.make_async_copy(w2_hbm.at[e], st2, sem.at[2]))

    def convert():
        r13 = st1.shape[0] // n_conv
        r2 = st2.shape[0] // n_conv

        def body(c, _):
            rows = pl.ds(pl.multiple_of(c * r13, r13), r13)
            w1b[rows, :] = st1[rows, :].astype(BF16)
            w3b[rows, :] = st3[rows, :].astype(BF16)
            rows2 = pl.ds(pl.multiple_of(c * r2, r2), r2)
            w2b[rows2, :] = st2[rows2, :].astype(BF16)
            return 0
        lax.fori_loop(0, n_conv, body, 0)

    @pl.when(i == 0)
    def _():
        for cp in copies(te_ref[0]):
            cp.start()
        for cp in copies(te_ref[0]):
            cp.wait()
        convert()

    @pl.when(used & (first_ref[i] == 1) & (nxt >= 0))
    def _():
        for cp in copies(nxt):
            cp.start(priority=1)

    def swiglu(n_rows):
        a, b = _unpack_bf16_pair(xs_ref[0:n_rows, :])
        a, b = a.astype(BF16), b.astype(BF16)

        def up(w_ref):
            return (jnp.dot(a, w_ref[0:half, :], preferred_element_type=F32)
                    + jnp.dot(b, w_ref[half:2 * half, :], preferred_element_type=F32))
        h1 = up(w1b)
        act = (h1 * _sigmoid(h1) * up(w3b)).astype(BF16)
        o = jnp.dot(act, w2b[...], preferred_element_type=F32)
        o_ref[0:n_rows, :] = _pack_bf16_pair(o[:, 0:half], o[:, half:2 * half])
        if n_rows < o_ref.shape[0]:
            o_ref[n_rows:, :] = jnp.zeros((o_ref.shape[0] - n_rows, half), I32)

    @pl.when(used & (short_ref[i] == 0))
    def _():
        swiglu(o_ref.shape[0])

    @pl.when(used & (short_ref[i] == 1))
    def _():
        swiglu(o_ref.shape[0] // 2)

    @pl.when(jnp.logical_not(used))
    def _():
        o_ref[...] = jnp.zeros_like(o_ref)

    @pl.when(used & (last_ref[i] == 1) & (nxt >= 0))
    def _():
        for cp in copies(nxt):
            cp.wait()
        convert()


def _experts(tile_expert, first, last, nxt, short, n_used, xs, w1, w3, w2, *, tme):
    rows, half = xs.shape
    n_e, d_model, d_exp = w1.shape
    grid_spec = pltpu.PrefetchScalarGridSpec(
        num_scalar_prefetch=6,
        grid=(rows // tme,),
        in_specs=[pl.BlockSpec((tme, half), lambda i, te, fi, la, nx, sh, nu: (jnp.minimum(i, nu[0] - 1), 0)),
                  pl.BlockSpec(memory_space=pl.ANY), pl.BlockSpec(memory_space=pl.ANY),
                  pl.BlockSpec(memory_space=pl.ANY)],
        out_specs=pl.BlockSpec((tme, half), lambda i, *_: (i, 0)),
        scratch_shapes=[pltpu.VMEM((d_model, d_exp), BF16), pltpu.VMEM((d_model, d_exp), BF16),
                        pltpu.VMEM((d_exp, d_model), BF16),
                        pltpu.VMEM((d_model, d_exp), F32), pltpu.VMEM((d_model, d_exp), F32),
                        pltpu.VMEM((d_exp, d_model), F32),
                        pltpu.SemaphoreType.DMA((3,))],
    )
    return pl.pallas_call(
        functools.partial(_experts_body, half=half, n_conv=8),
        grid_spec=grid_spec,
        out_shape=jax.ShapeDtypeStruct((rows, half), I32),
        compiler_params=_cparams(1),
        name="experts",
    )(tile_expert, first, last, nxt, short, n_used, xs, w1, w3, w2)


def _sc_gather(table, idx, *, chunk):
    n_rows, width = idx.shape[0], table.shape[1]
    n_workers = SC_CORES * SC_SUBCORES
    per_w = n_rows // n_workers
    n_chunks = per_w // chunk
    assert per_w * n_workers == n_rows and n_chunks * chunk == per_w and n_chunks % 2 == 0
    mesh = plsc.VectorSubcoreMesh(core_axis_name="c", subcore_axis_name="s",
                                  num_cores=SC_CORES, num_subcores=SC_SUBCORES)

    @functools.partial(
        pl.kernel, mesh=mesh,
        out_type=jax.ShapeDtypeStruct((n_rows, width), table.dtype),
        scratch_types=[pltpu.VMEM((per_w,), I32), pltpu.VMEM((2, chunk, width), table.dtype),
                       pltpu.SemaphoreType.DMA((2,))],
    )
    def gather_kernel(table_hbm, idx_hbm, out_hbm, idx_v, rows_v, sem):
        wid = lax.axis_index("s") * SC_CORES + lax.axis_index("c")
        base = pl.multiple_of(wid * per_w, per_w)
        pltpu.sync_copy(idx_hbm.at[pl.ds(base, per_w)], idx_v)

        def gather(j, slot):
            off = pl.multiple_of(j * chunk, chunk)
            return pltpu.make_async_copy(table_hbm.at[idx_v.at[pl.ds(off, chunk)]],
                                         rows_v.at[slot], sem.at[slot])

        gather(0, 0).start()

        @pl.loop(0, n_chunks, step=2)
        def _(j):
            for slot in range(2):
                jj = j + slot
                gather(jj, slot).wait()

                @pl.when(jj + 1 < n_chunks)
                def _():
                    gather(jj + 1, 1 - slot).start()
                off = pl.multiple_of(base + jj * chunk, chunk)
                pltpu.sync_copy(rows_v.at[slot], out_hbm.at[pl.ds(off, chunk)])

    return gather_kernel(table, idx)


def _sc_inverse(dest_flat, n_rows):
    n_assign = dest_flat.shape[0]
    n_tok = n_assign // 2
    n_workers = SC_CORES * SC_SUBCORES
    per_w = n_rows // n_workers
    assert per_w * n_workers == n_rows and per_w % SC_LANES == 0 and n_assign % SC_LANES == 0
    mesh = plsc.VectorSubcoreMesh(core_axis_name="c", subcore_axis_name="s",
                                  num_cores=SC_CORES, num_subcores=SC_SUBCORES)

    @functools.partial(
        pl.kernel, mesh=mesh,
        out_type=jax.ShapeDtypeStruct((n_rows,), I32),
        scratch_types=[pltpu.VMEM((n_assign,), I32), pltpu.VMEM((per_w,), I32)],
        compiler_params=pltpu.CompilerParams(needs_layout_passes=False),
    )
    def inverse_kernel(dest_hbm, src_hbm, dest_v, src_v):
        wid = lax.axis_index("s") * SC_CORES + lax.axis_index("c")
        lo = pl.multiple_of(wid * per_w, per_w)
        pltpu.sync_copy(dest_hbm, dest_v)
        lane = lax.iota(I32, SC_LANES)

        @pl.loop(0, per_w, step=SC_LANES)
        def _(i):
            src_v[pl.ds(pl.multiple_of(i, SC_LANES), SC_LANES)] = lax.rem(lo + i + lane, n_tok)

        @pl.loop(0, n_assign, step=SC_LANES)
        def _(i):
            d = dest_v[pl.ds(pl.multiple_of(i, SC_LANES), SC_LANES)]
            a = i + lane
            tok = jnp.where(a >= n_tok, a - n_tok, a)
            mine = (d >= lo) & (d < lo + per_w)
            plsc.store_scatter(src_v, [d - lo], tok, mask=mine)

        pltpu.sync_copy(src_v, src_hbm.at[pl.ds(lo, per_w)])

    return inverse_kernel(dest_flat)


def _final_body(g1_ref, g2_ref, h1_ref, route_ref, g_ref, y_ref, *, half):
    route = route_ref[...]
    w1, w2 = route[:, 2:3], route[:, 3:4]
    a1, b1 = _unpack_bf16_pair(g1_ref[...])
    a2, b2 = _unpack_bf16_pair(g2_ref[...])
    h1 = h1_ref[...]
    ha = h1[:, 0:half] + (a1 * w1 + a2 * w2)
    hb = h1[:, half:2 * half] + (b1 * w1 + b2 * w2)
    ms = (jnp.sum(ha * ha, axis=-1, keepdims=True)
          + jnp.sum(hb * hb, axis=-1, keepdims=True)) / (2 * half)
    inv = lax.rsqrt(ms + EPS)
    y_ref[:, 0:half] = ha * inv * g_ref[:, 0:half]
    y_ref[:, half:2 * half] = hb * inv * g_ref[:, half:2 * half]


def _final(gathered, h1, route, g, *, tf, row0, n_rows):
    t, d_model = h1.shape
    half = d_model // 2
    tile0, tiles_t = row0 // tf, t // tf
    return pl.pallas_call(
        functools.partial(_final_body, half=half),
        grid=(n_rows // tf,),
        in_specs=[pl.BlockSpec((tf, half), lambda i: (tile0 + i, 0)),
                  pl.BlockSpec((tf, half), lambda i: (tiles_t + tile0 + i, 0)),
                  pl.BlockSpec((tf, d_model), lambda i: (tile0 + i, 0)),
                  pl.BlockSpec((tf, LANES), lambda i: (tile0 + i, 0)),
                  pl.BlockSpec((1, d_model), lambda i: (0, 0))],
        out_specs=pl.BlockSpec((tf, d_model), lambda i: (i, 0)),
        out_shape=jax.ShapeDtypeStruct((n_rows, d_model), F32),
        compiler_params=_cparams(1),
        name="final",
    )(gathered, gathered, h1, route, g)


def kernel(x_prompt, x_sample, meta, norm_mix_g, w_in, ssm_lam_re, ssm_lam_im, ssm_log_step, ssm_b_re, ssm_b_im, ssm_c_re, ssm_c_im, ssm_d, ssm_w_glu, conv_w_dw, conv_b_dw, conv_ln_g, conv_ln_b, conv_w_pw, w_out, norm_ffn_g, router_group_w, router_group_b, router_expert_w, router_expert_b, expert_w1, expert_w3, expert_w2, final_g):
    assert w_in.shape[0] == 1, "single-layer trunk"
    bp, lp, d_model = x_prompt.shape
    bs, ls, _ = x_sample.shape
    n_meta = meta.shape[0]
    d_ssm = ssm_d.shape[-1]
    d_conv = conv_b_dw.shape[-1]
    n_ch = ssm_b_re.shape[-1]
    n_g = ssm_b_re.shape[2]
    n_grp = router_group_w.shape[-1]
    n_exp = router_expert_w.shape[-1]
    assert n_meta == CHUNK and lp % CHUNK == 0 and ls % CHUNK == 0
    n_p, n_s = bp * lp, bs * ls
    t = n_p + n_s
    tm = min(256, lp, ls)
    assert lp % tm == 0 and ls % tm == 0

    xp2 = x_prompt.reshape(n_p, d_model)
    xs2 = x_sample.reshape(n_s, d_model)
    row = lambda a: a.reshape(1, -1)
    w_in_bf = w_in[0].astype(BF16)

    xg, c, sga, sgb = _inproj(xp2, xs2, row(norm_mix_g[0]), w_in_bf, d_ssm=d_ssm, d_conv=d_conv, tm=tm)
    xg_meta, c_meta = _meta_inproj(meta, row(norm_mix_g[0]), w_in_bf, d_ssm=d_ssm, d_conv=d_conv)

    assert n_ch == CHUNK and LANES % n_ch == 0 and n_g % (LANES // n_ch) == 0
    intra, bst, cst, coef = _sg_prep(ssm_lam_re[0], ssm_lam_im[0], ssm_log_step[0], ssm_b_re[0],
                                     ssm_b_im[0], ssm_c_re[0], ssm_c_im[0], ssm_d[0])
    nc = t // CHUNK
    n_blk = d_ssm // LANES
    rt = min(512, nc)
    sf, sb, init = _sg_mm_state(xg, xg_meta, bst, rt=rt)
    n_sub = sf.shape[0] // (n_blk * nc)
    cb = min(64, lp // CHUNK, ls // CHUNK)
    xf, xb = _sg_scan(sf.reshape(n_blk, nc, n_sub, LANES), sb.reshape(n_blk, nc, n_sub, LANES),
                      coef, init, cb=cb, chunks_pseq=lp // CHUNK, chunks_sseq=ls // CHUNK,
                      n_pchunks=n_p // CHUNK)
    z = _sg_mm_out(xg, intra, xf.reshape(sf.shape), xb.reshape(sb.shape), cst, rt=rt)

    assert n_grp + n_exp <= LANES // 2
    rw32 = jnp.pad(jnp.concatenate([router_group_w[0], router_expert_w[0]], axis=1),
                   ((0, 0), (0, LANES // 2 - n_grp - n_exp)))
    rw_hi = rw32.astype(BF16)
    rw = jnp.concatenate([rw_hi, (rw32 - rw_hi.astype(F32)).astype(BF16)], axis=1)
    rb = jnp.zeros((1, LANES), F32).at[0, 0:n_grp].set(router_group_b[0]).at[
        0, n_grp:n_grp + n_exp].set(router_expert_b[0])
    conv_w = conv_w_dw.shape[1]
    off = CHUNK - conv_w // 2
    assert 0 <= off and conv_w + off <= 2 * CHUNK
    taps = jnp.pad(conv_w_dw[0], ((off, -(conv_w + off) % 8), (0, 0)))
    h1, v, route, cnt = _mix(
        xp2, xs2, z, c, c_meta, sga, sgb, ssm_w_glu[0].astype(BF16), conv_w_pw[0].astype(BF16),
        w_out[0].astype(BF16), _conv_weights(taps), row(conv_b_dw[0]), row(conv_ln_g[0]),
        row(conv_ln_b[0]), row(norm_ffn_g[0]), rw, rb,
        tm=tm, p_seq=lp, s_seq=ls, n_grp=n_grp, n_exp=n_exp)

    tme = 256
    counts = cnt[0, 0:n_exp].astype(I32)
    tiles_e = (counts + tme - 1) // tme
    tile_end = jnp.cumsum(tiles_e)
    n_used = tile_end[-1]
    offs = jnp.zeros((1, LANES), F32).at[0, 0:n_exp].set(((tile_end - tiles_e) * tme).astype(F32))
    n_tiles = (2 * t) // tme + n_exp
    ids = jnp.arange(n_tiles, dtype=I32)

    def expert_of(tile):
        return jnp.minimum(jnp.sum((tile[:, None] >= tile_end[None, :]).astype(I32), axis=1), n_exp - 1)
    te_map = expert_of(jnp.minimum(ids, n_used - 1))
    onehot = te_map[:, None] == jnp.arange(n_exp, dtype=I32)[None, :]
    run_end = jnp.sum(jnp.where(onehot, tile_end[None, :], 0), axis=1)
    run_start = run_end - jnp.sum(jnp.where(onehot, tiles_e[None, :], 0), axis=1)
    valid = ids < n_used
    first = (valid & (ids == run_start)).astype(I32)
    last = (valid & (ids == run_end - 1)).astype(I32)
    nxt = jnp.where(valid & (run_end < n_used), expert_of(run_end), -1)
    tail_rows = counts - (tiles_e - 1) * tme
    short_e = ((tiles_e > 0) & (tail_rows <= tme // 2)).astype(I32)
    short = last * jnp.sum(jnp.where(onehot, short_e[None, :], 0), axis=1)

    dest = _positions(route, offs, te=min(512, t))
    idx = jnp.concatenate([dest[:, 0], dest[:, 1]])
    src = _sc_inverse(idx, n_tiles * tme)
    xs = _sc_gather(v, src, chunk=32)
    o = _experts(te_map, first, last, nxt, short, n_used.reshape(1), xs, expert_w1[0],
                 expert_w3[0], expert_w2[0], tme=tme)

    tf = min(256, n_p, n_s)
    gathered = _sc_gather(o, idx, chunk=32)
    fg = row(final_g)
    y_p = _final(gathered, h1, route, fg, tf=tf, row0=0, n_rows=n_p)
    y_s = _final(gathered, h1, route, fg, tf=tf, row0=n_p, n_rows=n_s)
    return (y_p.reshape(bp, lp, d_model), y_s.reshape(bs, ls, d_model))
```

```python
import functools

import jax
import jax.numpy as jnp
from jax import lax
from jax.experimental import pallas as pl
from jax.experimental.pallas import tpu as pltpu
from jax.experimental.pallas import tpu_sc as plsc

F32 = jnp.float32
BF16 = jnp.bfloat16
U32 = jnp.uint32
I32 = jnp.int32

EPS = 1e-6
LAM_RE_MAX = -1e-4
CHUNK = 16
LANES = 128
VMEM_LIMIT = 56 << 20
SC_CORES = 2
SC_SUBCORES = 16
SC_LANES = 16
HIGHEST = lax.Precision.HIGHEST


def _cparams(n_axes):
    return pltpu.CompilerParams(dimension_semantics=("arbitrary",) * n_axes,
                                vmem_limit_bytes=VMEM_LIMIT)


def _resident(shape):
    nd = len(shape)
    return pl.BlockSpec(shape, lambda *_: (0,) * nd, pipeline_mode=pl.Buffered(1))


def _sigmoid(x):
    return 1.0 / (1.0 + jnp.exp(-x))


def _rms(x, g):
    return x * lax.rsqrt(jnp.mean(x * x, axis=-1, keepdims=True) + EPS) * g


def _pack_bf16_pair(a, b):
    def rnd(x):
        u = pltpu.bitcast(x, U32)
        return (u + jnp.uint32(0x7FFF) + ((u >> 16) & jnp.uint32(1))) >> 16
    return pltpu.bitcast((rnd(a) << 16) | rnd(b), I32)


def _unpack_bf16_pair(p):
    p = pltpu.bitcast(p, U32)
    a = pltpu.bitcast(p & jnp.uint32(0xFFFF0000), F32)
    b = pltpu.bitcast(p << 16, F32)
    return a, b


def _regroup(pieces, n_ch):
    per = LANES // n_ch
    lane = lax.broadcasted_iota(I32, pieces[0].shape, 1)
    masks = [(lane >= a * n_ch) & (lane < (a + 1) * n_ch) for a in range(per)]
    out = []
    for b in range(per):
        acc = None
        for a in range(per):
            shift = ((a - b) * n_ch) % LANES
            src = pieces[a] if shift == 0 else pltpu.roll(pieces[a], shift, 1)
            acc = jnp.where(masks[a], src, 0.0) if acc is None else jnp.where(masks[a], src, acc)
        out.append(acc)
    return out


def _inproj_body(xp_ref, xs_ref, g_ref, w_ref, xg_ref, c_ref, sga_ref, sgb_ref, usc, *,
                 tm, n_p_tiles, d_ssm, d_conv, d_model, col, n_ch):
    i = pl.program_id(0)
    x = jnp.where(i < n_p_tiles, xp_ref[...], xs_ref[...])
    y = _rms(x, g_ref[...]).astype(BF16)

    def proj(lo, n):
        return jnp.dot(y, w_ref[:, lo:lo + n], preferred_element_type=F32)

    for k in range(d_ssm // col):
        u = proj(k * col, col)
        for b in range(col // LANES):
            usc[k * (col // LANES) + b] = u[:, b * LANES:(b + 1) * LANES]
    per = LANES // n_ch
    for b in range(d_ssm // LANES):
        for hf in range(CHUNK // per):
            tokens = [usc[b, pl.ds(hf * per + a, tm // CHUNK, stride=CHUNK), :] for a in range(per)]
            for g, blk in enumerate(_regroup(tokens, n_ch)):
                c0 = (b * per + g) * CHUNK * n_ch + hf * LANES
                xg_ref[:, c0:c0 + LANES] = blk.astype(BF16)
    for k in range(d_conv // col):
        cv = proj(d_ssm + k * col, col)
        cg = proj(d_ssm + d_conv + k * col, col)
        c_ref[:, k * col:(k + 1) * col] = (cv * _sigmoid(cg)).astype(BF16)
    base = d_ssm + 2 * d_conv
    for k in range(d_model // col):
        sga_ref[:, k * col:(k + 1) * col] = _sigmoid(proj(base + k * col, col)).astype(BF16)
        sgb_ref[:, k * col:(k + 1) * col] = _sigmoid(
            proj(base + d_model + k * col, col)).astype(BF16)


def _inproj(xp2, xs2, g, w_bf, *, d_ssm, d_conv, tm, n_ch):
    n_p, d_model = xp2.shape
    n_s = xs2.shape[0]
    n_p_tiles, n_s_tiles = n_p // tm, n_s // tm
    t = n_p + n_s
    col = min(1024, d_ssm, d_conv, d_model)
    body = functools.partial(_inproj_body, tm=tm, n_p_tiles=n_p_tiles, d_ssm=d_ssm, d_conv=d_conv,
                             d_model=d_model, col=col, n_ch=n_ch)
    row = lambda i: (i, 0)
    return pl.pallas_call(
        body,
        grid=(n_p_tiles + n_s_tiles,),
        in_specs=[
            pl.BlockSpec((tm, d_model), lambda i: (jnp.minimum(i, n_p_tiles - 1), 0)),
            pl.BlockSpec((tm, d_model), lambda i: (jnp.maximum(i - n_p_tiles, 0), 0)),
            _resident((1, d_model)),
            _resident(w_bf.shape),
        ],
        out_specs=[pl.BlockSpec((tm // CHUNK, CHUNK * d_ssm), row), pl.BlockSpec((tm, d_conv), row),
                   pl.BlockSpec((tm, d_model), row), pl.BlockSpec((tm, d_model), row)],
        out_shape=[jax.ShapeDtypeStruct((t // CHUNK, CHUNK * d_ssm), BF16),
                   jax.ShapeDtypeStruct((t, d_conv), BF16),
                   jax.ShapeDtypeStruct((t, d_model), BF16), jax.ShapeDtypeStruct((t, d_model), BF16)],
        scratch_shapes=[pltpu.VMEM((d_ssm // LANES, tm, LANES), F32)],
        compiler_params=_cparams(1),
        name="inproj",
    )(xp2, xs2, g, w_bf)


def _meta_body(m_ref, g_ref, w_ref, xg_ref, c_ref, *, d_ssm, d_conv, n_ch):
    y = _rms(m_ref[...], g_ref[...]).astype(BF16)
    u = jnp.dot(y, w_ref[:, 0:d_ssm], preferred_element_type=F32)
    first = lax.broadcasted_iota(I32, (CHUNK, LANES), 0) == 0
    per = LANES // n_ch
    for b in range(d_ssm // LANES):
        for hf in range(CHUNK // per):
            tokens = [u[hf * per + a:hf * per + a + 1, b * LANES:(b + 1) * LANES] for a in range(per)]
            for g, blk in enumerate(_regroup(tokens, n_ch)):
                c0 = (b * per + g) * CHUNK * n_ch + hf * LANES
                piece = jnp.broadcast_to(blk, (CHUNK, LANES))
                xg_ref[:, c0:c0 + LANES] = jnp.where(first, piece, 0.0).astype(BF16)
    cv = jnp.dot(y, w_ref[:, d_ssm:d_ssm + d_conv], preferred_element_type=F32)
    cg = jnp.dot(y, w_ref[:, d_ssm + d_conv:d_ssm + 2 * d_conv], preferred_element_type=F32)
    c_ref[...] = (cv * _sigmoid(cg)).astype(BF16)


def _meta_inproj(meta, g, w_bf, *, d_ssm, d_conv, n_ch):
    n_meta, d_model = meta.shape
    ncol = d_ssm + 2 * d_conv
    return pl.pallas_call(
        functools.partial(_meta_body, d_ssm=d_ssm, d_conv=d_conv, n_ch=n_ch),
        grid=(1,),
        in_specs=[pl.BlockSpec((n_meta, d_model), lambda i: (0, 0)),
                  pl.BlockSpec((1, d_model), lambda i: (0, 0)),
                  pl.BlockSpec((d_model, ncol), lambda i: (0, 0))],
        out_specs=[pl.BlockSpec((CHUNK, CHUNK * d_ssm), lambda i: (0, 0)),
                   pl.BlockSpec((n_meta, d_conv), lambda i: (0, 0))],
        out_shape=[jax.ShapeDtypeStruct((CHUNK, CHUNK * d_ssm), BF16),
                   jax.ShapeDtypeStruct((n_meta, d_conv), BF16)],
        compiler_params=_cparams(1),
        name="meta_inproj",
    )(meta, g, w_bf)


def _cmul(ar, ai, br, bi):
    return ar * br - ai * bi, ar * bi + ai * br


def _discretize(lam_re, lam_im, log_step):
    lr = jnp.minimum(lam_re, LAM_RE_MAX)
    dt = jnp.exp(log_step)
    mag = jnp.exp(lr * dt)
    ar = mag * jnp.cos(lam_im * dt)
    ai = mag * jnp.sin(lam_im * dt)
    den = lr * lr + lam_im * lam_im
    nr = ar - 1.0
    fr = (nr * lr + ai * lam_im) / den
    fi = (ai * lr - nr * lam_im) / den
    return ar, ai, fr, fi


def _cpow(ar, ai, k, nbits, shape):
    pr = jnp.ones(shape, F32)
    pi = jnp.zeros(shape, F32)
    br = jnp.broadcast_to(ar, shape)
    bi = jnp.broadcast_to(ai, shape)
    kk = jnp.broadcast_to(k, shape)
    for b in range(nbits):
        sel = ((kk >> b) & 1) == 1
        nr, ni = _cmul(pr, pi, br, bi)
        pr = jnp.where(sel, nr, pr)
        pi = jnp.where(sel, ni, pi)
        br, bi = _cmul(br, bi, br, bi)
    return pr, pi


def _spread(x, n_ch, shift):
    rows, w = x.shape
    lane = lax.broadcasted_iota(I32, (rows, LANES), 1)
    keep = (lane >= shift) & (lane < shift + n_ch)
    out = []
    for b in range(w // n_ch):
        src = (b * n_ch // LANES) * LANES
        amount = lax.rem(shift + LANES - (b * n_ch) % LANES, LANES)
        out.append(jnp.where(keep, pltpu.roll(x[:, src:src + LANES], amount, 1), 0.0))
    return jnp.concatenate(out, axis=1)


def _tile_lanes(x, n_ch, width):
    span = n_ch
    while span < LANES:
        x = x + pltpu.roll(x, span, 1)
        span *= 2
    return jnp.concatenate([x] * (width // LANES), axis=1)


def _disc_body(lam_re_ref, lam_im_ref, ls_ref, ar_ref, ai_ref, fr_ref, fi_ref):
    ar, ai, fr, fi = _discretize(lam_re_ref[...], lam_im_ref[...], ls_ref[...])
    ar_ref[...] = ar
    ai_ref[...] = ai
    fr_ref[...] = fr
    fi_ref[...] = fi


def _ssm_discretize(lam_re, lam_im, log_step):
    shape = lam_re.shape
    rows = shape[0] * shape[1]
    flat = lambda x: x.reshape(rows, shape[2])
    ls = jnp.broadcast_to(log_step[..., None], shape)
    full = pl.BlockSpec((rows, shape[2]), lambda i: (0, 0))
    out = pl.pallas_call(
        _disc_body,
        grid=(1,),
        in_specs=[full, full, full],
        out_specs=[full] * 4,
        out_shape=[jax.ShapeDtypeStruct((rows, shape[2]), F32)] * 4,
        compiler_params=_cparams(1),
        name="ssm_disc",
    )(flat(lam_re), flat(lam_im), flat(ls))
    return [x.reshape(shape) for x in out]


def _sg_prep_body(ar_c, ai_c, fr_c, fi_c, ct_re, ct_im, bt_re, bt_im,
                  ar_r, ai_r, fr_r, fi_r, btr_re, btr_im, dpad,
                  m_ref, b_ref, c_ref, aq_ref, *, n_state, n_ch, per_step):
    per_blk = LANES // n_ch
    first = (pl.program_id(0) * per_step) % per_blk
    for j in range(per_step):
        _sg_prep_group(j, first + j, ar_c, ai_c, fr_c, fi_c, ct_re, ct_im, bt_re, bt_im,
                       ar_r, ai_r, fr_r, fi_r, btr_re, btr_im, dpad,
                       m_ref, b_ref, c_ref, aq_ref, n_state=n_state, n_ch=n_ch)


def _sg_prep_group(j, gl, ar_c, ai_c, fr_c, fi_c, ct_re, ct_im, bt_re, bt_im,
                   ar_r, ai_r, fr_r, fi_r, btr_re, btr_im, dpad,
                   m_ref, b_ref, c_ref, aq_ref, *, n_state, n_ch):
    q = CHUNK
    width = q * n_ch
    rowmask = (lax.broadcasted_iota(I32, (2 * n_state, width), 0) // n_state) == gl % 2
    kblk = lax.broadcasted_iota(I32, (1, width), 1) // n_ch
    strips = []
    for d in range(2):
        ar, ai, fr, fi = ar_c[d, j], ai_c[d, j], fr_c[d, j], fi_c[d, j]
        kexp = kblk if d == 0 else (q - 1) - kblk
        wr, wi = _cpow(ar, ai, kexp, 4, (n_state, width))
        gcr, gci = _cmul(_tile_lanes(ct_re[d, j], n_ch, width),
                         _tile_lanes(ct_im[d, j], n_ch, width), wr, wi)
        gfr, gfi = _cmul(gcr, gci, fr, fi)
        strips.append(jnp.dot(bt_re[d, j], gfr, precision=HIGHEST, preferred_element_type=F32)
                      - jnp.dot(bt_im[d, j], gfi, precision=HIGHEST, preferred_element_type=F32))
        g1r, g1i = _cmul(gcr, gci, ar, ai)
        for comp, val in ((0, g1r), (1, -g1i)):
            r0 = (2 * d + comp) * 2 * n_state
            both = jnp.concatenate([val, val], axis=0)
            c_ref[j, r0:r0 + 2 * n_state, :] = jnp.where(rowmask, both, 0.0).astype(BF16)

    zf, zb = strips
    zero = jnp.zeros((n_ch, width), F32)
    z512 = jnp.concatenate([zb, zero], axis=1) + pltpu.roll(
        jnp.concatenate([zf, zero], axis=1), (q - 1) * n_ch, 1)
    row = lax.broadcasted_iota(I32, (n_ch, 2 * width), 0)
    lane = lax.broadcasted_iota(I32, (n_ch, 2 * width), 1)
    z512 = z512 + jnp.where(lane - (q - 1) * n_ch == row, dpad[j], 0.0)
    for s in range(q):
        sh = (q - 1 - s) * n_ch
        blk = z512 if sh == 0 else pltpu.roll(z512, 2 * width - sh, 1)
        m_ref[j, n_ch * s:n_ch * (s + 1), :] = blk[:, 0:width].astype(BF16)

    parmask = (lax.broadcasted_iota(I32, (1, 2 * n_state), 1) // n_state) == gl % 2
    pieces = [[None] * 4 for _ in range(q)]
    for d in range(2):
        ar, ai, fr, fi = ar_r[d, j], ai_r[d, j], fr_r[d, j], fi_r[d, j]
        pw = [(jnp.ones_like(ar), jnp.zeros_like(ar))]
        for _ in range(q):
            pw.append(_cmul(pw[-1][0], pw[-1][1], ar, ai))
        for s in range(q):
            e = (q - 1 - s) if d == 0 else s
            cr, ci = _cmul(fr, fi, pw[e][0], pw[e][1])
            br, bi = _cmul(btr_re[d, j], btr_im[d, j], cr, ci)
            pieces[s][2 * d] = jnp.where(parmask, br, 0.0)
            pieces[s][2 * d + 1] = jnp.where(parmask, bi, 0.0)
        aq_ref[j, :, 2 * d * n_state:(2 * d + 1) * n_state] = pw[q][0][:, 0:n_state]
        aq_ref[j, :, (2 * d + 1) * n_state:(2 * d + 2) * n_state] = pw[q][1][:, 0:n_state]
    for s in range(q):
        b_ref[j, n_ch * s:n_ch * (s + 1), :] = jnp.concatenate(pieces[s], axis=1).astype(BF16)


def _sg_prep(lam_re, lam_im, log_step, b_re, b_im, c_re, c_im, d_skip):
    _, n_g, n_state, n_ch = b_re.shape
    q = CHUNK
    width = q * n_ch
    per_blk = LANES // n_ch
    n_blk = n_g // per_blk
    dup = lambda x: jnp.concatenate([x, x], axis=-1)
    lane_pad = lambda x: jnp.pad(x, ((0, 0),) * 3 + ((0, LANES - n_ch),))
    disc = _ssm_discretize(lam_re, lam_im, log_step)
    disc_c = [x[..., None] for x in disc]
    disc_r = [dup(x)[:, :, None, :] for x in disc]
    ct_re = lane_pad(jnp.swapaxes(c_re, -1, -2))
    ct_im = lane_pad(jnp.swapaxes(c_im, -1, -2))
    bt_re, bt_im = jnp.swapaxes(b_re, -1, -2), jnp.swapaxes(b_im, -1, -2)
    btr_re, btr_im = dup(bt_re), dup(bt_im)
    dpad = jnp.pad(d_skip.reshape(n_g, 1, n_ch), ((0, 0), (0, 0), ((q - 1) * n_ch, width)))

    per_step = 2
    sw = 8 * n_state

    def dspec(shape):
        return pl.BlockSpec((2, per_step) + shape, lambda g: (0, g, 0, 0))

    def gspec(shape):
        return pl.BlockSpec((per_step,) + shape, lambda g: (g, 0, 0))

    body = functools.partial(_sg_prep_body, n_state=n_state, n_ch=n_ch, per_step=per_step)
    intra, bst, cst, aq = pl.pallas_call(
        body,
        grid=(n_g // per_step,),
        in_specs=[dspec((n_state, 1))] * 4 + [
                  dspec((n_state, LANES)), dspec((n_state, LANES)),
                  dspec((n_ch, n_state)), dspec((n_ch, n_state))] + [dspec((1, 2 * n_state))] * 4 + [
                  dspec((n_ch, 2 * n_state)), dspec((n_ch, 2 * n_state)),
                  gspec((1, 2 * width))],
        out_specs=[gspec((width, width)), gspec((width, sw)), gspec((sw, width)),
                   gspec((1, 4 * n_state))],
        out_shape=[jax.ShapeDtypeStruct((n_g, width, width), BF16),
                   jax.ShapeDtypeStruct((n_g, width, sw), BF16),
                   jax.ShapeDtypeStruct((n_g, sw, width), BF16),
                   jax.ShapeDtypeStruct((n_g, 1, 4 * n_state), F32)],
        compiler_params=_cparams(1),
        name="ssm_prep",
    )(*disc_c, ct_re, ct_im, bt_re, bt_im, *disc_r, btr_re, btr_im, dpad)
    a = aq.reshape(n_blk, per_blk // 2, 2, 2, 2, n_state)
    a = a.transpose(3, 4, 0, 1, 2, 5).reshape(2, 2, n_blk, per_blk // 2, LANES)
    coef = []
    for d in range(2):
        re, im = a[d, 0], a[d, 1]
        coef.append((jnp.concatenate([re, re], axis=1), jnp.concatenate([-im, im], axis=1)))
    return intra, bst, cst, coef


def _sg_mm_state_body(x_ref, xm_ref, b_ref, sf_ref, sb_ref, init_ref, *, rt, n_pair, width):
    def pair_update(x, q):
        g0, g1 = 2 * q, 2 * q + 1
        return (jnp.dot(x[:, g0 * width:(g0 + 1) * width], b_ref[g0], preferred_element_type=F32)
                + jnp.dot(x[:, g1 * width:(g1 + 1) * width], b_ref[g1], preferred_element_type=F32))

    n_sub = 2 * n_pair
    for q in range(n_pair):
        r = pair_update(x_ref, q)
        sf_ref[pl.ds(q, rt, stride=n_sub), :] = r[:, 0:LANES]
        sf_ref[pl.ds(n_pair + q, rt, stride=n_sub), :] = r[:, LANES:2 * LANES]
        sb_ref[pl.ds(q, rt, stride=n_sub), :] = r[:, 2 * LANES:3 * LANES]
        sb_ref[pl.ds(n_pair + q, rt, stride=n_sub), :] = r[:, 3 * LANES:4 * LANES]

    @pl.when(pl.program_id(1) == 0)
    def _():
        for q in range(n_pair):
            r0 = pair_update(xm_ref, q)
            init_ref[q:q + 1, :] = r0[0:1, 0:LANES]
            init_ref[n_pair + q:n_pair + q + 1, :] = r0[0:1, LANES:2 * LANES]


def _sg_mm_state(xg, xg_meta, bst, *, rt, per_blk):
    nc = xg.shape[0]
    n_g, width, sw = bst.shape
    n_blk = n_g // per_blk
    kw = per_blk * width
    n_pair = per_blk // 2
    n_sub = 2 * n_pair
    n_rt = nc // rt
    body = functools.partial(_sg_mm_state_body, rt=rt, n_pair=n_pair, width=width)
    dense = jax.ShapeDtypeStruct((n_blk * nc * n_sub, LANES), F32)
    dspec = pl.BlockSpec((rt * n_sub, LANES), lambda b, i: (b * n_rt + i, 0))
    return pl.pallas_call(
        body,
        grid=(n_blk, n_rt),
        in_specs=[pl.BlockSpec((rt, kw), lambda b, i: (i, b)),
                  pl.BlockSpec((CHUNK, kw), lambda b, i: (0, b)),
                  pl.BlockSpec((per_blk, width, sw), lambda b, i: (b, 0, 0))],
        out_specs=[dspec, dspec, pl.BlockSpec((None, n_sub, LANES), lambda b, i: (b, 0, 0))],
        out_shape=[dense, dense, jax.ShapeDtypeStruct((n_blk, n_sub, LANES), F32)],
        compiler_params=_cparams(2),
        name="ssm_state",
    )(xg, xg_meta, bst)


def _sg_scan_body(sf_ref, sb_ref, a1f_ref, a2f_ref, a1b_ref, a2b_ref, init_ref, xf_ref, xb_ref,
                  stf, stb, *, cb, n_blk, half, n_pblk, blk_pseq, blk_sseq):
    j = pl.program_id(0)
    pos = jnp.where(j < n_pblk, lax.rem(j, blk_pseq), lax.rem(jnp.maximum(j - n_pblk, 0), blk_sseq))

    @pl.when(pos == 0)
    def _():
        stf[...] = init_ref[...]
        stb[...] = jnp.zeros_like(stb)

    def body(i, carry):
        ef, eb = carry
        ib = cb - 1 - i
        nf, nb = [], []
        for g in range(n_blk):
            xf_ref[g, i] = ef[g]
            xb_ref[g, ib] = eb[g]
            nf.append(a1f_ref[g] * ef[g] + a2f_ref[g] * pltpu.roll(ef[g], half, 0) + sf_ref[g, i])
            nb.append(a1b_ref[g] * eb[g] + a2b_ref[g] * pltpu.roll(eb[g], half, 0) + sb_ref[g, ib])
        return tuple(nf), tuple(nb)

    ef, eb = lax.fori_loop(0, cb, body, (tuple(stf[g] for g in range(n_blk)),
                                         tuple(stb[g] for g in range(n_blk))))
    for g in range(n_blk):
        stf[g] = ef[g]
        stb[g] = eb[g]


def _sg_scan(sf, sb, coef, init, *, cb, chunks_pseq, chunks_sseq, n_pchunks):
    n_blk, nc, n_sub, _ = sf.shape
    n_pblk, blk_pseq, blk_sseq = n_pchunks // cb, chunks_pseq // cb, chunks_sseq // cb

    def bwd_block(j):
        in_p = j < n_pblk
        pos = jnp.where(in_p, lax.rem(j, blk_pseq), lax.rem(jnp.maximum(j - n_pblk, 0), blk_sseq))
        ln = jnp.where(in_p, blk_pseq, blk_sseq)
        return j - pos + ln - 1 - pos

    body = functools.partial(_sg_scan_body, cb=cb, n_blk=n_blk, half=n_sub // 2, n_pblk=n_pblk,
                             blk_pseq=blk_pseq, blk_sseq=blk_sseq)
    fwd = pl.BlockSpec((n_blk, cb, n_sub, LANES), lambda j: (0, j, 0, 0))
    bwd = pl.BlockSpec((n_blk, cb, n_sub, LANES), lambda j: (0, bwd_block(j), 0, 0))
    small = pl.BlockSpec((n_blk, n_sub, LANES), lambda j: (0, 0, 0))
    return pl.pallas_call(
        body,
        grid=(nc // cb,),
        in_specs=[fwd, bwd, small, small, small, small, small],
        out_specs=[fwd, bwd],
        out_shape=[jax.ShapeDtypeStruct(sf.shape, F32), jax.ShapeDtypeStruct(sb.shape, F32)],
        scratch_shapes=[pltpu.VMEM((n_blk, n_sub, LANES), F32), pltpu.VMEM((n_blk, n_sub, LANES), F32)],
        compiler_params=_cparams(1),
        name="ssm_scan",
    )(sf, sb, coef[0][0], coef[0][1], coef[1][0], coef[1][1], init)


def _gelu_tanh(x):
    return 0.5 * x * (1.0 + jnp.tanh(0.7978845608028654 * (x + 0.044715 * (x * x * x))))


def _sg_mm_out_body(x_ref, m_ref, xf_ref, xb_ref, c_ref, z_ref, *, rt, n_pair, width):
    n_sub = 2 * n_pair
    for q in range(n_pair):
        state = jnp.concatenate(
            [xf_ref[pl.ds(q, rt, stride=n_sub), :], xf_ref[pl.ds(n_pair + q, rt, stride=n_sub), :],
             xb_ref[pl.ds(q, rt, stride=n_sub), :], xb_ref[pl.ds(n_pair + q, rt, stride=n_sub), :]],
            axis=1).astype(BF16)
        for g in (2 * q, 2 * q + 1):
            cols = slice(g * width, (g + 1) * width)
            y = (jnp.dot(x_ref[:, cols], m_ref[g], preferred_element_type=F32)
                 + jnp.dot(state, c_ref[g], preferred_element_type=F32))
            z_ref[:, cols] = _gelu_tanh(y).astype(BF16)


def _sg_mm_out(xg, intra, xf, xb, cst, *, rt, per_blk):
    nc = xg.shape[0]
    n_g, sw, width = cst.shape
    n_blk = n_g // per_blk
    kw = per_blk * width
    n_pair = per_blk // 2
    n_sub = 2 * n_pair
    n_rt = nc // rt
    dspec = pl.BlockSpec((rt * n_sub, LANES), lambda b, i: (b * n_rt + i, 0))
    return pl.pallas_call(
        functools.partial(_sg_mm_out_body, rt=rt, n_pair=n_pair, width=width),
        grid=(n_blk, n_rt),
        in_specs=[pl.BlockSpec((rt, kw), lambda b, i: (i, b)),
                  pl.BlockSpec((per_blk, width, width), lambda b, i: (b, 0, 0)), dspec, dspec,
                  pl.BlockSpec((per_blk, sw, width), lambda b, i: (b, 0, 0))],
        out_specs=pl.BlockSpec((rt, kw), lambda b, i: (i, b)),
        out_shape=jax.ShapeDtypeStruct((nc, n_blk * kw), BF16),
        compiler_params=_cparams(2),
        name="ssm_out",
    )(xg, intra, xf, xb, cst)


def _mix_body(xp_ref, xs_ref, z_ref, c_ref, cprev_ref, cnext_ref, cmeta_ref, sga_ref, sgb_ref,
              wglu_ref, wpw_ref, wout_ref, wconv_ref, bdw_ref, lng_ref, lnb_ref, gffn_ref,
              rw_ref, rb_ref,
              h1_ref, v_ref, route_ref, cnt_ref,
              cw_ref, conv_ref, zsc_ref, *,
              tm, n_p_tiles, tiles_per_pseq, tiles_per_sseq, d_model, d_ssm, d_conv,
              n_grp, n_exp, exp_per_grp):
    i = pl.program_id(0)
    in_prompt = i < n_p_tiles
    x = jnp.where(in_prompt, xp_ref[...], xs_ref[...])
    pos_p = lax.rem(i, tiles_per_pseq)
    pos_s = lax.rem(jnp.maximum(i - n_p_tiles, 0), tiles_per_sseq)
    is_start = jnp.where(in_prompt, pos_p == 0, pos_s == 0)
    is_end = jnp.where(in_prompt, pos_p == tiles_per_pseq - 1, pos_s == tiles_per_sseq - 1)

    halo = CHUNK
    cw_ref[0:halo, :] = jnp.where(is_start, cmeta_ref[...], cprev_ref[...]).astype(F32)
    cw_ref[halo:halo + tm, :] = c_ref[...].astype(F32)
    cw_ref[halo + tm:2 * halo + tm, :] = jnp.where(is_end, 0.0, cnext_ref[...].astype(F32))
    sub = 8
    nq = wconv_ref.shape[1] // LANES
    for lc in range(d_conv // LANES):
        ls = slice(lc * LANES, (lc + 1) * LANES)
        stacked = jnp.concatenate(
            [cw_ref[sub * q:sub * q + tm + sub, ls] for q in range(nq)], axis=1).astype(BF16)
        part = jnp.dot(stacked, wconv_ref[lc], preferred_element_type=F32)
        out = part[0:tm, 0:LANES]
        for r in range(1, sub):
            out = out + part[r:r + tm, r * LANES:(r + 1) * LANES]
        conv_ref[:, ls] = out
    cc = conv_ref[...] + bdw_ref[...]
    mu = jnp.mean(cc, axis=-1, keepdims=True)
    var = jnp.mean(jnp.square(cc - mu), axis=-1, keepdims=True)
    cc = (cc - mu) * lax.rsqrt(var + EPS) * lng_ref[...] + lnb_ref[...]
    cc = (cc * _sigmoid(cc)).astype(BF16)
    y_b = jnp.dot(cc, wpw_ref[...], preferred_element_type=F32)

    per = LANES // CHUNK
    for b in range(d_ssm // LANES):
        for hf in range(CHUNK // per):
            groups = [z_ref[:, (b * per + g) * CHUNK * CHUNK + hf * LANES:
                            (b * per + g) * CHUNK * CHUNK + (hf + 1) * LANES].astype(F32)
                      for g in range(per)]
            for a, blk in enumerate(_regroup(groups, CHUNK)):
                zsc_ref[b, pl.ds(hf * per + a, tm // CHUNK, stride=CHUNK), :] = blk
    z = jnp.concatenate([zsc_ref[b] for b in range(d_ssm // LANES)], axis=1).astype(BF16)
    va = jnp.dot(z, wglu_ref[:, 0:d_model], preferred_element_type=F32)
    ga = jnp.dot(z, wglu_ref[:, d_model:2 * d_model], preferred_element_type=F32)
    y_a = va * _sigmoid(ga)
    merged = (sga_ref[...].astype(F32) * y_a + sgb_ref[...].astype(F32) * y_b).astype(BF16)
    h1 = x + jnp.dot(merged, wout_ref[...], preferred_element_type=F32)
    h1_ref[...] = h1
    v = _rms(h1, gffn_ref[...])
    half = d_model // 2
    v_ref[...] = _pack_bf16_pair(v[:, 0:half], v[:, half:d_model])

    v_hi = v.astype(BF16)
    v_lo = (v - v_hi.astype(F32)).astype(BF16)
    acc = (jnp.dot(v_hi, rw_ref[...], preferred_element_type=F32)
           + jnp.dot(v_lo, rw_ref[...], preferred_element_type=F32))
    logits = acc + pltpu.roll(acc, LANES // 2, 1) + rb_ref[...]
    lane = lax.broadcasted_iota(I32, (tm, LANES), 1).astype(F32)
    big = jnp.float32(1e9)
    neg = jnp.float32(-jnp.inf)
    gmask = lane < n_grp
    lg = jnp.where(gmask, logits, neg)
    gmax = jnp.max(lg, axis=-1, keepdims=True)
    grp = jnp.min(jnp.where(lg == gmax, lane, big), axis=-1, keepdims=True)
    p_grp = 1.0 / jnp.sum(jnp.where(gmask, jnp.exp(logits - gmax), 0.0), axis=-1, keepdims=True)
    lo = n_grp + grp * exp_per_grp
    emask = (lane >= lo) & (lane < lo + exp_per_grp)
    le = jnp.where(emask, logits, neg)
    m1 = jnp.max(le, axis=-1, keepdims=True)
    i1 = jnp.min(jnp.where(le == m1, lane, big), axis=-1, keepdims=True)
    le2 = jnp.where(lane == i1, neg, le)
    m2 = jnp.max(le2, axis=-1, keepdims=True)
    i2 = jnp.min(jnp.where(le2 == m2, lane, big), axis=-1, keepdims=True)
    t = jnp.exp(m2 - m1)
    w1 = 1.0 / (1.0 + t)
    e1 = i1 - n_grp
    e2 = i2 - n_grp
    route_ref[...] = jnp.where(lane == 0, e1, jnp.where(lane == 1, e2, jnp.where(
        lane == 2, p_grp * w1, jnp.where(lane == 3, p_grp * (t * w1), 0.0))))

    @pl.when(i == 0)
    def _():
        cnt_ref[...] = jnp.zeros_like(cnt_ref)

    hot = jnp.where((lane == e1) | (lane == e2), 1.0, 0.0)
    cnt_ref[...] += jnp.sum(hot, axis=0, keepdims=True)


def _conv_weights_body(taps_ref, w_ref, *, n_taps):
    eye = (lax.broadcasted_iota(I32, (LANES, LANES), 0)
           == lax.broadcasted_iota(I32, (LANES, LANES), 1))
    for p in range(n_taps):
        q, r = divmod(p, 8)
        diag = jnp.where(eye, taps_ref[p:p + 1, :], 0.0)
        w_ref[q * LANES:(q + 1) * LANES, r * LANES:(r + 1) * LANES] = diag.astype(BF16)


def _conv_weights(taps):
    n_taps, d_conv = taps.shape
    nq = n_taps // 8
    return pl.pallas_call(
        functools.partial(_conv_weights_body, n_taps=n_taps),
        grid=(d_conv // LANES,),
        in_specs=[pl.BlockSpec((n_taps, LANES), lambda b: (0, b))],
        out_specs=pl.BlockSpec((None, nq * LANES, 8 * LANES), lambda b: (b, 0, 0)),
        out_shape=jax.ShapeDtypeStruct((d_conv // LANES, nq * LANES, 8 * LANES), BF16),
        compiler_params=_cparams(1),
        name="conv_weights",
    )(taps)


def _mix(xp2, xs2, z, c, c_meta, sga, sgb, wglu, wpw, wout, wconv, bdw, lng, lnb, gffn, rw, rb, *,
         tm, p_seq, s_seq, n_grp, n_exp):
    n_p, d_model = xp2.shape
    n_s = xs2.shape[0]
    t = n_p + n_s
    d_ssm, d_conv = z.shape[1] // CHUNK, c.shape[1]
    n_p_tiles, n_s_tiles = n_p // tm, n_s // tm
    hpt = tm // CHUNK
    n_hblk = t // CHUNK
    body = functools.partial(
        _mix_body, tm=tm, n_p_tiles=n_p_tiles, tiles_per_pseq=p_seq // tm,
        tiles_per_sseq=s_seq // tm, d_model=d_model, d_ssm=d_ssm, d_conv=d_conv,
        n_grp=n_grp, n_exp=n_exp, exp_per_grp=n_exp // n_grp)
    row = lambda i: (i, 0)
    return pl.pallas_call(
        body,
        grid=(n_p_tiles + n_s_tiles,),
        in_specs=[
            pl.BlockSpec((tm, d_model), lambda i: (jnp.minimum(i, n_p_tiles - 1), 0)),
            pl.BlockSpec((tm, d_model), lambda i: (jnp.maximum(i - n_p_tiles, 0), 0)),
            pl.BlockSpec((tm // CHUNK, CHUNK * d_ssm), row),
            pl.BlockSpec((tm, d_conv), row),
            pl.BlockSpec((CHUNK, d_conv), lambda i: (jnp.maximum(i * hpt - 1, 0), 0)),
            pl.BlockSpec((CHUNK, d_conv), lambda i: (jnp.minimum((i + 1) * hpt, n_hblk - 1), 0)),
            _resident(c_meta.shape),
            pl.BlockSpec((tm, d_model), row),
            pl.BlockSpec((tm, d_model), row),
            _resident(wglu.shape), _resident(wpw.shape), _resident(wout.shape),
            _resident(wconv.shape), _resident(bdw.shape), _resident(lng.shape), _resident(lnb.shape),
            _resident(gffn.shape), _resident(rw.shape), _resident(rb.shape),
        ],
        out_specs=[pl.BlockSpec((tm, d_model), row), pl.BlockSpec((tm, d_model // 2), row),
                   pl.BlockSpec((tm, LANES), row), pl.BlockSpec((1, LANES), lambda i: (0, 0))],
        out_shape=[jax.ShapeDtypeStruct((t, d_model), F32),
                   jax.ShapeDtypeStruct((t, d_model // 2), I32),
                   jax.ShapeDtypeStruct((t, LANES), F32),
                   jax.ShapeDtypeStruct((1, LANES), F32)],
        scratch_shapes=[pltpu.VMEM((tm + 2 * CHUNK, d_conv), F32), pltpu.VMEM((tm, d_conv), F32),
                        pltpu.VMEM((d_ssm // LANES, tm, LANES), F32)],
        compiler_params=_cparams(1),
        name="mix",
    )(xp2, xs2, z, c, c, c, c_meta, sga, sgb, wglu, wpw, wout, wconv, bdw, lng, lnb, gffn, rw, rb)


def _positions_body(route_ref, offs_ref, dest_ref, carry_ref, *, te):
    @pl.when(pl.program_id(0) == 0)
    def _():
        carry_ref[...] = jnp.zeros_like(carry_ref)

    lane = lax.broadcasted_iota(I32, (te, LANES), 1).astype(F32)
    r = route_ref[...]
    oh1 = lane == r[:, 0:1]
    oh2 = lane == r[:, 1:2]
    both = jnp.where(oh1 | oh2, 1.0, 0.0)
    tri = jnp.where(lax.broadcasted_iota(I32, (te, te), 0) > lax.broadcasted_iota(I32, (te, te), 1),
                    1.0, 0.0).astype(BF16)
    before = jnp.dot(tri, both.astype(BF16), preferred_element_type=F32)
    base = before + carry_ref[...] + offs_ref[...]
    d1 = jnp.sum(jnp.where(oh1, base, 0.0), axis=-1, keepdims=True)
    d2 = jnp.sum(jnp.where(oh2, base, 0.0), axis=-1, keepdims=True)
    dest_ref[...] = jnp.where(lane == 0, d1, jnp.where(lane == 1, d2, 0.0)).astype(I32)
    carry_ref[...] += jnp.sum(both, axis=0, keepdims=True)


def _positions(route, offs, *, te):
    t = route.shape[0]
    return pl.pallas_call(
        functools.partial(_positions_body, te=te),
        grid=(t // te,),
        in_specs=[pl.BlockSpec((te, LANES), lambda i: (i, 0)),
                  pl.BlockSpec((1, LANES), lambda i: (0, 0))],
        out_specs=pl.BlockSpec((te, LANES), lambda i: (i, 0)),
        out_shape=jax.ShapeDtypeStruct((t, LANES), I32),
        scratch_shapes=[pltpu.VMEM((1, LANES), F32)],
        compiler_params=_cparams(1),
        name="positions",
    )(route, offs)


def _experts_body(te_ref, first_ref, last_ref, nxt_ref, short_ref, nused_ref,
                  xs_ref, w1_hbm, w3_hbm, w2_hbm, o_ref,
                  w1b, w3b, w2b, st1, st3, st2, sem, *, half, n_conv):
    i = pl.program_id(0)
    used = i < nused_ref[0]
    nxt = nxt_ref[i]

    def copies(e):
        return (pltpu.make_async_copy(w1_hbm.at[e], st1, sem.at[0]),
                pltpu.make_async_copy(w3_hbm.at[e], st3, sem.at[1]),
                pltpu.make_async_copy(w2_hbm.at[e], st2, sem.at[2]))

    def convert():
        r13 = st1.shape[0] // n_conv
        r2 = st2.shape[0] // n_conv

        def body(c, _):
            rows = pl.ds(pl.multiple_of(c * r13, r13), r13)
            w1b[rows, :] = st1[rows, :].astype(BF16)
            w3b[rows, :] = st3[rows, :].astype(BF16)
            rows2 = pl.ds(pl.multiple_of(c * r2, r2), r2)
            w2b[rows2, :] = st2[rows2, :].astype(BF16)
            return 0
        lax.fori_loop(0, n_conv, body, 0)

    @pl.when(i == 0)
    def _():
        for cp in copies(te_ref[0]):
            cp.start()
        for cp in copies(te_ref[0]):
            cp.wait()
        convert()

    @pl.when(used & (first_ref[i] == 1) & (nxt >= 0))
    def _():
        for cp in copies(nxt):
            cp.start(priority=1)

    def swiglu(n_rows):
        a, b = _unpack_bf16_pair(xs_ref[0:n_rows, :])
        a, b = a.astype(BF16), b.astype(BF16)

        def up(w_ref):
            return (jnp.dot(a, w_ref[0:half, :], preferred_element_type=F32)
                    + jnp.dot(b, w_ref[half:2 * half, :], preferred_element_type=F32))
        h1 = up(w1b)
        act = (h1 * _sigmoid(h1) * up(w3b)).astype(BF16)
        o = jnp.dot(act, w2b[...], preferred_element_type=F32)
        o_ref[0:n_rows, :] = _pack_bf16_pair(o[:, 0:half], o[:, half:2 * half])
        if n_rows < o_ref.shape[0]:
            o_ref[n_rows:, :] = jnp.zeros((o_ref.shape[0] - n_rows, half), I32)

    @pl.when(used & (short_ref[i] == 0))
    def _():
        swiglu(o_ref.shape[0])

    @pl.when(used & (short_ref[i] == 1))
    def _():
        swiglu(o_ref.shape[0] // 2)

    @pl.when(jnp.logical_not(used))
    def _():
        o_ref[...] = jnp.zeros_like(o_ref)

    @pl.when(used & (last_ref[i] == 1) & (nxt >= 0))
    def _():
        for cp in copies(nxt):
            cp.wait()
        convert()


def _experts(tile_expert, first, last, nxt, short, n_used, xs, w1, w3, w2, *, tme):
    rows, half = xs.shape
    n_e, d_model, d_exp = w1.shape
    grid_spec = pltpu.PrefetchScalarGridSpec(
        num_scalar_prefetch=6,
        grid=(rows // tme,),
        in_specs=[pl.BlockSpec((tme, half), lambda i, te, fi, la, nx, sh, nu: (jnp.minimum(i, nu[0] - 1), 0)),
                  pl.BlockSpec(memory_space=pl.ANY), pl.BlockSpec(memory_space=pl.ANY),
                  pl.BlockSpec(memory_space=pl.ANY)],
        out_specs=pl.BlockSpec((tme, half), lambda i, *_: (i, 0)),
        scratch_shapes=[pltpu.VMEM((d_model, d_exp), BF16), pltpu.VMEM((d_model, d_exp), BF16),
                        pltpu.VMEM((d_exp, d_model), BF16),
                        pltpu.VMEM((d_model, d_exp), F32), pltpu.VMEM((d_model, d_exp), F32),
                        pltpu.VMEM((d_exp, d_model), F32),
                        pltpu.SemaphoreType.DMA((3,))],
    )
    return pl.pallas_call(
        functools.partial(_experts_body, half=half, n_conv=8),
        grid_spec=grid_spec,
        out_shape=jax.ShapeDtypeStruct((rows, half), I32),
        compiler_params=_cparams(1),
        name="experts",
    )(tile_expert, first, last, nxt, short, n_used, xs, w1, w3, w2)


def _sc_gather(table, idx, *, chunk):
    n_rows, width = idx.shape[0], table.shape[1]
    n_workers = SC_CORES * SC_SUBCORES
    per_w = n_rows // n_workers
    n_chunks = per_w // chunk
    assert per_w * n_workers == n_rows and n_chunks * chunk == per_w and n_chunks % 2 == 0
    mesh = plsc.VectorSubcoreMesh(core_axis_name="c", subcore_axis_name="s",
                                  num_cores=SC_CORES, num_subcores=SC_SUBCORES)

    @functools.partial(
        pl.kernel, mesh=mesh,
        out_type=jax.ShapeDtypeStruct((n_rows, width), table.dtype),
        scratch_types=[pltpu.VMEM((per_w,), I32), pltpu.VMEM((2, chunk, width), table.dtype),
                       pltpu.SemaphoreType.DMA((2,))],
    )
    def gather_kernel(table_hbm, idx_hbm, out_hbm, idx_v, rows_v, sem):
        wid = lax.axis_index("s") * SC_CORES + lax.axis_index("c")
        base = pl.multiple_of(wid * per_w, per_w)
        pltpu.sync_copy(idx_hbm.at[pl.ds(base, per_w)], idx_v)

        def gather(j, slot):
            off = pl.multiple_of(j * chunk, chunk)
            return pltpu.make_async_copy(table_hbm.at[idx_v.at[pl.ds(off, chunk)]],
                                         rows_v.at[slot], sem.at[slot])

        gather(0, 0).start()

        @pl.loop(0, n_chunks, step=2)
        def _(j):
            for slot in range(2):
                jj = j + slot
                gather(jj, slot).wait()

                @pl.when(jj + 1 < n_chunks)
                def _():
                    gather(jj + 1, 1 - slot).start()
                off = pl.multiple_of(base + jj * chunk, chunk)
                pltpu.sync_copy(rows_v.at[slot], out_hbm.at[pl.ds(off, chunk)])

    return gather_kernel(table, idx)


def _sc_inverse(dest_flat, n_rows):
    n_assign = dest_flat.shape[0]
    n_tok = n_assign // 2
    n_workers = SC_CORES * SC_SUBCORES
    per_w = n_rows // n_workers
    assert per_w * n_workers == n_rows and per_w % SC_LANES == 0 and n_assign % SC_LANES == 0
    mesh = plsc.VectorSubcoreMesh(core_axis_name="c", subcore_axis_name="s",
                                  num_cores=SC_CORES, num_subcores=SC_SUBCORES)

    @functools.partial(
        pl.kernel, mesh=mesh,
        out_type=jax.ShapeDtypeStruct((n_rows,), I32),
        scratch_types=[pltpu.VMEM((n_assign,), I32), pltpu.VMEM((per_w,), I32)],
        compiler_params=pltpu.CompilerParams(needs_layout_passes=False),
    )
    def inverse_kernel(dest_hbm, src_hbm, dest_v, src_v):
        wid = lax.axis_index("s") * SC_CORES + lax.axis_index("c")
        lo = pl.multiple_of(wid * per_w, per_w)
        pltpu.sync_copy(dest_hbm, dest_v)
        lane = lax.iota(I32, SC_LANES)

        @pl.loop(0, per_w, step=SC_LANES)
        def _(i):
            src_v[pl.ds(pl.multiple_of(i, SC_LANES), SC_LANES)] = lax.rem(lo + i + lane, n_tok)

        @pl.loop(0, n_assign, step=SC_LANES)
        def _(i):
            d = dest_v[pl.ds(pl.multiple_of(i, SC_LANES), SC_LANES)]
            a = i + lane
            tok = jnp.where(a >= n_tok, a - n_tok, a)
            mine = (d >= lo) & (d < lo + per_w)
            plsc.store_scatter(src_v, [d - lo], tok, mask=mine)

        pltpu.sync_copy(src_v, src_hbm.at[pl.ds(lo, per_w)])

    return inverse_kernel(dest_flat)


def _final_body(g1_ref, g2_ref, h1_ref, route_ref, g_ref, y_ref, *, half):
    route = route_ref[...]
    w1, w2 = route[:, 2:3], route[:, 3:4]
    a1, b1 = _unpack_bf16_pair(g1_ref[...])
    a2, b2 = _unpack_bf16_pair(g2_ref[...])
    h1 = h1_ref[...]
    ha = h1[:, 0:half] + (a1 * w1 + a2 * w2)
    hb = h1[:, half:2 * half] + (b1 * w1 + b2 * w2)
    ms = (jnp.sum(ha * ha, axis=-1, keepdims=True)
          + jnp.sum(hb * hb, axis=-1, keepdims=True)) / (2 * half)
    inv = lax.rsqrt(ms + EPS)
    y_ref[:, 0:half] = ha * inv * g_ref[:, 0:half]
    y_ref[:, half:2 * half] = hb * inv * g_ref[:, half:2 * half]


def _final(gathered, h1, route, g, *, tf, row0, n_rows):
    t, d_model = h1.shape
    half = d_model // 2
    tile0, tiles_t = row0 // tf, t // tf
    return pl.pallas_call(
        functools.partial(_final_body, half=half),
        grid=(n_rows // tf,),
        in_specs=[pl.BlockSpec((tf, half), lambda i: (tile0 + i, 0)),
                  pl.BlockSpec((tf, half), lambda i: (tiles_t + tile0 + i, 0)),
                  pl.BlockSpec((tf, d_model), lambda i: (tile0 + i, 0)),
                  pl.BlockSpec((tf, LANES), lambda i: (tile0 + i, 0)),
                  pl.BlockSpec((1, d_model), lambda i: (0, 0))],
        out_specs=pl.BlockSpec((tf, d_model), lambda i: (i, 0)),
        out_shape=jax.ShapeDtypeStruct((n_rows, d_model), F32),
        compiler_params=_cparams(1),
        name="final",
    )(gathered, gathered, h1, route, g)


def kernel(x_prompt, x_sample, meta, norm_mix_g, w_in, ssm_lam_re, ssm_lam_im, ssm_log_step, ssm_b_re, ssm_b_im, ssm_c_re, ssm_c_im, ssm_d, ssm_w_glu, conv_w_dw, conv_b_dw, conv_ln_g, conv_ln_b, conv_w_pw, w_out, norm_ffn_g, router_group_w, router_group_b, router_expert_w, router_expert_b, expert_w1, expert_w3, expert_w2, final_g):
    assert w_in.shape[0] == 1, "single-layer trunk"
    bp, lp, d_model = x_prompt.shape
    bs, ls, _ = x_sample.shape
    n_meta = meta.shape[0]
    d_ssm = ssm_d.shape[-1]
    d_conv = conv_b_dw.shape[-1]
    n_ch = ssm_b_re.shape[-1]
    n_g = ssm_b_re.shape[2]
    n_grp = router_group_w.shape[-1]
    n_exp = router_expert_w.shape[-1]
    assert n_meta == CHUNK and lp % CHUNK == 0 and ls % CHUNK == 0
    n_p, n_s = bp * lp, bs * ls
    t = n_p + n_s
    tm = min(256, lp, ls)
    assert lp % tm == 0 and ls % tm == 0

    xp2 = x_prompt.reshape(n_p, d_model)
    xs2 = x_sample.reshape(n_s, d_model)
    row = lambda a: a.reshape(1, -1)
    w_in_bf = w_in[0].astype(BF16)

    assert n_ch == CHUNK and LANES % n_ch == 0 and n_g % (LANES // n_ch) == 0
    per_blk = LANES // n_ch
    xg, c, sga, sgb = _inproj(xp2, xs2, row(norm_mix_g[0]), w_in_bf, d_ssm=d_ssm, d_conv=d_conv,
                              tm=tm, n_ch=n_ch)
    xg_meta, c_meta = _meta_inproj(meta, row(norm_mix_g[0]), w_in_bf, d_ssm=d_ssm, d_conv=d_conv,
                                   n_ch=n_ch)

    intra, bst, cst, coef = _sg_prep(ssm_lam_re[0], ssm_lam_im[0], ssm_log_step[0], ssm_b_re[0],
                                     ssm_b_im[0], ssm_c_re[0], ssm_c_im[0], ssm_d[0])
    nc = t // CHUNK
    n_blk = d_ssm // LANES
    rt = min(512, nc)
    sf, sb, init = _sg_mm_state(xg, xg_meta, bst, rt=rt, per_blk=per_blk)
    n_sub = sf.shape[0] // (n_blk * nc)
    cb = min(64, lp // CHUNK, ls // CHUNK)
    xf, xb = _sg_scan(sf.reshape(n_blk, nc, n_sub, LANES), sb.reshape(n_blk, nc, n_sub, LANES),
                      coef, init, cb=cb, chunks_pseq=lp // CHUNK, chunks_sseq=ls // CHUNK,
                      n_pchunks=n_p // CHUNK)
    z = _sg_mm_out(xg, intra, xf.reshape(sf.shape), xb.reshape(sb.shape), cst, rt=rt,
                   per_blk=per_blk)

    assert n_grp + n_exp <= LANES // 2
    rw32 = jnp.pad(jnp.concatenate([router_group_w[0], router_expert_w[0]], axis=1),
                   ((0, 0), (0, LANES // 2 - n_grp - n_exp)))
    rw_hi = rw32.astype(BF16)
    rw = jnp.concatenate([rw_hi, (rw32 - rw_hi.astype(F32)).astype(BF16)], axis=1)
    rb = jnp.zeros((1, LANES), F32).at[0, 0:n_grp].set(router_group_b[0]).at[
        0, n_grp:n_grp + n_exp].set(router_expert_b[0])
    conv_w = conv_w_dw.shape[1]
    off = CHUNK - conv_w // 2
    assert 0 <= off and conv_w + off <= 2 * CHUNK
    taps = jnp.pad(conv_w_dw[0], ((off, -(conv_w + off) % 8), (0, 0)))
    h1, v, route, cnt = _mix(
        xp2, xs2, z, c, c_meta, sga, sgb, ssm_w_glu[0].astype(BF16), conv_w_pw[0].astype(BF16),
        w_out[0].astype(BF16), _conv_weights(taps), row(conv_b_dw[0]), row(conv_ln_g[0]),
        row(conv_ln_b[0]), row(norm_ffn_g[0]), rw, rb,
        tm=tm, p_seq=lp, s_seq=ls, n_grp=n_grp, n_exp=n_exp)

    tme = 256
    counts = cnt[0, 0:n_exp].astype(I32)
    tiles_e = (counts + tme - 1) // tme
    tile_end = jnp.cumsum(tiles_e)
    n_used = tile_end[-1]
    offs = jnp.zeros((1, LANES), F32).at[0, 0:n_exp].set(((tile_end - tiles_e) * tme).astype(F32))
    n_tiles = (2 * t) // tme + n_exp
    ids = jnp.arange(n_tiles, dtype=I32)

    def expert_of(tile):
        return jnp.minimum(jnp.sum((tile[:, None] >= tile_end[None, :]).astype(I32), axis=1), n_exp - 1)
    te_map = expert_of(jnp.minimum(ids, n_used - 1))
    onehot = te_map[:, None] == jnp.arange(n_exp, dtype=I32)[None, :]
    run_end = jnp.sum(jnp.where(onehot, tile_end[None, :], 0), axis=1)
    run_start = run_end - jnp.sum(jnp.where(onehot, tiles_e[None, :], 0), axis=1)
    valid = ids < n_used
    first = (valid & (ids == run_start)).astype(I32)
    last = (valid & (ids == run_end - 1)).astype(I32)
    nxt = jnp.where(valid & (run_end < n_used), expert_of(run_end), -1)
    tail_rows = counts - (tiles_e - 1) * tme
    short_e = ((tiles_e > 0) & (tail_rows <= tme // 2)).astype(I32)
    short = last * jnp.sum(jnp.where(onehot, short_e[None, :], 0), axis=1)

    dest = _positions(route, offs, te=min(512, t))
    idx = jnp.concatenate([dest[:, 0], dest[:, 1]])
    src = _sc_inverse(idx, n_tiles * tme)
    xs = _sc_gather(v, src, chunk=32)
    o = _experts(te_map, first, last, nxt, short, n_used.reshape(1), xs, expert_w1[0],
                 expert_w3[0], expert_w2[0], tme=tme)

    tf = min(256, n_p, n_s)
    gathered = _sc_gather(o, idx, chunk=32)
    fg = row(final_g)
    y_p = _final(gathered, h1, route, fg, tf=tf, row0=0, n_rows=n_p)
    y_s = _final(gathered, h1, route, fg, tf=tf, row0=n_p, n_rows=n_s)
    return (y_p.reshape(bp, lp, d_model), y_s.reshape(bs, ls, d_model))
```

```python
import functools

import jax
import jax.numpy as jnp
from jax import lax
from jax.experimental import pallas as pl
from jax.experimental.pallas import tpu as pltpu
from jax.experimental.pallas import tpu_sc as plsc

F32 = jnp.float32
BF16 = jnp.bfloat16
U32 = jnp.uint32
I32 = jnp.int32

EPS = 1e-6
LAM_RE_MAX = -1e-4
CHUNK = 16
LANES = 128
VMEM_LIMIT = 56 << 20
SC_CORES = 2
SC_SUBCORES = 16
SC_LANES = 16
HIGHEST = lax.Precision.HIGHEST


def _cparams(n_axes):
    return pltpu.CompilerParams(dimension_semantics=("arbitrary",) * n_axes,
                                vmem_limit_bytes=VMEM_LIMIT)


def _resident(shape):
    nd = len(shape)
    return pl.BlockSpec(shape, lambda *_: (0,) * nd, pipeline_mode=pl.Buffered(1))


def _sigmoid(x):
    return 1.0 / (1.0 + jnp.exp(-x))


def _rms(x, g):
    return x * lax.rsqrt(jnp.mean(x * x, axis=-1, keepdims=True) + EPS) * g


def _pack_bf16_pair(a, b):
    def rnd(x):
        u = pltpu.bitcast(x, U32)
        return (u + jnp.uint32(0x7FFF) + ((u >> 16) & jnp.uint32(1))) >> 16
    return pltpu.bitcast((rnd(a) << 16) | rnd(b), I32)


def _unpack_bf16_pair(p):
    p = pltpu.bitcast(p, U32)
    a = pltpu.bitcast(p & jnp.uint32(0xFFFF0000), F32)
    b = pltpu.bitcast(p << 16, F32)
    return a, b


def _regroup(pieces, n_ch):
    per = LANES // n_ch
    lane = lax.broadcasted_iota(I32, pieces[0].shape, 1)
    masks = [(lane >= a * n_ch) & (lane < (a + 1) * n_ch) for a in range(per)]
    out = []
    for b in range(per):
        acc = None
        for a in range(per):
            shift = ((a - b) * n_ch) % LANES
            src = pieces[a] if shift == 0 else pltpu.roll(pieces[a], shift, 1)
            acc = jnp.where(masks[a], src, 0.0) if acc is None else jnp.where(masks[a], src, acc)
        out.append(acc)
    return out


def _inproj_body(xp_ref, xs_ref, g_ref, w_ref, xg_ref, c_ref, sga_ref, sgb_ref, usc, *,
                 tm, n_p_tiles, d_ssm, d_conv, d_model, col, n_ch):
    i = pl.program_id(0)
    x = jnp.where(i < n_p_tiles, xp_ref[...], xs_ref[...])
    y = _rms(x, g_ref[...]).astype(BF16)

    def proj(lo, n):
        return jnp.dot(y, w_ref[:, lo:lo + n], preferred_element_type=F32)

    for k in range(d_ssm // col):
        u = proj(k * col, col)
        for b in range(col // LANES):
            usc[k * (col // LANES) + b] = u[:, b * LANES:(b + 1) * LANES]
    per = LANES // n_ch
    for b in range(d_ssm // LANES):
        for hf in range(CHUNK // per):
            tokens = [usc[b, pl.ds(hf * per + a, tm // CHUNK, stride=CHUNK), :] for a in range(per)]
            for g, blk in enumerate(_regroup(tokens, n_ch)):
                c0 = (b * per + g) * CHUNK * n_ch + hf * LANES
                xg_ref[:, c0:c0 + LANES] = blk.astype(BF16)
    for k in range(d_conv // col):
        cv = proj(d_ssm + k * col, col)
        cg = proj(d_ssm + d_conv + k * col, col)
        c_ref[:, k * col:(k + 1) * col] = (cv * _sigmoid(cg)).astype(BF16)
    base = d_ssm + 2 * d_conv
    for k in range(d_model // col):
        sga_ref[:, k * col:(k + 1) * col] = _sigmoid(proj(base + k * col, col)).astype(BF16)
        sgb_ref[:, k * col:(k + 1) * col] = _sigmoid(
            proj(base + d_model + k * col, col)).astype(BF16)


def _inproj(xp2, xs2, g, w_bf, *, d_ssm, d_conv, tm, n_ch):
    n_p, d_model = xp2.shape
    n_s = xs2.shape[0]
    n_p_tiles, n_s_tiles = n_p // tm, n_s // tm
    t = n_p + n_s
    col = min(1024, d_ssm, d_conv, d_model)
    body = functools.partial(_inproj_body, tm=tm, n_p_tiles=n_p_tiles, d_ssm=d_ssm, d_conv=d_conv,
                             d_model=d_model, col=col, n_ch=n_ch)
    row = lambda i: (i, 0)
    return pl.pallas_call(
        body,
        grid=(n_p_tiles + n_s_tiles,),
        in_specs=[
            pl.BlockSpec((tm, d_model), lambda i: (jnp.minimum(i, n_p_tiles - 1), 0)),
            pl.BlockSpec((tm, d_model), lambda i: (jnp.maximum(i - n_p_tiles, 0), 0)),
            _resident((1, d_model)),
            _resident(w_bf.shape),
        ],
        out_specs=[pl.BlockSpec((tm // CHUNK, CHUNK * d_ssm), row), pl.BlockSpec((tm, d_conv), row),
                   pl.BlockSpec((tm, d_model), row), pl.BlockSpec((tm, d_model), row)],
        out_shape=[jax.ShapeDtypeStruct((t // CHUNK, CHUNK * d_ssm), BF16),
                   jax.ShapeDtypeStruct((t, d_conv), BF16),
                   jax.ShapeDtypeStruct((t, d_model), BF16), jax.ShapeDtypeStruct((t, d_model), BF16)],
        scratch_shapes=[pltpu.VMEM((d_ssm // LANES, tm, LANES), F32)],
        compiler_params=_cparams(1),
        name="inproj",
    )(xp2, xs2, g, w_bf)


def _meta_body(m_ref, g_ref, w_ref, xg_ref, c_ref, *, d_ssm, d_conv, n_ch):
    y = _rms(m_ref[...], g_ref[...]).astype(BF16)
    u = jnp.dot(y, w_ref[:, 0:d_ssm], preferred_element_type=F32)
    first = lax.broadcasted_iota(I32, (CHUNK, LANES), 0) == 0
    per = LANES // n_ch
    for b in range(d_ssm // LANES):
        for hf in range(CHUNK // per):
            tokens = [u[hf * per + a:hf * per + a + 1, b * LANES:(b + 1) * LANES] for a in range(per)]
            for g, blk in enumerate(_regroup(tokens, n_ch)):
                c0 = (b * per + g) * CHUNK * n_ch + hf * LANES
                piece = jnp.broadcast_to(blk, (CHUNK, LANES))
                xg_ref[:, c0:c0 + LANES] = jnp.where(first, piece, 0.0).astype(BF16)
    cv = jnp.dot(y, w_ref[:, d_ssm:d_ssm + d_conv], preferred_element_type=F32)
    cg = jnp.dot(y, w_ref[:, d_ssm + d_conv:d_ssm + 2 * d_conv], preferred_element_type=F32)
    c_ref[...] = (cv * _sigmoid(cg)).astype(BF16)


def _meta_inproj(meta, g, w_bf, *, d_ssm, d_conv, n_ch):
    n_meta, d_model = meta.shape
    ncol = d_ssm + 2 * d_conv
    return pl.pallas_call(
        functools.partial(_meta_body, d_ssm=d_ssm, d_conv=d_conv, n_ch=n_ch),
        grid=(1,),
        in_specs=[pl.BlockSpec((n_meta, d_model), lambda i: (0, 0)),
                  pl.BlockSpec((1, d_model), lambda i: (0, 0)),
                  pl.BlockSpec((d_model, ncol), lambda i: (0, 0))],
        out_specs=[pl.BlockSpec((CHUNK, CHUNK * d_ssm), lambda i: (0, 0)),
                   pl.BlockSpec((n_meta, d_conv), lambda i: (0, 0))],
        out_shape=[jax.ShapeDtypeStruct((CHUNK, CHUNK * d_ssm), BF16),
                   jax.ShapeDtypeStruct((n_meta, d_conv), BF16)],
        compiler_params=_cparams(1),
        name="meta_inproj",
    )(meta, g, w_bf)


def _cmul(ar, ai, br, bi):
    return ar * br - ai * bi, ar * bi + ai * br


def _discretize(lam_re, lam_im, log_step):
    lr = jnp.minimum(lam_re, LAM_RE_MAX)
    dt = jnp.exp(log_step)
    mag = jnp.exp(lr * dt)
    ar = mag * jnp.cos(lam_im * dt)
    ai = mag * jnp.sin(lam_im * dt)
    den = lr * lr + lam_im * lam_im
    nr = ar - 1.0
    fr = (nr * lr + ai * lam_im) / den
    fi = (ai * lr - nr * lam_im) / den
    return ar, ai, fr, fi


def _cpow(ar, ai, k, nbits, shape):
    pr = jnp.ones(shape, F32)
    pi = jnp.zeros(shape, F32)
    br = jnp.broadcast_to(ar, shape)
    bi = jnp.broadcast_to(ai, shape)
    kk = jnp.broadcast_to(k, shape)
    for b in range(nbits):
        sel = ((kk >> b) & 1) == 1
        nr, ni = _cmul(pr, pi, br, bi)
        pr = jnp.where(sel, nr, pr)
        pi = jnp.where(sel, ni, pi)
        br, bi = _cmul(br, bi, br, bi)
    return pr, pi


def _tile_lanes(x, n_ch, width):
    span = n_ch
    while span < LANES:
        x = x + pltpu.roll(x, span, 1)
        span *= 2
    return jnp.concatenate([x] * (width // LANES), axis=1)


def _disc_body(lam_re_ref, lam_im_ref, ls_ref, ar_ref, ai_ref, fr_ref, fi_ref):
    ar, ai, fr, fi = _discretize(lam_re_ref[...], lam_im_ref[...], ls_ref[...])
    ar_ref[...] = ar
    ai_ref[...] = ai
    fr_ref[...] = fr
    fi_ref[...] = fi


def _ssm_discretize(lam_re, lam_im, log_step):
    shape = lam_re.shape
    rows = shape[0] * shape[1]
    flat = lambda x: x.reshape(rows, shape[2])
    ls = jnp.broadcast_to(log_step[..., None], shape)
    full = pl.BlockSpec((rows, shape[2]), lambda i: (0, 0))
    out = pl.pallas_call(
        _disc_body,
        grid=(1,),
        in_specs=[full, full, full],
        out_specs=[full] * 4,
        out_shape=[jax.ShapeDtypeStruct((rows, shape[2]), F32)] * 4,
        compiler_params=_cparams(1),
        name="ssm_disc",
    )(flat(lam_re), flat(lam_im), flat(ls))
    return [x.reshape(shape) for x in out]


def _sg_prep_body(ar_c, ai_c, fr_c, fi_c, ct_re, ct_im, bt_re, bt_im,
                  ar_r, ai_r, fr_r, fi_r, btr_re, btr_im, dpad,
                  m_ref, b_ref, c_ref, aq_ref, *, n_state, n_ch, per_step):
    per_blk = LANES // n_ch
    first = (pl.program_id(0) * per_step) % per_blk
    for j in range(per_step):
        _sg_prep_group(j, first + j, ar_c, ai_c, fr_c, fi_c, ct_re, ct_im, bt_re, bt_im,
                       ar_r, ai_r, fr_r, fi_r, btr_re, btr_im, dpad,
                       m_ref, b_ref, c_ref, aq_ref, n_state=n_state, n_ch=n_ch)


def _sg_prep_group(j, gl, ar_c, ai_c, fr_c, fi_c, ct_re, ct_im, bt_re, bt_im,
                   ar_r, ai_r, fr_r, fi_r, btr_re, btr_im, dpad,
                   m_ref, b_ref, c_ref, aq_ref, *, n_state, n_ch):
    q = CHUNK
    width = q * n_ch
    rowmask = (lax.broadcasted_iota(I32, (2 * n_state, width), 0) // n_state) == gl % 2
    kblk = lax.broadcasted_iota(I32, (1, width), 1) // n_ch
    strips = []
    for d in range(2):
        ar, ai, fr, fi = ar_c[d, j], ai_c[d, j], fr_c[d, j], fi_c[d, j]
        kexp = kblk if d == 0 else (q - 1) - kblk
        wr, wi = _cpow(ar, ai, kexp, 4, (n_state, width))
        gcr, gci = _cmul(_tile_lanes(ct_re[d, j], n_ch, width),
                         _tile_lanes(ct_im[d, j], n_ch, width), wr, wi)
        gfr, gfi = _cmul(gcr, gci, fr, fi)
        strips.append(jnp.dot(bt_re[d, j], gfr, precision=HIGHEST, preferred_element_type=F32)
                      - jnp.dot(bt_im[d, j], gfi, precision=HIGHEST, preferred_element_type=F32))
        g1r, g1i = _cmul(gcr, gci, ar, ai)
        for comp, val in ((0, g1r), (1, -g1i)):
            r0 = (2 * d + comp) * 2 * n_state
            both = jnp.concatenate([val, val], axis=0)
            c_ref[j, r0:r0 + 2 * n_state, :] = jnp.where(rowmask, both, 0.0).astype(BF16)

    zf, zb = strips
    zero = jnp.zeros((n_ch, width), F32)
    z512 = jnp.concatenate([zb, zero], axis=1) + pltpu.roll(
        jnp.concatenate([zf, zero], axis=1), (q - 1) * n_ch, 1)
    row = lax.broadcasted_iota(I32, (n_ch, 2 * width), 0)
    lane = lax.broadcasted_iota(I32, (n_ch, 2 * width), 1)
    z512 = z512 + jnp.where(lane - (q - 1) * n_ch == row, dpad[j], 0.0)
    for s in range(q):
        sh = (q - 1 - s) * n_ch
        blk = z512 if sh == 0 else pltpu.roll(z512, 2 * width - sh, 1)
        m_ref[j, n_ch * s:n_ch * (s + 1), :] = blk[:, 0:width].astype(BF16)

    parmask = (lax.broadcasted_iota(I32, (1, 2 * n_state), 1) // n_state) == gl % 2
    pieces = [[None] * 4 for _ in range(q)]
    for d in range(2):
        ar, ai, fr, fi = ar_r[d, j], ai_r[d, j], fr_r[d, j], fi_r[d, j]
        pw = [(jnp.ones_like(ar), jnp.zeros_like(ar))]
        for _ in range(q):
            pw.append(_cmul(pw[-1][0], pw[-1][1], ar, ai))
        for s in range(q):
            e = (q - 1 - s) if d == 0 else s
            cr, ci = _cmul(fr, fi, pw[e][0], pw[e][1])
            br, bi = _cmul(btr_re[d, j], btr_im[d, j], cr, ci)
            pieces[s][2 * d] = jnp.where(parmask, br, 0.0)
            pieces[s][2 * d + 1] = jnp.where(parmask, bi, 0.0)
        aq_ref[j, :, 2 * d * n_state:(2 * d + 1) * n_state] = pw[q][0][:, 0:n_state]
        aq_ref[j, :, (2 * d + 1) * n_state:(2 * d + 2) * n_state] = pw[q][1][:, 0:n_state]
    for s in range(q):
        b_ref[j, n_ch * s:n_ch * (s + 1), :] = jnp.concatenate(pieces[s], axis=1).astype(BF16)


def _sg_prep(lam_re, lam_im, log_step, b_re, b_im, c_re, c_im, d_skip):
    _, n_g, n_state, n_ch = b_re.shape
    q = CHUNK
    width = q * n_ch
    per_blk = LANES // n_ch
    n_blk = n_g // per_blk
    dup = lambda x: jnp.concatenate([x, x], axis=-1)
    lane_pad = lambda x: jnp.pad(x, ((0, 0),) * 3 + ((0, LANES - n_ch),))
    disc = _ssm_discretize(lam_re, lam_im, log_step)
    disc_c = [x[..., None] for x in disc]
    disc_r = [dup(x)[:, :, None, :] for x in disc]
    ct_re = lane_pad(jnp.swapaxes(c_re, -1, -2))
    ct_im = lane_pad(jnp.swapaxes(c_im, -1, -2))
    bt_re, bt_im = jnp.swapaxes(b_re, -1, -2), jnp.swapaxes(b_im, -1, -2)
    btr_re, btr_im = dup(bt_re), dup(bt_im)
    dpad = jnp.pad(d_skip.reshape(n_g, 1, n_ch), ((0, 0), (0, 0), ((q - 1) * n_ch, width)))

    per_step = 2
    sw = 8 * n_state

    def dspec(shape):
        return pl.BlockSpec((2, per_step) + shape, lambda g: (0, g, 0, 0))

    def gspec(shape):
        return pl.BlockSpec((per_step,) + shape, lambda g: (g, 0, 0))

    body = functools.partial(_sg_prep_body, n_state=n_state, n_ch=n_ch, per_step=per_step)
    intra, bst, cst, aq = pl.pallas_call(
        body,
        grid=(n_g // per_step,),
        in_specs=[dspec((n_state, 1))] * 4 + [
                  dspec((n_state, LANES)), dspec((n_state, LANES)),
                  dspec((n_ch, n_state)), dspec((n_ch, n_state))] + [dspec((1, 2 * n_state))] * 4 + [
                  dspec((n_ch, 2 * n_state)), dspec((n_ch, 2 * n_state)),
                  gspec((1, 2 * width))],
        out_specs=[gspec((width, width)), gspec((width, sw)), gspec((sw, width)),
                   gspec((1, 4 * n_state))],
        out_shape=[jax.ShapeDtypeStruct((n_g, width, width), BF16),
                   jax.ShapeDtypeStruct((n_g, width, sw), BF16),
                   jax.ShapeDtypeStruct((n_g, sw, width), BF16),
                   jax.ShapeDtypeStruct((n_g, 1, 4 * n_state), F32)],
        compiler_params=_cparams(1),
        name="ssm_prep",
    )(*disc_c, ct_re, ct_im, bt_re, bt_im, *disc_r, btr_re, btr_im, dpad)
    a = aq.reshape(n_blk, per_blk // 2, 2, 2, 2, n_state)
    a = a.transpose(3, 4, 0, 1, 2, 5).reshape(2, 2, n_blk, per_blk // 2, LANES)
    coef = []
    for d in range(2):
        re, im = a[d, 0], a[d, 1]
        coef.append((jnp.concatenate([re, re], axis=1), jnp.concatenate([-im, im], axis=1)))
    return intra, bst, cst, coef


def _sg_mm_state_body(x_ref, xm_ref, b_ref, sf_ref, sb_ref, init_ref, *, rt, n_pair, width):
    def pair_update(x, q):
        g0, g1 = 2 * q, 2 * q + 1
        return (jnp.dot(x[:, g0 * width:(g0 + 1) * width], b_ref[g0], preferred_element_type=F32)
                + jnp.dot(x[:, g1 * width:(g1 + 1) * width], b_ref[g1], preferred_element_type=F32))

    n_sub = 2 * n_pair
    for q in range(n_pair):
        r = pair_update(x_ref, q)
        sf_ref[pl.ds(q, rt, stride=n_sub), :] = r[:, 0:LANES]
        sf_ref[pl.ds(n_pair + q, rt, stride=n_sub), :] = r[:, LANES:2 * LANES]
        sb_ref[pl.ds(q, rt, stride=n_sub), :] = r[:, 2 * LANES:3 * LANES]
        sb_ref[pl.ds(n_pair + q, rt, stride=n_sub), :] = r[:, 3 * LANES:4 * LANES]

    @pl.when(pl.program_id(1) == 0)
    def _():
        for q in range(n_pair):
            r0 = pair_update(xm_ref, q)
            init_ref[q:q + 1, :] = r0[0:1, 0:LANES]
            init_ref[n_pair + q:n_pair + q + 1, :] = r0[0:1, LANES:2 * LANES]


def _sg_mm_state(xg, xg_meta, bst, *, rt, per_blk):
    nc = xg.shape[0]
    n_g, width, sw = bst.shape
    n_blk = n_g // per_blk
    kw = per_blk * width
    n_pair = per_blk // 2
    n_sub = 2 * n_pair
    n_rt = nc // rt
    body = functools.partial(_sg_mm_state_body, rt=rt, n_pair=n_pair, width=width)
    dense = jax.ShapeDtypeStruct((n_blk * nc * n_sub, LANES), F32)
    dspec = pl.BlockSpec((rt * n_sub, LANES), lambda b, i: (b * n_rt + i, 0))
    return pl.pallas_call(
        body,
        grid=(n_blk, n_rt),
        in_specs=[pl.BlockSpec((rt, kw), lambda b, i: (i, b)),
                  pl.BlockSpec((CHUNK, kw), lambda b, i: (0, b)),
                  pl.BlockSpec((per_blk, width, sw), lambda b, i: (b, 0, 0))],
        out_specs=[dspec, dspec, pl.BlockSpec((None, n_sub, LANES), lambda b, i: (b, 0, 0))],
        out_shape=[dense, dense, jax.ShapeDtypeStruct((n_blk, n_sub, LANES), F32)],
        compiler_params=_cparams(2),
        name="ssm_state",
    )(xg, xg_meta, bst)


def _sg_scan_body(sf_ref, sb_ref, a1f_ref, a2f_ref, a1b_ref, a2b_ref, init_ref, xf_ref, xb_ref,
                  stf, stb, *, cb, n_blk, half, n_pblk, blk_pseq, blk_sseq):
    j = pl.program_id(0)
    pos = jnp.where(j < n_pblk, lax.rem(j, blk_pseq), lax.rem(jnp.maximum(j - n_pblk, 0), blk_sseq))

    @pl.when(pos == 0)
    def _():
        stf[...] = init_ref[...]
        stb[...] = jnp.zeros_like(stb)

    def body(i, carry):
        ef, eb = carry
        ib = cb - 1 - i
        nf, nb = [], []
        for g in range(n_blk):
            xf_ref[g, i] = ef[g]
            xb_ref[g, ib] = eb[g]
            nf.append(a1f_ref[g] * ef[g] + a2f_ref[g] * pltpu.roll(ef[g], half, 0) + sf_ref[g, i])
            nb.append(a1b_ref[g] * eb[g] + a2b_ref[g] * pltpu.roll(eb[g], half, 0) + sb_ref[g, ib])
        return tuple(nf), tuple(nb)

    ef, eb = lax.fori_loop(0, cb, body, (tuple(stf[g] for g in range(n_blk)),
                                         tuple(stb[g] for g in range(n_blk))))
    for g in range(n_blk):
        stf[g] = ef[g]
        stb[g] = eb[g]


def _sg_scan(sf, sb, coef, init, *, cb, chunks_pseq, chunks_sseq, n_pchunks):
    n_blk, nc, n_sub, _ = sf.shape
    n_pblk, blk_pseq, blk_sseq = n_pchunks // cb, chunks_pseq // cb, chunks_sseq // cb

    def bwd_block(j):
        in_p = j < n_pblk
        pos = jnp.where(in_p, lax.rem(j, blk_pseq), lax.rem(jnp.maximum(j - n_pblk, 0), blk_sseq))
        ln = jnp.where(in_p, blk_pseq, blk_sseq)
        return j - pos + ln - 1 - pos

    body = functools.partial(_sg_scan_body, cb=cb, n_blk=n_blk, half=n_sub // 2, n_pblk=n_pblk,
                             blk_pseq=blk_pseq, blk_sseq=blk_sseq)
    fwd = pl.BlockSpec((n_blk, cb, n_sub, LANES), lambda j: (0, j, 0, 0))
    bwd = pl.BlockSpec((n_blk, cb, n_sub, LANES), lambda j: (0, bwd_block(j), 0, 0))
    small = pl.BlockSpec((n_blk, n_sub, LANES), lambda j: (0, 0, 0))
    return pl.pallas_call(
        body,
        grid=(nc // cb,),
        in_specs=[fwd, bwd, small, small, small, small, small],
        out_specs=[fwd, bwd],
        out_shape=[jax.ShapeDtypeStruct(sf.shape, F32), jax.ShapeDtypeStruct(sb.shape, F32)],
        scratch_shapes=[pltpu.VMEM((n_blk, n_sub, LANES), F32), pltpu.VMEM((n_blk, n_sub, LANES), F32)],
        compiler_params=_cparams(1),
        name="ssm_scan",
    )(sf, sb, coef[0][0], coef[0][1], coef[1][0], coef[1][1], init)


def _gelu_tanh(x):
    return 0.5 * x * (1.0 + jnp.tanh(0.7978845608028654 * (x + 0.044715 * (x * x * x))))


def _sg_mm_out_body(x_ref, m_ref, xf_ref, xb_ref, c_ref, z_ref, *, rt, n_pair, width):
    n_sub = 2 * n_pair
    for q in range(n_pair):
        state = jnp.concatenate(
            [xf_ref[pl.ds(q, rt, stride=n_sub), :], xf_ref[pl.ds(n_pair + q, rt, stride=n_sub), :],
             xb_ref[pl.ds(q, rt, stride=n_sub), :], xb_ref[pl.ds(n_pair + q, rt, stride=n_sub), :]],
            axis=1).astype(BF16)
        for g in (2 * q, 2 * q + 1):
            cols = slice(g * width, (g + 1) * width)
            y = (jnp.dot(x_ref[:, cols], m_ref[g], preferred_element_type=F32)
                 + jnp.dot(state, c_ref[g], preferred_element_type=F32))
            z_ref[:, cols] = _gelu_tanh(y).astype(BF16)


def _sg_mm_out(xg, intra, xf, xb, cst, *, rt, per_blk):
    nc = xg.shape[0]
    n_g, sw, width = cst.shape
    n_blk = n_g // per_blk
    kw = per_blk * width
    n_pair = per_blk // 2
    n_sub = 2 * n_pair
    n_rt = nc // rt
    dspec = pl.BlockSpec((rt * n_sub, LANES), lambda b, i: (b * n_rt + i, 0))
    return pl.pallas_call(
        functools.partial(_sg_mm_out_body, rt=rt, n_pair=n_pair, width=width),
        grid=(n_blk, n_rt),
        in_specs=[pl.BlockSpec((rt, kw), lambda b, i: (i, b)),
                  pl.BlockSpec((per_blk, width, width), lambda b, i: (b, 0, 0)), dspec, dspec,
                  pl.BlockSpec((per_blk, sw, width), lambda b, i: (b, 0, 0))],
        out_specs=pl.BlockSpec((rt, kw), lambda b, i: (i, b)),
        out_shape=jax.ShapeDtypeStruct((nc, n_blk * kw), BF16),
        compiler_params=_cparams(2),
        name="ssm_out",
    )(xg, intra, xf, xb, cst)


def _mix_body(xp_ref, xs_ref, z_ref, c_ref, cprev_ref, cnext_ref, cmeta_ref, sga_ref, sgb_ref,
              wglu_ref, wpw_ref, wout_ref, wconv_ref, bdw_ref, lng_ref, lnb_ref, gffn_ref,
              rw_ref, rb_ref,
              h1_ref, v_ref, route_ref, cnt_ref,
              cw_ref, conv_ref, zsc_ref, *,
              tm, n_p_tiles, tiles_per_pseq, tiles_per_sseq, d_model, d_ssm, d_conv,
              n_grp, n_exp, exp_per_grp):
    i = pl.program_id(0)
    in_prompt = i < n_p_tiles
    x = jnp.where(in_prompt, xp_ref[...], xs_ref[...])
    pos_p = lax.rem(i, tiles_per_pseq)
    pos_s = lax.rem(jnp.maximum(i - n_p_tiles, 0), tiles_per_sseq)
    is_start = jnp.where(in_prompt, pos_p == 0, pos_s == 0)
    is_end = jnp.where(in_prompt, pos_p == tiles_per_pseq - 1, pos_s == tiles_per_sseq - 1)

    halo = CHUNK
    cw_ref[0:halo, :] = jnp.where(is_start, cmeta_ref[...], cprev_ref[...]).astype(F32)
    cw_ref[halo:halo + tm, :] = c_ref[...].astype(F32)
    cw_ref[halo + tm:2 * halo + tm, :] = jnp.where(is_end, 0.0, cnext_ref[...].astype(F32))
    sub = 8
    nq = wconv_ref.shape[1] // LANES
    for lc in range(d_conv // LANES):
        ls = slice(lc * LANES, (lc + 1) * LANES)
        stacked = jnp.concatenate(
            [cw_ref[sub * q:sub * q + tm + sub, ls] for q in range(nq)], axis=1).astype(BF16)
        part = jnp.dot(stacked, wconv_ref[lc], preferred_element_type=F32)
        out = part[0:tm, 0:LANES]
        for r in range(1, sub):
            out = out + part[r:r + tm, r * LANES:(r + 1) * LANES]
        conv_ref[:, ls] = out
    cc = conv_ref[...] + bdw_ref[...]
    mu = jnp.mean(cc, axis=-1, keepdims=True)
    var = jnp.mean(jnp.square(cc - mu), axis=-1, keepdims=True)
    cc = (cc - mu) * lax.rsqrt(var + EPS) * lng_ref[...] + lnb_ref[...]
    cc = (cc * _sigmoid(cc)).astype(BF16)
    y_b = jnp.dot(cc, wpw_ref[...], preferred_element_type=F32)

    per = LANES // CHUNK
    for b in range(d_ssm // LANES):
        for hf in range(CHUNK // per):
            groups = [z_ref[:, (b * per + g) * CHUNK * CHUNK + hf * LANES:
                            (b * per + g) * CHUNK * CHUNK + (hf + 1) * LANES].astype(F32)
                      for g in range(per)]
            for a, blk in enumerate(_regroup(groups, CHUNK)):
                zsc_ref[b, pl.ds(hf * per + a, tm // CHUNK, stride=CHUNK), :] = blk
    z = jnp.concatenate([zsc_ref[b] for b in range(d_ssm // LANES)], axis=1).astype(BF16)
    va = jnp.dot(z, wglu_ref[:, 0:d_model], preferred_element_type=F32)
    ga = jnp.dot(z, wglu_ref[:, d_model:2 * d_model], preferred_element_type=F32)
    y_a = va * _sigmoid(ga)
    merged = (sga_ref[...].astype(F32) * y_a + sgb_ref[...].astype(F32) * y_b).astype(BF16)
    h1 = x + jnp.dot(merged, wout_ref[...], preferred_element_type=F32)
    h1_ref[...] = h1
    v = _rms(h1, gffn_ref[...])
    half = d_model // 2
    v_ref[...] = _pack_bf16_pair(v[:, 0:half], v[:, half:d_model])

    v_hi = v.astype(BF16)
    v_lo = (v - v_hi.astype(F32)).astype(BF16)
    acc = (jnp.dot(v_hi, rw_ref[...], preferred_element_type=F32)
           + jnp.dot(v_lo, rw_ref[...], preferred_element_type=F32))
    logits = acc + pltpu.roll(acc, LANES // 2, 1) + rb_ref[...]
    lane = lax.broadcasted_iota(I32, (tm, LANES), 1).astype(F32)
    big = jnp.float32(1e9)
    neg = jnp.float32(-jnp.inf)
    gmask = lane < n_grp
    lg = jnp.where(gmask, logits, neg)
    gmax = jnp.max(lg, axis=-1, keepdims=True)
    grp = jnp.min(jnp.where(lg == gmax, lane, big), axis=-1, keepdims=True)
    p_grp = 1.0 / jnp.sum(jnp.where(gmask, jnp.exp(logits - gmax), 0.0), axis=-1, keepdims=True)
    lo = n_grp + grp * exp_per_grp
    emask = (lane >= lo) & (lane < lo + exp_per_grp)
    le = jnp.where(emask, logits, neg)
    m1 = jnp.max(le, axis=-1, keepdims=True)
    i1 = jnp.min(jnp.where(le == m1, lane, big), axis=-1, keepdims=True)
    le2 = jnp.where(lane == i1, neg, le)
    m2 = jnp.max(le2, axis=-1, keepdims=True)
    i2 = jnp.min(jnp.where(le2 == m2, lane, big), axis=-1, keepdims=True)
    t = jnp.exp(m2 - m1)
    w1 = 1.0 / (1.0 + t)
    e1 = i1 - n_grp
    e2 = i2 - n_grp
    route_ref[...] = jnp.where(lane == 0, e1, jnp.where(lane == 1, e2, jnp.where(
        lane == 2, p_grp * w1, jnp.where(lane == 3, p_grp * (t * w1), 0.0))))

    @pl.when(i == 0)
    def _():
        cnt_ref[...] = jnp.zeros_like(cnt_ref)

    hot = jnp.where((lane == e1) | (lane == e2), 1.0, 0.0)
    cnt_ref[...] += jnp.sum(hot, axis=0, keepdims=True)


def _conv_weights_body(taps_ref, w_ref, *, n_taps):
    eye = (lax.broadcasted_iota(I32, (LANES, LANES), 0)
           == lax.broadcasted_iota(I32, (LANES, LANES), 1))
    for p in range(n_taps):
        q, r = divmod(p, 8)
        diag = jnp.where(eye, taps_ref[p:p + 1, :], 0.0)
        w_ref[q * LANES:(q + 1) * LANES, r * LANES:(r + 1) * LANES] = diag.astype(BF16)


def _conv_weights(taps):
    n_taps, d_conv = taps.shape
    nq = n_taps // 8
    return pl.pallas_call(
        functools.partial(_conv_weights_body, n_taps=n_taps),
        grid=(d_conv // LANES,),
        in_specs=[pl.BlockSpec((n_taps, LANES), lambda b: (0, b))],
        out_specs=pl.BlockSpec((None, nq * LANES, 8 * LANES), lambda b: (b, 0, 0)),
        out_shape=jax.ShapeDtypeStruct((d_conv // LANES, nq * LANES, 8 * LANES), BF16),
        compiler_params=_cparams(1),
        name="conv_weights",
    )(taps)


def _mix(xp2, xs2, z, c, c_meta, sga, sgb, wglu, wpw, wout, wconv, bdw, lng, lnb, gffn, rw, rb, *,
         tm, p_seq, s_seq, n_grp, n_exp):
    n_p, d_model = xp2.shape
    n_s = xs2.shape[0]
    t = n_p + n_s
    d_ssm, d_conv = z.shape[1] // CHUNK, c.shape[1]
    n_p_tiles, n_s_tiles = n_p // tm, n_s // tm
    hpt = tm // CHUNK
    n_hblk = t // CHUNK
    body = functools.partial(
        _mix_body, tm=tm, n_p_tiles=n_p_tiles, tiles_per_pseq=p_seq // tm,
        tiles_per_sseq=s_seq // tm, d_model=d_model, d_ssm=d_ssm, d_conv=d_conv,
        n_grp=n_grp, n_exp=n_exp, exp_per_grp=n_exp // n_grp)
    row = lambda i: (i, 0)
    return pl.pallas_call(
        body,
        grid=(n_p_tiles + n_s_tiles,),
        in_specs=[
            pl.BlockSpec((tm, d_model), lambda i: (jnp.minimum(i, n_p_tiles - 1), 0)),
            pl.BlockSpec((tm, d_model), lambda i: (jnp.maximum(i - n_p_tiles, 0), 0)),
            pl.BlockSpec((tm // CHUNK, CHUNK * d_ssm), row),
            pl.BlockSpec((tm, d_conv), row),
            pl.BlockSpec((CHUNK, d_conv), lambda i: (jnp.maximum(i * hpt - 1, 0), 0)),
            pl.BlockSpec((CHUNK, d_conv), lambda i: (jnp.minimum((i + 1) * hpt, n_hblk - 1), 0)),
            _resident(c_meta.shape),
            pl.BlockSpec((tm, d_model), row),
            pl.BlockSpec((tm, d_model), row),
            _resident(wglu.shape), _resident(wpw.shape), _resident(wout.shape),
            _resident(wconv.shape), _resident(bdw.shape), _resident(lng.shape), _resident(lnb.shape),
            _resident(gffn.shape), _resident(rw.shape), _resident(rb.shape),
        ],
        out_specs=[pl.BlockSpec((tm, d_model), row), pl.BlockSpec((tm, d_model // 2), row),
                   pl.BlockSpec((tm, LANES), row), pl.BlockSpec((1, LANES), lambda i: (0, 0))],
        out_shape=[jax.ShapeDtypeStruct((t, d_model), F32),
                   jax.ShapeDtypeStruct((t, d_model // 2), I32),
                   jax.ShapeDtypeStruct((t, LANES), F32),
                   jax.ShapeDtypeStruct((1, LANES), F32)],
        scratch_shapes=[pltpu.VMEM((tm + 2 * CHUNK, d_conv), F32), pltpu.VMEM((tm, d_conv), F32),
                        pltpu.VMEM((d_ssm // LANES, tm, LANES), F32)],
        compiler_params=_cparams(1),
        name="mix",
    )(xp2, xs2, z, c, c, c, c_meta, sga, sgb, wglu, wpw, wout, wconv, bdw, lng, lnb, gffn, rw, rb)


def _positions_body(route_ref, offs_ref, dest_ref, carry_ref, *, te):
    @pl.when(pl.program_id(0) == 0)
    def _():
        carry_ref[...] = jnp.zeros_like(carry_ref)

    lane = lax.broadcasted_iota(I32, (te, LANES), 1).astype(F32)
    r = route_ref[...]
    oh1 = lane == r[:, 0:1]
    oh2 = lane == r[:, 1:2]
    both = jnp.where(oh1 | oh2, 1.0, 0.0)
    tri = jnp.where(lax.broadcasted_iota(I32, (te, te), 0) > lax.broadcasted_iota(I32, (te, te), 1),
                    1.0, 0.0).astype(BF16)
    before = jnp.dot(tri, both.astype(BF16), preferred_element_type=F32)
    base = before + carry_ref[...] + offs_ref[...]
    d1 = jnp.sum(jnp.where(oh1, base, 0.0), axis=-1, keepdims=True)
    d2 = jnp.sum(jnp.where(oh2, base, 0.0), axis=-1, keepdims=True)
    dest_ref[...] = jnp.where(lane == 0, d1, jnp.where(lane == 1, d2, 0.0)).astype(I32)
    carry_ref[...] += jnp.sum(both, axis=0, keepdims=True)


def _positions(route, offs, *, te):
    t = route.shape[0]
    return pl.pallas_call(
        functools.partial(_positions_body, te=te),
        grid=(t // te,),
        in_specs=[pl.BlockSpec((te, LANES), lambda i: (i, 0)),
                  pl.BlockSpec((1, LANES), lambda i: (0, 0))],
        out_specs=pl.BlockSpec((te, LANES), lambda i: (i, 0)),
        out_shape=jax.ShapeDtypeStruct((t, LANES), I32),
        scratch_shapes=[pltpu.VMEM((1, LANES), F32)],
        compiler_params=_cparams(1),
        name="positions",
    )(route, offs)


def _experts_body(te_ref, first_ref, last_ref, nxt_ref, short_ref, nused_ref,
                  xs_ref, w1_hbm, w3_hbm, w2_hbm, o_ref,
                  w1b, w3b, w2b, st1, st3, st2, sem, *, half, n_conv):
    i = pl.program_id(0)
    used = i < nused_ref[0]
    nxt = nxt_ref[i]

    def copies(e):
        return (pltpu.make_async_copy(w1_hbm.at[e], st1, sem.at[0]),
                pltpu.make_async_copy(w3_hbm.at[e], st3, sem.at[1]),
                pltpu.make_async_copy(w2_hbm.at[e], st2, sem.at[2]))

    def convert():
        r13 = st1.shape[0] // n_conv
        r2 = st2.shape[0] // n_conv

        def body(c, _):
            rows = pl.ds(pl.multiple_of(c * r13, r13), r13)
            w1b[rows, :] = st1[rows, :].astype(BF16)
            w3b[rows, :] = st3[rows, :].astype(BF16)
            rows2 = pl.ds(pl.multiple_of(c * r2, r2), r2)
            w2b[rows2, :] = st2[rows2, :].astype(BF16)
            return 0
        lax.fori_loop(0, n_conv, body, 0)

    @pl.when(i == 0)
    def _():
        for cp in copies(te_ref[0]):
            cp.start()
        for cp in copies(te_ref[0]):
            cp.wait()
        convert()

    @pl.when(used & (first_ref[i] == 1) & (nxt >= 0))
    def _():
        for cp in copies(nxt):
            cp.start(priority=1)

    def swiglu(n_rows):
        a, b = _unpack_bf16_pair(xs_ref[0:n_rows, :])
        a, b = a.astype(BF16), b.astype(BF16)

        def up(w_ref):
            return (jnp.dot(a, w_ref[0:half, :], preferred_element_type=F32)
                    + jnp.dot(b, w_ref[half:2 * half, :], preferred_element_type=F32))
        h1 = up(w1b)
        act = (h1 * _sigmoid(h1) * up(w3b)).astype(BF16)
        o = jnp.dot(act, w2b[...], preferred_element_type=F32)
        o_ref[0:n_rows, :] = _pack_bf16_pair(o[:, 0:half], o[:, half:2 * half])
        if n_rows < o_ref.shape[0]:
            o_ref[n_rows:, :] = jnp.zeros((o_ref.shape[0] - n_rows, half), I32)

    @pl.when(used & (short_ref[i] == 0))
    def _():
        swiglu(o_ref.shape[0])

    @pl.when(used & (short_ref[i] == 1))
    def _():
        swiglu(o_ref.shape[0] // 2)

    @pl.when(jnp.logical_not(used))
    def _():
        o_ref[...] = jnp.zeros_like(o_ref)

    @pl.when(used & (last_ref[i] == 1) & (nxt >= 0))
    def _():
        for cp in copies(nxt):
            cp.wait()
        convert()


def _experts(tile_expert, first, last, nxt, short, n_used, xs, w1, w3, w2, *, tme):
    rows, half = xs.shape
    n_e, d_model, d_exp = w1.shape
    grid_spec = pltpu.PrefetchScalarGridSpec(
        num_scalar_prefetch=6,
        grid=(rows // tme,),
        in_specs=[pl.BlockSpec((tme, half), lambda i, te, fi, la, nx, sh, nu: (jnp.minimum(i, nu[0] - 1), 0)),
                  pl.BlockSpec(memory_space=pl.ANY), pl.BlockSpec(memory_space=pl.ANY),
                  pl.BlockSpec(memory_space=pl.ANY)],
        out_specs=pl.BlockSpec((tme, half), lambda i, *_: (i, 0)),
        scratch_shapes=[pltpu.VMEM((d_model, d_exp), BF16), pltpu.VMEM((d_model, d_exp), BF16),
                        pltpu.VMEM((d_exp, d_model), BF16),
                        pltpu.VMEM((d_model, d_exp), F32), pltpu.VMEM((d_model, d_exp), F32),
                        pltpu.VMEM((d_exp, d_model), F32),
                        pltpu.SemaphoreType.DMA((3,))],
    )
    return pl.pallas_call(
        functools.partial(_experts_body, half=half, n_conv=8),
        grid_spec=grid_spec,
        out_shape=jax.ShapeDtypeStruct((rows, half), I32),
        compiler_params=_cparams(1),
        name="experts",
    )(tile_expert, first, last, nxt, short, n_used, xs, w1, w3, w2)


def _sc_gather(table, idx, *, chunk):
    n_rows, width = idx.shape[0], table.shape[1]
    n_workers = SC_CORES * SC_SUBCORES
    per_w = n_rows // n_workers
    n_chunks = per_w // chunk
    assert per_w * n_workers == n_rows and n_chunks * chunk == per_w and n_chunks % 2 == 0
    mesh = plsc.VectorSubcoreMesh(core_axis_name="c", subcore_axis_name="s",
                                  num_cores=SC_CORES, num_subcores=SC_SUBCORES)

    @functools.partial(
        pl.kernel, mesh=mesh,
        out_type=jax.ShapeDtypeStruct((n_rows, width), table.dtype),
        scratch_types=[pltpu.VMEM((per_w,), I32), pltpu.VMEM((2, chunk, width), table.dtype),
                       pltpu.SemaphoreType.DMA((2,))],
    )
    def gather_kernel(table_hbm, idx_hbm, out_hbm, idx_v, rows_v, sem):
        wid = lax.axis_index("s") * SC_CORES + lax.axis_index("c")
        base = pl.multiple_of(wid * per_w, per_w)
        pltpu.sync_copy(idx_hbm.at[pl.ds(base, per_w)], idx_v)

        def gather(j, slot):
            off = pl.multiple_of(j * chunk, chunk)
            return pltpu.make_async_copy(table_hbm.at[idx_v.at[pl.ds(off, chunk)]],
                                         rows_v.at[slot], sem.at[slot])

        gather(0, 0).start()

        @pl.loop(0, n_chunks, step=2)
        def _(j):
            for slot in range(2):
                jj = j + slot
                gather(jj, slot).wait()

                @pl.when(jj + 1 < n_chunks)
                def _():
                    gather(jj + 1, 1 - slot).start()
                off = pl.multiple_of(base + jj * chunk, chunk)
                pltpu.sync_copy(rows_v.at[slot], out_hbm.at[pl.ds(off, chunk)])

    return gather_kernel(table, idx)


def _sc_inverse(dest_flat, n_rows):
    n_assign = dest_flat.shape[0]
    n_tok = n_assign // 2
    n_workers = SC_CORES * SC_SUBCORES
    per_w = n_rows // n_workers
    assert per_w * n_workers == n_rows and per_w % SC_LANES == 0 and n_assign % SC_LANES == 0
    mesh = plsc.VectorSubcoreMesh(core_axis_name="c", subcore_axis_name="s",
                                  num_cores=SC_CORES, num_subcores=SC_SUBCORES)

    @functools.partial(
        pl.kernel, mesh=mesh,
        out_type=jax.ShapeDtypeStruct((n_rows,), I32),
        scratch_types=[pltpu.VMEM((n_assign,), I32), pltpu.VMEM((per_w,), I32)],
        compiler_params=pltpu.CompilerParams(needs_layout_passes=False),
    )
    def inverse_kernel(dest_hbm, src_hbm, dest_v, src_v):
        wid = lax.axis_index("s") * SC_CORES + lax.axis_index("c")
        lo = pl.multiple_of(wid * per_w, per_w)
        pltpu.sync_copy(dest_hbm, dest_v)
        lane = lax.iota(I32, SC_LANES)

        @pl.loop(0, per_w, step=SC_LANES)
        def _(i):
            src_v[pl.ds(pl.multiple_of(i, SC_LANES), SC_LANES)] = lax.rem(lo + i + lane, n_tok)

        @pl.loop(0, n_assign, step=SC_LANES)
        def _(i):
            d = dest_v[pl.ds(pl.multiple_of(i, SC_LANES), SC_LANES)]
            a = i + lane
            tok = jnp.where(a >= n_tok, a - n_tok, a)
            mine = (d >= lo) & (d < lo + per_w)
            plsc.store_scatter(src_v, [d - lo], tok, mask=mine)

        pltpu.sync_copy(src_v, src_hbm.at[pl.ds(lo, per_w)])

    return inverse_kernel(dest_flat)


def _final_body(g1_ref, g2_ref, h1_ref, route_ref, g_ref, y_ref, *, half):
    route = route_ref[...]
    w1, w2 = route[:, 2:3], route[:, 3:4]
    a1, b1 = _unpack_bf16_pair(g1_ref[...])
    a2, b2 = _unpack_bf16_pair(g2_ref[...])
    h1 = h1_ref[...]
    ha = h1[:, 0:half] + (a1 * w1 + a2 * w2)
    hb = h1[:, half:2 * half] + (b1 * w1 + b2 * w2)
    ms = (jnp.sum(ha * ha, axis=-1, keepdims=True)
          + jnp.sum(hb * hb, axis=-1, keepdims=True)) / (2 * half)
    inv = lax.rsqrt(ms + EPS)
    y_ref[:, 0:half] = ha * inv * g_ref[:, 0:half]
    y_ref[:, half:2 * half] = hb * inv * g_ref[:, half:2 * half]


def _final(gathered, h1, route, g, *, tf, row0, n_rows):
    t, d_model = h1.shape
    half = d_model // 2
    tile0, tiles_t = row0 // tf, t // tf
    return pl.pallas_call(
        functools.partial(_final_body, half=half),
        grid=(n_rows // tf,),
        in_specs=[pl.BlockSpec((tf, half), lambda i: (tile0 + i, 0)),
                  pl.BlockSpec((tf, half), lambda i: (tiles_t + tile0 + i, 0)),
                  pl.BlockSpec((tf, d_model), lambda i: (tile0 + i, 0)),
                  pl.BlockSpec((tf, LANES), lambda i: (tile0 + i, 0)),
                  pl.BlockSpec((1, d_model), lambda i: (0, 0))],
        out_specs=pl.BlockSpec((tf, d_model), lambda i: (i, 0)),
        out_shape=jax.ShapeDtypeStruct((n_rows, d_model), F32),
        compiler_params=_cparams(1),
        name="final",
    )(gathered, gathered, h1, route, g)


def kernel(x_prompt, x_sample, meta, norm_mix_g, w_in, ssm_lam_re, ssm_lam_im, ssm_log_step, ssm_b_re, ssm_b_im, ssm_c_re, ssm_c_im, ssm_d, ssm_w_glu, conv_w_dw, conv_b_dw, conv_ln_g, conv_ln_b, conv_w_pw, w_out, norm_ffn_g, router_group_w, router_group_b, router_expert_w, router_expert_b, expert_w1, expert_w3, expert_w2, final_g):
    assert w_in.shape[0] == 1, "single-layer trunk"
    bp, lp, d_model = x_prompt.shape
    bs, ls, _ = x_sample.shape
    n_meta = meta.shape[0]
    d_ssm = ssm_d.shape[-1]
    d_conv = conv_b_dw.shape[-1]
    n_ch = ssm_b_re.shape[-1]
    n_g = ssm_b_re.shape[2]
    n_grp = router_group_w.shape[-1]
    n_exp = router_expert_w.shape[-1]
    assert n_meta == CHUNK and lp % CHUNK == 0 and ls % CHUNK == 0
    n_p, n_s = bp * lp, bs * ls
    t = n_p + n_s
    tm = min(256, lp, ls)
    assert lp % tm == 0 and ls % tm == 0

    xp2 = x_prompt.reshape(n_p, d_model)
    xs2 = x_sample.reshape(n_s, d_model)
    row = lambda a: a.reshape(1, -1)
    w_in_bf = w_in[0].astype(BF16)

    assert n_ch == CHUNK and LANES % n_ch == 0 and n_g % (LANES // n_ch) == 0
    per_blk = LANES // n_ch
    xg, c, sga, sgb = _inproj(xp2, xs2, row(norm_mix_g[0]), w_in_bf, d_ssm=d_ssm, d_conv=d_conv,
                              tm=tm, n_ch=n_ch)
    xg_meta, c_meta = _meta_inproj(meta, row(norm_mix_g[0]), w_in_bf, d_ssm=d_ssm, d_conv=d_conv,
                                   n_ch=n_ch)

    intra, bst, cst, coef = _sg_prep(ssm_lam_re[0], ssm_lam_im[0], ssm_log_step[0], ssm_b_re[0],
                                     ssm_b_im[0], ssm_c_re[0], ssm_c_im[0], ssm_d[0])
    nc = t // CHUNK
    n_blk = d_ssm // LANES
    rt = min(512, nc)
    sf, sb, init = _sg_mm_state(xg, xg_meta, bst, rt=rt, per_blk=per_blk)
    n_sub = sf.shape[0] // (n_blk * nc)
    cb = min(64, lp // CHUNK, ls // CHUNK)
    xf, xb = _sg_scan(sf.reshape(n_blk, nc, n_sub, LANES), sb.reshape(n_blk, nc, n_sub, LANES),
                      coef, init, cb=cb, chunks_pseq=lp // CHUNK, chunks_sseq=ls // CHUNK,
                      n_pchunks=n_p // CHUNK)
    z = _sg_mm_out(xg, intra, xf.reshape(sf.shape), xb.reshape(sb.shape), cst, rt=rt,
                   per_blk=per_blk)

    assert n_grp + n_exp <= LANES // 2
    rw32 = jnp.pad(jnp.concatenate([router_group_w[0], router_expert_w[0]], axis=1),
                   ((0, 0), (0, LANES // 2 - n_grp - n_exp)))
    rw_hi = rw32.astype(BF16)
    rw = jnp.concatenate([rw_hi, (rw32 - rw_hi.astype(F32)).astype(BF16)], axis=1)
    rb = jnp.zeros((1, LANES), F32).at[0, 0:n_grp].set(router_group_b[0]).at[
        0, n_grp:n_grp + n_exp].set(router_expert_b[0])
    conv_w = conv_w_dw.shape[1]
    off = CHUNK - conv_w // 2
    assert 0 <= off and conv_w + off <= 2 * CHUNK
    taps = jnp.pad(conv_w_dw[0], ((off, -(conv_w + off) % 8), (0, 0)))
    h1, v, route, cnt = _mix(
        xp2, xs2, z, c, c_meta, sga, sgb, ssm_w_glu[0].astype(BF16), conv_w_pw[0].astype(BF16),
        w_out[0].astype(BF16), _conv_weights(taps), row(conv_b_dw[0]), row(conv_ln_g[0]),
        row(conv_ln_b[0]), row(norm_ffn_g[0]), rw, rb,
        tm=tm, p_seq=lp, s_seq=ls, n_grp=n_grp, n_exp=n_exp)

    tme = 256
    counts = cnt[0, 0:n_exp].astype(I32)
    tiles_e = (counts + tme - 1) // tme
    tile_end = jnp.cumsum(tiles_e)
    n_used = tile_end[-1]
    offs = jnp.zeros((1, LANES), F32).at[0, 0:n_exp].set(((tile_end - tiles_e) * tme).astype(F32))
    n_tiles = (2 * t) // tme + n_exp
    ids = jnp.arange(n_tiles, dtype=I32)

    def expert_of(tile):
        return jnp.minimum(jnp.sum((tile[:, None] >= tile_end[None, :]).astype(I32), axis=1), n_exp - 1)
    te_map = expert_of(jnp.minimum(ids, n_used - 1))
    onehot = te_map[:, None] == jnp.arange(n_exp, dtype=I32)[None, :]
    run_end = jnp.sum(jnp.where(onehot, tile_end[None, :], 0), axis=1)
    run_start = run_end - jnp.sum(jnp.where(onehot, tiles_e[None, :], 0), axis=1)
    valid = ids < n_used
    first = (valid & (ids == run_start)).astype(I32)
    last = (valid & (ids == run_end - 1)).astype(I32)
    nxt = jnp.where(valid & (run_end < n_used), expert_of(run_end), -1)
    tail_rows = counts - (tiles_e - 1) * tme
    short_e = ((tiles_e > 0) & (tail_rows <= tme // 2)).astype(I32)
    short = last * jnp.sum(jnp.where(onehot, short_e[None, :], 0), axis=1)

    dest = _positions(route, offs, te=min(512, t))
    idx = jnp.concatenate([dest[:, 0], dest[:, 1]])
    src = _sc_inverse(idx, n_tiles * tme)
    xs = _sc_gather(v, src, chunk=32)
    o = _experts(te_map, first, last, nxt, short, n_used.reshape(1), xs, expert_w1[0],
                 expert_w3[0], expert_w2[0], tme=tme)

    tf = min(256, n_p, n_s)
    gathered = _sc_gather(o, idx, chunk=32)
    fg = row(final_g)
    y_p = _final(gathered, h1, route, fg, tf=tf, row0=0, n_rows=n_p)
    y_s = _final(gathered, h1, route, fg, tf=tf, row0=n_p, n_rows=n_s)
    return (y_p.reshape(bp, lp, d_model), y_s.reshape(bs, ls, d_model))
```

```python
import functools

import jax
import jax.numpy as jnp
from jax import lax
from jax.experimental import pallas as pl
from jax.experimental.pallas import tpu as pltpu
from jax.experimental.pallas import tpu_sc as plsc

F32 = jnp.float32
BF16 = jnp.bfloat16
U32 = jnp.uint32
I32 = jnp.int32

EPS = 1e-6
LAM_RE_MAX = -1e-4
CHUNK = 16
LANES = 128
VMEM_LIMIT = 56 << 20
SC_CORES = 2
SC_SUBCORES = 16
SC_LANES = 16
HIGHEST = lax.Precision.HIGHEST


def _cparams(n_axes):
    return pltpu.CompilerParams(dimension_semantics=("arbitrary",) * n_axes,
                                vmem_limit_bytes=VMEM_LIMIT)


def _resident(shape):
    nd = len(shape)
    return pl.BlockSpec(shape, lambda *_: (0,) * nd, pipeline_mode=pl.Buffered(1))


def _sigmoid(x):
    return 1.0 / (1.0 + jnp.exp(-x))


def _rms(x, g):
    return x * lax.rsqrt(jnp.mean(x * x, axis=-1, keepdims=True) + EPS) * g


def _pack_bf16_pair(a, b):
    def rnd(x):
        u = pltpu.bitcast(x, U32)
        return (u + jnp.uint32(0x7FFF) + ((u >> 16) & jnp.uint32(1))) >> 16
    return pltpu.bitcast((rnd(a) << 16) | rnd(b), I32)


def _unpack_bf16_pair(p):
    p = pltpu.bitcast(p, U32)
    a = pltpu.bitcast(p & jnp.uint32(0xFFFF0000), F32)
    b = pltpu.bitcast(p << 16, F32)
    return a, b


def _regroup(pieces, n_ch):
    per = LANES // n_ch
    lane = lax.broadcasted_iota(I32, pieces[0].shape, 1)
    masks = [(lane >= a * n_ch) & (lane < (a + 1) * n_ch) for a in range(per)]
    out = []
    for b in range(per):
        acc = None
        for a in range(per):
            shift = ((a - b) * n_ch) % LANES
            src = pieces[a] if shift == 0 else pltpu.roll(pieces[a], shift, 1)
            acc = jnp.where(masks[a], src, 0.0) if acc is None else jnp.where(masks[a], src, acc)
        out.append(acc)
    return out


def _inproj_body(xp_ref, xs_ref, g_ref, w_ref, xg_ref, c_ref, sga_ref, sgb_ref, usc, *,
                 tm, n_p_tiles, d_ssm, d_conv, d_model, col, n_ch):
    i = pl.program_id(0)
    x = jnp.where(i < n_p_tiles, xp_ref[...], xs_ref[...])
    y = _rms(x, g_ref[...]).astype(BF16)

    def proj(lo, n):
        return jnp.dot(y, w_ref[:, lo:lo + n], preferred_element_type=F32)

    for k in range(d_ssm // col):
        u = proj(k * col, col)
        for b in range(col // LANES):
            usc[k * (col // LANES) + b] = u[:, b * LANES:(b + 1) * LANES]
    per = LANES // n_ch
    for b in range(d_ssm // LANES):
        for hf in range(CHUNK // per):
            tokens = [usc[b, pl.ds(hf * per + a, tm // CHUNK, stride=CHUNK), :] for a in range(per)]
            for g, blk in enumerate(_regroup(tokens, n_ch)):
                c0 = (b * per + g) * CHUNK * n_ch + hf * LANES
                xg_ref[:, c0:c0 + LANES] = blk.astype(BF16)
    for k in range(d_conv // col):
        cv = proj(d_ssm + k * col, col)
        cg = proj(d_ssm + d_conv + k * col, col)
        c_ref[:, k * col:(k + 1) * col] = (cv * _sigmoid(cg)).astype(BF16)
    base = d_ssm + 2 * d_conv
    for k in range(d_model // col):
        sga_ref[:, k * col:(k + 1) * col] = _sigmoid(proj(base + k * col, col)).astype(BF16)
        sgb_ref[:, k * col:(k + 1) * col] = _sigmoid(
            proj(base + d_model + k * col, col)).astype(BF16)


def _inproj(xp2, xs2, g, w_bf, *, d_ssm, d_conv, tm, n_ch):
    n_p, d_model = xp2.shape
    n_s = xs2.shape[0]
    n_p_tiles, n_s_tiles = n_p // tm, n_s // tm
    t = n_p + n_s
    col = min(1024, d_ssm, d_conv, d_model)
    body = functools.partial(_inproj_body, tm=tm, n_p_tiles=n_p_tiles, d_ssm=d_ssm, d_conv=d_conv,
                             d_model=d_model, col=col, n_ch=n_ch)
    row = lambda i: (i, 0)
    return pl.pallas_call(
        body,
        grid=(n_p_tiles + n_s_tiles,),
        in_specs=[
            pl.BlockSpec((tm, d_model), lambda i: (jnp.minimum(i, n_p_tiles - 1), 0)),
            pl.BlockSpec((tm, d_model), lambda i: (jnp.maximum(i - n_p_tiles, 0), 0)),
            _resident((1, d_model)),
            _resident(w_bf.shape),
        ],
        out_specs=[pl.BlockSpec((tm // CHUNK, CHUNK * d_ssm), row), pl.BlockSpec((tm, d_conv), row),
                   pl.BlockSpec((tm, d_model), row), pl.BlockSpec((tm, d_model), row)],
        out_shape=[jax.ShapeDtypeStruct((t // CHUNK, CHUNK * d_ssm), BF16),
                   jax.ShapeDtypeStruct((t, d_conv), BF16),
                   jax.ShapeDtypeStruct((t, d_model), BF16), jax.ShapeDtypeStruct((t, d_model), BF16)],
        scratch_shapes=[pltpu.VMEM((d_ssm // LANES, tm, LANES), F32)],
        compiler_params=_cparams(1),
        name="inproj",
    )(xp2, xs2, g, w_bf)


def _meta_body(m_ref, g_ref, w_ref, xg_ref, c_ref, *, d_ssm, d_conv, n_ch):
    y = _rms(m_ref[...], g_ref[...]).astype(BF16)
    u = jnp.dot(y, w_ref[:, 0:d_ssm], preferred_element_type=F32)
    first = lax.broadcasted_iota(I32, (CHUNK, LANES), 0) == 0
    per = LANES // n_ch
    for b in range(d_ssm // LANES):
        for hf in range(CHUNK // per):
            tokens = [u[hf * per + a:hf * per + a + 1, b * LANES:(b + 1) * LANES] for a in range(per)]
            for g, blk in enumerate(_regroup(tokens, n_ch)):
                c0 = (b * per + g) * CHUNK * n_ch + hf * LANES
                piece = jnp.broadcast_to(blk, (CHUNK, LANES))
                xg_ref[:, c0:c0 + LANES] = jnp.where(first, piece, 0.0).astype(BF16)
    cv = jnp.dot(y, w_ref[:, d_ssm:d_ssm + d_conv], preferred_element_type=F32)
    cg = jnp.dot(y, w_ref[:, d_ssm + d_conv:d_ssm + 2 * d_conv], preferred_element_type=F32)
    c_ref[...] = (cv * _sigmoid(cg)).astype(BF16)


def _meta_inproj(meta, g, w_bf, *, d_ssm, d_conv, n_ch):
    n_meta, d_model = meta.shape
    ncol = d_ssm + 2 * d_conv
    return pl.pallas_call(
        functools.partial(_meta_body, d_ssm=d_ssm, d_conv=d_conv, n_ch=n_ch),
        grid=(1,),
        in_specs=[pl.BlockSpec((n_meta, d_model), lambda i: (0, 0)),
                  pl.BlockSpec((1, d_model), lambda i: (0, 0)),
                  pl.BlockSpec((d_model, ncol), lambda i: (0, 0))],
        out_specs=[pl.BlockSpec((CHUNK, CHUNK * d_ssm), lambda i: (0, 0)),
                   pl.BlockSpec((n_meta, d_conv), lambda i: (0, 0))],
        out_shape=[jax.ShapeDtypeStruct((CHUNK, CHUNK * d_ssm), BF16),
                   jax.ShapeDtypeStruct((n_meta, d_conv), BF16)],
        compiler_params=_cparams(1),
        name="meta_inproj",
    )(meta, g, w_bf)


def _cmul(ar, ai, br, bi):
    return ar * br - ai * bi, ar * bi + ai * br


def _discretize(lam_re, lam_im, log_step):
    lr = jnp.minimum(lam_re, LAM_RE_MAX)
    dt = jnp.exp(log_step)
    mag = jnp.exp(lr * dt)
    ar = mag * jnp.cos(lam_im * dt)
    ai = mag * jnp.sin(lam_im * dt)
    den = lr * lr + lam_im * lam_im
    nr = ar - 1.0
    fr = (nr * lr + ai * lam_im) / den
    fi = (ai * lr - nr * lam_im) / den
    return ar, ai, fr, fi


def _cpow(ar, ai, k, nbits, shape):
    pr = jnp.ones(shape, F32)
    pi = jnp.zeros(shape, F32)
    br = jnp.broadcast_to(ar, shape)
    bi = jnp.broadcast_to(ai, shape)
    kk = jnp.broadcast_to(k, shape)
    for b in range(nbits):
        sel = ((kk >> b) & 1) == 1
        nr, ni = _cmul(pr, pi, br, bi)
        pr = jnp.where(sel, nr, pr)
        pi = jnp.where(sel, ni, pi)
        br, bi = _cmul(br, bi, br, bi)
    return pr, pi


def _tile_lanes(x, n_ch, width):
    span = n_ch
    while span < LANES:
        x = x + pltpu.roll(x, span, 1)
        span *= 2
    return jnp.concatenate([x] * (width // LANES), axis=1)


def _disc_body(lam_re_ref, lam_im_ref, ls_ref, ar_ref, ai_ref, fr_ref, fi_ref):
    ar, ai, fr, fi = _discretize(lam_re_ref[...], lam_im_ref[...], ls_ref[...])
    ar_ref[...] = ar
    ai_ref[...] = ai
    fr_ref[...] = fr
    fi_ref[...] = fi


def _ssm_discretize(lam_re, lam_im, log_step):
    shape = lam_re.shape
    rows = shape[0] * shape[1]
    flat = lambda x: x.reshape(rows, shape[2])
    ls = jnp.broadcast_to(log_step[..., None], shape)
    full = pl.BlockSpec((rows, shape[2]), lambda i: (0, 0))
    out = pl.pallas_call(
        _disc_body,
        grid=(1,),
        in_specs=[full, full, full],
        out_specs=[full] * 4,
        out_shape=[jax.ShapeDtypeStruct((rows, shape[2]), F32)] * 4,
        compiler_params=_cparams(1),
        name="ssm_disc",
    )(flat(lam_re), flat(lam_im), flat(ls))
    return [x.reshape(shape) for x in out]


def _sg_prep_body(ct_re, ct_im, bt_re, bt_im,
                  ar_r, ai_r, fr_r, fi_r, btr_re, btr_im, dpad,
                  m_ref, b_ref, c_ref, aq_ref, *, n_state, n_ch, per_step):
    per_blk = LANES // n_ch
    first = (pl.program_id(0) * per_step) % per_blk
    for j in range(per_step):
        _sg_prep_group(j, first + j, ct_re, ct_im, bt_re, bt_im,
                       ar_r, ai_r, fr_r, fi_r, btr_re, btr_im, dpad,
                       m_ref, b_ref, c_ref, aq_ref, n_state=n_state, n_ch=n_ch)


def _sg_prep_group(j, gl, ct_re, ct_im, bt_re, bt_im,
                   ar_r, ai_r, fr_r, fi_r, btr_re, btr_im, dpad,
                   m_ref, b_ref, c_ref, aq_ref, *, n_state, n_ch):
    q = CHUNK
    width = q * n_ch
    rowmask = (lax.broadcasted_iota(I32, (2 * n_state, width), 0) // n_state) == gl % 2
    kblk = lax.broadcasted_iota(I32, (1, width), 1) // n_ch
    eye = (lax.broadcasted_iota(I32, (n_state, n_state), 0)
           == lax.broadcasted_iota(I32, (n_state, n_state), 1))
    col = lambda r: jnp.sum(jnp.where(eye, r[:, 0:n_state], 0.0), axis=1, keepdims=True)
    strips = []
    for d in range(2):
        ar, ai, fr, fi = (col(r[d, j]) for r in (ar_r, ai_r, fr_r, fi_r))
        kexp = kblk if d == 0 else (q - 1) - kblk
        wr, wi = _cpow(ar, ai, kexp, 4, (n_state, width))
        gcr, gci = _cmul(_tile_lanes(ct_re[d, j], n_ch, width),
                         _tile_lanes(ct_im[d, j], n_ch, width), wr, wi)
        gfr, gfi = _cmul(gcr, gci, fr, fi)
        strips.append(jnp.dot(bt_re[d, j], gfr, precision=HIGHEST, preferred_element_type=F32)
                      - jnp.dot(bt_im[d, j], gfi, precision=HIGHEST, preferred_element_type=F32))
        g1r, g1i = _cmul(gcr, gci, ar, ai)
        for comp, val in ((0, g1r), (1, -g1i)):
            r0 = (2 * d + comp) * 2 * n_state
            both = jnp.concatenate([val, val], axis=0)
            c_ref[j, r0:r0 + 2 * n_state, :] = jnp.where(rowmask, both, 0.0).astype(BF16)

    zf, zb = strips
    zero = jnp.zeros((n_ch, width), F32)
    z512 = jnp.concatenate([zb, zero], axis=1) + pltpu.roll(
        jnp.concatenate([zf, zero], axis=1), (q - 1) * n_ch, 1)
    row = lax.broadcasted_iota(I32, (n_ch, 2 * width), 0)
    lane = lax.broadcasted_iota(I32, (n_ch, 2 * width), 1)
    z512 = z512 + jnp.where(lane - (q - 1) * n_ch == row, dpad[j], 0.0)
    for s in range(q):
        sh = (q - 1 - s) * n_ch
        blk = z512 if sh == 0 else pltpu.roll(z512, 2 * width - sh, 1)
        m_ref[j, n_ch * s:n_ch * (s + 1), :] = blk[:, 0:width].astype(BF16)

    parmask = (lax.broadcasted_iota(I32, (1, 2 * n_state), 1) // n_state) == gl % 2
    pieces = [[None] * 4 for _ in range(q)]
    for d in range(2):
        ar, ai, fr, fi = ar_r[d, j], ai_r[d, j], fr_r[d, j], fi_r[d, j]
        pw = [(jnp.ones_like(ar), jnp.zeros_like(ar))]
        for _ in range(q):
            pw.append(_cmul(pw[-1][0], pw[-1][1], ar, ai))
        for s in range(q):
            e = (q - 1 - s) if d == 0 else s
            cr, ci = _cmul(fr, fi, pw[e][0], pw[e][1])
            br, bi = _cmul(btr_re[d, j], btr_im[d, j], cr, ci)
            pieces[s][2 * d] = jnp.where(parmask, br, 0.0)
            pieces[s][2 * d + 1] = jnp.where(parmask, bi, 0.0)
        aq_ref[j, :, 2 * d * n_state:(2 * d + 1) * n_state] = pw[q][0][:, 0:n_state]
        aq_ref[j, :, (2 * d + 1) * n_state:(2 * d + 2) * n_state] = pw[q][1][:, 0:n_state]
    for s in range(q):
        b_ref[j, n_ch * s:n_ch * (s + 1), :] = jnp.concatenate(pieces[s], axis=1).astype(BF16)


def _sg_prep(lam_re, lam_im, log_step, b_re, b_im, c_re, c_im, d_skip):
    _, n_g, n_state, n_ch = b_re.shape
    q = CHUNK
    width = q * n_ch
    per_blk = LANES // n_ch
    n_blk = n_g // per_blk
    dup = lambda x: jnp.concatenate([x, x], axis=-1)
    lane_pad = lambda x: jnp.pad(x, ((0, 0),) * 3 + ((0, LANES - n_ch),))
    disc = _ssm_discretize(lam_re, lam_im, log_step)
    disc_r = [dup(x)[:, :, None, :] for x in disc]
    ct_re = lane_pad(jnp.swapaxes(c_re, -1, -2))
    ct_im = lane_pad(jnp.swapaxes(c_im, -1, -2))
    bt_re, bt_im = jnp.swapaxes(b_re, -1, -2), jnp.swapaxes(b_im, -1, -2)
    btr_re, btr_im = dup(bt_re), dup(bt_im)
    dpad = jnp.pad(d_skip.reshape(n_g, 1, n_ch), ((0, 0), (0, 0), ((q - 1) * n_ch, width)))

    per_step = 2
    sw = 8 * n_state

    def dspec(shape):
        return pl.BlockSpec((2, per_step) + shape, lambda g: (0, g, 0, 0))

    def gspec(shape):
        return pl.BlockSpec((per_step,) + shape, lambda g: (g, 0, 0))

    body = functools.partial(_sg_prep_body, n_state=n_state, n_ch=n_ch, per_step=per_step)
    intra, bst, cst, aq = pl.pallas_call(
        body,
        grid=(n_g // per_step,),
        in_specs=[dspec((n_state, LANES)), dspec((n_state, LANES)),
                  dspec((n_ch, n_state)), dspec((n_ch, n_state))] + [dspec((1, 2 * n_state))] * 4 + [
                  dspec((n_ch, 2 * n_state)), dspec((n_ch, 2 * n_state)),
                  gspec((1, 2 * width))],
        out_specs=[gspec((width, width)), gspec((width, sw)), gspec((sw, width)),
                   gspec((1, 4 * n_state))],
        out_shape=[jax.ShapeDtypeStruct((n_g, width, width), BF16),
                   jax.ShapeDtypeStruct((n_g, width, sw), BF16),
                   jax.ShapeDtypeStruct((n_g, sw, width), BF16),
                   jax.ShapeDtypeStruct((n_g, 1, 4 * n_state), F32)],
        compiler_params=_cparams(1),
        name="ssm_prep",
    )(ct_re, ct_im, bt_re, bt_im, *disc_r, btr_re, btr_im, dpad)
    a = aq.reshape(n_blk, per_blk // 2, 2, 2, 2, n_state)
    a = a.transpose(3, 4, 0, 1, 2, 5).reshape(2, 2, n_blk, per_blk // 2, LANES)
    coef = []
    for d in range(2):
        re, im = a[d, 0], a[d, 1]
        coef.append((jnp.concatenate([re, re], axis=1), jnp.concatenate([-im, im], axis=1)))
    return intra, bst, cst, coef


def _sg_mm_state_body(x_ref, xm_ref, b_ref, sf_ref, sb_ref, init_ref, *, rt, n_pair, width):
    def pair_update(x, q):
        g0, g1 = 2 * q, 2 * q + 1
        return (jnp.dot(x[:, g0 * width:(g0 + 1) * width], b_ref[g0], preferred_element_type=F32)
                + jnp.dot(x[:, g1 * width:(g1 + 1) * width], b_ref[g1], preferred_element_type=F32))

    n_sub = 2 * n_pair
    for q in range(n_pair):
        r = pair_update(x_ref, q)
        sf_ref[pl.ds(q, rt, stride=n_sub), :] = r[:, 0:LANES]
        sf_ref[pl.ds(n_pair + q, rt, stride=n_sub), :] = r[:, LANES:2 * LANES]
        sb_ref[pl.ds(q, rt, stride=n_sub), :] = r[:, 2 * LANES:3 * LANES]
        sb_ref[pl.ds(n_pair + q, rt, stride=n_sub), :] = r[:, 3 * LANES:4 * LANES]

    @pl.when(pl.program_id(1) == 0)
    def _():
        for q in range(n_pair):
            r0 = pair_update(xm_ref, q)
            init_ref[q:q + 1, :] = r0[0:1, 0:LANES]
            init_ref[n_pair + q:n_pair + q + 1, :] = r0[0:1, LANES:2 * LANES]


def _sg_mm_state(xg, xg_meta, bst, *, rt, per_blk):
    nc = xg.shape[0]
    n_g, width, sw = bst.shape
    n_blk = n_g // per_blk
    kw = per_blk * width
    n_pair = per_blk // 2
    n_sub = 2 * n_pair
    n_rt = nc // rt
    body = functools.partial(_sg_mm_state_body, rt=rt, n_pair=n_pair, width=width)
    dense = jax.ShapeDtypeStruct((n_blk * nc * n_sub, LANES), F32)
    dspec = pl.BlockSpec((rt * n_sub, LANES), lambda b, i: (b * n_rt + i, 0))
    return pl.pallas_call(
        body,
        grid=(n_blk, n_rt),
        in_specs=[pl.BlockSpec((rt, kw), lambda b, i: (i, b)),
                  pl.BlockSpec((CHUNK, kw), lambda b, i: (0, b)),
                  pl.BlockSpec((per_blk, width, sw), lambda b, i: (b, 0, 0))],
        out_specs=[dspec, dspec, pl.BlockSpec((None, n_sub, LANES), lambda b, i: (b, 0, 0))],
        out_shape=[dense, dense, jax.ShapeDtypeStruct((n_blk, n_sub, LANES), F32)],
        compiler_params=_cparams(2),
        name="ssm_state",
    )(xg, xg_meta, bst)


def _sg_scan_body(sf_ref, sb_ref, a1f_ref, a2f_ref, a1b_ref, a2b_ref, init_ref, xf_ref, xb_ref,
                  stf, stb, *, cb, n_blk, half, n_pblk, blk_pseq, blk_sseq):
    j = pl.program_id(0)
    pos = jnp.where(j < n_pblk, lax.rem(j, blk_pseq), lax.rem(jnp.maximum(j - n_pblk, 0), blk_sseq))

    @pl.when(pos == 0)
    def _():
        stf[...] = init_ref[...]
        stb[...] = jnp.zeros_like(stb)

    def body(i, carry):
        ef, eb = carry
        ib = cb - 1 - i
        nf, nb = [], []
        for g in range(n_blk):
            xf_ref[g, i] = ef[g]
            xb_ref[g, ib] = eb[g]
            nf.append(a1f_ref[g] * ef[g] + a2f_ref[g] * pltpu.roll(ef[g], half, 0) + sf_ref[g, i])
            nb.append(a1b_ref[g] * eb[g] + a2b_ref[g] * pltpu.roll(eb[g], half, 0) + sb_ref[g, ib])
        return tuple(nf), tuple(nb)

    ef, eb = lax.fori_loop(0, cb, body, (tuple(stf[g] for g in range(n_blk)),
                                         tuple(stb[g] for g in range(n_blk))))
    for g in range(n_blk):
        stf[g] = ef[g]
        stb[g] = eb[g]


def _sg_scan(sf, sb, coef, init, *, cb, chunks_pseq, chunks_sseq, n_pchunks):
    n_blk, nc, n_sub, _ = sf.shape
    n_pblk, blk_pseq, blk_sseq = n_pchunks // cb, chunks_pseq // cb, chunks_sseq // cb

    def bwd_block(j):
        in_p = j < n_pblk
        pos = jnp.where(in_p, lax.rem(j, blk_pseq), lax.rem(jnp.maximum(j - n_pblk, 0), blk_sseq))
        ln = jnp.where(in_p, blk_pseq, blk_sseq)
        return j - pos + ln - 1 - pos

    body = functools.partial(_sg_scan_body, cb=cb, n_blk=n_blk, half=n_sub // 2, n_pblk=n_pblk,
                             blk_pseq=blk_pseq, blk_sseq=blk_sseq)
    fwd = pl.BlockSpec((n_blk, cb, n_sub, LANES), lambda j: (0, j, 0, 0))
    bwd = pl.BlockSpec((n_blk, cb, n_sub, LANES), lambda j: (0, bwd_block(j), 0, 0))
    small = pl.BlockSpec((n_blk, n_sub, LANES), lambda j: (0, 0, 0))
    return pl.pallas_call(
        body,
        grid=(nc // cb,),
        in_specs=[fwd, bwd, small, small, small, small, small],
        out_specs=[fwd, bwd],
        out_shape=[jax.ShapeDtypeStruct(sf.shape, F32), jax.ShapeDtypeStruct(sb.shape, F32)],
        scratch_shapes=[pltpu.VMEM((n_blk, n_sub, LANES), F32), pltpu.VMEM((n_blk, n_sub, LANES), F32)],
        compiler_params=_cparams(1),
        name="ssm_scan",
    )(sf, sb, coef[0][0], coef[0][1], coef[1][0], coef[1][1], init)


def _gelu_tanh(x):
    return 0.5 * x * (1.0 + jnp.tanh(0.7978845608028654 * (x + 0.044715 * (x * x * x))))


def _sg_mm_out_body(x_ref, m_ref, xf_ref, xb_ref, c_ref, z_ref, *, rt, n_pair, width):
    n_sub = 2 * n_pair
    for q in range(n_pair):
        state = jnp.concatenate(
            [xf_ref[pl.ds(q, rt, stride=n_sub), :], xf_ref[pl.ds(n_pair + q, rt, stride=n_sub), :],
             xb_ref[pl.ds(q, rt, stride=n_sub), :], xb_ref[pl.ds(n_pair + q, rt, stride=n_sub), :]],
            axis=1).astype(BF16)
        for g in (2 * q, 2 * q + 1):
            cols = slice(g * width, (g + 1) * width)
            y = (jnp.dot(x_ref[:, cols], m_ref[g], preferred_element_type=F32)
                 + jnp.dot(state, c_ref[g], preferred_element_type=F32))
            z_ref[:, cols] = _gelu_tanh(y).astype(BF16)


def _sg_mm_out(xg, intra, xf, xb, cst, *, rt, per_blk):
    nc = xg.shape[0]
    n_g, sw, width = cst.shape
    n_blk = n_g // per_blk
    kw = per_blk * width
    n_pair = per_blk // 2
    n_sub = 2 * n_pair
    n_rt = nc // rt
    dspec = pl.BlockSpec((rt * n_sub, LANES), lambda b, i: (b * n_rt + i, 0))
    return pl.pallas_call(
        functools.partial(_sg_mm_out_body, rt=rt, n_pair=n_pair, width=width),
        grid=(n_blk, n_rt),
        in_specs=[pl.BlockSpec((rt, kw), lambda b, i: (i, b)),
                  pl.BlockSpec((per_blk, width, width), lambda b, i: (b, 0, 0)), dspec, dspec,
                  pl.BlockSpec((per_blk, sw, width), lambda b, i: (b, 0, 0))],
        out_specs=pl.BlockSpec((rt, kw), lambda b, i: (i, b)),
        out_shape=jax.ShapeDtypeStruct((nc, n_blk * kw), BF16),
        compiler_params=_cparams(2),
        name="ssm_out",
    )(xg, intra, xf, xb, cst)


def _mix_body(xp_ref, xs_ref, z_ref, c_ref, cprev_ref, cnext_ref, cmeta_ref, sga_ref, sgb_ref,
              wglu_ref, wpw_ref, wout_ref, wconv_ref, bdw_ref, lng_ref, lnb_ref, gffn_ref,
              rw_ref, rb_ref,
              h1_ref, v_ref, route_ref, cnt_ref,
              cw_ref, conv_ref, zsc_ref, *,
              tm, n_p_tiles, tiles_per_pseq, tiles_per_sseq, d_model, d_ssm, d_conv,
              n_grp, n_exp, exp_per_grp):
    i = pl.program_id(0)
    in_prompt = i < n_p_tiles
    x = jnp.where(in_prompt, xp_ref[...], xs_ref[...])
    pos_p = lax.rem(i, tiles_per_pseq)
    pos_s = lax.rem(jnp.maximum(i - n_p_tiles, 0), tiles_per_sseq)
    is_start = jnp.where(in_prompt, pos_p == 0, pos_s == 0)
    is_end = jnp.where(in_prompt, pos_p == tiles_per_pseq - 1, pos_s == tiles_per_sseq - 1)

    halo = CHUNK
    cw_ref[0:halo, :] = jnp.where(is_start, cmeta_ref[...], cprev_ref[...]).astype(F32)
    cw_ref[halo:halo + tm, :] = c_ref[...].astype(F32)
    cw_ref[halo + tm:2 * halo + tm, :] = jnp.where(is_end, 0.0, cnext_ref[...].astype(F32))
    sub = 8
    nq = wconv_ref.shape[1] // LANES
    for lc in range(d_conv // LANES):
        ls = slice(lc * LANES, (lc + 1) * LANES)
        stacked = jnp.concatenate(
            [cw_ref[sub * q:sub * q + tm + sub, ls] for q in range(nq)], axis=1).astype(BF16)
        part = jnp.dot(stacked, wconv_ref[lc], preferred_element_type=F32)
        out = part[0:tm, 0:LANES]
        for r in range(1, sub):
            out = out + part[r:r + tm, r * LANES:(r + 1) * LANES]
        conv_ref[:, ls] = out
    cc = conv_ref[...] + bdw_ref[...]
    mu = jnp.mean(cc, axis=-1, keepdims=True)
    var = jnp.mean(jnp.square(cc - mu), axis=-1, keepdims=True)
    cc = (cc - mu) * lax.rsqrt(var + EPS) * lng_ref[...] + lnb_ref[...]
    cc = (cc * _sigmoid(cc)).astype(BF16)
    y_b = jnp.dot(cc, wpw_ref[...], preferred_element_type=F32)

    per = LANES // CHUNK
    for b in range(d_ssm // LANES):
        for hf in range(CHUNK // per):
            groups = [z_ref[:, (b * per + g) * CHUNK * CHUNK + hf * LANES:
                            (b * per + g) * CHUNK * CHUNK + (hf + 1) * LANES].astype(F32)
                      for g in range(per)]
            for a, blk in enumerate(_regroup(groups, CHUNK)):
                zsc_ref[b, pl.ds(hf * per + a, tm // CHUNK, stride=CHUNK), :] = blk
    z = jnp.concatenate([zsc_ref[b] for b in range(d_ssm // LANES)], axis=1).astype(BF16)
    va = jnp.dot(z, wglu_ref[:, 0:d_model], preferred_element_type=F32)
    ga = jnp.dot(z, wglu_ref[:, d_model:2 * d_model], preferred_element_type=F32)
    y_a = va * _sigmoid(ga)
    merged = (sga_ref[...].astype(F32) * y_a + sgb_ref[...].astype(F32) * y_b).astype(BF16)
    h1 = x + jnp.dot(merged, wout_ref[...], preferred_element_type=F32)
    h1_ref[...] = h1
    v = _rms(h1, gffn_ref[...])
    half = d_model // 2
    v_ref[...] = _pack_bf16_pair(v[:, 0:half], v[:, half:d_model])

    v_hi = v.astype(BF16)
    v_lo = (v - v_hi.astype(F32)).astype(BF16)
    acc = (jnp.dot(v_hi, rw_ref[...], preferred_element_type=F32)
           + jnp.dot(v_lo, rw_ref[...], preferred_element_type=F32))
    logits = acc + pltpu.roll(acc, LANES // 2, 1) + rb_ref[...]
    lane = lax.broadcasted_iota(I32, (tm, LANES), 1).astype(F32)
    big = jnp.float32(1e9)
    neg = jnp.float32(-jnp.inf)
    gmask = lane < n_grp
    lg = jnp.where(gmask, logits, neg)
    gmax = jnp.max(lg, axis=-1, keepdims=True)
    grp = jnp.min(jnp.where(lg == gmax, lane, big), axis=-1, keepdims=True)
    p_grp = 1.0 / jnp.sum(jnp.where(gmask, jnp.exp(logits - gmax), 0.0), axis=-1, keepdims=True)
    lo = n_grp + grp * exp_per_grp
    emask = (lane >= lo) & (lane < lo + exp_per_grp)
    le = jnp.where(emask, logits, neg)
    m1 = jnp.max(le, axis=-1, keepdims=True)
    i1 = jnp.min(jnp.where(le == m1, lane, big), axis=-1, keepdims=True)
    le2 = jnp.where(lane == i1, neg, le)
    m2 = jnp.max(le2, axis=-1, keepdims=True)
    i2 = jnp.min(jnp.where(le2 == m2, lane, big), axis=-1, keepdims=True)
    t = jnp.exp(m2 - m1)
    w1 = 1.0 / (1.0 + t)
    e1 = i1 - n_grp
    e2 = i2 - n_grp
    route_ref[...] = jnp.where(lane == 0, e1, jnp.where(lane == 1, e2, jnp.where(
        lane == 2, p_grp * w1, jnp.where(lane == 3, p_grp * (t * w1), 0.0))))

    @pl.when(i == 0)
    def _():
        cnt_ref[...] = jnp.zeros_like(cnt_ref)

    hot = jnp.where((lane == e1) | (lane == e2), 1.0, 0.0)
    cnt_ref[...] += jnp.sum(hot, axis=0, keepdims=True)


def _conv_weights_body(taps_ref, w_ref, *, n_taps):
    eye = (lax.broadcasted_iota(I32, (LANES, LANES), 0)
           == lax.broadcasted_iota(I32, (LANES, LANES), 1))
    for p in range(n_taps):
        q, r = divmod(p, 8)
        diag = jnp.where(eye, taps_ref[p:p + 1, :], 0.0)
        w_ref[q * LANES:(q + 1) * LANES, r * LANES:(r + 1) * LANES] = diag.astype(BF16)


def _conv_weights(taps):
    n_taps, d_conv = taps.shape
    nq = n_taps // 8
    return pl.pallas_call(
        functools.partial(_conv_weights_body, n_taps=n_taps),
        grid=(d_conv // LANES,),
        in_specs=[pl.BlockSpec((n_taps, LANES), lambda b: (0, b))],
        out_specs=pl.BlockSpec((None, nq * LANES, 8 * LANES), lambda b: (b, 0, 0)),
        out_shape=jax.ShapeDtypeStruct((d_conv // LANES, nq * LANES, 8 * LANES), BF16),
        compiler_params=_cparams(1),
        name="conv_weights",
    )(taps)


def _mix(xp2, xs2, z, c, c_meta, sga, sgb, wglu, wpw, wout, wconv, bdw, lng, lnb, gffn, rw, rb, *,
         tm, p_seq, s_seq, n_grp, n_exp):
    n_p, d_model = xp2.shape
    n_s = xs2.shape[0]
    t = n_p + n_s
    d_ssm, d_conv = z.shape[1] // CHUNK, c.shape[1]
    n_p_tiles, n_s_tiles = n_p // tm, n_s // tm
    hpt = tm // CHUNK
    n_hblk = t // CHUNK
    body = functools.partial(
        _mix_body, tm=tm, n_p_tiles=n_p_tiles, tiles_per_pseq=p_seq // tm,
        tiles_per_sseq=s_seq // tm, d_model=d_model, d_ssm=d_ssm, d_conv=d_conv,
        n_grp=n_grp, n_exp=n_exp, exp_per_grp=n_exp // n_grp)
    row = lambda i: (i, 0)
    return pl.pallas_call(
        body,
        grid=(n_p_tiles + n_s_tiles,),
        in_specs=[
            pl.BlockSpec((tm, d_model), lambda i: (jnp.minimum(i, n_p_tiles - 1), 0)),
            pl.BlockSpec((tm, d_model), lambda i: (jnp.maximum(i - n_p_tiles, 0), 0)),
            pl.BlockSpec((tm // CHUNK, CHUNK * d_ssm), row),
            pl.BlockSpec((tm, d_conv), row),
            pl.BlockSpec((CHUNK, d_conv), lambda i: (jnp.maximum(i * hpt - 1, 0), 0)),
            pl.BlockSpec((CHUNK, d_conv), lambda i: (jnp.minimum((i + 1) * hpt, n_hblk - 1), 0)),
            _resident(c_meta.shape),
            pl.BlockSpec((tm, d_model), row),
            pl.BlockSpec((tm, d_model), row),
            _resident(wglu.shape), _resident(wpw.shape), _resident(wout.shape),
            _resident(wconv.shape), _resident(bdw.shape), _resident(lng.shape), _resident(lnb.shape),
            _resident(gffn.shape), _resident(rw.shape), _resident(rb.shape),
        ],
        out_specs=[pl.BlockSpec((tm, d_model), row), pl.BlockSpec((tm, d_model // 2), row),
                   pl.BlockSpec((tm, LANES), row), pl.BlockSpec((1, LANES), lambda i: (0, 0))],
        out_shape=[jax.ShapeDtypeStruct((t, d_model), F32),
                   jax.ShapeDtypeStruct((t, d_model // 2), I32),
                   jax.ShapeDtypeStruct((t, LANES), F32),
                   jax.ShapeDtypeStruct((1, LANES), F32)],
        scratch_shapes=[pltpu.VMEM((tm + 2 * CHUNK, d_conv), F32), pltpu.VMEM((tm, d_conv), F32),
                        pltpu.VMEM((d_ssm // LANES, tm, LANES), F32)],
        compiler_params=_cparams(1),
        name="mix",
    )(xp2, xs2, z, c, c, c, c_meta, sga, sgb, wglu, wpw, wout, wconv, bdw, lng, lnb, gffn, rw, rb)


def _positions_body(route_ref, offs_ref, dest_ref, carry_ref, *, te):
    @pl.when(pl.program_id(0) == 0)
    def _():
        carry_ref[...] = jnp.zeros_like(carry_ref)

    lane = lax.broadcasted_iota(I32, (te, LANES), 1).astype(F32)
    r = route_ref[...]
    oh1 = lane == r[:, 0:1]
    oh2 = lane == r[:, 1:2]
    both = jnp.where(oh1 | oh2, 1.0, 0.0)
    tri = jnp.where(lax.broadcasted_iota(I32, (te, te), 0) > lax.broadcasted_iota(I32, (te, te), 1),
                    1.0, 0.0).astype(BF16)
    before = jnp.dot(tri, both.astype(BF16), preferred_element_type=F32)
    base = before + carry_ref[...] + offs_ref[...]
    d1 = jnp.sum(jnp.where(oh1, base, 0.0), axis=-1, keepdims=True)
    d2 = jnp.sum(jnp.where(oh2, base, 0.0), axis=-1, keepdims=True)
    dest_ref[...] = jnp.where(lane == 0, d1, jnp.where(lane == 1, d2, 0.0)).astype(I32)
    carry_ref[...] += jnp.sum(both, axis=0, keepdims=True)


def _positions(route, offs, *, te):
    t = route.shape[0]
    return pl.pallas_call(
        functools.partial(_positions_body, te=te),
        grid=(t // te,),
        in_specs=[pl.BlockSpec((te, LANES), lambda i: (i, 0)),
                  pl.BlockSpec((1, LANES), lambda i: (0, 0))],
        out_specs=pl.BlockSpec((te, LANES), lambda i: (i, 0)),
        out_shape=jax.ShapeDtypeStruct((t, LANES), I32),
        scratch_shapes=[pltpu.VMEM((1, LANES), F32)],
        compiler_params=_cparams(1),
        name="positions",
    )(route, offs)


def _experts_body(te_ref, first_ref, last_ref, nxt_ref, short_ref, nused_ref,
                  xs_ref, w1_hbm, w3_hbm, w2_hbm, o_ref,
                  w1b, w3b, w2b, st1, st3, st2, sem, *, half, n_conv):
    i = pl.program_id(0)
    used = i < nused_ref[0]
    nxt = nxt_ref[i]

    def copies(e):
        return (pltpu.make_async_copy(w1_hbm.at[e], st1, sem.at[0]),
                pltpu.make_async_copy(w3_hbm.at[e], st3, sem.at[1]),
                pltpu.make_async_copy(w2_hbm.at[e], st2, sem.at[2]))

    def convert():
        r13 = st1.shape[0] // n_conv
        r2 = st2.shape[0] // n_conv

        def body(c, _):
            rows = pl.ds(pl.multiple_of(c * r13, r13), r13)
            w1b[rows, :] = st1[rows, :].astype(BF16)
            w3b[rows, :] = st3[rows, :].astype(BF16)
            rows2 = pl.ds(pl.multiple_of(c * r2, r2), r2)
            w2b[rows2, :] = st2[rows2, :].astype(BF16)
            return 0
        lax.fori_loop(0, n_conv, body, 0)

    @pl.when(i == 0)
    def _():
        for cp in copies(te_ref[0]):
            cp.start()
        for cp in copies(te_ref[0]):
            cp.wait()
        convert()

    @pl.when(used & (first_ref[i] == 1) & (nxt >= 0))
    def _():
        for cp in copies(nxt):
            cp.start(priority=1)

    def swiglu(n_rows):
        a, b = _unpack_bf16_pair(xs_ref[0:n_rows, :])
        a, b = a.astype(BF16), b.astype(BF16)

        def up(w_ref):
            return (jnp.dot(a, w_ref[0:half, :], preferred_element_type=F32)
                    + jnp.dot(b, w_ref[half:2 * half, :], preferred_element_type=F32))
        h1 = up(w1b)
        act = (h1 * _sigmoid(h1) * up(w3b)).astype(BF16)
        o = jnp.dot(act, w2b[...], preferred_element_type=F32)
        o_ref[0:n_rows, :] = _pack_bf16_pair(o[:, 0:half], o[:, half:2 * half])
        if n_rows < o_ref.shape[0]:
            o_ref[n_rows:, :] = jnp.zeros((o_ref.shape[0] - n_rows, half), I32)

    @pl.when(used & (short_ref[i] == 0))
    def _():
        swiglu(o_ref.shape[0])

    @pl.when(used & (short_ref[i] == 1))
    def _():
        swiglu(o_ref.shape[0] // 2)

    @pl.when(jnp.logical_not(used))
    def _():
        o_ref[...] = jnp.zeros_like(o_ref)

    @pl.when(used & (last_ref[i] == 1) & (nxt >= 0))
    def _():
        for cp in copies(nxt):
            cp.wait()
        convert()


def _experts(tile_expert, first, last, nxt, short, n_used, xs, w1, w3, w2, *, tme):
    rows, half = xs.shape
    n_e, d_model, d_exp = w1.shape
    grid_spec = pltpu.PrefetchScalarGridSpec(
        num_scalar_prefetch=6,
        grid=(rows // tme,),
        in_specs=[pl.BlockSpec((tme, half), lambda i, te, fi, la, nx, sh, nu: (jnp.minimum(i, nu[0] - 1), 0)),
                  pl.BlockSpec(memory_space=pl.ANY), pl.BlockSpec(memory_space=pl.ANY),
                  pl.BlockSpec(memory_space=pl.ANY)],
        out_specs=pl.BlockSpec((tme, half), lambda i, *_: (i, 0)),
        scratch_shapes=[pltpu.VMEM((d_model, d_exp), BF16), pltpu.VMEM((d_model, d_exp), BF16),
                        pltpu.VMEM((d_exp, d_model), BF16),
                        pltpu.VMEM((d_model, d_exp), F32), pltpu.VMEM((d_model, d_exp), F32),
                        pltpu.VMEM((d_exp, d_model), F32),
                        pltpu.SemaphoreType.DMA((3,))],
    )
    return pl.pallas_call(
        functools.partial(_experts_body, half=half, n_conv=8),
        grid_spec=grid_spec,
        out_shape=jax.ShapeDtypeStruct((rows, half), I32),
        compiler_params=_cparams(1),
        name="experts",
    )(tile_expert, first, last, nxt, short, n_used, xs, w1, w3, w2)


def _sc_gather(table, idx, *, chunk):
    n_rows, width = idx.shape[0], table.shape[1]
    n_workers = SC_CORES * SC_SUBCORES
    per_w = n_rows // n_workers
    n_chunks = per_w // chunk
    assert per_w * n_workers == n_rows and n_chunks * chunk == per_w and n_chunks % 2 == 0
    mesh = plsc.VectorSubcoreMesh(core_axis_name="c", subcore_axis_name="s",
                                  num_cores=SC_CORES, num_subcores=SC_SUBCORES)

    @functools.partial(
        pl.kernel, mesh=mesh,
        out_type=jax.ShapeDtypeStruct((n_rows, width), table.dtype),
        scratch_types=[pltpu.VMEM((per_w,), I32), pltpu.VMEM((2, chunk, width), table.dtype),
                       pltpu.SemaphoreType.DMA((2,))],
    )
    def gather_kernel(table_hbm, idx_hbm, out_hbm, idx_v, rows_v, sem):
        wid = lax.axis_index("s") * SC_CORES + lax.axis_index("c")
        base = pl.multiple_of(wid * per_w, per_w)
        pltpu.sync_copy(idx_hbm.at[pl.ds(base, per_w)], idx_v)

        def gather(j, slot):
            off = pl.multiple_of(j * chunk, chunk)
            return pltpu.make_async_copy(table_hbm.at[idx_v.at[pl.ds(off, chunk)]],
                                         rows_v.at[slot], sem.at[slot])

        gather(0, 0).start()

        @pl.loop(0, n_chunks, step=2)
        def _(j):
            for slot in range(2):
                jj = j + slot
                gather(jj, slot).wait()

                @pl.when(jj + 1 < n_chunks)
                def _():
                    gather(jj + 1, 1 - slot).start()
                off = pl.multiple_of(base + jj * chunk, chunk)
                pltpu.sync_copy(rows_v.at[slot], out_hbm.at[pl.ds(off, chunk)])

    return gather_kernel(table, idx)


def _sc_inverse(dest_flat, n_rows):
    n_assign = dest_flat.shape[0]
    n_tok = n_assign // 2
    n_workers = SC_CORES * SC_SUBCORES
    per_w = n_rows // n_workers
    assert per_w * n_workers == n_rows and per_w % SC_LANES == 0 and n_assign % SC_LANES == 0
    mesh = plsc.VectorSubcoreMesh(core_axis_name="c", subcore_axis_name="s",
                                  num_cores=SC_CORES, num_subcores=SC_SUBCORES)

    @functools.partial(
        pl.kernel, mesh=mesh,
        out_type=jax.ShapeDtypeStruct((n_rows,), I32),
        scratch_types=[pltpu.VMEM((n_assign,), I32), pltpu.VMEM((per_w,), I32)],
        compiler_params=pltpu.CompilerParams(needs_layout_passes=False),
    )
    def inverse_kernel(dest_hbm, src_hbm, dest_v, src_v):
        wid = lax.axis_index("s") * SC_CORES + lax.axis_index("c")
        lo = pl.multiple_of(wid * per_w, per_w)
        pltpu.sync_copy(dest_hbm, dest_v)
        lane = lax.iota(I32, SC_LANES)

        @pl.loop(0, per_w, step=SC_LANES)
        def _(i):
            src_v[pl.ds(pl.multiple_of(i, SC_LANES), SC_LANES)] = lax.rem(lo + i + lane, n_tok)

        @pl.loop(0, n_assign, step=SC_LANES)
        def _(i):
            d = dest_v[pl.ds(pl.multiple_of(i, SC_LANES), SC_LANES)]
            a = i + lane
            tok = jnp.where(a >= n_tok, a - n_tok, a)
            mine = (d >= lo) & (d < lo + per_w)
            plsc.store_scatter(src_v, [d - lo], tok, mask=mine)

        pltpu.sync_copy(src_v, src_hbm.at[pl.ds(lo, per_w)])

    return inverse_kernel(dest_flat)


def _final_body(g1_ref, g2_ref, h1_ref, route_ref, g_ref, y_ref, *, half):
    route = route_ref[...]
    w1, w2 = route[:, 2:3], route[:, 3:4]
    a1, b1 = _unpack_bf16_pair(g1_ref[...])
    a2, b2 = _unpack_bf16_pair(g2_ref[...])
    h1 = h1_ref[...]
    ha = h1[:, 0:half] + (a1 * w1 + a2 * w2)
    hb = h1[:, half:2 * half] + (b1 * w1 + b2 * w2)
    ms = (jnp.sum(ha * ha, axis=-1, keepdims=True)
          + jnp.sum(hb * hb, axis=-1, keepdims=True)) / (2 * half)
    inv = lax.rsqrt(ms + EPS)
    y_ref[:, 0:half] = ha * inv * g_ref[:, 0:half]
    y_ref[:, half:2 * half] = hb * inv * g_ref[:, half:2 * half]


def _final(gathered, h1, route, g, *, tf, row0, n_rows):
    t, d_model = h1.shape
    half = d_model // 2
    tile0, tiles_t = row0 // tf, t // tf
    return pl.pallas_call(
        functools.partial(_final_body, half=half),
        grid=(n_rows // tf,),
        in_specs=[pl.BlockSpec((tf, half), lambda i: (tile0 + i, 0)),
                  pl.BlockSpec((tf, half), lambda i: (tiles_t + tile0 + i, 0)),
                  pl.BlockSpec((tf, d_model), lambda i: (tile0 + i, 0)),
                  pl.BlockSpec((tf, LANES), lambda i: (tile0 + i, 0)),
                  pl.BlockSpec((1, d_model), lambda i: (0, 0))],
        out_specs=pl.BlockSpec((tf, d_model), lambda i: (i, 0)),
        out_shape=jax.ShapeDtypeStruct((n_rows, d_model), F32),
        compiler_params=_cparams(1),
        name="final",
    )(gathered, gathered, h1, route, g)


def kernel(x_prompt, x_sample, meta, norm_mix_g, w_in, ssm_lam_re, ssm_lam_im, ssm_log_step, ssm_b_re, ssm_b_im, ssm_c_re, ssm_c_im, ssm_d, ssm_w_glu, conv_w_dw, conv_b_dw, conv_ln_g, conv_ln_b, conv_w_pw, w_out, norm_ffn_g, router_group_w, router_group_b, router_expert_w, router_expert_b, expert_w1, expert_w3, expert_w2, final_g):
    assert w_in.shape[0] == 1, "single-layer trunk"
    bp, lp, d_model = x_prompt.shape
    bs, ls, _ = x_sample.shape
    n_meta = meta.shape[0]
    d_ssm = ssm_d.shape[-1]
    d_conv = conv_b_dw.shape[-1]
    n_ch = ssm_b_re.shape[-1]
    n_g = ssm_b_re.shape[2]
    n_grp = router_group_w.shape[-1]
    n_exp = router_expert_w.shape[-1]
    assert n_meta == CHUNK and lp % CHUNK == 0 and ls % CHUNK == 0
    n_p, n_s = bp * lp, bs * ls
    t = n_p + n_s
    tm = min(256, lp, ls)
    assert lp % tm == 0 and ls % tm == 0

    xp2 = x_prompt.reshape(n_p, d_model)
    xs2 = x_sample.reshape(n_s, d_model)
    row = lambda a: a.reshape(1, -1)
    w_in_bf = w_in[0].astype(BF16)

    assert n_ch == CHUNK and LANES % n_ch == 0 and n_g % (LANES // n_ch) == 0
    per_blk = LANES // n_ch
    xg, c, sga, sgb = _inproj(xp2, xs2, row(norm_mix_g[0]), w_in_bf, d_ssm=d_ssm, d_conv=d_conv,
                              tm=tm, n_ch=n_ch)
    xg_meta, c_meta = _meta_inproj(meta, row(norm_mix_g[0]), w_in_bf, d_ssm=d_ssm, d_conv=d_conv,
                                   n_ch=n_ch)

    intra, bst, cst, coef = _sg_prep(ssm_lam_re[0], ssm_lam_im[0], ssm_log_step[0], ssm_b_re[0],
                                     ssm_b_im[0], ssm_c_re[0], ssm_c_im[0], ssm_d[0])
    nc = t // CHUNK
    n_blk = d_ssm // LANES
    rt = min(512, nc)
    sf, sb, init = _sg_mm_state(xg, xg_meta, bst, rt=rt, per_blk=per_blk)
    n_sub = sf.shape[0] // (n_blk * nc)
    cb = min(64, lp // CHUNK, ls // CHUNK)
    xf, xb = _sg_scan(sf.reshape(n_blk, nc, n_sub, LANES), sb.reshape(n_blk, nc, n_sub, LANES),
                      coef, init, cb=cb, chunks_pseq=lp // CHUNK, chunks_sseq=ls // CHUNK,
                      n_pchunks=n_p // CHUNK)
    z = _sg_mm_out(xg, intra, xf.reshape(sf.shape), xb.reshape(sb.shape), cst, rt=rt,
                   per_blk=per_blk)

    assert n_grp + n_exp <= LANES // 2
    rw32 = jnp.pad(jnp.concatenate([router_group_w[0], router_expert_w[0]], axis=1),
                   ((0, 0), (0, LANES // 2 - n_grp - n_exp)))
    rw_hi = rw32.astype(BF16)
    rw = jnp.concatenate([rw_hi, (rw32 - rw_hi.astype(F32)).astype(BF16)], axis=1)
    rb = jnp.zeros((1, LANES), F32).at[0, 0:n_grp].set(router_group_b[0]).at[
        0, n_grp:n_grp + n_exp].set(router_expert_b[0])
    conv_w = conv_w_dw.shape[1]
    off = CHUNK - conv_w // 2
    assert 0 <= off and conv_w + off <= 2 * CHUNK
    taps = jnp.pad(conv_w_dw[0], ((off, -(conv_w + off) % 8), (0, 0)))
    h1, v, route, cnt = _mix(
        xp2, xs2, z, c, c_meta, sga, sgb, ssm_w_glu[0].astype(BF16), conv_w_pw[0].astype(BF16),
        w_out[0].astype(BF16), _conv_weights(taps), row(conv_b_dw[0]), row(conv_ln_g[0]),
        row(conv_ln_b[0]), row(norm_ffn_g[0]), rw, rb,
        tm=tm, p_seq=lp, s_seq=ls, n_grp=n_grp, n_exp=n_exp)

    tme = 256
    counts = cnt[0, 0:n_exp].astype(I32)
    tiles_e = (counts + tme - 1) // tme
    tile_end = jnp.cumsum(tiles_e)
    n_used = tile_end[-1]
    offs = jnp.zeros((1, LANES), F32).at[0, 0:n_exp].set(((tile_end - tiles_e) * tme).astype(F32))
    n_tiles = (2 * t) // tme + n_exp
    ids = jnp.arange(n_tiles, dtype=I32)

    def expert_of(tile):
        return jnp.minimum(jnp.sum((tile[:, None] >= tile_end[None, :]).astype(I32), axis=1), n_exp - 1)
    te_map = expert_of(jnp.minimum(ids, n_used - 1))
    onehot = te_map[:, None] == jnp.arange(n_exp, dtype=I32)[None, :]
    run_end = jnp.sum(jnp.where(onehot, tile_end[None, :], 0), axis=1)
    run_start = run_end - jnp.sum(jnp.where(onehot, tiles_e[None, :], 0), axis=1)
    valid = ids < n_used
    first = (valid & (ids == run_start)).astype(I32)
    last = (valid & (ids == run_end - 1)).astype(I32)
    nxt = jnp.where(valid & (run_end < n_used), expert_of(run_end), -1)
    tail_rows = counts - (tiles_e - 1) * tme
    short_e = ((tiles_e > 0) & (tail_rows <= tme // 2)).astype(I32)
    short = last * jnp.sum(jnp.where(onehot, short_e[None, :], 0), axis=1)

    dest = _positions(route, offs, te=min(512, t))
    idx = jnp.concatenate([dest[:, 0], dest[:, 1]])
    src = _sc_inverse(idx, n_tiles * tme)
    xs = _sc_gather(v, src, chunk=32)
    o = _experts(te_map, first, last, nxt, short, n_used.reshape(1), xs, expert_w1[0],
                 expert_w3[0], expert_w2[0], tme=tme)

    tf = min(256, n_p, n_s)
    gathered = _sc_gather(o, idx, chunk=32)
    fg = row(final_g)
    y_p = _final(gathered, h1, route, fg, tf=tf, row0=0, n_rows=n_p)
    y_s = _final(gathered, h1, route, fg, tf=tf, row0=n_p, n_rows=n_s)
    return (y_p.reshape(bp, lp, d_model), y_s.reshape(bs, ls, d_model))
```
